```python
import math
import jax
import jax.numpy as jnp
from jax import lax
import numpy as np

D_MODEL = 2048
BATCH = 4
SEQ = 2048
DEPTH = 2
DEC_BATCH = 128
DEC_SEQ = 8
PAST_LEN = 16384
PAGE_SIZE = 128

S5_WIDTH = D_MODEL // 4
S5_GROUP = 16
S5_GROUPS = S5_WIDTH // S5_GROUP
S5_STATE = 64
S5_DT_MIN = 0.001
S5_DT_MAX = 0.1
RWKV_HEAD_DIM = 64
RWKV_WIDTH = D_MODEL // 4
RWKV_HEADS = RWKV_WIDTH // RWKV_HEAD_DIM
RWKV_DECAY_LORA = 96
RWKV_ICLR_LORA = 96
RWKV_GATE_LORA = 256
RWKV_COLS = 3 * RWKV_WIDTH + RWKV_DECAY_LORA + RWKV_ICLR_LORA + RWKV_GATE_LORA
RWKV_GN_EPS = 64e-5
GDN_HEAD_DIM = 128
GDN_WIDTH = D_MODEL // 2
GDN_HEADS = GDN_WIDTH // GDN_HEAD_DIM
GDN_CONV = 4
GDN_CHUNK = 64
N_BRANCH = 3
IN_SIZES = (S5_WIDTH, RWKV_COLS, 3 * GDN_WIDTH, GDN_WIDTH, GDN_HEADS, GDN_HEADS, D_MODEL, D_MODEL, D_MODEL)
IN_COLS = S5_WIDTH + RWKV_COLS + 4 * GDN_WIDTH + 2 * GDN_HEADS + N_BRANCH * D_MODEL
D_FF = 5632
FFN_CONV = 3
NORM_EPS = 1e-6

kernel_name = "hybrid_s5_rwkv7_gdn_convffn_step"


def split_cols(x, sizes):
    idx = np.cumsum(np.asarray(sizes))[:-1].tolist()
    return jnp.split(x, idx, axis=-1)


def rmsnorm(x, g):
    xf = x.astype(jnp.float32)
    ms = jnp.mean(xf * xf, axis=-1, keepdims=True)
    return (xf * lax.rsqrt(ms + NORM_EPS) * g.astype(jnp.float32)).astype(x.dtype)


def l2norm(x):
    return x * lax.rsqrt(jnp.sum(x * x, axis=-1, keepdims=True) + 1e-6)


def causal_dwconv(u, buf, w):
    width, ch = w.shape
    ext = jnp.concatenate([buf.astype(u.dtype), u], axis=1)
    out = lax.conv_general_dilated(ext, w.astype(u.dtype)[:, None, :], window_strides=(1,), padding='VALID',
                                   dimension_numbers=('NWC', 'WIO', 'NWC'), feature_group_count=ch)
    return out, ext[:, ext.shape[1] - (width - 1):]


def _linear_combine(left, right):
    a1, b1 = left
    a2, b2 = right
    return a1 * a2, a2 * b1 + b2


def s5_mixer(u, x0_re, x0_im, lam_re, lam_im, b_re, b_im, c_re, c_im, d, log_step, w_glu):
    f32 = jnp.float32
    bsz, L, _ = u.shape
    uf = u.astype(f32)
    ug = uf.reshape(bsz, L, S5_GROUPS, S5_GROUP)
    lam = lax.complex(lam_re.astype(f32), lam_im.astype(f32))
    step = jnp.exp(log_step.astype(f32))[:, None]
    lam_bar = jnp.exp(lam * step)
    b_bar = ((lam_bar - 1.0) / lam)[..., None] * lax.complex(b_re.astype(f32), b_im.astype(f32))
    bu = lax.complex(jnp.einsum('gpc,blgc->blgp', b_bar.real, ug),
                     jnp.einsum('gpc,blgc->blgp', b_bar.imag, ug))
    x0 = lax.complex(x0_re.astype(f32), x0_im.astype(f32))
    bu = bu.at[:, 0].add(lam_bar * x0)
    a = jnp.broadcast_to(lam_bar, bu.shape)
    _, xs = lax.associative_scan(_linear_combine, (a, bu), axis=1)
    y = (jnp.einsum('gcp,blgp->blgc', c_re.astype(f32), xs.real)
         - jnp.einsum('gcp,blgp->blgc', c_im.astype(f32), xs.imag))
    y = y.reshape(bsz, L, S5_WIDTH) + d.astype(f32) * uf
    y = jax.nn.gelu(y)
    y = y * jax.nn.sigmoid(y @ w_glu.astype(f32))
    x_last = xs[:, -1]
    return y.astype(u.dtype), x_last.real, x_last.imag


def _rwkv7_step(S, inp):
    r_t, w_t, k_t, v_t, a_t, b_t = inp
    sa = jnp.einsum('bhvk,bhk->bhv', S, a_t)
    S = S * w_t[:, :, None, :] + sa[..., None] * b_t[:, :, None, :] + v_t[..., None] * k_t[:, :, None, :]
    return S, jnp.einsum('bhvk,bhk->bhv', S, r_t)


def rwkv7_mixer(h, shift0, wkv0, mu, w0, w2, a0, a2, g2, k_k, k_a, r_k, ln_w, ln_b):
    f32 = jnp.float32
    bsz, L, _ = h.shape
    hf = h.astype(f32)
    prev = jnp.concatenate([shift0.astype(f32)[:, None], hf[:, :-1]], axis=1)
    hs = hf + (prev - hf) * mu.astype(f32)
    r, k, v, wd, ad, gd = split_cols(hs, (RWKV_WIDTH, RWKV_WIDTH, RWKV_WIDTH,
                                          RWKV_DECAY_LORA, RWKV_ICLR_LORA, RWKV_GATE_LORA))
    w_log = -jax.nn.softplus(-(w0.astype(f32) + jnp.tanh(wd) @ w2.astype(f32))) - 0.5
    decay = jnp.exp(-jnp.exp(w_log))
    a = jax.nn.sigmoid(a0.astype(f32) + ad @ a2.astype(f32))
    g = jax.nn.sigmoid(gd) @ g2.astype(f32)
    heads = lambda t: t.reshape(bsz, L, RWKV_HEADS, RWKV_HEAD_DIM)
    kk = l2norm(heads(k * k_k.astype(f32)))
    k = k * (1.0 + (a - 1.0) * k_a.astype(f32))
    r, decay, k, v, a = heads(r), heads(decay), heads(k), heads(v), heads(a)
    seq = tuple(jnp.moveaxis(t, 1, 0) for t in (r, decay, k, v, -kk, kk * a))
    wkv_last, ys = lax.scan(_rwkv7_step, wkv0.astype(f32), seq)
    y = jnp.moveaxis(ys, 0, 1)
    mean = jnp.mean(y, axis=-1, keepdims=True)
    var = jnp.mean(jnp.square(y - mean), axis=-1, keepdims=True)
    y = ((y - mean) * lax.rsqrt(var + RWKV_GN_EPS)).reshape(bsz, L, RWKV_WIDTH)
    y = y * ln_w.astype(f32) + ln_b.astype(f32)
    bonus = jnp.sum(r * k * r_k.astype(f32), axis=-1, keepdims=True) * v
    y = (y + bonus.reshape(bsz, L, RWKV_WIDTH)) * g
    return y.astype(h.dtype), hf[:, -1], wkv_last


def chunk_gated_delta_rule(q, k, v, g, beta, s0):
    bsz, L, H, dk = q.shape
    dv = v.shape[-1]
    c = min(GDN_CHUNK, L)
    n = -(-L // c)
    pad = n * c - L
    if pad:
        pw = ((0, 0), (0, pad), (0, 0), (0, 0))
        q, k, v = jnp.pad(q, pw), jnp.pad(k, pw), jnp.pad(v, pw)
        g, beta = jnp.pad(g, pw[:3]), jnp.pad(beta, pw[:3])

    def chunks(t):
        t = t.reshape((bsz, n, c, H) + t.shape[3:])
        return jnp.moveaxis(jnp.moveaxis(t, 1, 0), 3, 2)

    qc, kc, vc = chunks(q), chunks(k), chunks(v)
    gc = jnp.cumsum(chunks(g), axis=-1)
    bc = chunks(beta)
    kb, vb = kc * bc[..., None], vc * bc[..., None]
    causal = jnp.tril(jnp.ones((c, c), dtype=bool))
    strict = jnp.tril(jnp.ones((c, c), dtype=bool), -1)
    decay = jnp.exp(jnp.where(causal, gc[..., :, None] - gc[..., None, :], -jnp.inf))
    lower = jnp.where(strict, jnp.einsum('nbhid,nbhjd->nbhij', kb, kc) * decay, 0.0)
    rhs = jnp.concatenate([vb, kb * jnp.exp(gc)[..., None]], axis=-1)
    sol = lax.linalg.triangular_solve(lower + jnp.eye(c, dtype=q.dtype), rhs, left_side=True,
                                      lower=True, unit_diagonal=True)
    u, w = sol[..., :dv], sol[..., dv:]
    intra = jnp.einsum('nbhid,nbhjd->nbhij', qc, kc) * decay

    def body(S, xs):
        q_i, k_i, u_i, w_i, g_i, a_i = xs
        v_new = u_i - jnp.einsum('bhcd,bhde->bhce', w_i, S)
        o_i = (jnp.einsum('bhcd,bhde->bhce', q_i * jnp.exp(g_i)[..., None], S)
               + jnp.einsum('bhij,bhje->bhie', a_i, v_new))
        g_last = g_i[..., -1:]
        S = (S * jnp.exp(g_last)[..., None]
             + jnp.einsum('bhcd,bhce->bhde', k_i * jnp.exp(g_last - g_i)[..., None], v_new))
        return S, o_i

    s_last, o = lax.scan(body, s0, (qc, kc, u, w, gc, intra))
    o = o.transpose(1, 0, 3, 2, 4).reshape(bsz, n * c, H, dv)[:, :L]
    return o, s_last


def gdn_mixer(qkv_raw, z, beta_raw, a_raw, conv0, s0, conv_w, a_log, dt_bias, norm_g):
    f32 = jnp.float32
    bsz, L, _ = qkv_raw.shape
    qkv, conv_last = causal_dwconv(qkv_raw.astype(f32), conv0, conv_w.astype(f32))
    qkv = jax.nn.silu(qkv)
    q, k, v = [t.reshape(bsz, L, GDN_HEADS, GDN_HEAD_DIM) for t in jnp.split(qkv, 3, axis=-1)]
    q = l2norm(q) * (GDN_HEAD_DIM ** -0.5)
    k = l2norm(k)
    beta = jax.nn.sigmoid(beta_raw.astype(f32))
    g = -jnp.exp(a_log.astype(f32)) * jax.nn.softplus(a_raw.astype(f32) + dt_bias.astype(f32))
    o, s_last = chunk_gated_delta_rule(q, k, v, g, beta, s0.astype(f32))
    o = o * lax.rsqrt(jnp.mean(o * o, axis=-1, keepdims=True) + NORM_EPS) * norm_g.astype(f32)
    o = o.reshape(bsz, L, GDN_WIDTH) * jax.nn.silu(z.astype(f32))
    return o.astype(z.dtype), conv_last, s_last


def decoder_layer(x, st, p):
    s5_re, s5_im, rw_shift, rw_wkv, gdn_conv, gdn_state, ffn_conv = st
    hn = rmsnorm(x, p['norm1'])
    proj = hn @ p['w_in']
    u_s5, h_rw, qkv, z, b_raw, a_raw, gt_s5, gt_rw, gt_gd = split_cols(proj, IN_SIZES)
    y_s5, n_s5_re, n_s5_im = s5_mixer(u_s5, s5_re, s5_im, p['s5_lambda_re'], p['s5_lambda_im'], p['s5_b_re'],
                                      p['s5_b_im'], p['s5_c_re'], p['s5_c_im'], p['s5_d'], p['s5_log_step'],
                                      p['s5_w_glu'])
    y_rw, n_shift, n_wkv = rwkv7_mixer(h_rw, rw_shift, rw_wkv, p['rwkv_mu'], p['rwkv_w0'], p['rwkv_w2'],
                                       p['rwkv_a0'], p['rwkv_a2'], p['rwkv_g2'], p['rwkv_k_k'], p['rwkv_k_a'],
                                       p['rwkv_r_k'], p['rwkv_ln_w'], p['rwkv_ln_b'])
    y_gd, n_gconv, n_gstate = gdn_mixer(qkv, z, b_raw, a_raw, gdn_conv, gdn_state, p['gdn_conv_w'],
                                        p['gdn_a_log'], p['gdn_dt_bias'], p['gdn_norm_g'])
    merged = (jax.nn.sigmoid(gt_s5) * (y_s5 @ p['w_br_s5'])
              + jax.nn.sigmoid(gt_rw) * (y_rw @ p['w_br_rwkv'])
              + jax.nn.sigmoid(gt_gd) * (y_gd @ p['w_br_gdn']))
    x = x + merged @ p['w_out']
    hn2 = rmsnorm(x, p['norm2'])
    up = hn2 @ p['ffn_w_up']
    up_c, n_ffn = causal_dwconv(up, ffn_conv, p['ffn_conv_w'])
    gate, val = jnp.split(up_c + p['ffn_conv_b'], 2, axis=-1)
    x = x + (jax.nn.silu(gate) * val) @ p['ffn_w_down']
    return x, (n_s5_re, n_s5_im, n_shift, n_wkv, n_gconv, n_gstate, n_ffn)


def run_trunk(x, states, layer_params, final_g):
    new = []
    for l in range(DEPTH):
        x, ns = decoder_layer(x, tuple(s[l] for s in states), layer_params[l])
        new.append(ns)
    new_states = tuple(jnp.stack([ns[i] for ns in new]).astype(states[i].dtype) for i in range(len(states)))
    return rmsnorm(x, final_g), new_states


def setup_inputs(seed: int = 0) -> dict:
    key = jax.random.key(seed)
    ks = jax.random.split(key, 64)
    counter = [0]
    f32 = jnp.float32

    def nk():
        counter[0] += 1
        return ks[counter[0] - 1]

    def nrm(shape, scale):
        return scale * jax.random.normal(nk(), shape, f32)

    def unif(shape, lo, hi):
        return jax.random.uniform(nk(), shape, f32, lo, hi)

    D, G, P, GS = D_MODEL, S5_GROUPS, S5_STATE, S5_GROUP
    dt = jnp.exp(unif((DEPTH, GDN_HEADS), math.log(0.001), math.log(0.1)))
    return {
        'x_prompt': nrm((BATCH, SEQ, D), 1.0),
        'x_sample': nrm((DEC_BATCH, DEC_SEQ, D), 1.0),
        'state_s5_re': nrm((DEPTH, DEC_BATCH, G, P), 0.1),
        'state_s5_im': nrm((DEPTH, DEC_BATCH, G, P), 0.1),
        'state_rwkv_shift': nrm((DEPTH, DEC_BATCH, RWKV_COLS), 1.0),
        'state_rwkv_wkv': nrm((DEPTH, DEC_BATCH, RWKV_HEADS, RWKV_HEAD_DIM, RWKV_HEAD_DIM), 0.3),
        'state_gdn_conv': nrm((DEPTH, DEC_BATCH, GDN_CONV - 1, 3 * GDN_WIDTH), 1.0),
        'state_gdn': nrm((DEPTH, DEC_BATCH, GDN_HEADS, GDN_HEAD_DIM, GDN_HEAD_DIM), 0.1),
        'state_ffn_conv': nrm((DEPTH, DEC_BATCH, FFN_CONV - 1, 2 * D_FF), 1.0),
        'norm1_g': 1.0 + nrm((DEPTH, D), 0.02),
        'norm2_g': 1.0 + nrm((DEPTH, D), 0.02),
        'final_norm_g': 1.0 + nrm((D,), 0.02),
        'w_in': nrm((DEPTH, D, IN_COLS), D ** -0.5),
        's5_lambda_re': -0.5 + nrm((DEPTH, G, P), 0.01),
        's5_lambda_im': math.pi * jnp.arange(P, dtype=f32) + nrm((DEPTH, G, P), 0.01),
        's5_b_re': nrm((DEPTH, G, P, GS), (2 * GS) ** -0.5),
        's5_b_im': nrm((DEPTH, G, P, GS), (2 * GS) ** -0.5),
        's5_c_re': nrm((DEPTH, G, GS, P), (2 * P) ** -0.5),
        's5_c_im': nrm((DEPTH, G, GS, P), (2 * P) ** -0.5),
        's5_d': nrm((DEPTH, S5_WIDTH), 1.0),
        's5_log_step': unif((DEPTH, G), math.log(S5_DT_MIN), math.log(S5_DT_MAX)),
        's5_w_glu': nrm((DEPTH, S5_WIDTH, S5_WIDTH), S5_WIDTH ** -0.5),
        'rwkv_mu': unif((DEPTH, RWKV_COLS), 0.0, 1.0),
        'rwkv_w0': unif((DEPTH, RWKV_WIDTH), -6.0, -1.0),
        'rwkv_w2': nrm((DEPTH, RWKV_DECAY_LORA, RWKV_WIDTH), 0.1 * RWKV_DECAY_LORA ** -0.5),
        'rwkv_a0': nrm((DEPTH, RWKV_WIDTH), 0.1),
        'rwkv_a2': nrm((DEPTH, RWKV_ICLR_LORA, RWKV_WIDTH), RWKV_ICLR_LORA ** -0.5),
        'rwkv_g2': nrm((DEPTH, RWKV_GATE_LORA, RWKV_WIDTH), RWKV_GATE_LORA ** -0.5),
        'rwkv_k_k': 0.85 + nrm((DEPTH, RWKV_WIDTH), 0.02),
        'rwkv_k_a': 1.0 + nrm((DEPTH, RWKV_WIDTH), 0.02),
        'rwkv_r_k': nrm((DEPTH, RWKV_HEADS, RWKV_HEAD_DIM), 0.1),
        'rwkv_ln_w': 1.0 + nrm((DEPTH, RWKV_WIDTH), 0.02),
        'rwkv_ln_b': nrm((DEPTH, RWKV_WIDTH), 0.02),
        'gdn_conv_w': nrm((DEPTH, GDN_CONV, 3 * GDN_WIDTH), GDN_CONV ** -0.5),
        'gdn_a_log': jnp.log(unif((DEPTH, GDN_HEADS), 1.0, 16.0)),
        'gdn_dt_bias': dt + jnp.log(-jnp.expm1(-dt)),
        'gdn_norm_g': 1.0 + nrm((DEPTH, GDN_HEAD_DIM), 0.02),
        'w_br_s5': nrm((DEPTH, S5_WIDTH, D), S5_WIDTH ** -0.5),
        'w_br_rwkv': nrm((DEPTH, RWKV_WIDTH, D), RWKV_WIDTH ** -0.5),
        'w_br_gdn': nrm((DEPTH, GDN_WIDTH, D), GDN_WIDTH ** -0.5),
        'w_out': nrm((DEPTH, D, D), D ** -0.5),
        'ffn_w_up': nrm((DEPTH, D, 2 * D_FF), D ** -0.5),
        'ffn_conv_w': nrm((DEPTH, FFN_CONV, 2 * D_FF), FFN_CONV ** -0.5),
        'ffn_conv_b': nrm((DEPTH, 2 * D_FF), 0.02),
        'ffn_w_down': nrm((DEPTH, D_FF, D), D_FF ** -0.5),
    }


def reference(x_prompt, x_sample, state_s5_re, state_s5_im, state_rwkv_shift, state_rwkv_wkv, state_gdn_conv,
              state_gdn, state_ffn_conv, norm1_g, norm2_g, final_norm_g, w_in, s5_lambda_re, s5_lambda_im,
              s5_b_re, s5_b_im, s5_c_re, s5_c_im, s5_d, s5_log_step, s5_w_glu, rwkv_mu, rwkv_w0, rwkv_w2,
              rwkv_a0, rwkv_a2, rwkv_g2, rwkv_k_k, rwkv_k_a, rwkv_r_k, rwkv_ln_w, rwkv_ln_b, gdn_conv_w,
              gdn_a_log, gdn_dt_bias, gdn_norm_g, w_br_s5, w_br_rwkv, w_br_gdn, w_out, ffn_w_up, ffn_conv_w,
              ffn_conv_b, ffn_w_down):
    layer_params = [dict(norm1=norm1_g[l], norm2=norm2_g[l], w_in=w_in[l],
                         s5_lambda_re=s5_lambda_re[l], s5_lambda_im=s5_lambda_im[l], s5_b_re=s5_b_re[l],
                         s5_b_im=s5_b_im[l], s5_c_re=s5_c_re[l], s5_c_im=s5_c_im[l], s5_d=s5_d[l],
                         s5_log_step=s5_log_step[l], s5_w_glu=s5_w_glu[l],
                         rwkv_mu=rwkv_mu[l], rwkv_w0=rwkv_w0[l], rwkv_w2=rwkv_w2[l], rwkv_a0=rwkv_a0[l],
                         rwkv_a2=rwkv_a2[l], rwkv_g2=rwkv_g2[l], rwkv_k_k=rwkv_k_k[l], rwkv_k_a=rwkv_k_a[l],
                         rwkv_r_k=rwkv_r_k[l], rwkv_ln_w=rwkv_ln_w[l], rwkv_ln_b=rwkv_ln_b[l],
                         gdn_conv_w=gdn_conv_w[l], gdn_a_log=gdn_a_log[l], gdn_dt_bias=gdn_dt_bias[l],
                         gdn_norm_g=gdn_norm_g[l], w_br_s5=w_br_s5[l], w_br_rwkv=w_br_rwkv[l],
                         w_br_gdn=w_br_gdn[l], w_out=w_out[l], ffn_w_up=ffn_w_up[l],
                         ffn_conv_w=ffn_conv_w[l], ffn_conv_b=ffn_conv_b[l], ffn_w_down=ffn_w_down[l])
                    for l in range(DEPTH)]
    sample_states = (state_s5_re, state_s5_im, state_rwkv_shift, state_rwkv_wkv, state_gdn_conv, state_gdn,
                     state_ffn_conv)
    n_prompt = x_prompt.shape[0]
    prompt_states = tuple(jnp.zeros((DEPTH, n_prompt) + s.shape[2:], s.dtype) for s in sample_states)
    y_prompt, (p_s5_re, p_s5_im, p_rwkv_shift, p_rwkv_wkv, p_gdn_conv, p_gdn, p_ffn_conv) = run_trunk(
        x_prompt, prompt_states, layer_params, final_norm_g)
    y_sample, (s_s5_re, s_s5_im, s_rwkv_shift, s_rwkv_wkv, s_gdn_conv, s_gdn, s_ffn_conv) = run_trunk(
        x_sample, sample_states, layer_params, final_norm_g)
    return (y_prompt, y_sample, p_s5_re, p_s5_im, p_rwkv_shift, p_rwkv_wkv, p_gdn_conv, p_gdn, p_ffn_conv,
            s_s5_re, s_s5_im, s_rwkv_shift, s_rwkv_wkv, s_gdn_conv, s_gdn, s_ffn_conv)
```

```python
import functools
import math

import jax
import jax.numpy as jnp
from jax import lax
from jax.experimental import pallas as pl
from jax.experimental.pallas import tpu as pltpu

F32 = jnp.float32
BF16 = jnp.bfloat16

SUBLANES = 8
LANES = 128
VMEM_LIMIT = 52 * 1024 * 1024

D_MODEL = 2048
DEPTH = 2
S5_WIDTH = 512
S5_GROUP = 16
S5_GROUPS = 32
S5_STATE = 64
S5_NSTATE = S5_GROUPS * S5_STATE
RW_WIDTH = 512
RW_HEADS = 8
RW_HD = 64
RW_LORA = 96
RW_GATE = 256
RW_COLS = 3 * RW_WIDTH + 2 * RW_LORA + RW_GATE
RW_PCOLS = 2048
RW_GN_EPS = 64e-5
GD_WIDTH = 1024
GD_HEADS = 8
GD_HD = 128
GD_CONV = 4
D_FF = 5632
FFN_CONV = 3
NORM_EPS = 1e-6
IN_COLS = 12752
P_GATES, P_RW, P_Q, P_Z, P_S5, P_BA, P_COLS = 0, 6144, 8192, 11264, 12288, 12800, 13312


def _cparams(sem):
    return pltpu.CompilerParams(dimension_semantics=sem, vmem_limit_bytes=VMEM_LIMIT)


def _mm(a, b):
    return jnp.dot(a, b, preferred_element_type=F32)


def _mm_nt(a, b):
    return lax.dot_general(a, b, (((1,), (1,)), ((), ())), preferred_element_type=F32)


def _mm_tn(a, b):
    return lax.dot_general(a, b, (((0,), (0,)), ((), ())), preferred_element_type=F32)


def _sigmoid(x):
    return 1.0 / (1.0 + jnp.exp(-x))


def _softplus(x):
    return jnp.maximum(x, 0.0) + jnp.log1p(jnp.exp(-jnp.abs(x)))


def _gelu_tanh(x):
    return 0.5 * x * (1.0 + jnp.tanh(math.sqrt(2.0 / math.pi) * (x + 0.044715 * (x * x * x))))


def _row_iota(shape):
    return lax.broadcasted_iota(jnp.int32, shape, 0)


def _shift_chain(x, d, prev8):
    rolled = pltpu.roll(x, d, 0)
    row = _row_iota(prev8.shape)
    head = jnp.where(row < d, pltpu.roll(prev8, d, 0), rolled[:SUBLANES])
    return jnp.concatenate([head, rolled[SUBLANES:]], axis=0)


def _shift_seq8(x, d, frame, width):
    n = x.shape[0]
    rolled = pltpu.roll(x, d, 0)
    patch = pltpu.roll(frame, (n - (width - 1 - d)) % n, 0) if (width - 1 - d) else frame
    row = _row_iota(x.shape) % SUBLANES
    return jnp.where(row < d, patch, rolled)


def _cumsum_rows(x):
    c = x.shape[0]
    row = _row_iota(x.shape) % SUBLANES
    for d in (1, 2, 4):
        x = x + jnp.where(row >= d, pltpu.roll(x, d, 0), 0.0)
    if c > SUBLANES:
        blocks = [x[SUBLANES * i:SUBLANES * (i + 1)] for i in range(c // SUBLANES)]
        for i in range(1, len(blocks)):
            blocks[i] = blocks[i] + blocks[i - 1][SUBLANES - 1:SUBLANES, :]
        x = jnp.concatenate(blocks, axis=0)
    return x


def _fwd_subst(low, rhs):
    c = low.shape[0]
    nb = c // SUBLANES
    xb = [rhs[SUBLANES * i:SUBLANES * (i + 1)] for i in range(nb)]
    for j in range(c - 1):
        jb, jr = divmod(j, SUBLANES)
        row = xb[jb][jr:jr + 1, :]
        for i in range((j + 1) // SUBLANES, nb):
            xb[i] = xb[i] - low[SUBLANES * i:SUBLANES * (i + 1), j:j + 1] * row
    return jnp.concatenate(xb, axis=0) if nb > 1 else xb[0]


def _norm_matmul_kernel(x_ref, g_ref, w_ref, o_ref, xn_ref, *, rows, chunk):
    @pl.when(pl.program_id(1) == 0)
    def _():
        def body(i, carry):
            r0 = pl.multiple_of(i * chunk, chunk)
            x = x_ref[pl.ds(r0, chunk), :]
            ms = jnp.mean(x * x, axis=-1, keepdims=True)
            xn_ref[pl.ds(r0, chunk), :] = (x * lax.rsqrt(ms + NORM_EPS) * g_ref[...]).astype(BF16)
            return carry
        lax.fori_loop(0, rows // chunk, body, 0)

    o_ref[...] = _mm(xn_ref[...], w_ref[...])


def _norm_matmul(x, g, w, tm, tn):
    m, k = x.shape
    n = w.shape[1]
    return pl.pallas_call(
        functools.partial(_norm_matmul_kernel, rows=tm, chunk=128),
        grid=(m // tm, n // tn),
        in_specs=[pl.BlockSpec((tm, k), lambda i, j: (i, 0)),
                  pl.BlockSpec((1, k), lambda i, j: (0, 0)),
                  pl.BlockSpec((k, tn), lambda i, j: (0, j))],
        out_specs=pl.BlockSpec((tm, tn), lambda i, j: (i, j)),
        out_shape=jax.ShapeDtypeStruct((m, n), F32),
        scratch_shapes=[pltpu.VMEM((tm, k), BF16)],
        compiler_params=_cparams(("parallel", "arbitrary")),
        name="norm_matmul",
    )(x, g.reshape(1, k), w)


def _matmul_res_kernel(a_ref, w_ref, r_ref, o_ref):
    o_ref[...] = r_ref[...] + _mm(a_ref[...], w_ref[...])


def _matmul_res(a, w, res, tm, tn):
    m, k = a.shape
    n = w.shape[1]
    return pl.pallas_call(
        _matmul_res_kernel,
        grid=(m // tm, n // tn),
        in_specs=[pl.BlockSpec((tm, k), lambda i, j: (i, 0)),
                  pl.BlockSpec((k, tn), lambda i, j: (0, j)),
                  pl.BlockSpec((tm, tn), lambda i, j: (i, j))],
        out_specs=pl.BlockSpec((tm, tn), lambda i, j: (i, j)),
        out_shape=jax.ShapeDtypeStruct((m, n), F32),
        compiler_params=_cparams(("parallel", "arbitrary")),
        name="matmul_res",
    )(a, w, res)


def _merge_kernel(y1_ref, y2_ref, y3_ref, w1_ref, w2_ref, w3_ref, g1_ref, g2_ref, g3_ref, o_ref):
    acc = _sigmoid(g1_ref[...]) * _mm(y1_ref[...], w1_ref[...])
    acc = acc + _sigmoid(g2_ref[...]) * _mm(y2_ref[...], w2_ref[...])
    acc = acc + _sigmoid(g3_ref[...]) * _mm(y3_ref[...], w3_ref[...])
    o_ref[...] = acc.astype(BF16)


def _merge(y_s5, y_rw, y_gd, w1, w2, w3, proj, tm, tn):
    m = y_s5.shape[0]
    n = w1.shape[1]
    nj = n // tn
    gate_spec = lambda b: pl.BlockSpec((tm, tn), lambda i, j: (i, P_GATES // tn + b * nj + j))
    return pl.pallas_call(
        _merge_kernel,
        grid=(m // tm, nj),
        in_specs=[pl.BlockSpec((tm, S5_WIDTH), lambda i, j: (i, 0)),
                  pl.BlockSpec((tm, RW_WIDTH), lambda i, j: (i, 0)),
                  pl.BlockSpec((tm, GD_WIDTH), lambda i, j: (i, 0)),
                  pl.BlockSpec((S5_WIDTH, tn), lambda i, j: (0, j)),
                  pl.BlockSpec((RW_WIDTH, tn), lambda i, j: (0, j)),
                  pl.BlockSpec((GD_WIDTH, tn), lambda i, j: (0, j)),
                  gate_spec(0), gate_spec(1), gate_spec(2)],
        out_specs=pl.BlockSpec((tm, tn), lambda i, j: (i, j)),
        out_shape=jax.ShapeDtypeStruct((m, n), BF16),
        compiler_params=_cparams(("parallel", "arbitrary")),
        name="merge",
    )(y_s5, y_rw, y_gd, w1, w2, w3, proj, proj, proj)


def _rmsnorm_kernel(x_ref, g_ref, o_ref):
    x = x_ref[...]
    ms = jnp.mean(x * x, axis=-1, keepdims=True)
    o_ref[...] = x * lax.rsqrt(ms + NORM_EPS) * g_ref[...]


def _rmsnorm(x, g, tm):
    m, k = x.shape
    return pl.pallas_call(
        _rmsnorm_kernel,
        grid=(m // tm,),
        in_specs=[pl.BlockSpec((tm, k), lambda i: (i, 0)), pl.BlockSpec((1, k), lambda i: (0, 0))],
        out_specs=pl.BlockSpec((tm, k), lambda i: (i, 0)),
        out_shape=jax.ShapeDtypeStruct((m, k), F32),
        compiler_params=_cparams(("parallel",)),
        name="final_norm",
    )(x, g.reshape(1, k))


def _ffn_act_kernel(ug_ref, uv_ref, pg_ref, pv_ref, wg_ref, wv_ref, bg_ref, bv_ref, o_ref, *, chained):
    def conv(u_ref, p_ref, w_ref, b_ref):
        u = u_ref[...]
        if chained:
            p = jnp.where(pl.program_id(1) == 0, 0.0, p_ref[...])
            s1, s2 = _shift_chain(u, 1, p), _shift_chain(u, 2, p)
        else:
            p = p_ref[...]
            s1, s2 = _shift_seq8(u, 1, p, FFN_CONV), _shift_seq8(u, 2, p, FFN_CONV)
        return w_ref[2:3, :] * u + w_ref[1:2, :] * s1 + w_ref[0:1, :] * s2 + b_ref[...]

    gate = conv(ug_ref, pg_ref, wg_ref, bg_ref)
    val = conv(uv_ref, pv_ref, wv_ref, bv_ref)
    o_ref[...] = (gate * _sigmoid(gate) * val).astype(BF16)


def _ffn_act_prompt(up, conv_w, conv_b, nb, seq, tm, tc):
    nt, nj = seq // tm, D_FF // tc
    tb = tm // SUBLANES
    prev = lambda off: pl.BlockSpec(
        (SUBLANES, tc), lambda b, t, j: (jnp.maximum((b * nt + t) * tb - 1, 0), off + j))
    cur = lambda off: pl.BlockSpec((tm, tc), lambda b, t, j: (b * nt + t, off + j))
    par = lambda rows, off: pl.BlockSpec((rows, tc), lambda b, t, j: (0, off + j))
    return pl.pallas_call(
        functools.partial(_ffn_act_kernel, chained=True),
        grid=(nb, nt, nj),
        in_specs=[cur(0), cur(nj), prev(0), prev(nj), par(FFN_CONV, 0), par(FFN_CONV, nj), par(1, 0), par(1, nj)],
        out_specs=pl.BlockSpec((tm, tc), lambda b, t, j: (b * nt + t, j)),
        out_shape=jax.ShapeDtypeStruct((nb * seq, D_FF), BF16),
        compiler_params=_cparams(("parallel", "arbitrary", "arbitrary")),
        name="ffn_act_prompt",
    )(up, up, up, up, conv_w, conv_w, conv_b, conv_b)


def _ffn_act_sample(up, frames, conv_w, conv_b, row0, rows, tm, tc):
    nj = D_FF // tc
    r0 = row0 // tm
    cur = lambda off: pl.BlockSpec((tm, tc), lambda i, j: (r0 + i, off + j))
    frm = lambda off: pl.BlockSpec((tm, tc), lambda i, j: (i, off + j))
    par = lambda nrow, off: pl.BlockSpec((nrow, tc), lambda i, j: (0, off + j))
    return pl.pallas_call(
        functools.partial(_ffn_act_kernel, chained=False),
        grid=(rows // tm, nj),
        in_specs=[cur(0), cur(nj), frm(0), frm(nj), par(FFN_CONV, 0), par(FFN_CONV, nj), par(1, 0), par(1, nj)],
        out_specs=pl.BlockSpec((tm, tc), lambda i, j: (i, j)),
        out_shape=jax.ShapeDtypeStruct((rows, D_FF), BF16),
        compiler_params=_cparams(("parallel", "arbitrary")),
        name="ffn_act_sample",
    )(up, up, frames, frames, conv_w, conv_w, conv_b, conv_b)


S5_Q = 4
S5_QW = S5_NSTATE // S5_Q


def _s5_kernel(*refs, chained, tm):
    if chained:
        (u_ref, bre_ref, bim_ref, cre_ref, cim_ref, mre_ref, mim_ref, pre_ref, pim_ref, d_ref, wglu_ref,
         y_ref, xlre_ref, xlim_ref, sre, sim, car_re, car_im) = refs
    else:
        (u_ref, x0re_ref, x0im_ref, bre_ref, bim_ref, cre_ref, cim_ref, mre_ref, mim_ref, pre_ref, pim_ref,
         d_ref, wglu_ref, y_ref, xlre_ref, xlim_ref, sre, sim) = refs

    u = u_ref[...]
    ub = u.astype(BF16)
    for q in range(S5_Q):
        uq = ub[:, LANES * q:LANES * (q + 1)]
        sre[:, S5_QW * q:S5_QW * (q + 1)] = _mm(uq, bre_ref[q])
        sim[:, S5_QW * q:S5_QW * (q + 1)] = _mm(uq, bim_ref[q])

    if chained:
        @pl.when(pl.program_id(1) == 0)
        def _():
            car_re[...] = jnp.zeros_like(car_re)
            car_im[...] = jnp.zeros_like(car_im)

    def blk(i, carry):
        r0 = pl.multiple_of(i * SUBLANES, SUBLANES)
        for q in range(S5_Q):
            sl = slice(S5_QW * q, S5_QW * (q + 1))
            xr = sre[pl.ds(r0, SUBLANES), sl]
            xi = sim[pl.ds(r0, SUBLANES), sl]
            for li, d in enumerate((1, 2, 4)):
                mr, mi = mre_ref[li, :, sl], mim_ref[li, :, sl]
                sr, si = pltpu.roll(xr, d, 0), pltpu.roll(xi, d, 0)
                xr, xi = xr + (mr * sr - mi * si), xi + (mr * si + mi * sr)
            if chained:
                cr, ci = car_re[:, sl], car_im[:, sl]
            else:
                cr, ci = x0re_ref[pl.ds(i, 1), sl], x0im_ref[pl.ds(i, 1), sl]
            pr, pi_ = pre_ref[:, sl], pim_ref[:, sl]
            xr, xi = xr + (pr * cr - pi_ * ci), xi + (pr * ci + pi_ * cr)
            sre[pl.ds(r0, SUBLANES), sl] = xr
            sim[pl.ds(r0, SUBLANES), sl] = xi
            if chained:
                car_re[:, sl] = xr[SUBLANES - 1:SUBLANES, :]
                car_im[:, sl] = xi[SUBLANES - 1:SUBLANES, :]
            else:
                xlre_ref[pl.ds(i, 1), sl] = xr[SUBLANES - 1:SUBLANES, :]
                xlim_ref[pl.ds(i, 1), sl] = xi[SUBLANES - 1:SUBLANES, :]
        return carry

    lax.fori_loop(0, tm // SUBLANES, blk, 0)

    if chained:
        xlre_ref[0] = car_re[...]
        xlim_ref[0] = car_im[...]

    ys = []
    for q in range(S5_Q):
        sl = slice(S5_QW * q, S5_QW * (q + 1))
        ys.append(_mm(sre[:, sl].astype(BF16), cre_ref[q]) - _mm(sim[:, sl].astype(BF16), cim_ref[q]))
    y = jnp.concatenate(ys, axis=1) + d_ref[...] * u
    y = _gelu_tanh(y)
    y = y * _sigmoid(_mm(y.astype(BF16), wglu_ref[...]))
    y_ref[...] = y.astype(BF16)


def _s5_param_specs(nidx):
    z = (0,) * 3
    c3 = lambda shape: pl.BlockSpec(shape, (lambda *a: (0, 0, 0)))
    c2 = lambda shape: pl.BlockSpec(shape, (lambda *a: (0, 0)))
    del z, nidx
    return [c3((S5_Q, LANES, S5_QW)), c3((S5_Q, LANES, S5_QW)), c3((S5_Q, S5_QW, LANES)), c3((S5_Q, S5_QW, LANES)),
            c3((3, SUBLANES, S5_NSTATE)), c3((3, SUBLANES, S5_NSTATE)),
            c2((SUBLANES, S5_NSTATE)), c2((SUBLANES, S5_NSTATE)), c2((1, S5_WIDTH)), c2((S5_WIDTH, S5_WIDTH))]


def _s5_prompt(proj, sp, nb, seq, tm):
    nt = seq // tm
    cb = P_S5 // S5_WIDTH
    y, xre, xim = pl.pallas_call(
        functools.partial(_s5_kernel, chained=True, tm=tm),
        grid=(nb, nt),
        in_specs=[pl.BlockSpec((tm, S5_WIDTH), lambda b, t: (b * nt + t, cb))] + _s5_param_specs(2),
        out_specs=[pl.BlockSpec((tm, S5_WIDTH), lambda b, t: (b * nt + t, 0)),
                   pl.BlockSpec((1, 1, S5_NSTATE), lambda b, t: (b, 0, 0)),
                   pl.BlockSpec((1, 1, S5_NSTATE), lambda b, t: (b, 0, 0))],
        out_shape=[jax.ShapeDtypeStruct((nb * seq, S5_WIDTH), BF16),
                   jax.ShapeDtypeStruct((nb, 1, S5_NSTATE), F32),
                   jax.ShapeDtypeStruct((nb, 1, S5_NSTATE), F32)],
        scratch_shapes=[pltpu.VMEM((tm, S5_NSTATE), F32), pltpu.VMEM((tm, S5_NSTATE), F32),
                        pltpu.VMEM((1, S5_NSTATE), F32), pltpu.VMEM((1, S5_NSTATE), F32)],
        compiler_params=_cparams(("parallel", "arbitrary")),
        name="s5_prompt",
    )(proj, *sp)
    return y, xre.reshape(nb, S5_GROUPS, S5_STATE), xim.reshape(nb, S5_GROUPS, S5_STATE)


def _s5_sample(proj, x0re, x0im, sp, row0, nseq, tm):
    ns = tm // SUBLANES
    cb = P_S5 // S5_WIDTH
    r0 = row0 // tm
    st = pl.BlockSpec((ns, S5_NSTATE), lambda i: (i, 0))
    y, xre, xim = pl.pallas_call(
        functools.partial(_s5_kernel, chained=False, tm=tm),
        grid=(nseq // ns,),
        in_specs=[pl.BlockSpec((tm, S5_WIDTH), lambda i: (r0 + i, cb)), st, st] + _s5_param_specs(1),
        out_specs=[pl.BlockSpec((tm, S5_WIDTH), lambda i: (i, 0)), st, st],
        out_shape=[jax.ShapeDtypeStruct((nseq * SUBLANES, S5_WIDTH), BF16),
                   jax.ShapeDtypeStruct((nseq, S5_NSTATE), F32),
                   jax.ShapeDtypeStruct((nseq, S5_NSTATE), F32)],
        scratch_shapes=[pltpu.VMEM((tm, S5_NSTATE), F32), pltpu.VMEM((tm, S5_NSTATE), F32)],
        compiler_params=_cparams(("parallel",)),
        name="s5_sample",
    )(proj, x0re.reshape(nseq, S5_NSTATE), x0im.reshape(nseq, S5_NSTATE), *sp)
    return y, xre.reshape(nseq, S5_GROUPS, S5_STATE), xim.reshape(nseq, S5_GROUPS, S5_STATE)


def _s5_params(lam_re, lam_im, b_re, b_im, c_re, c_im, d, log_step, w_glu):
    lam = lax.complex(lam_re, lam_im)
    step = jnp.exp(log_step)[:, None]
    lam_bar = jnp.exp(lam * step)
    b_bar = ((lam_bar - 1.0) / lam)[..., None] * lax.complex(b_re, b_im)

    def bblk(bm):
        bm = bm.reshape(S5_Q, 8, S5_STATE, S5_GROUP)
        eye = jnp.eye(8, dtype=F32)
        out = jnp.einsum('qgpc,gh->qgchp', bm, eye)
        return out.reshape(S5_Q, LANES, S5_QW).astype(BF16)

    def cblk(cm):
        cm = cm.reshape(S5_Q, 8, S5_GROUP, S5_STATE)
        eye = jnp.eye(8, dtype=F32)
        out = jnp.einsum('qgcp,gh->qgphc', cm, eye)
        return out.reshape(S5_Q, S5_QW, LANES).astype(BF16)

    lam_flat = lam_bar.reshape(1, S5_NSTATE)
    pows = [lam_flat]
    for _ in range(SUBLANES - 1):
        pows.append(pows[-1] * lam_flat)
    row = jnp.arange(SUBLANES)[:, None]
    m = jnp.stack([jnp.where(row >= dd, pows[dd - 1], 0.0) for dd in (1, 2, 4)])
    p = jnp.concatenate(pows, axis=0)
    return (bblk(b_bar.real), bblk(b_bar.imag), cblk(c_re), cblk(c_im),
            m.real, m.imag, p.real, p.imag, d.reshape(1, S5_WIDTH), w_glu.astype(BF16))


def _seg64_sum(x):
    outs = []
    for t in range(x.shape[1] // LANES):
        xt = x[:, LANES * t:LANES * (t + 1)]
        lo = lax.broadcasted_iota(jnp.int32, xt.shape, 1) < RW_HD
        s_lo = jnp.sum(jnp.where(lo, xt, 0.0), axis=-1, keepdims=True)
        s_hi = jnp.sum(jnp.where(lo, 0.0, xt), axis=-1, keepdims=True)
        outs.append(jnp.where(lo, s_lo, s_hi))
    return jnp.concatenate(outs, axis=1)


def _rwkv_kernel(*refs, chained, tm, c):
    if chained:
        (h_ref, mu_ref, w0_ref, w2_ref, a0_ref, a2_ref, g2_ref, kk_ref, ka_ref, rk_ref, lnw_ref, lnb_ref,
         y_ref, so_ref, r_s, k_s, v_s, a_s, b_s, lw_s, y_s, prev_s, st_s) = refs
    else:
        (h_ref, fr_ref, si_ref, mu_ref, w0_ref, w2_ref, a0_ref, a2_ref, g2_ref, kk_ref, ka_ref, rk_ref,
         lnw_ref, lnb_ref, y_ref, so_ref, r_s, k_s, v_s, a_s, b_s, lw_s, y_s) = refs
    mdt = BF16 if c >= 16 else F32
    w = RW_WIDTH

    h = h_ref[...]
    if chained:
        @pl.when(pl.program_id(1) == 0)
        def _():
            prev_s[...] = jnp.zeros_like(prev_s)
            st_s[...] = jnp.zeros_like(st_s)
        prev = _shift_chain(h, 1, prev_s[...])
        prev_s[...] = h[tm - SUBLANES:, :]
    else:
        prev = _shift_seq8(h, 1, fr_ref[...], 2)
    hs = h + (prev - h) * mu_ref[...]
    r = hs[:, 0:w]
    k = hs[:, w:2 * w]
    v = hs[:, 2 * w:3 * w]
    wd = hs[:, 3 * w:3 * w + LANES]
    ad = hs[:, 3 * w + LANES:3 * w + 2 * LANES]
    gd = hs[:, 3 * w + 2 * LANES:]
    w_log = -_softplus(-(w0_ref[...] + _mm(jnp.tanh(wd).astype(BF16), w2_ref[...]))) - 0.5
    lw_s[...] = -jnp.exp(w_log)
    a_ic = _sigmoid(a0_ref[...] + _mm(ad.astype(BF16), a2_ref[...]))
    gate = _mm(_sigmoid(gd).astype(BF16), g2_ref[...])
    kx = k * kk_ref[...]
    kkn = kx * lax.rsqrt(_seg64_sum(kx * kx) + 1e-6)
    k2 = k * (1.0 + (a_ic - 1.0) * ka_ref[...])
    r_s[...] = r
    k_s[...] = k2
    v_s[...] = v
    a_s[...] = -kkn
    b_s[...] = kkn * a_ic

    ri = lax.broadcasted_iota(jnp.int32, (c, c), 0)
    ci = lax.broadcasted_iota(jnp.int32, (c, c), 1)
    strict = ri > ci
    causal = ri >= ci

    def chunk(idx, carry):
        r0 = pl.multiple_of(idx * c, c)
        rows = pl.ds(r0, c)
        lw = lw_s[rows, :]
        cum = _cumsum_rows(lw)
        cl = cum[c - 1:c, :]
        e_pos, e_neg, e_end = jnp.exp(cum), jnp.exp(-cum), jnp.exp(cl - cum)
        rr, kc, vc, ac, bc = r_s[rows, :], k_s[rows, :], v_s[rows, :], a_s[rows, :], b_s[rows, :]
        rt, kt, bt = rr * e_pos, kc * e_neg, bc * e_neg
        at = ac * jnp.exp(cum - lw)
        kh, bh = kc * e_end, bc * e_end
        wc = jnp.exp(cl)
        for hd in range(RW_HEADS):
            sl = slice(RW_HD * hd, RW_HD * (hd + 1))
            ath, vh = at[:, sl], vc[:, sl]
            lhs = jnp.concatenate([ath, rt[:, sl]], axis=0).astype(mdt)
            rhs = jnp.concatenate([kt[:, sl], bt[:, sl]], axis=0).astype(mdt)
            ab = _mm_nt(lhs, rhs)
            a_ak = jnp.where(strict, ab[:c, :c], 0.0)
            a_ab = jnp.where(strict, ab[:c, c:], 0.0)
            a_rk = jnp.where(causal, ab[c:, :c], 0.0)
            a_rb = jnp.where(causal, ab[c:, c:], 0.0)
            sol = _fwd_subst(-a_ab, jnp.concatenate([ath, _mm(a_ak.astype(mdt), vh.astype(mdt))], axis=1))
            p1, p2 = sol[:, :RW_HD], sol[:, RW_HD:]
            st = st_s[hd] if chained else si_ref[idx, hd]
            stm = st.astype(mdt)
            u = _mm_nt(p1.astype(mdt), stm) + p2
            vu = jnp.concatenate([vh, u], axis=0).astype(mdt)
            y = _mm_nt(rt[:, sl].astype(mdt), stm) + _mm(jnp.concatenate([a_rk, a_rb], axis=1).astype(mdt), vu)
            kb = jnp.concatenate([kh[:, sl], bh[:, sl]], axis=0).astype(mdt)
            st_new = st * wc[:, sl] + _mm_tn(vu, kb)
            if chained:
                st_s[hd] = st_new
            else:
                so_ref[idx, hd] = st_new
            y_s[rows, sl] = y
        return carry

    lax.fori_loop(0, tm // c, chunk, 0)

    if chained:
        so_ref[0] = st_s[...]

    y = y_s[...]
    mean = _seg64_sum(y) * (1.0 / RW_HD)
    yc = y - mean
    var = _seg64_sum(yc * yc) * (1.0 / RW_HD)
    yn = yc * lax.rsqrt(var + RW_GN_EPS) * lnw_ref[...] + lnb_ref[...]
    bonus = _seg64_sum(r_s[...] * k_s[...] * rk_ref[...]) * v_s[...]
    y_ref[...] = ((yn + bonus) * gate).astype(BF16)


def _rwkv_param_specs():
    c2 = lambda shape: pl.BlockSpec(shape, (lambda *a: (0, 0)))
    row = c2((1, RW_WIDTH))
    return [c2((1, RW_PCOLS)), row, c2((LANES, RW_WIDTH)), row, c2((LANES, RW_WIDTH)), c2((RW_GATE, RW_WIDTH)),
            row, row, row, row, row]


def _rwkv_scratch(tm):
    return [pltpu.VMEM((tm, RW_WIDTH), F32) for _ in range(7)]


def _rwkv_prompt(proj, rp, nb, seq, tm, c):
    nt = seq // tm
    cb = P_RW // RW_PCOLS
    y, st = pl.pallas_call(
        functools.partial(_rwkv_kernel, chained=True, tm=tm, c=c),
        grid=(nb, nt),
        in_specs=[pl.BlockSpec((tm, RW_PCOLS), lambda b, t: (b * nt + t, cb))] + _rwkv_param_specs(),
        out_specs=[pl.BlockSpec((tm, RW_WIDTH), lambda b, t: (b * nt + t, 0)),
                   pl.BlockSpec((1, RW_HEADS, RW_HD, RW_HD), lambda b, t: (b, 0, 0, 0))],
        out_shape=[jax.ShapeDtypeStruct((nb * seq, RW_WIDTH), BF16),
                   jax.ShapeDtypeStruct((nb, RW_HEADS, RW_HD, RW_HD), F32)],
        scratch_shapes=_rwkv_scratch(tm) + [pltpu.VMEM((SUBLANES, RW_PCOLS), F32),
                                            pltpu.VMEM((RW_HEADS, RW_HD, RW_HD), F32)],
        compiler_params=_cparams(("parallel", "arbitrary")),
        name="rwkv_prompt",
    )(proj, *rp)
    return y, st


def _rwkv_sample(proj, frames, st0, rp, row0, nseq, tm):
    ns = tm // SUBLANES
    cb = P_RW // RW_PCOLS
    r0 = row0 // tm
    st_spec = pl.BlockSpec((ns, RW_HEADS, RW_HD, RW_HD), lambda i: (i, 0, 0, 0))
    y, st = pl.pallas_call(
        functools.partial(_rwkv_kernel, chained=False, tm=tm, c=SUBLANES),
        grid=(nseq // ns,),
        in_specs=[pl.BlockSpec((tm, RW_PCOLS), lambda i: (r0 + i, cb)),
                  pl.BlockSpec((tm, RW_PCOLS), lambda i: (i, 0)), st_spec] + _rwkv_param_specs(),
        out_specs=[pl.BlockSpec((tm, RW_WIDTH), lambda i: (i, 0)), st_spec],
        out_shape=[jax.ShapeDtypeStruct((nseq * SUBLANES, RW_WIDTH), BF16),
                   jax.ShapeDtypeStruct((nseq, RW_HEADS, RW_HD, RW_HD), F32)],
        scratch_shapes=_rwkv_scratch(tm),
        compiler_params=_cparams(("parallel",)),
        name="rwkv_sample",
    )(proj, frames, st0, *rp)
    return y, st


def _rw_pad_cols(x):
    z = jnp.zeros(x.shape[:-1] + (LANES - RW_LORA,), x.dtype)
    a, b = 3 * RW_WIDTH, 3 * RW_WIDTH + RW_LORA
    return jnp.concatenate([x[..., :a], x[..., a:b], z, x[..., b:b + RW_LORA], z, x[..., b + RW_LORA:]], axis=-1)


def _rw_unpad_cols(x):
    a = 3 * RW_WIDTH
    return jnp.concatenate([x[..., :a + RW_LORA], x[..., a + LANES:a + LANES + RW_LORA], x[..., a + 2 * LANES:]],
                           axis=-1)


def _rwkv_params(mu, w0, w2, a0, a2, g2, k_k, k_a, r_k, ln_w, ln_b):
    row = lambda t: t.reshape(1, RW_WIDTH)
    padr = lambda t: jnp.concatenate([t, jnp.zeros((LANES - RW_LORA, RW_WIDTH), t.dtype)], axis=0).astype(BF16)
    return (_rw_pad_cols(mu).reshape(1, RW_PCOLS), row(w0), padr(w2), row(a0), padr(a2), g2.astype(BF16),
            row(k_k), row(k_a), row(r_k), row(ln_w), row(ln_b))


def _gdn_kernel(*refs, chained, tm, c):
    if chained:
        (q_ref, k_ref, v_ref, z_ref, ba_ref, cwq_ref, cwk_ref, cwv_ref, al_ref, dt_ref, ng_ref,
         y_ref, so_ref, q_s, k_s, v_s, ba_s, o_s, pq_s, pk_s, pv_s, st_s) = refs
        frames = (None, None, None)
    else:
        (q_ref, k_ref, v_ref, z_ref, ba_ref, fq_ref, fk_ref, fv_ref, si_ref, cwq_ref, cwk_ref, cwv_ref,
         al_ref, dt_ref, ng_ref, y_ref, so_ref, q_s, k_s, v_s, ba_s, o_s) = refs
        frames = (fq_ref, fk_ref, fv_ref)
    mdt = BF16 if c >= 16 else F32

    if chained:
        @pl.when(pl.program_id(1) == 0)
        def _():
            for p in (pq_s, pk_s, pv_s):
                p[...] = jnp.zeros_like(p)
            st_s[...] = jnp.zeros_like(st_s)
        prevs = (pq_s, pk_s, pv_s)

    def conv_act(idx, u_ref, cw_ref):
        u = u_ref[...]
        if chained:
            p = prevs[idx][...]
            sh = [_shift_chain(u, d, p) for d in (1, 2, 3)]
            prevs[idx][...] = u[tm - SUBLANES:, :]
        else:
            f = frames[idx][...]
            sh = [_shift_seq8(u, d, f, GD_CONV) for d in (1, 2, 3)]
        x = cw_ref[3:4, :] * u + cw_ref[2:3, :] * sh[0] + cw_ref[1:2, :] * sh[1] + cw_ref[0:1, :] * sh[2]
        return x * _sigmoid(x)

    def l2n(x):
        outs = []
        for hd in range(GD_HEADS):
            xh = x[:, GD_HD * hd:GD_HD * (hd + 1)]
            outs.append(xh * lax.rsqrt(jnp.sum(xh * xh, axis=-1, keepdims=True) + 1e-6))
        return jnp.concatenate(outs, axis=1)

    q_s[...] = l2n(conv_act(0, q_ref, cwq_ref)) * (GD_HD ** -0.5)
    k_s[...] = l2n(conv_act(1, k_ref, cwk_ref))
    v_s[...] = conv_act(2, v_ref, cwv_ref)
    ba = ba_ref[...]
    lane = lax.broadcasted_iota(jnp.int32, ba.shape, 1)
    g_all = -jnp.exp(al_ref[...]) * _softplus(ba + dt_ref[...])
    ba_s[...] = jnp.where(lane < GD_HEADS, _sigmoid(ba), g_all)

    ri = lax.broadcasted_iota(jnp.int32, (c, c), 0)
    ci = lax.broadcasted_iota(jnp.int32, (c, c), 1)
    strict = ri > ci
    causal = ri >= ci

    def chunk(idx, carry):
        r0 = pl.multiple_of(idx * c, c)
        rows = pl.ds(r0, c)
        bg = ba_s[rows, :]
        gc_all = _cumsum_rows(bg)
        gc_t = gc_all.T
        qa, ka, va = q_s[rows, :], k_s[rows, :], v_s[rows, :]
        for hd in range(GD_HEADS):
            sl = slice(GD_HD * hd, GD_HD * (hd + 1))
            beta = bg[:, hd:hd + 1]
            gc = gc_all[:, GD_HEADS + hd:GD_HEADS + hd + 1]
            gr = gc_t[GD_HEADS + hd:GD_HEADS + hd + 1, :]
            dec = jnp.exp(jnp.where(causal, gc - gr, -1e30))
            qh, kh, vh = qa[:, sl], ka[:, sl], va[:, sl]
            kb = kh * beta
            kq = _mm_nt(jnp.concatenate([kb, qh], axis=0).astype(mdt), kh.astype(mdt))
            low = jnp.where(strict, kq[:c] * dec, 0.0)
            intra = kq[c:] * dec
            eg = jnp.exp(gc)
            sol = _fwd_subst(low, jnp.concatenate([vh * beta, kb * eg], axis=1))
            u, w = sol[:, :GD_HD], sol[:, GD_HD:]
            st = st_s[hd] if chained else si_ref[idx, hd]
            ws = _mm(jnp.concatenate([w, qh * eg], axis=0).astype(mdt), st.astype(mdt))
            v_new = u - ws[:c]
            o = ws[c:] + _mm(intra.astype(mdt), v_new.astype(mdt))
            g_last = gc[c - 1:c, :]
            st_new = st * jnp.exp(g_last) + _mm_tn((kh * jnp.exp(g_last - gc)).astype(mdt), v_new.astype(mdt))
            if chained:
                st_s[hd] = st_new
            else:
                so_ref[idx, hd] = st_new
            o_s[rows, sl] = o
        return carry

    lax.fori_loop(0, tm // c, chunk, 0)

    if chained:
        so_ref[0] = st_s[...]

    outs = []
    for hd in range(GD_HEADS):
        sl = slice(GD_HD * hd, GD_HD * (hd + 1))
        o = o_s[:, sl]
        o = o * lax.rsqrt(jnp.mean(o * o, axis=-1, keepdims=True) + NORM_EPS) * ng_ref[...]
        z = z_ref[:, sl]
        outs.append(o * (z * _sigmoid(z)))
    y_ref[...] = jnp.concatenate(outs, axis=1).astype(BF16)


def _gdn_param_specs():
    c2 = lambda shape, j=0: pl.BlockSpec(shape, (lambda *a: (0, j)))
    return [c2((GD_CONV, GD_WIDTH), 0), c2((GD_CONV, GD_WIDTH), 1), c2((GD_CONV, GD_WIDTH), 2),
            c2((1, LANES)), c2((1, LANES)), c2((1, GD_HD))]


def _gdn_scratch(tm):
    return ([pltpu.VMEM((tm, GD_WIDTH), F32) for _ in range(3)] + [pltpu.VMEM((tm, LANES), F32),
                                                                  pltpu.VMEM((tm, GD_WIDTH), F32)])


def _gdn_prompt(proj, gp, nb, seq, tm, c):
    nt = seq // tm
    qb = P_Q // GD_WIDTH
    blk = lambda j: pl.BlockSpec((tm, GD_WIDTH), lambda b, t: (b * nt + t, j))
    y, st = pl.pallas_call(
        functools.partial(_gdn_kernel, chained=True, tm=tm, c=c),
        grid=(nb, nt),
        in_specs=[blk(qb), blk(qb + 1), blk(qb + 2), blk(P_Z // GD_WIDTH),
                  pl.BlockSpec((tm, LANES), lambda b, t: (b * nt + t, P_BA // LANES))] + _gdn_param_specs(),
        out_specs=[pl.BlockSpec((tm, GD_WIDTH), lambda b, t: (b * nt + t, 0)),
                   pl.BlockSpec((1, GD_HEADS, GD_HD, GD_HD), lambda b, t: (b, 0, 0, 0))],
        out_shape=[jax.ShapeDtypeStruct((nb * seq, GD_WIDTH), BF16),
                   jax.ShapeDtypeStruct((nb, GD_HEADS, GD_HD, GD_HD), F32)],
        scratch_shapes=_gdn_scratch(tm) + [pltpu.VMEM((SUBLANES, GD_WIDTH), F32) for _ in range(3)]
        + [pltpu.VMEM((GD_HEADS, GD_HD, GD_HD), F32)],
        compiler_params=_cparams(("parallel", "arbitrary")),
        name="gdn_prompt",
    )(proj, proj, proj, proj, proj, gp[0], gp[0], gp[0], *gp[1:])
    return y, st


def _gdn_sample(proj, frames, st0, gp, row0, nseq, tm):
    ns = tm // SUBLANES
    qb = P_Q // GD_WIDTH
    r0 = row0 // tm
    blk = lambda j: pl.BlockSpec((tm, GD_WIDTH), lambda i: (r0 + i, j))
    frm = lambda j: pl.BlockSpec((tm, GD_WIDTH), lambda i: (i, j))
    st_spec = pl.BlockSpec((ns, GD_HEADS, GD_HD, GD_HD), lambda i: (i, 0, 0, 0))
    y, st = pl.pallas_call(
        functools.partial(_gdn_kernel, chained=False, tm=tm, c=SUBLANES),
        grid=(nseq // ns,),
        in_specs=[blk(qb), blk(qb + 1), blk(qb + 2), blk(P_Z // GD_WIDTH),
                  pl.BlockSpec((tm, LANES), lambda i: (r0 + i, P_BA // LANES)),
                  frm(0), frm(1), frm(2), st_spec] + _gdn_param_specs(),
        out_specs=[pl.BlockSpec((tm, GD_WIDTH), lambda i: (i, 0)), st_spec],
        out_shape=[jax.ShapeDtypeStruct((nseq * SUBLANES, GD_WIDTH), BF16),
                   jax.ShapeDtypeStruct((nseq, GD_HEADS, GD_HD, GD_HD), F32)],
        scratch_shapes=_gdn_scratch(tm),
        compiler_params=_cparams(("parallel",)),
        name="gdn_sample",
    )(proj, proj, proj, proj, proj, frames, frames, frames, st0, gp[0], gp[0], gp[0], *gp[1:])
    return y, st


def _gdn_params(conv_w, a_log, dt_bias, norm_g):
    pad = lambda t: jnp.concatenate([jnp.zeros((GD_HEADS,), F32), t, jnp.zeros((LANES - 2 * GD_HEADS,), F32)]
                                    ).reshape(1, LANES)
    return (conv_w, pad(a_log), pad(dt_bias), norm_g.reshape(1, GD_HD))


def _permute_w_in(w):
    o_rw, o_qkv = S5_WIDTH, S5_WIDTH + RW_COLS
    o_z = o_qkv + 3 * GD_WIDTH
    o_b = o_z + GD_WIDTH
    o_g = o_b + 2 * GD_HEADS
    parts = [w[:, o_g:], _rw_pad_cols(w[:, o_rw:o_qkv]), w[:, o_qkv:o_z], w[:, o_z:o_b], w[:, :S5_WIDTH],
             w[:, o_b:o_g], jnp.zeros((w.shape[0], P_COLS - P_BA - 2 * GD_HEADS), w.dtype)]
    return jnp.concatenate(parts, axis=1).astype(BF16)


def _frames(state):
    n, w1, ch = state.shape
    return jnp.pad(state, ((0, 0), (0, SUBLANES - w1), (0, 0))).reshape(n * SUBLANES, ch)


def kernel(x_prompt, x_sample, state_s5_re, state_s5_im, state_rwkv_shift, state_rwkv_wkv, state_gdn_conv, state_gdn, state_ffn_conv, norm1_g, norm2_g, final_norm_g, w_in, s5_lambda_re, s5_lambda_im, s5_b_re, s5_b_im, s5_c_re, s5_c_im, s5_d, s5_log_step, s5_w_glu, rwkv_mu, rwkv_w0, rwkv_w2, rwkv_a0, rwkv_a2, rwkv_g2, rwkv_k_k, rwkv_k_a, rwkv_r_k, rwkv_ln_w, rwkv_ln_b, gdn_conv_w, gdn_a_log, gdn_dt_bias, gdn_norm_g, w_br_s5, w_br_rwkv, w_br_gdn, w_out, ffn_w_up, ffn_conv_w, ffn_conv_b, ffn_w_down):
    nb, seq, d = x_prompt.shape
    ns, sl, _ = x_sample.shape
    assert sl == SUBLANES and d == D_MODEL
    mp, ms = nb * seq, ns * sl
    x = jnp.concatenate([x_prompt.reshape(mp, d), x_sample.reshape(ms, d)], axis=0)

    new_p = [[] for _ in range(7)]
    new_s = [[] for _ in range(7)]
    for l in range(DEPTH):
        proj = _norm_matmul(x, norm1_g[l], _permute_w_in(w_in[l]), tm=1024, tn=512)

        sp = _s5_params(s5_lambda_re[l], s5_lambda_im[l], s5_b_re[l], s5_b_im[l], s5_c_re[l], s5_c_im[l],
                        s5_d[l], s5_log_step[l], s5_w_glu[l])
        ys5_p, p_re, p_im = _s5_prompt(proj, sp, nb, seq, tm=256)
        ys5_s, s_re, s_im = _s5_sample(proj, state_s5_re[l], state_s5_im[l], sp, mp, ns, tm=128)

        rp = _rwkv_params(rwkv_mu[l], rwkv_w0[l], rwkv_w2[l], rwkv_a0[l], rwkv_a2[l], rwkv_g2[l], rwkv_k_k[l],
                          rwkv_k_a[l], rwkv_r_k[l].reshape(RW_WIDTH), rwkv_ln_w[l], rwkv_ln_b[l])
        yrw_p, p_wkv = _rwkv_prompt(proj, rp, nb, seq, tm=256, c=64)
        rw_frames = _frames(_rw_pad_cols(state_rwkv_shift[l])[:, None, :])
        yrw_s, s_wkv = _rwkv_sample(proj, rw_frames, state_rwkv_wkv[l], rp, mp, ns, tm=64)
        h_rw = proj[:, P_RW:P_RW + RW_PCOLS]
        p_shift = _rw_unpad_cols(h_rw[:mp].reshape(nb, seq, RW_PCOLS)[:, -1])
        s_shift = _rw_unpad_cols(h_rw[mp:].reshape(ns, sl, RW_PCOLS)[:, -1])

        gp = _gdn_params(gdn_conv_w[l], gdn_a_log[l], gdn_dt_bias[l], gdn_norm_g[l])
        ygd_p, p_gdn = _gdn_prompt(proj, gp, nb, seq, tm=256, c=64)
        ygd_s, s_gdn = _gdn_sample(proj, _frames(state_gdn_conv[l]), state_gdn[l], gp, mp, ns, tm=64)
        qkv_raw = proj[:, P_Q:P_Q + 3 * GD_WIDTH]
        p_gconv = qkv_raw[:mp].reshape(nb, seq, 3 * GD_WIDTH)[:, seq - (GD_CONV - 1):]
        s_gconv = qkv_raw[mp:].reshape(ns, sl, 3 * GD_WIDTH)[:, sl - (GD_CONV - 1):]

        y_s5 = jnp.concatenate([ys5_p, ys5_s], axis=0)
        y_rw = jnp.concatenate([yrw_p, yrw_s], axis=0)
        y_gd = jnp.concatenate([ygd_p, ygd_s], axis=0)
        merged = _merge(y_s5, y_rw, y_gd, w_br_s5[l].astype(BF16), w_br_rwkv[l].astype(BF16),
                        w_br_gdn[l].astype(BF16), proj, tm=512, tn=512)
        x = _matmul_res(merged, w_out[l].astype(BF16), x, tm=1024, tn=512)

        up = _norm_matmul(x, norm2_g[l], ffn_w_up[l].astype(BF16), tm=1024, tn=512)
        cb = ffn_conv_b[l].reshape(1, 2 * D_FF)
        h_p = _ffn_act_prompt(up, ffn_conv_w[l], cb, nb, seq, tm=512, tc=512)
        h_s = _ffn_act_sample(up, _frames(state_ffn_conv[l]), ffn_conv_w[l], cb, mp, ms, tm=512, tc=512)
        p_ffn = up[:mp].reshape(nb, seq, 2 * D_FF)[:, seq - (FFN_CONV - 1):]
        s_ffn = up[mp:].reshape(ns, sl, 2 * D_FF)[:, sl - (FFN_CONV - 1):]
        x = _matmul_res(jnp.concatenate([h_p, h_s], axis=0), ffn_w_down[l].astype(BF16), x, tm=512, tn=512)

        for lst, vals in ((new_p, (p_re, p_im, p_shift, p_wkv, p_gconv, p_gdn, p_ffn)),
                          (new_s, (s_re, s_im, s_shift, s_wkv, s_gconv, s_gdn, s_ffn))):
            for acc, val in zip(lst, vals):
                acc.append(val)

    y = _rmsnorm(x, final_norm_g, tm=512)
    y_prompt = y[:mp].reshape(nb, seq, d)
    y_sample = y[mp:].reshape(ns, sl, d)
    stack = lambda lst: tuple(jnp.stack(v) for v in lst)
    return (y_prompt, y_sample) + stack(new_p) + stack(new_s)
```

```python
import functools
import math

import jax
import jax.numpy as jnp
from jax import lax
from jax.experimental import pallas as pl
from jax.experimental.pallas import tpu as pltpu

F32 = jnp.float32
BF16 = jnp.bfloat16

SUBLANES = 8
LANES = 128
VMEM_LIMIT = 52 * 1024 * 1024
CHUNK = 64
SEQ_PER_CHUNK = CHUNK // SUBLANES

D_MODEL = 2048
DEPTH = 2
S5_WIDTH = 512
S5_GROUP = 16
S5_GROUPS = 32
S5_STATE = 64
S5_NSTATE = S5_GROUPS * S5_STATE
RW_WIDTH = 512
RW_HEADS = 8
RW_HD = 64
RW_LORA = 96
RW_GATE = 256
RW_COLS = 3 * RW_WIDTH + 2 * RW_LORA + RW_GATE
RW_PCOLS = 2048
RW_GN_EPS = 64e-5
GD_WIDTH = 1024
GD_HEADS = 8
GD_HD = 128
GD_CONV = 4
D_FF = 5632
FFN_CONV = 3
NORM_EPS = 1e-6
P_GATES, P_RW, P_Q, P_Z, P_S5, P_BA, P_COLS = 0, 6144, 8192, 11264, 12288, 12800, 13312


def _cparams(sem):
    return pltpu.CompilerParams(dimension_semantics=sem, vmem_limit_bytes=VMEM_LIMIT)


def _mm(a, b):
    return jnp.dot(a, b, preferred_element_type=F32)


def _mm_nt(a, b):
    return lax.dot_general(a, b, (((1,), (1,)), ((), ())), preferred_element_type=F32)


def _mm_tn(a, b):
    return lax.dot_general(a, b, (((0,), (0,)), ((), ())), preferred_element_type=F32)


def _sigmoid(x):
    return 1.0 / (1.0 + jnp.exp(-x))


def _softplus(x):
    return jnp.maximum(x, 0.0) + jnp.log1p(jnp.exp(-jnp.abs(x)))


def _gelu_tanh(x):
    return 0.5 * x * (1.0 + jnp.tanh(math.sqrt(2.0 / math.pi) * (x + 0.044715 * (x * x * x))))


def _row_iota(shape):
    return lax.broadcasted_iota(jnp.int32, shape, 0)


def _shift_chain(x, d, prev8):
    rolled = pltpu.roll(x, d, 0)
    row = _row_iota(prev8.shape)
    head = jnp.where(row < d, pltpu.roll(prev8, d, 0), rolled[:SUBLANES])
    return jnp.concatenate([head, rolled[SUBLANES:]], axis=0)


def _shift_seq8(x, d, state):
    n, ch = x.shape
    ns, w1, _ = state.shape
    out = pltpu.roll(x, d, 0).reshape(ns, SUBLANES, ch)
    row = lax.broadcasted_iota(jnp.int32, out.shape, 1)
    for r in range(d):
        out = jnp.where(row == r, state[:, w1 - d + r:w1 - d + r + 1, :], out)
    return out.reshape(n, ch)


def _group_last(x):
    c, n = x.shape
    x3 = x.reshape(c // SUBLANES, SUBLANES, n)
    return jnp.broadcast_to(x3[:, SUBLANES - 1:SUBLANES, :], x3.shape).reshape(c, n)


def _cumsum_rows(x, groupwise=False):
    c = x.shape[0]
    row = _row_iota(x.shape) % SUBLANES
    for d in (1, 2, 4):
        x = x + jnp.where(row >= d, pltpu.roll(x, d, 0), 0.0)
    if c > SUBLANES and not groupwise:
        blocks = [x[SUBLANES * i:SUBLANES * (i + 1)] for i in range(c // SUBLANES)]
        for i in range(1, len(blocks)):
            blocks[i] = blocks[i] + blocks[i - 1][SUBLANES - 1:SUBLANES, :]
        x = jnp.concatenate(blocks, axis=0)
    return x


def _split2(x):
    hi = x.astype(BF16)
    return hi, (x - hi.astype(F32)).astype(BF16)


def _mm_hilo_each(a_list, b_list):
    sa = [_split2(a) for a in a_list]
    sb = [_split2(b) for b in b_list]
    hh = [_mm(a[0], b[0]) for a, b in zip(sa, sb)]
    hl = [_mm(a[0], b[1]) for a, b in zip(sa, sb)]
    lh = [_mm(a[1], b[0]) for a, b in zip(sa, sb)]
    return [x + (y + z) for x, y, z in zip(hh, hl, lh)]


def _solve_unit_lower_each(lows, rhss, diag_only):
    c = lows[0].shape[0]
    nb = c // SUBLANES
    nh = len(lows)
    blk = lambda a, i: a[SUBLANES * i:SUBLANES * (i + 1)]
    rb = [[blk(r, i) for i in range(nb)] for r in rhss]
    with_off = nb > 1 and not diag_only
    if with_off:
        same = (lax.broadcasted_iota(jnp.int32, (c, c), 0) // SUBLANES
                == lax.broadcasted_iota(jnp.int32, (c, c), 1) // SUBLANES)
        ob = [[blk(jnp.where(same, 0.0, low), i) for i in range(nb)] for low in lows]
    for j in range(SUBLANES - 1):
        for h in range(nh):
            for i in range(nb):
                col = lows[h][SUBLANES * i:SUBLANES * (i + 1), SUBLANES * i + j:SUBLANES * i + j + 1]
                rb[h][i] = rb[h][i] - col * rb[h][i][j:j + 1, :]
                if with_off and i > 0:
                    ob[h][i] = ob[h][i] - col * ob[h][i][j:j + 1, :]
    r1 = [jnp.concatenate(b, axis=0) if nb > 1 else b[0] for b in rb]
    if not with_off:
        return r1
    n1 = [jnp.concatenate(b, axis=0) for b in ob]
    n2 = _mm_hilo_each(n1, n1)
    t = _mm_hilo_each(n1 + n2, r1 + n2)
    y = [r - nr for r, nr in zip(r1, t[:nh])]
    n4 = t[nh:]
    y = [a + b for a, b in zip(y, _mm_hilo_each(n2, y))]
    return [a + b for a, b in zip(y, _mm_hilo_each(n4, y))]


def _chunk_masks(per_seq):
    ri = lax.broadcasted_iota(jnp.int32, (CHUNK, CHUNK), 0)
    ci = lax.broadcasted_iota(jnp.int32, (CHUNK, CHUNK), 1)
    strict, causal = ri > ci, ri >= ci
    if per_seq:
        same = (ri // SUBLANES) == (ci // SUBLANES)
        strict, causal = jnp.logical_and(strict, same), jnp.logical_and(causal, same)
    return strict, causal


def _seq_block_mask(rows, width):
    r = lax.broadcasted_iota(jnp.int32, (rows, SEQ_PER_CHUNK * width), 0)
    l = lax.broadcasted_iota(jnp.int32, (rows, SEQ_PER_CHUNK * width), 1)
    return ((r % CHUNK) // SUBLANES) == (l // width)


def _pick_own_seq(a, width):
    return jnp.concatenate([a[SUBLANES * s:SUBLANES * (s + 1), width * s:width * (s + 1)]
                            for s in range(SEQ_PER_CHUNK)], axis=0)


def _rmsnorm_rows(x_ref, g_ref, o_ref, rows, chunk=128):
    def body(i, carry):
        r0 = pl.multiple_of(i * chunk, chunk)
        x = x_ref[pl.ds(r0, chunk), :]
        ms = jnp.mean(x * x, axis=-1, keepdims=True)
        o_ref[pl.ds(r0, chunk), :] = (x * lax.rsqrt(ms + NORM_EPS) * g_ref[...]).astype(o_ref.dtype)
        return carry
    lax.fori_loop(0, rows // chunk, body, 0)


def _norm_matmul_kernel(x_ref, g_ref, w_ref, o_ref, xn_ref, *, rows):
    @pl.when(pl.program_id(1) == 0)
    def _():
        _rmsnorm_rows(x_ref, g_ref, xn_ref, rows)

    o_ref[...] = _mm(xn_ref[...], w_ref[...])


def _norm_matmul(x, g, w, tm, tn):
    m, k = x.shape
    n = w.shape[1]
    return pl.pallas_call(
        functools.partial(_norm_matmul_kernel, rows=tm),
        grid=(m // tm, n // tn),
        in_specs=[pl.BlockSpec((tm, k), lambda i, j: (i, 0)),
                  pl.BlockSpec((1, k), lambda i, j: (0, 0)),
                  pl.BlockSpec((k, tn), lambda i, j: (0, j))],
        out_specs=pl.BlockSpec((tm, tn), lambda i, j: (i, j)),
        out_shape=jax.ShapeDtypeStruct((m, n), F32),
        scratch_shapes=[pltpu.VMEM((tm, k), BF16)],
        compiler_params=_cparams(("parallel", "arbitrary")),
        name="norm_matmul",
    )(x, g.reshape(1, k), w)


def _matmul_res_kernel(a_ref, w_ref, r_ref, o_ref):
    o_ref[...] = r_ref[...] + _mm(a_ref[...], w_ref[...])


def _matmul_res(a, w, res, tm, tn):
    m, k = a.shape
    n = w.shape[1]
    return pl.pallas_call(
        _matmul_res_kernel,
        grid=(m // tm, n // tn),
        in_specs=[pl.BlockSpec((tm, k), lambda i, j: (i, 0)),
                  pl.BlockSpec((k, tn), lambda i, j: (0, j)),
                  pl.BlockSpec((tm, tn), lambda i, j: (i, j))],
        out_specs=pl.BlockSpec((tm, tn), lambda i, j: (i, j)),
        out_shape=jax.ShapeDtypeStruct((m, n), F32),
        compiler_params=_cparams(("parallel", "arbitrary")),
        name="matmul_res",
    )(a, w, res)


def _out_norm_kernel(a_ref, w_ref, r_ref, g_ref, x_ref, xn_ref, *, rows):
    x_ref[...] = r_ref[...] + _mm(a_ref[...], w_ref[...])
    _rmsnorm_rows(x_ref, g_ref, xn_ref, rows)


def _out_norm(a, w, res, g, tm):
    m, k = a.shape
    n = w.shape[1]
    return pl.pallas_call(
        functools.partial(_out_norm_kernel, rows=tm),
        grid=(m // tm,),
        in_specs=[pl.BlockSpec((tm, k), lambda i: (i, 0)),
                  pl.BlockSpec((k, n), lambda i: (0, 0)),
                  pl.BlockSpec((tm, n), lambda i: (i, 0)),
                  pl.BlockSpec((1, n), lambda i: (0, 0))],
        out_specs=[pl.BlockSpec((tm, n), lambda i: (i, 0)), pl.BlockSpec((tm, n), lambda i: (i, 0))],
        out_shape=[jax.ShapeDtypeStruct((m, n), F32), jax.ShapeDtypeStruct((m, n), BF16)],
        compiler_params=_cparams(("parallel",)),
        name="out_norm",
    )(a, w, res, g.reshape(1, n))


def _merge_kernel(y1p, y2p, y3p, y1s, y2s, y3s, w1_ref, w2_ref, w3_ref, g1_ref, g2_ref, g3_ref, o_ref, *,
                  prompt_tiles):
    is_p = pl.program_id(0) < prompt_tiles
    pick = lambda p, s: jnp.where(is_p, p[...], s[...])
    acc = _sigmoid(g1_ref[...]) * _mm(pick(y1p, y1s), w1_ref[...])
    acc = acc + _sigmoid(g2_ref[...]) * _mm(pick(y2p, y2s), w2_ref[...])
    acc = acc + _sigmoid(g3_ref[...]) * _mm(pick(y3p, y3s), w3_ref[...])
    o_ref[...] = acc.astype(BF16)


def _merge(ys_p, ys_s, ws, proj, tm, tn):
    mp, ms = ys_p[0].shape[0], ys_s[0].shape[0]
    n = ws[0].shape[1]
    nj = n // tn
    pt = mp // tm
    gate_spec = lambda b: pl.BlockSpec((tm, tn), lambda i, j: (i, P_GATES // tn + b * nj + j))
    p_spec = lambda y: pl.BlockSpec((tm, y.shape[1]), lambda i, j: (jnp.minimum(i, pt - 1), 0))
    s_spec = lambda y: pl.BlockSpec((tm, y.shape[1]), lambda i, j: (jnp.maximum(i - pt, 0), 0))
    w_spec = lambda w: pl.BlockSpec((w.shape[0], tn), lambda i, j: (0, j))
    return pl.pallas_call(
        functools.partial(_merge_kernel, prompt_tiles=pt),
        grid=((mp + ms) // tm, nj),
        in_specs=[p_spec(y) for y in ys_p] + [s_spec(y) for y in ys_s] + [w_spec(w) for w in ws]
        + [gate_spec(0), gate_spec(1), gate_spec(2)],
        out_specs=pl.BlockSpec((tm, tn), lambda i, j: (i, j)),
        out_shape=jax.ShapeDtypeStruct((mp + ms, n), BF16),
        compiler_params=_cparams(("parallel", "arbitrary")),
        name="merge",
    )(*ys_p, *ys_s, *ws, proj, proj, proj)


def _rmsnorm_kernel(x_ref, g_ref, o_ref):
    x = x_ref[...]
    ms = jnp.mean(x * x, axis=-1, keepdims=True)
    o_ref[...] = x * lax.rsqrt(ms + NORM_EPS) * g_ref[...]


def _rmsnorm(x, g, tm):
    m, k = x.shape
    return pl.pallas_call(
        _rmsnorm_kernel,
        grid=(m // tm,),
        in_specs=[pl.BlockSpec((tm, k), lambda i: (i, 0)), pl.BlockSpec((1, k), lambda i: (0, 0))],
        out_specs=pl.BlockSpec((tm, k), lambda i: (i, 0)),
        out_shape=jax.ShapeDtypeStruct((m, k), F32),
        compiler_params=_cparams(("parallel",)),
        name="final_norm",
    )(x, g.reshape(1, k))


def _ffn_up_kernel(xn_ref, wg_ref, wv_ref, cwg_ref, cwv_ref, bg_ref, bv_ref, sg_ref, sv_ref,
                   h_ref, pg_ref, pv_ref, og_ref, ov_ref, wgb, wvb, cg, cv, *, tm, prompt_tiles, tiles_per_seq):
    i = pl.program_id(1)

    @pl.when(i == 0)
    def _():
        wgb[...] = wg_ref[...].astype(BF16)
        wvb[...] = wv_ref[...].astype(BF16)

    xn = xn_ref[...]
    ug = _mm(xn, wgb[...])
    uv = _mm(xn, wvb[...])

    def conv(u, s1, s2, cw_ref, b_ref):
        return cw_ref[2:3, :] * u + cw_ref[1:2, :] * s1 + cw_ref[0:1, :] * s2 + b_ref[...]

    @pl.when(i < prompt_tiles)
    def _():
        first = (i % tiles_per_seq) == 0
        pg = jnp.where(first, 0.0, cg[...])
        pv = jnp.where(first, 0.0, cv[...])
        gate = conv(ug, _shift_chain(ug, 1, pg), _shift_chain(ug, 2, pg), cwg_ref, bg_ref)
        val = conv(uv, _shift_chain(uv, 1, pv), _shift_chain(uv, 2, pv), cwv_ref, bv_ref)
        h_ref[...] = (gate * _sigmoid(gate) * val).astype(BF16)
        cg[...] = ug[tm - SUBLANES:, :]
        cv[...] = uv[tm - SUBLANES:, :]
        pg_ref[0] = ug[tm - (FFN_CONV - 1):, :]
        pv_ref[0] = uv[tm - (FFN_CONV - 1):, :]

    @pl.when(i >= prompt_tiles)
    def _():
        sg, sv = sg_ref[...], sv_ref[...]
        gate = conv(ug, _shift_seq8(ug, 1, sg), _shift_seq8(ug, 2, sg), cwg_ref, bg_ref)
        val = conv(uv, _shift_seq8(uv, 1, sv), _shift_seq8(uv, 2, sv), cwv_ref, bv_ref)
        h_ref[...] = (gate * _sigmoid(gate) * val).astype(BF16)
        ns = tm // SUBLANES
        og_ref[...] = ug.reshape(ns, SUBLANES, ug.shape[1])[:, SUBLANES - (FFN_CONV - 1):, :]
        ov_ref[...] = uv.reshape(ns, SUBLANES, uv.shape[1])[:, SUBLANES - (FFN_CONV - 1):, :]


def _ffn_up(xn, w_up, conv_w, conv_b, state, nb, seq, ns, tm, tn):
    m, k = xn.shape
    mp = nb * seq
    assert ns * SUBLANES == tm and (m - mp) == tm and seq % tm == 0
    nj, pt, tps = D_FF // tn, mp // tm, seq // tm
    w1 = FFN_CONV - 1
    wsp = lambda off: pl.BlockSpec((k, tn), lambda j, i: (0, off + j))
    par = lambda rows, off: pl.BlockSpec((rows, tn), lambda j, i: (0, off + j))
    stsp = lambda off: pl.BlockSpec((ns, w1, tn), lambda j, i: (0, 0, off + j))
    psp = pl.BlockSpec((1, w1, tn), lambda j, i: (jnp.minimum(i, pt - 1) // tps, 0, j))
    osp = pl.BlockSpec((ns, w1, tn), lambda j, i: (0, 0, j))
    h, pg, pv, og, ov = pl.pallas_call(
        functools.partial(_ffn_up_kernel, tm=tm, prompt_tiles=pt, tiles_per_seq=tps),
        grid=(nj, m // tm),
        in_specs=[pl.BlockSpec((tm, k), lambda j, i: (i, 0)), wsp(0), wsp(nj),
                  par(FFN_CONV, 0), par(FFN_CONV, nj), par(1, 0), par(1, nj), stsp(0), stsp(nj)],
        out_specs=[pl.BlockSpec((tm, tn), lambda j, i: (i, j)), psp, psp, osp, osp],
        out_shape=[jax.ShapeDtypeStruct((m, D_FF), BF16),
                   jax.ShapeDtypeStruct((nb, w1, D_FF), F32), jax.ShapeDtypeStruct((nb, w1, D_FF), F32),
                   jax.ShapeDtypeStruct((ns, w1, D_FF), F32), jax.ShapeDtypeStruct((ns, w1, D_FF), F32)],
        scratch_shapes=[pltpu.VMEM((k, tn), BF16), pltpu.VMEM((k, tn), BF16),
                        pltpu.VMEM((SUBLANES, tn), F32), pltpu.VMEM((SUBLANES, tn), F32)],
        compiler_params=_cparams(("arbitrary", "arbitrary")),
        name="ffn_up",
    )(xn, w_up, w_up, conv_w, conv_w, conv_b, conv_b, state, state)
    return h, jnp.concatenate([pg, pv], axis=-1), jnp.concatenate([og, ov], axis=-1)


S5_Q = 4
S5_QW = S5_NSTATE // S5_Q


def _s5_kernel(*refs, chained, tm):
    if chained:
        (u_ref, bre_ref, bim_ref, cre_ref, cim_ref, mre_ref, mim_ref, pre_ref, pim_ref, d_ref, wglu_ref,
         y_ref, xlre_ref, xlim_ref, sre, sim, car_re, car_im) = refs
    else:
        (u_ref, x0re_ref, x0im_ref, bre_ref, bim_ref, cre_ref, cim_ref, mre_ref, mim_ref, pre_ref, pim_ref,
         d_ref, wglu_ref, y_ref, xlre_ref, xlim_ref, sre, sim) = refs

    u = u_ref[...]
    ub = u.astype(BF16)
    for q in range(S5_Q):
        uq = ub[:, LANES * q:LANES * (q + 1)]
        sre[:, S5_QW * q:S5_QW * (q + 1)] = _mm(uq, bre_ref[q])
        sim[:, S5_QW * q:S5_QW * (q + 1)] = _mm(uq, bim_ref[q])

    if chained:
        @pl.when(pl.program_id(1) == 0)
        def _():
            car_re[...] = jnp.zeros_like(car_re)
            car_im[...] = jnp.zeros_like(car_im)

    def blk(i, carry):
        r0 = pl.multiple_of(i * SUBLANES, SUBLANES)
        for q in range(S5_Q):
            sl = slice(S5_QW * q, S5_QW * (q + 1))
            xr = sre[pl.ds(r0, SUBLANES), sl]
            xi = sim[pl.ds(r0, SUBLANES), sl]
            for li, d in enumerate((1, 2, 4)):
                mr, mi = mre_ref[li, :, sl], mim_ref[li, :, sl]
                sr, si = pltpu.roll(xr, d, 0), pltpu.roll(xi, d, 0)
                xr, xi = xr + (mr * sr - mi * si), xi + (mr * si + mi * sr)
            if chained:
                cr, ci = car_re[:, sl], car_im[:, sl]
            else:
                cr, ci = x0re_ref[pl.ds(i, 1), sl], x0im_ref[pl.ds(i, 1), sl]
            pr, pi_ = pre_ref[:, sl], pim_ref[:, sl]
            xr, xi = xr + (pr * cr - pi_ * ci), xi + (pr * ci + pi_ * cr)
            sre[pl.ds(r0, SUBLANES), sl] = xr
            sim[pl.ds(r0, SUBLANES), sl] = xi
            if chained:
                car_re[:, sl] = xr[SUBLANES - 1:SUBLANES, :]
                car_im[:, sl] = xi[SUBLANES - 1:SUBLANES, :]
            else:
                xlre_ref[pl.ds(i, 1), sl] = xr[SUBLANES - 1:SUBLANES, :]
                xlim_ref[pl.ds(i, 1), sl] = xi[SUBLANES - 1:SUBLANES, :]
        return carry

    lax.fori_loop(0, tm // SUBLANES, blk, 0)

    if chained:
        xlre_ref[0] = car_re[...]
        xlim_ref[0] = car_im[...]

    ys = []
    for q in range(S5_Q):
        sl = slice(S5_QW * q, S5_QW * (q + 1))
        ys.append(_mm(sre[:, sl].astype(BF16), cre_ref[q]) - _mm(sim[:, sl].astype(BF16), cim_ref[q]))
    y = jnp.concatenate(ys, axis=1) + d_ref[...] * u
    y = _gelu_tanh(y)
    y = y * _sigmoid(_mm(y.astype(BF16), wglu_ref[...]))
    y_ref[...] = y.astype(BF16)


def _const_spec(shape):
    zeros = (0,) * len(shape)
    return pl.BlockSpec(shape, lambda *a: zeros)


def _s5_param_specs():
    return [_const_spec((S5_Q, LANES, S5_QW)), _const_spec((S5_Q, LANES, S5_QW)),
            _const_spec((S5_Q, S5_QW, LANES)), _const_spec((S5_Q, S5_QW, LANES)),
            _const_spec((3, SUBLANES, S5_NSTATE)), _const_spec((3, SUBLANES, S5_NSTATE)),
            _const_spec((SUBLANES, S5_NSTATE)), _const_spec((SUBLANES, S5_NSTATE)),
            _const_spec((1, S5_WIDTH)), _const_spec((S5_WIDTH, S5_WIDTH))]


def _s5_prompt(proj, sp, nb, seq, tm):
    nt = seq // tm
    cb = P_S5 // S5_WIDTH
    y, xre, xim = pl.pallas_call(
        functools.partial(_s5_kernel, chained=True, tm=tm),
        grid=(nb, nt),
        in_specs=[pl.BlockSpec((tm, S5_WIDTH), lambda b, t: (b * nt + t, cb))] + _s5_param_specs(),
        out_specs=[pl.BlockSpec((tm, S5_WIDTH), lambda b, t: (b * nt + t, 0)),
                   pl.BlockSpec((1, 1, S5_NSTATE), lambda b, t: (b, 0, 0)),
                   pl.BlockSpec((1, 1, S5_NSTATE), lambda b, t: (b, 0, 0))],
        out_shape=[jax.ShapeDtypeStruct((nb * seq, S5_WIDTH), BF16),
                   jax.ShapeDtypeStruct((nb, 1, S5_NSTATE), F32),
                   jax.ShapeDtypeStruct((nb, 1, S5_NSTATE), F32)],
        scratch_shapes=[pltpu.VMEM((tm, S5_NSTATE), F32), pltpu.VMEM((tm, S5_NSTATE), F32),
                        pltpu.VMEM((1, S5_NSTATE), F32), pltpu.VMEM((1, S5_NSTATE), F32)],
        compiler_params=_cparams(("parallel", "arbitrary")),
        name="s5_prompt",
    )(proj, *sp)
    return y, xre.reshape(nb, S5_GROUPS, S5_STATE), xim.reshape(nb, S5_GROUPS, S5_STATE)


def _s5_sample(proj, x0re, x0im, sp, row0, nseq, tm):
    ns = tm // SUBLANES
    cb = P_S5 // S5_WIDTH
    r0 = row0 // tm
    st = pl.BlockSpec((ns, S5_NSTATE), lambda i: (i, 0))
    y, xre, xim = pl.pallas_call(
        functools.partial(_s5_kernel, chained=False, tm=tm),
        grid=(nseq // ns,),
        in_specs=[pl.BlockSpec((tm, S5_WIDTH), lambda i: (r0 + i, cb)), st, st] + _s5_param_specs(),
        out_specs=[pl.BlockSpec((tm, S5_WIDTH), lambda i: (i, 0)), st, st],
        out_shape=[jax.ShapeDtypeStruct((nseq * SUBLANES, S5_WIDTH), BF16),
                   jax.ShapeDtypeStruct((nseq, S5_NSTATE), F32),
                   jax.ShapeDtypeStruct((nseq, S5_NSTATE), F32)],
        scratch_shapes=[pltpu.VMEM((tm, S5_NSTATE), F32), pltpu.VMEM((tm, S5_NSTATE), F32)],
        compiler_params=_cparams(("parallel",)),
        name="s5_sample",
    )(proj, x0re.reshape(nseq, S5_NSTATE), x0im.reshape(nseq, S5_NSTATE), *sp)
    return y, xre.reshape(nseq, S5_GROUPS, S5_STATE), xim.reshape(nseq, S5_GROUPS, S5_STATE)


def _s5_params(lam_re, lam_im, b_re, b_im, c_re, c_im, d, log_step, w_glu):
    lam = lax.complex(lam_re, lam_im)
    step = jnp.exp(log_step)[:, None]
    lam_bar = jnp.exp(lam * step)
    b_bar = ((lam_bar - 1.0) / lam)[..., None] * lax.complex(b_re, b_im)

    def bblk(bm):
        bm = bm.reshape(S5_Q, 8, S5_STATE, S5_GROUP)
        eye = jnp.eye(8, dtype=F32)
        out = jnp.einsum('qgpc,gh->qgchp', bm, eye)
        return out.reshape(S5_Q, LANES, S5_QW).astype(BF16)

    def cblk(cm):
        cm = cm.reshape(S5_Q, 8, S5_GROUP, S5_STATE)
        eye = jnp.eye(8, dtype=F32)
        out = jnp.einsum('qgcp,gh->qgphc', cm, eye)
        return out.reshape(S5_Q, S5_QW, LANES).astype(BF16)

    lam_flat = lam_bar.reshape(1, S5_NSTATE)
    pows = [lam_flat]
    for _ in range(SUBLANES - 1):
        pows.append(pows[-1] * lam_flat)
    row = jnp.arange(SUBLANES)[:, None]
    m = jnp.stack([jnp.where(row >= dd, pows[dd - 1], 0.0) for dd in (1, 2, 4)])
    p = jnp.concatenate(pows, axis=0)
    return (bblk(b_bar.real), bblk(b_bar.imag), cblk(c_re), cblk(c_im),
            m.real, m.imag, p.real, p.imag, d.reshape(1, S5_WIDTH), w_glu.astype(BF16))


def _seg64_sum(x):
    outs = []
    for t in range(x.shape[1] // LANES):
        xt = x[:, LANES * t:LANES * (t + 1)]
        lo = lax.broadcasted_iota(jnp.int32, xt.shape, 1) < RW_HD
        s_lo = jnp.sum(jnp.where(lo, xt, 0.0), axis=-1, keepdims=True)
        s_hi = jnp.sum(jnp.where(lo, 0.0, xt), axis=-1, keepdims=True)
        outs.append(jnp.where(lo, s_lo, s_hi))
    return jnp.concatenate(outs, axis=1)


def _rwkv_kernel(*refs, per_seq, tm):
    if per_seq:
        (h_ref, sh_ref, si_ref, mu_ref, w0_ref, w2_ref, a0_ref, a2_ref, g2_ref, kk_ref, ka_ref, rk_ref,
         lnw_ref, lnb_ref, y_ref, so_ref, r_s, k_s, v_s, a_s, b_s, lw_s, y_s) = refs
    else:
        (h_ref, mu_ref, w0_ref, w2_ref, a0_ref, a2_ref, g2_ref, kk_ref, ka_ref, rk_ref, lnw_ref, lnb_ref,
         y_ref, so_ref, r_s, k_s, v_s, a_s, b_s, lw_s, y_s, prev_s, st_s) = refs
    c, w, hd_w = CHUNK, RW_WIDTH, RW_HD

    h = h_ref[...]
    if per_seq:
        prev = _shift_seq8(h, 1, sh_ref[...])
    else:
        @pl.when(pl.program_id(1) == 0)
        def _():
            prev_s[...] = jnp.zeros_like(prev_s)
            st_s[...] = jnp.zeros_like(st_s)
        prev = _shift_chain(h, 1, prev_s[...])
        prev_s[...] = h[tm - SUBLANES:, :]
    hs = h + (prev - h) * mu_ref[...]
    r = hs[:, 0:w]
    k = hs[:, w:2 * w]
    v = hs[:, 2 * w:3 * w]
    wd = hs[:, 3 * w:3 * w + LANES]
    ad = hs[:, 3 * w + LANES:3 * w + 2 * LANES]
    gd = hs[:, 3 * w + 2 * LANES:]
    w_log = -_softplus(-(w0_ref[...] + _mm(jnp.tanh(wd).astype(BF16), w2_ref[...]))) - 0.5
    lw_s[...] = -jnp.exp(w_log)
    a_ic = _sigmoid(a0_ref[...] + _mm(ad.astype(BF16), a2_ref[...]))
    gate = _mm(_sigmoid(gd).astype(BF16), g2_ref[...])
    kx = k * kk_ref[...]
    kkn = kx * lax.rsqrt(_seg64_sum(kx * kx) + 1e-6)
    k2 = k * (1.0 + (a_ic - 1.0) * ka_ref[...])
    r_s[...] = r
    k_s[...] = k2
    v_s[...] = v
    a_s[...] = -kkn
    b_s[...] = kkn * a_ic

    strict, causal = _chunk_masks(per_seq)
    if per_seq:
        big_mask = _seq_block_mask(2 * c, hd_w)

    def chunk(idx, carry):
        r0 = pl.multiple_of(idx * c, c)
        rows = pl.ds(r0, c)
        lw = lw_s[rows, :]
        cum = _cumsum_rows(lw, groupwise=per_seq)
        cl = _group_last(cum) if per_seq else cum[c - 1:c, :]
        e_pos, e_neg, e_end = jnp.exp(cum), jnp.exp(-cum), jnp.exp(cl - cum)
        rr, kc, vc, ac, bc = r_s[rows, :], k_s[rows, :], v_s[rows, :], a_s[rows, :], b_s[rows, :]
        rt, kt, bt = rr * e_pos, kc * e_neg, bc * e_neg
        at = ac * jnp.exp(cum - lw)
        kh, bh = kc * e_end, bc * e_end
        wc = jnp.exp(cl)
        heads = range(RW_HEADS)
        sls = [slice(hd_w * hd, hd_w * (hd + 1)) for hd in heads]
        ath = [at[:, sl] for sl in sls]
        vh = [vc[:, sl] for sl in sls]
        rth = [rt[:, sl] for sl in sls]
        seqs = pl.ds(idx * SEQ_PER_CHUNK, SEQ_PER_CHUNK)
        if per_seq:
            st = [si_ref[seqs, hd].reshape(SEQ_PER_CHUNK * hd_w, hd_w) for hd in heads]
        else:
            st = [st_s[hd] for hd in heads]
        ab = [_mm_nt(jnp.concatenate([ath[hd], rth[hd]], axis=0).astype(BF16),
                     jnp.concatenate([kt[:, sls[hd]], bt[:, sls[hd]]], axis=0).astype(BF16))
              for hd in heads]
        a_ak = [jnp.where(strict, m[:c, :c], 0.0) for m in ab]
        n_ab = [jnp.where(strict, -m[:c, c:], 0.0) for m in ab]
        a_r = [jnp.concatenate([jnp.where(causal, m[c:, :c], 0.0), jnp.where(causal, m[c:, c:], 0.0)],
                               axis=1).astype(BF16) for m in ab]
        akv = [_mm(a_ak[hd].astype(BF16), vh[hd].astype(BF16)) for hd in heads]
        sol = _solve_unit_lower_each(
            n_ab, [jnp.concatenate([ath[hd], akv[hd]], axis=1) for hd in heads], per_seq)
        pr = [_mm_nt(jnp.concatenate([sol[hd][:, :hd_w], rth[hd]], axis=0).astype(BF16), st[hd].astype(BF16))
              for hd in heads]
        if per_seq:
            u = [_pick_own_seq(pr[hd][:c], hd_w) + sol[hd][:, hd_w:] for hd in heads]
            yst = [_pick_own_seq(pr[hd][c:], hd_w) for hd in heads]
        else:
            u = [pr[hd][:c] + sol[hd][:, hd_w:] for hd in heads]
            yst = [pr[hd][c:] for hd in heads]
        vu = [jnp.concatenate([vh[hd], u[hd]], axis=0) for hd in heads]
        ya = [_mm(a_r[hd], vu[hd].astype(BF16)) for hd in heads]
        kb = [jnp.concatenate([kh[:, sl], bh[:, sl]], axis=0).astype(BF16) for sl in sls]
        if per_seq:
            upd = [_mm_tn(jnp.where(big_mask, jnp.concatenate([vu[hd]] * SEQ_PER_CHUNK, axis=1), 0.0
                                    ).astype(BF16), kb[hd]) for hd in heads]
        else:
            upd = [_mm_tn(vu[hd].astype(BF16), kb[hd]) for hd in heads]
        for hd in heads:
            if per_seq:
                wcb = jnp.concatenate([jnp.broadcast_to(wc[SUBLANES * s:SUBLANES * s + 1, sls[hd]], (hd_w, hd_w))
                                       for s in range(SEQ_PER_CHUNK)], axis=0)
                so_ref[seqs, hd] = (st[hd] * wcb + upd[hd]).reshape(SEQ_PER_CHUNK, hd_w, hd_w)
            else:
                st_s[hd] = st[hd] * wc[:, sls[hd]] + upd[hd]
            y_s[rows, sls[hd]] = yst[hd] + ya[hd]
        return carry

    lax.fori_loop(0, tm // c, chunk, 0)

    if not per_seq:
        so_ref[0] = st_s[...]

    y = y_s[...]
    mean = _seg64_sum(y) * (1.0 / hd_w)
    yc = y - mean
    var = _seg64_sum(yc * yc) * (1.0 / hd_w)
    yn = yc * lax.rsqrt(var + RW_GN_EPS) * lnw_ref[...] + lnb_ref[...]
    bonus = _seg64_sum(r_s[...] * k_s[...] * rk_ref[...]) * v_s[...]
    y_ref[...] = ((yn + bonus) * gate).astype(BF16)


def _rwkv_param_specs():
    row = _const_spec((1, RW_WIDTH))
    return [_const_spec((1, RW_PCOLS)), row, _const_spec((LANES, RW_WIDTH)), row, _const_spec((LANES, RW_WIDTH)),
            _const_spec((RW_GATE, RW_WIDTH)), row, row, row, row, row]


def _rwkv_scratch(tm):
    return [pltpu.VMEM((tm, RW_WIDTH), F32) for _ in range(7)]


def _rwkv_prompt(proj, rp, nb, seq, tm):
    nt = seq // tm
    cb = P_RW // RW_PCOLS
    y, st = pl.pallas_call(
        functools.partial(_rwkv_kernel, per_seq=False, tm=tm),
        grid=(nb, nt),
        in_specs=[pl.BlockSpec((tm, RW_PCOLS), lambda b, t: (b * nt + t, cb))] + _rwkv_param_specs(),
        out_specs=[pl.BlockSpec((tm, RW_WIDTH), lambda b, t: (b * nt + t, 0)),
                   pl.BlockSpec((1, RW_HEADS, RW_HD, RW_HD), lambda b, t: (b, 0, 0, 0))],
        out_shape=[jax.ShapeDtypeStruct((nb * seq, RW_WIDTH), BF16),
                   jax.ShapeDtypeStruct((nb, RW_HEADS, RW_HD, RW_HD), F32)],
        scratch_shapes=_rwkv_scratch(tm) + [pltpu.VMEM((SUBLANES, RW_PCOLS), F32),
                                            pltpu.VMEM((RW_HEADS, RW_HD, RW_HD), F32)],
        compiler_params=_cparams(("parallel", "arbitrary")),
        name="rwkv_prompt",
    )(proj, *rp)
    return y, st


def _rwkv_sample(proj, shift0, st0, rp, row0, nseq, tm):
    ns = tm // SUBLANES
    cb = P_RW // RW_PCOLS
    r0 = row0 // tm
    st_spec = pl.BlockSpec((ns, RW_HEADS, RW_HD, RW_HD), lambda i: (i, 0, 0, 0))
    y, st = pl.pallas_call(
        functools.partial(_rwkv_kernel, per_seq=True, tm=tm),
        grid=(nseq // ns,),
        in_specs=[pl.BlockSpec((tm, RW_PCOLS), lambda i: (r0 + i, cb)),
                  pl.BlockSpec((ns, 1, RW_PCOLS), lambda i: (i, 0, 0)), st_spec] + _rwkv_param_specs(),
        out_specs=[pl.BlockSpec((tm, RW_WIDTH), lambda i: (i, 0)), st_spec],
        out_shape=[jax.ShapeDtypeStruct((nseq * SUBLANES, RW_WIDTH), BF16),
                   jax.ShapeDtypeStruct((nseq, RW_HEADS, RW_HD, RW_HD), F32)],
        scratch_shapes=_rwkv_scratch(tm),
        compiler_params=_cparams(("parallel",)),
        name="rwkv_sample",
    )(proj, shift0, st0, *rp)
    return y, st


def _rw_pad_cols(x):
    z = jnp.zeros(x.shape[:-1] + (LANES - RW_LORA,), x.dtype)
    a, b = 3 * RW_WIDTH, 3 * RW_WIDTH + RW_LORA
    return jnp.concatenate([x[..., :a], x[..., a:b], z, x[..., b:b + RW_LORA], z, x[..., b + RW_LORA:]], axis=-1)


def _rw_unpad_cols(x):
    a = 3 * RW_WIDTH
    return jnp.concatenate([x[..., :a + RW_LORA], x[..., a + LANES:a + LANES + RW_LORA], x[..., a + 2 * LANES:]],
                           axis=-1)


def _rwkv_params(mu, w0, w2, a0, a2, g2, k_k, k_a, r_k, ln_w, ln_b):
    row = lambda t: t.reshape(1, RW_WIDTH)
    padr = lambda t: jnp.concatenate([t, jnp.zeros((LANES - RW_LORA, RW_WIDTH), t.dtype)], axis=0).astype(BF16)
    return (_rw_pad_cols(mu).reshape(1, RW_PCOLS), row(w0), padr(w2), row(a0), padr(a2), g2.astype(BF16),
            row(k_k), row(k_a), row(r_k), row(ln_w), row(ln_b))


def _gdn_kernel(*refs, per_seq, tm):
    if per_seq:
        (q_ref, k_ref, v_ref, z_ref, ba_ref, fq_ref, fk_ref, fv_ref, si_ref, cwq_ref, cwk_ref, cwv_ref,
         al_ref, dt_ref, ng_ref, y_ref, so_ref, q_s, k_s, v_s, ba_s, o_s) = refs
        conv_state = (fq_ref, fk_ref, fv_ref)
    else:
        (q_ref, k_ref, v_ref, z_ref, ba_ref, cwq_ref, cwk_ref, cwv_ref, al_ref, dt_ref, ng_ref,
         y_ref, so_ref, q_s, k_s, v_s, ba_s, o_s, pq_s, pk_s, pv_s, st_s) = refs
        prevs = (pq_s, pk_s, pv_s)

        @pl.when(pl.program_id(1) == 0)
        def _():
            for p in prevs:
                p[...] = jnp.zeros_like(p)
            st_s[...] = jnp.zeros_like(st_s)
    c, hw = CHUNK, GD_HD

    def conv_act(idx, u_ref, cw_ref):
        u = u_ref[...]
        if per_seq:
            f = conv_state[idx][...]
            sh = [_shift_seq8(u, d, f) for d in (1, 2, 3)]
        else:
            p = prevs[idx][...]
            sh = [_shift_chain(u, d, p) for d in (1, 2, 3)]
            prevs[idx][...] = u[tm - SUBLANES:, :]
        x = cw_ref[3:4, :] * u + cw_ref[2:3, :] * sh[0] + cw_ref[1:2, :] * sh[1] + cw_ref[0:1, :] * sh[2]
        return x * _sigmoid(x)

    def l2n(x):
        outs = []
        for hd in range(GD_HEADS):
            xh = x[:, hw * hd:hw * (hd + 1)]
            outs.append(xh * lax.rsqrt(jnp.sum(xh * xh, axis=-1, keepdims=True) + 1e-6))
        return jnp.concatenate(outs, axis=1)

    q_s[...] = l2n(conv_act(0, q_ref, cwq_ref)) * (hw ** -0.5)
    k_s[...] = l2n(conv_act(1, k_ref, cwk_ref))
    v_s[...] = conv_act(2, v_ref, cwv_ref)
    ba = ba_ref[...]
    lane = lax.broadcasted_iota(jnp.int32, ba.shape, 1)
    g_all = -jnp.exp(al_ref[...]) * _softplus(ba + dt_ref[...])
    ba_s[...] = jnp.where(lane < GD_HEADS, _sigmoid(ba), g_all)

    strict, causal = _chunk_masks(per_seq)
    if per_seq:
        big_mask = _seq_block_mask(c, hw)

    def chunk(idx, carry):
        r0 = pl.multiple_of(idx * c, c)
        rows = pl.ds(r0, c)
        bg = ba_s[rows, :]
        gc_all = _cumsum_rows(bg, groupwise=per_seq)
        gc_t = gc_all.T
        gl_all = _group_last(gc_all) if per_seq else gc_all[c - 1:c, :]
        qa, ka, va = q_s[rows, :], k_s[rows, :], v_s[rows, :]
        heads = range(GD_HEADS)
        sls = [slice(hw * hd, hw * (hd + 1)) for hd in heads]
        seqs = pl.ds(idx * SEQ_PER_CHUNK, SEQ_PER_CHUNK)
        beta = [bg[:, hd:hd + 1] for hd in heads]
        gc = [gc_all[:, GD_HEADS + hd:GD_HEADS + hd + 1] for hd in heads]
        gl = [gl_all[:, GD_HEADS + hd:GD_HEADS + hd + 1] for hd in heads]
        dec = [jnp.exp(jnp.where(causal, gc[hd] - gc_t[GD_HEADS + hd:GD_HEADS + hd + 1, :], -1e30))
               for hd in heads]
        qh = [qa[:, sl] for sl in sls]
        kh = [ka[:, sl] for sl in sls]
        kb = [kh[hd] * beta[hd] for hd in heads]
        eg = [jnp.exp(g) for g in gc]
        if per_seq:
            st3 = [si_ref[seqs, hd] for hd in heads]
            st_rhs = [jnp.concatenate([s3[s] for s in range(SEQ_PER_CHUNK)], axis=1).astype(BF16)
                      for s3 in st3]
        else:
            st = [st_s[hd] for hd in heads]
            st_rhs = [s.astype(BF16) for s in st]
        kq = [_mm_nt(jnp.concatenate([kb[hd], qh[hd]], axis=0).astype(BF16), kh[hd].astype(BF16))
              for hd in heads]
        low = [jnp.where(strict, kq[hd][:c] * dec[hd], 0.0) for hd in heads]
        intra = [(kq[hd][c:] * dec[hd]).astype(BF16) for hd in heads]
        sol = _solve_unit_lower_each(
            low, [jnp.concatenate([va[:, sls[hd]] * beta[hd], kb[hd] * eg[hd]], axis=1) for hd in heads], per_seq)
        ws = [_mm(jnp.concatenate([sol[hd][:, hw:], qh[hd] * eg[hd]], axis=0).astype(BF16), st_rhs[hd])
              for hd in heads]
        if per_seq:
            v_new = [sol[hd][:, :hw] - _pick_own_seq(ws[hd][:c], hw) for hd in heads]
            o_st = [_pick_own_seq(ws[hd][c:], hw) for hd in heads]
        else:
            v_new = [sol[hd][:, :hw] - ws[hd][:c] for hd in heads]
            o_st = [ws[hd][c:] for hd in heads]
        vnb = [v.astype(BF16) for v in v_new]
        o_in = [_mm(intra[hd], vnb[hd]) for hd in heads]
        kdec = [kh[hd] * jnp.exp(gl[hd] - gc[hd]) for hd in heads]
        if per_seq:
            upd = [_mm_tn(jnp.where(big_mask, jnp.concatenate([kdec[hd]] * SEQ_PER_CHUNK, axis=1), 0.0
                                    ).astype(BF16), vnb[hd]) for hd in heads]
        else:
            upd = [_mm_tn(kdec[hd].astype(BF16), vnb[hd]) for hd in heads]
        for hd in heads:
            if per_seq:
                egl = jnp.exp(gl[hd])
                mult = jnp.concatenate([jnp.broadcast_to(egl[SUBLANES * s:SUBLANES * s + 1, :], (hw, hw))
                                        for s in range(SEQ_PER_CHUNK)], axis=0)
                st_new = st3[hd].reshape(SEQ_PER_CHUNK * hw, hw) * mult + upd[hd]
                so_ref[seqs, hd] = st_new.reshape(SEQ_PER_CHUNK, hw, hw)
            else:
                st_s[hd] = st[hd] * jnp.exp(gl[hd]) + upd[hd]
            o_s[rows, sls[hd]] = o_st[hd] + o_in[hd]
        return carry

    lax.fori_loop(0, tm // c, chunk, 0)

    if not per_seq:
        so_ref[0] = st_s[...]

    outs = []
    for hd in range(GD_HEADS):
        sl = slice(hw * hd, hw * (hd + 1))
        o = o_s[:, sl]
        o = o * lax.rsqrt(jnp.mean(o * o, axis=-1, keepdims=True) + NORM_EPS) * ng_ref[...]
        z = z_ref[:, sl]
        outs.append(o * (z * _sigmoid(z)))
    y_ref[...] = jnp.concatenate(outs, axis=1).astype(BF16)


def _gdn_param_specs():
    c2 = lambda shape, j=0: pl.BlockSpec(shape, (lambda *a: (0, j)))
    return [c2((GD_CONV, GD_WIDTH), 0), c2((GD_CONV, GD_WIDTH), 1), c2((GD_CONV, GD_WIDTH), 2),
            c2((1, LANES)), c2((1, LANES)), c2((1, GD_HD))]


def _gdn_scratch(tm):
    return ([pltpu.VMEM((tm, GD_WIDTH), F32) for _ in range(3)] + [pltpu.VMEM((tm, LANES), F32),
                                                                  pltpu.VMEM((tm, GD_WIDTH), F32)])


def _gdn_prompt(proj, gp, nb, seq, tm):
    nt = seq // tm
    qb = P_Q // GD_WIDTH
    blk = lambda j: pl.BlockSpec((tm, GD_WIDTH), lambda b, t: (b * nt + t, j))
    y, st = pl.pallas_call(
        functools.partial(_gdn_kernel, per_seq=False, tm=tm),
        grid=(nb, nt),
        in_specs=[blk(qb), blk(qb + 1), blk(qb + 2), blk(P_Z // GD_WIDTH),
                  pl.BlockSpec((tm, LANES), lambda b, t: (b * nt + t, P_BA // LANES))] + _gdn_param_specs(),
        out_specs=[pl.BlockSpec((tm, GD_WIDTH), lambda b, t: (b * nt + t, 0)),
                   pl.BlockSpec((1, GD_HEADS, GD_HD, GD_HD), lambda b, t: (b, 0, 0, 0))],
        out_shape=[jax.ShapeDtypeStruct((nb * seq, GD_WIDTH), BF16),
                   jax.ShapeDtypeStruct((nb, GD_HEADS, GD_HD, GD_HD), F32)],
        scratch_shapes=_gdn_scratch(tm) + [pltpu.VMEM((SUBLANES, GD_WIDTH), F32) for _ in range(3)]
        + [pltpu.VMEM((GD_HEADS, GD_HD, GD_HD), F32)],
        compiler_params=_cparams(("parallel", "arbitrary")),
        name="gdn_prompt",
    )(proj, proj, proj, proj, proj, gp[0], gp[0], gp[0], *gp[1:])
    return y, st


def _gdn_sample(proj, conv0, st0, gp, row0, nseq, tm):
    ns = tm // SUBLANES
    qb = P_Q // GD_WIDTH
    r0 = row0 // tm
    blk = lambda j: pl.BlockSpec((tm, GD_WIDTH), lambda i: (r0 + i, j))
    frm = lambda j: pl.BlockSpec((ns, GD_CONV - 1, GD_WIDTH), lambda i: (i, 0, j))
    st_spec = pl.BlockSpec((ns, GD_HEADS, GD_HD, GD_HD), lambda i: (i, 0, 0, 0))
    y, st = pl.pallas_call(
        functools.partial(_gdn_kernel, per_seq=True, tm=tm),
        grid=(nseq // ns,),
        in_specs=[blk(qb), blk(qb + 1), blk(qb + 2), blk(P_Z // GD_WIDTH),
                  pl.BlockSpec((tm, LANES), lambda i: (r0 + i, P_BA // LANES)),
                  frm(0), frm(1), frm(2), st_spec] + _gdn_param_specs(),
        out_specs=[pl.BlockSpec((tm, GD_WIDTH), lambda i: (i, 0)), st_spec],
        out_shape=[jax.ShapeDtypeStruct((nseq * SUBLANES, GD_WIDTH), BF16),
                   jax.ShapeDtypeStruct((nseq, GD_HEADS, GD_HD, GD_HD), F32)],
        scratch_shapes=_gdn_scratch(tm),
        compiler_params=_cparams(("parallel",)),
        name="gdn_sample",
    )(proj, proj, proj, proj, proj, conv0, conv0, conv0, st0, gp[0], gp[0], gp[0], *gp[1:])
    return y, st


def _gdn_params(conv_w, a_log, dt_bias, norm_g):
    pad = lambda t: jnp.concatenate([jnp.zeros((GD_HEADS,), F32), t, jnp.zeros((LANES - 2 * GD_HEADS,), F32)]
                                    ).reshape(1, LANES)
    return (conv_w, pad(a_log), pad(dt_bias), norm_g.reshape(1, GD_HD))


def _permute_w_in(w):
    o_rw, o_qkv = S5_WIDTH, S5_WIDTH + RW_COLS
    o_z = o_qkv + 3 * GD_WIDTH
    o_b = o_z + GD_WIDTH
    o_g = o_b + 2 * GD_HEADS
    parts = [w[:, o_g:], _rw_pad_cols(w[:, o_rw:o_qkv]), w[:, o_qkv:o_z], w[:, o_z:o_b], w[:, :S5_WIDTH],
             w[:, o_b:o_g], jnp.zeros((w.shape[0], P_COLS - P_BA - 2 * GD_HEADS), w.dtype)]
    return jnp.concatenate(parts, axis=1).astype(BF16)


def _tail_rows(a, cols, nrows, mp, seq):
    a3 = a.reshape(a.shape[0] // SUBLANES, SUBLANES, a.shape[1])
    g = seq // SUBLANES
    lo = SUBLANES - nrows
    return (a3[g - 1:mp // SUBLANES:g, lo:, cols[0]:cols[1]], a3[mp // SUBLANES:, lo:, cols[0]:cols[1]])


def kernel(x_prompt, x_sample, state_s5_re, state_s5_im, state_rwkv_shift, state_rwkv_wkv, state_gdn_conv, state_gdn, state_ffn_conv, norm1_g, norm2_g, final_norm_g, w_in, s5_lambda_re, s5_lambda_im, s5_b_re, s5_b_im, s5_c_re, s5_c_im, s5_d, s5_log_step, s5_w_glu, rwkv_mu, rwkv_w0, rwkv_w2, rwkv_a0, rwkv_a2, rwkv_g2, rwkv_k_k, rwkv_k_a, rwkv_r_k, rwkv_ln_w, rwkv_ln_b, gdn_conv_w, gdn_a_log, gdn_dt_bias, gdn_norm_g, w_br_s5, w_br_rwkv, w_br_gdn, w_out, ffn_w_up, ffn_conv_w, ffn_conv_b, ffn_w_down):
    nb, seq, d = x_prompt.shape
    ns, sl, _ = x_sample.shape
    assert sl == SUBLANES and d == D_MODEL
    mp, ms = nb * seq, ns * sl
    x = jnp.concatenate([x_prompt.reshape(mp, d), x_sample.reshape(ms, d)], axis=0)

    new_p = [[] for _ in range(7)]
    new_s = [[] for _ in range(7)]
    for l in range(DEPTH):
        proj = _norm_matmul(x, norm1_g[l], _permute_w_in(w_in[l]), tm=1024, tn=512)

        sp = _s5_params(s5_lambda_re[l], s5_lambda_im[l], s5_b_re[l], s5_b_im[l], s5_c_re[l], s5_c_im[l],
                        s5_d[l], s5_log_step[l], s5_w_glu[l])
        ys5_p, p_re, p_im = _s5_prompt(proj, sp, nb, seq, tm=256)
        ys5_s, s_re, s_im = _s5_sample(proj, state_s5_re[l], state_s5_im[l], sp, mp, ns, tm=128)

        rp = _rwkv_params(rwkv_mu[l], rwkv_w0[l], rwkv_w2[l], rwkv_a0[l], rwkv_a2[l], rwkv_g2[l], rwkv_k_k[l],
                          rwkv_k_a[l], rwkv_r_k[l].reshape(RW_WIDTH), rwkv_ln_w[l], rwkv_ln_b[l])
        yrw_p, p_wkv = _rwkv_prompt(proj, rp, nb, seq, tm=256)
        yrw_s, s_wkv = _rwkv_sample(proj, _rw_pad_cols(state_rwkv_shift[l])[:, None, :], state_rwkv_wkv[l], rp,
                                    mp, ns, tm=128)
        p_shift, s_shift = [_rw_unpad_cols(t[:, 0]) for t in _tail_rows(proj, (P_RW, P_RW + RW_PCOLS), 1, mp, seq)]

        gp = _gdn_params(gdn_conv_w[l], gdn_a_log[l], gdn_dt_bias[l], gdn_norm_g[l])
        ygd_p, p_gdn = _gdn_prompt(proj, gp, nb, seq, tm=256)
        ygd_s, s_gdn = _gdn_sample(proj, state_gdn_conv[l], state_gdn[l], gp, mp, ns, tm=64)
        p_gconv, s_gconv = _tail_rows(proj, (P_Q, P_Q + 3 * GD_WIDTH), GD_CONV - 1, mp, seq)

        merged = _merge((ys5_p, yrw_p, ygd_p), (ys5_s, yrw_s, ygd_s),
                        (w_br_s5[l].astype(BF16), w_br_rwkv[l].astype(BF16), w_br_gdn[l].astype(BF16)),
                        proj, tm=512, tn=512)
        x, xn = _out_norm(merged, w_out[l].astype(BF16), x, norm2_g[l], tm=512)

        h, p_ffn, s_ffn = _ffn_up(xn, ffn_w_up[l], ffn_conv_w[l], ffn_conv_b[l].reshape(1, 2 * D_FF),
                                  state_ffn_conv[l], nb, seq, ns, tm=1024, tn=512)
        x = _matmul_res(h, ffn_w_down[l].astype(BF16), x, tm=512, tn=512)

        for lst, vals in ((new_p, (p_re, p_im, p_shift, p_wkv, p_gconv, p_gdn, p_ffn)),
                          (new_s, (s_re, s_im, s_shift, s_wkv, s_gconv, s_gdn, s_ffn))):
            for acc, val in zip(lst, vals):
                acc.append(val)

    y = _rmsnorm(x, final_norm_g, tm=512)
    y_prompt = y[:mp].reshape(nb, seq, d)
    y_sample = y[mp:].reshape(ns, sl, d)
    stack = lambda lst: tuple(jnp.stack(v) for v in lst)
    return (y_prompt, y_sample) + stack(new_p) + stack(new_s)
```

```python
import functools
import math

import jax
import jax.numpy as jnp
from jax import lax
from jax.experimental import pallas as pl
from jax.experimental.pallas import tpu as pltpu

F32 = jnp.float32
BF16 = jnp.bfloat16

SUBLANES = 8
LANES = 128
VMEM_LIMIT = 52 * 1024 * 1024
CHUNK = 64
SEQ_PER_CHUNK = CHUNK // SUBLANES

D_MODEL = 2048
DEPTH = 2
S5_WIDTH = 512
S5_GROUP = 16
S5_GROUPS = 32
S5_STATE = 64
S5_NSTATE = S5_GROUPS * S5_STATE
RW_WIDTH = 512
RW_HEADS = 8
RW_HD = 64
RW_LORA = 96
RW_GATE = 256
RW_COLS = 3 * RW_WIDTH + 2 * RW_LORA + RW_GATE
RW_PCOLS = 2048
RW_GN_EPS = 64e-5
GD_WIDTH = 1024
GD_HEADS = 8
GD_HD = 128
GD_CONV = 4
D_FF = 5632
FFN_CONV = 3
NORM_EPS = 1e-6
P_GATES, P_RW, P_Q, P_Z, P_S5, P_BA, P_COLS = 0, 6144, 8192, 11264, 12288, 12800, 13312


def _cparams(sem):
    return pltpu.CompilerParams(dimension_semantics=sem, vmem_limit_bytes=VMEM_LIMIT)


def _mm(a, b):
    return jnp.dot(a, b, preferred_element_type=F32)


def _mm_nt(a, b):
    return lax.dot_general(a, b, (((1,), (1,)), ((), ())), preferred_element_type=F32)


def _mm_tn(a, b):
    return lax.dot_general(a, b, (((0,), (0,)), ((), ())), preferred_element_type=F32)


def _sigmoid(x):
    return 1.0 / (1.0 + jnp.exp(-x))


def _softplus(x):
    return jnp.maximum(x, 0.0) + jnp.log1p(jnp.exp(-jnp.abs(x)))


def _gelu_tanh(x):
    return 0.5 * x * (1.0 + jnp.tanh(math.sqrt(2.0 / math.pi) * (x + 0.044715 * (x * x * x))))


def _row_iota(shape):
    return lax.broadcasted_iota(jnp.int32, shape, 0)


def _shift_chain(x, d, prev8):
    rolled = pltpu.roll(x, d, 0)
    row = _row_iota(prev8.shape)
    head = jnp.where(row < d, pltpu.roll(prev8, d, 0), rolled[:SUBLANES])
    return jnp.concatenate([head, rolled[SUBLANES:]], axis=0)


def _shift_seq8(x, d, state):
    n, ch = x.shape
    ns, w1, _ = state.shape
    out = pltpu.roll(x, d, 0).reshape(ns, SUBLANES, ch)
    row = lax.broadcasted_iota(jnp.int32, out.shape, 1)
    for r in range(d):
        out = jnp.where(row == r, state[:, w1 - d + r:w1 - d + r + 1, :], out)
    return out.reshape(n, ch)


def _group_last(x):
    c, n = x.shape
    x3 = x.reshape(c // SUBLANES, SUBLANES, n)
    return jnp.broadcast_to(x3[:, SUBLANES - 1:SUBLANES, :], x3.shape).reshape(c, n)


def _cumsum_rows(x, groupwise=False):
    c = x.shape[0]
    row = _row_iota(x.shape) % SUBLANES
    for d in (1, 2, 4):
        x = x + jnp.where(row >= d, pltpu.roll(x, d, 0), 0.0)
    if c > SUBLANES and not groupwise:
        blocks = [x[SUBLANES * i:SUBLANES * (i + 1)] for i in range(c // SUBLANES)]
        for i in range(1, len(blocks)):
            blocks[i] = blocks[i] + blocks[i - 1][SUBLANES - 1:SUBLANES, :]
        x = jnp.concatenate(blocks, axis=0)
    return x


def _split2(x):
    hi = x.astype(BF16)
    return hi, (x - hi.astype(F32)).astype(BF16)


def _mm_hilo_each(a_list, b_list):
    sa = [_split2(a) for a in a_list]
    sb = [_split2(b) for b in b_list]
    hh = [_mm(a[0], b[0]) for a, b in zip(sa, sb)]
    hl = [_mm(a[0], b[1]) for a, b in zip(sa, sb)]
    lh = [_mm(a[1], b[0]) for a, b in zip(sa, sb)]
    return [x + (y + z) for x, y, z in zip(hh, hl, lh)]


def _solve_unit_lower_each(lows, rhss, diag_only):
    c = lows[0].shape[0]
    nb = c // SUBLANES
    nh = len(lows)
    n = rhss[0].shape[1]
    blk = lambda a, i: a[SUBLANES * i:SUBLANES * (i + 1)]
    rb = [[blk(r, i) for i in range(nb)] for r in rhss]
    with_off = nb > 1 and not diag_only
    same = (lax.broadcasted_iota(jnp.int32, (c, c), 0) // SUBLANES
            == lax.broadcasted_iota(jnp.int32, (c, c), 1) // SUBLANES)
    if with_off:
        ob = [[blk(jnp.where(same, 0.0, low), i) for i in range(nb)] for low in lows]
    sel = (lax.broadcasted_iota(jnp.int32, (c, (SUBLANES - 1) * LANES), 0) % SUBLANES
           == lax.broadcasted_iota(jnp.int32, (c, (SUBLANES - 1) * LANES), 1) // LANES)
    sel = jnp.where(sel, 1.0, 0.0).astype(BF16)
    diag = [_split2(jnp.where(same, low, 0.0)) if not diag_only else _split2(low) for low in lows]
    cols_hi = [_mm(d[0], sel) for d in diag]
    cols_lo = [_mm(d[1], sel) for d in diag]
    cols = [a + b for a, b in zip(cols_hi, cols_lo)]
    reps = n // LANES
    for j in range(SUBLANES - 1):
        for h in range(nh):
            cj = cols[h][:, LANES * j:LANES * (j + 1)]
            cj_n = jnp.concatenate([cj] * reps, axis=1) if reps > 1 else cj
            for i in range(nb):
                rb[h][i] = rb[h][i] - blk(cj_n, i) * rb[h][i][j:j + 1, :]
                if with_off and i > 0:
                    ob[h][i] = ob[h][i] - blk(cj, i)[:, :c] * ob[h][i][j:j + 1, :]
    r1 = [jnp.concatenate(b, axis=0) if nb > 1 else b[0] for b in rb]
    if not with_off:
        return r1
    n1 = [jnp.concatenate(b, axis=0) for b in ob]
    n2 = _mm_hilo_each(n1, n1)
    t = _mm_hilo_each(n1 + n2, r1 + n2)
    y = [r - nr for r, nr in zip(r1, t[:nh])]
    n4 = t[nh:]
    y = [a + b for a, b in zip(y, _mm_hilo_each(n2, y))]
    return [a + b for a, b in zip(y, _mm_hilo_each(n4, y))]


def _chunks_per_iteration(n_chunks):
    return 2 if n_chunks % 2 == 0 else 1


def _chunk_masks(per_seq):
    ri = lax.broadcasted_iota(jnp.int32, (CHUNK, CHUNK), 0)
    ci = lax.broadcasted_iota(jnp.int32, (CHUNK, CHUNK), 1)
    strict, causal = ri > ci, ri >= ci
    if per_seq:
        same = (ri // SUBLANES) == (ci // SUBLANES)
        strict, causal = jnp.logical_and(strict, same), jnp.logical_and(causal, same)
    return strict, causal


def _seq_block_mask(rows, width):
    r = lax.broadcasted_iota(jnp.int32, (rows, SEQ_PER_CHUNK * width), 0)
    l = lax.broadcasted_iota(jnp.int32, (rows, SEQ_PER_CHUNK * width), 1)
    return ((r % CHUNK) // SUBLANES) == (l // width)


def _pick_own_seq(a, width):
    return jnp.concatenate([a[SUBLANES * s:SUBLANES * (s + 1), width * s:width * (s + 1)]
                            for s in range(SEQ_PER_CHUNK)], axis=0)


def _rmsnorm_rows(x_ref, g_ref, o_ref, rows, chunk=128):
    def body(i, carry):
        r0 = pl.multiple_of(i * chunk, chunk)
        x = x_ref[pl.ds(r0, chunk), :]
        ms = jnp.mean(x * x, axis=-1, keepdims=True)
        o_ref[pl.ds(r0, chunk), :] = (x * lax.rsqrt(ms + NORM_EPS) * g_ref[...]).astype(o_ref.dtype)
        return carry
    lax.fori_loop(0, rows // chunk, body, 0)


def _two_group_specs(tm, width, prompt_tiles, sample_off, pipeline_mode=None):
    p = pl.BlockSpec((tm, width), lambda i, *_: (jnp.minimum(i, prompt_tiles - 1), 0), pipeline_mode=pipeline_mode)
    s = pl.BlockSpec((tm, width), lambda i, *_: (jnp.maximum(i - prompt_tiles, 0) + sample_off, 0),
                     pipeline_mode=pipeline_mode)
    return p, s


def _layer_spec(shape, index_map, layer):
    return pl.BlockSpec((None,) + shape, lambda *a: (layer,) + tuple(index_map(*a)))


def _norm_matmul_kernel(xp_ref, xs_ref, g_ref, w_ref, o_ref, xn_ref, *, rows, prompt_tiles):
    i = pl.program_id(0)

    @pl.when(pl.program_id(1) == 0)
    def _():
        @pl.when(i < prompt_tiles)
        def _():
            _rmsnorm_rows(xp_ref, g_ref, xn_ref, rows)

        @pl.when(i >= prompt_tiles)
        def _():
            _rmsnorm_rows(xs_ref, g_ref, xn_ref, rows)

    o_ref[...] = _mm(xn_ref[...], w_ref[...])


def _norm_matmul(xp, xs, sample_off, m, prompt_tiles, g, w, layer, tm, tn):
    k = xp.shape[1]
    n = w.shape[-1]
    psp, ssp = _two_group_specs(tm, k, prompt_tiles, sample_off, pipeline_mode=pl.Buffered(1))
    return pl.pallas_call(
        functools.partial(_norm_matmul_kernel, rows=tm, prompt_tiles=prompt_tiles),
        grid=(m // tm, n // tn),
        in_specs=[psp, ssp, pl.BlockSpec((1, k), lambda i, j: (0, 0)),
                  _layer_spec((k, tn), lambda i, j: (0, j), layer)],
        out_specs=pl.BlockSpec((tm, tn), lambda i, j: (i, j)),
        out_shape=jax.ShapeDtypeStruct((m, n), F32),
        scratch_shapes=[pltpu.VMEM((tm, k), BF16)],
        compiler_params=_cparams(("parallel", "arbitrary")),
        name="norm_matmul",
    )(xp, xs, g.reshape(1, k), w)


def _matmul_res_kernel(a_ref, w_ref, r_ref, o_ref):
    o_ref[...] = r_ref[...] + _mm(a_ref[...], w_ref[...])


def _matmul_res(a, w, layer, res, tm, tn):
    m, k = a.shape
    n = w.shape[-1]
    return pl.pallas_call(
        _matmul_res_kernel,
        grid=(m // tm, n // tn),
        in_specs=[pl.BlockSpec((tm, k), lambda i, j: (i, 0), pipeline_mode=pl.Buffered(1)),
                  _layer_spec((k, tn), lambda i, j: (0, j), layer),
                  pl.BlockSpec((tm, tn), lambda i, j: (i, j))],
        out_specs=pl.BlockSpec((tm, tn), lambda i, j: (i, j)),
        out_shape=jax.ShapeDtypeStruct((m, n), F32),
        compiler_params=_cparams(("parallel", "arbitrary")),
        name="matmul_res",
    )(a, w, res)


def _out_norm_kernel(a_ref, w_ref, rp_ref, rs_ref, g_ref, x_ref, xn_ref, *, rows, prompt_tiles):
    res = jnp.where(pl.program_id(0) < prompt_tiles, rp_ref[...], rs_ref[...])
    x_ref[...] = res + _mm(a_ref[...], w_ref[...])
    _rmsnorm_rows(x_ref, g_ref, xn_ref, rows)


def _out_norm(a, w, layer, rp, rs, sample_off, prompt_tiles, g, tm):
    m, k = a.shape
    n = w.shape[-1]
    psp, ssp = _two_group_specs(tm, n, prompt_tiles, sample_off)
    return pl.pallas_call(
        functools.partial(_out_norm_kernel, rows=tm, prompt_tiles=prompt_tiles),
        grid=(m // tm,),
        in_specs=[pl.BlockSpec((tm, k), lambda i: (i, 0)),
                  _layer_spec((k, n), lambda i: (0, 0), layer),
                  psp, ssp, pl.BlockSpec((1, n), lambda i: (0, 0))],
        out_specs=[pl.BlockSpec((tm, n), lambda i: (i, 0)), pl.BlockSpec((tm, n), lambda i: (i, 0))],
        out_shape=[jax.ShapeDtypeStruct((m, n), F32), jax.ShapeDtypeStruct((m, n), BF16)],
        compiler_params=_cparams(("parallel",)),
        name="out_norm",
    )(a, w, rp, rs, g.reshape(1, n))


def _merge_kernel(y1p, y2p, y3p, y1s, y2s, y3s, w1_ref, w2_ref, w3_ref, g1_ref, g2_ref, g3_ref, o_ref, *,
                  prompt_tiles, tm, rc):
    is_p = pl.program_id(0) < prompt_tiles
    for r in range(tm // rc):
        rows = slice(r * rc, (r + 1) * rc)
        pick = lambda p, s: jnp.where(is_p, p[rows, :], s[rows, :])
        acc = _sigmoid(g1_ref[rows, :]) * _mm(pick(y1p, y1s), w1_ref[...])
        acc = acc + _sigmoid(g2_ref[rows, :]) * _mm(pick(y2p, y2s), w2_ref[...])
        acc = acc + _sigmoid(g3_ref[rows, :]) * _mm(pick(y3p, y3s), w3_ref[...])
        o_ref[rows, :] = acc.astype(BF16)


def _merge(ys_p, ys_s, ws, layer, proj, tm, tn):
    mp, ms = ys_p[0].shape[0], ys_s[0].shape[0]
    n = ws[0].shape[-1]
    nj = n // tn
    pt = mp // tm
    gate_spec = lambda b: pl.BlockSpec((tm, tn), lambda i, j: (i, P_GATES // tn + b * nj + j))
    y_specs = [_two_group_specs(tm, y.shape[1], pt, 0) for y in ys_p]
    w_spec = lambda w: _layer_spec((w.shape[1], tn), lambda i, j: (0, j), layer)
    return pl.pallas_call(
        functools.partial(_merge_kernel, prompt_tiles=pt, tm=tm, rc=128),
        grid=((mp + ms) // tm, nj),
        in_specs=[s[0] for s in y_specs] + [s[1] for s in y_specs] + [w_spec(w) for w in ws]
        + [gate_spec(0), gate_spec(1), gate_spec(2)],
        out_specs=pl.BlockSpec((tm, tn), lambda i, j: (i, j)),
        out_shape=jax.ShapeDtypeStruct((mp + ms, n), BF16),
        compiler_params=_cparams(("parallel", "arbitrary")),
        name="merge",
    )(*ys_p, *ys_s, *ws, proj, proj, proj)


def _final_norm_kernel(x_ref, g_ref, yp_ref, ys_ref, *, prompt_tiles):
    x = x_ref[...]
    ms = jnp.mean(x * x, axis=-1, keepdims=True)
    y = x * lax.rsqrt(ms + NORM_EPS) * g_ref[...]
    i = pl.program_id(0)

    @pl.when(i < prompt_tiles)
    def _():
        yp_ref[...] = y

    @pl.when(i >= prompt_tiles)
    def _():
        ys_ref[...] = y


def _final_norm(x, g, mp, tm):
    m, k = x.shape
    pt = mp // tm
    psp, ssp = _two_group_specs(tm, k, pt, 0)
    return pl.pallas_call(
        functools.partial(_final_norm_kernel, prompt_tiles=pt),
        grid=(m // tm,),
        in_specs=[pl.BlockSpec((tm, k), lambda i: (i, 0)), pl.BlockSpec((1, k), lambda i: (0, 0))],
        out_specs=[psp, ssp],
        out_shape=[jax.ShapeDtypeStruct((mp, k), F32), jax.ShapeDtypeStruct((m - mp, k), F32)],
        compiler_params=_cparams(("arbitrary",)),
        name="final_norm",
    )(x, g.reshape(1, k))


def _ffn_up_kernel(xn_ref, wg_ref, wv_ref, cwg_ref, cwv_ref, bg_ref, bv_ref, sg_ref, sv_ref,
                   h_ref, pg_ref, pv_ref, og_ref, ov_ref, wgb, wvb, cg, cv, *, tm, rc, prompt_tiles, tiles_per_seq):
    i = pl.program_id(1)
    w1 = FFN_CONV - 1

    @pl.when(i == 0)
    def _():
        wgb[...] = wg_ref[...].astype(BF16)
        wvb[...] = wv_ref[...].astype(BF16)

    def conv(u, s1, s2, cw_ref, b_ref):
        return cw_ref[2:3, :] * u + cw_ref[1:2, :] * s1 + cw_ref[0:1, :] * s2 + b_ref[...]

    def up(rows):
        x = xn_ref[rows, :]
        return _mm(x, wgb[...]), _mm(x, wvb[...])

    @pl.when(i < prompt_tiles)
    def _():
        first = (i % tiles_per_seq) == 0
        pg = jnp.where(first, 0.0, cg[...])
        pv = jnp.where(first, 0.0, cv[...])
        for r in range(tm // rc):
            rows = slice(r * rc, (r + 1) * rc)
            ug, uv = up(rows)
            gate = conv(ug, _shift_chain(ug, 1, pg), _shift_chain(ug, 2, pg), cwg_ref, bg_ref)
            val = conv(uv, _shift_chain(uv, 1, pv), _shift_chain(uv, 2, pv), cwv_ref, bv_ref)
            h_ref[rows, :] = (gate * _sigmoid(gate) * val).astype(BF16)
            pg, pv = ug[rc - SUBLANES:, :], uv[rc - SUBLANES:, :]
        cg[...] = pg
        cv[...] = pv
        pg_ref[0] = pg[SUBLANES - w1:, :]
        pv_ref[0] = pv[SUBLANES - w1:, :]

    @pl.when(i >= prompt_tiles)
    def _():
        nsr = rc // SUBLANES
        for r in range(tm // rc):
            rows = slice(r * rc, (r + 1) * rc)
            seqs = slice(r * nsr, (r + 1) * nsr)
            ug, uv = up(rows)
            sg, sv = sg_ref[seqs], sv_ref[seqs]
            gate = conv(ug, _shift_seq8(ug, 1, sg), _shift_seq8(ug, 2, sg), cwg_ref, bg_ref)
            val = conv(uv, _shift_seq8(uv, 1, sv), _shift_seq8(uv, 2, sv), cwv_ref, bv_ref)
            h_ref[rows, :] = (gate * _sigmoid(gate) * val).astype(BF16)
            og_ref[seqs] = ug.reshape(nsr, SUBLANES, ug.shape[1])[:, SUBLANES - w1:, :]
            ov_ref[seqs] = uv.reshape(nsr, SUBLANES, uv.shape[1])[:, SUBLANES - w1:, :]


def _ffn_up(xn, w_up, conv_w, conv_b, state, layer, nb, seq, ns, tm, tn):
    m, k = xn.shape
    mp = nb * seq
    assert ns * SUBLANES == tm and (m - mp) == tm and seq % tm == 0
    nj, pt, tps = D_FF // tn, mp // tm, seq // tm
    w1 = FFN_CONV - 1
    wsp = lambda off: _layer_spec((k, tn), lambda j, i: (0, off + j), layer)
    par = lambda rows, off: _layer_spec((rows, tn), lambda j, i: (0, off + j), layer)
    stsp = lambda off: _layer_spec((ns, w1, tn), lambda j, i: (0, 0, off + j), layer)
    psp = pl.BlockSpec((1, w1, tn), lambda j, i: (jnp.minimum(i, pt - 1) // tps, 0, j))
    osp = pl.BlockSpec((ns, w1, tn), lambda j, i: (0, 0, j))
    h, pg, pv, og, ov = pl.pallas_call(
        functools.partial(_ffn_up_kernel, tm=tm, rc=256, prompt_tiles=pt, tiles_per_seq=tps),
        grid=(nj, m // tm),
        in_specs=[pl.BlockSpec((tm, k), lambda j, i: (i, 0)), wsp(0), wsp(nj),
                  par(FFN_CONV, 0), par(FFN_CONV, nj), par(1, 0), par(1, nj), stsp(0), stsp(nj)],
        out_specs=[pl.BlockSpec((tm, tn), lambda j, i: (i, j)), psp, psp, osp, osp],
        out_shape=[jax.ShapeDtypeStruct((m, D_FF), BF16),
                   jax.ShapeDtypeStruct((nb, w1, D_FF), F32), jax.ShapeDtypeStruct((nb, w1, D_FF), F32),
                   jax.ShapeDtypeStruct((ns, w1, D_FF), F32), jax.ShapeDtypeStruct((ns, w1, D_FF), F32)],
        scratch_shapes=[pltpu.VMEM((k, tn), BF16), pltpu.VMEM((k, tn), BF16),
                        pltpu.VMEM((SUBLANES, tn), F32), pltpu.VMEM((SUBLANES, tn), F32)],
        compiler_params=_cparams(("arbitrary", "arbitrary")),
        name="ffn_up",
    )(xn, w_up, w_up, conv_w, conv_w, conv_b, conv_b, state, state)
    return h, jnp.concatenate([pg, pv], axis=-1), jnp.concatenate([og, ov], axis=-1)


S5_Q = 4
S5_QW = S5_NSTATE // S5_Q


def _s5_kernel(*refs, chained, tm):
    if chained:
        (u_ref, bre_ref, bim_ref, cre_ref, cim_ref, mre_ref, mim_ref, pre_ref, pim_ref, d_ref, wglu_ref,
         y_ref, xlre_ref, xlim_ref, sre, sim, car_re, car_im) = refs
    else:
        (u_ref, x0re_ref, x0im_ref, bre_ref, bim_ref, cre_ref, cim_ref, mre_ref, mim_ref, pre_ref, pim_ref,
         d_ref, wglu_ref, y_ref, xlre_ref, xlim_ref, sre, sim) = refs

    u = u_ref[...]
    ub = u.astype(BF16)
    for q in range(S5_Q):
        uq = ub[:, LANES * q:LANES * (q + 1)]
        sre[:, S5_QW * q:S5_QW * (q + 1)] = _mm(uq, bre_ref[q])
        sim[:, S5_QW * q:S5_QW * (q + 1)] = _mm(uq, bim_ref[q])

    if chained:
        @pl.when(pl.program_id(1) == 0)
        def _():
            car_re[...] = jnp.zeros_like(car_re)
            car_im[...] = jnp.zeros_like(car_im)

    def blk(i, carry):
        r0 = pl.multiple_of(i * SUBLANES, SUBLANES)
        for q in range(S5_Q):
            sl = slice(S5_QW * q, S5_QW * (q + 1))
            xr = sre[pl.ds(r0, SUBLANES), sl]
            xi = sim[pl.ds(r0, SUBLANES), sl]
            for li, d in enumerate((1, 2, 4)):
                mr, mi = mre_ref[li, :, sl], mim_ref[li, :, sl]
                sr, si = pltpu.roll(xr, d, 0), pltpu.roll(xi, d, 0)
                xr, xi = xr + (mr * sr - mi * si), xi + (mr * si + mi * sr)
            if chained:
                cr, ci = car_re[:, sl], car_im[:, sl]
            else:
                cr, ci = x0re_ref[pl.ds(i, 1), sl], x0im_ref[pl.ds(i, 1), sl]
            pr, pi_ = pre_ref[:, sl], pim_ref[:, sl]
            xr, xi = xr + (pr * cr - pi_ * ci), xi + (pr * ci + pi_ * cr)
            sre[pl.ds(r0, SUBLANES), sl] = xr
            sim[pl.ds(r0, SUBLANES), sl] = xi
            if chained:
                car_re[:, sl] = xr[SUBLANES - 1:SUBLANES, :]
                car_im[:, sl] = xi[SUBLANES - 1:SUBLANES, :]
            else:
                xlre_ref[pl.ds(i, 1), sl] = xr[SUBLANES - 1:SUBLANES, :]
                xlim_ref[pl.ds(i, 1), sl] = xi[SUBLANES - 1:SUBLANES, :]
        return carry

    lax.fori_loop(0, tm // SUBLANES, blk, 0)

    if chained:
        xlre_ref[0] = car_re[...]
        xlim_ref[0] = car_im[...]

    ys = []
    for q in range(S5_Q):
        sl = slice(S5_QW * q, S5_QW * (q + 1))
        ys.append(_mm(sre[:, sl].astype(BF16), cre_ref[q]) - _mm(sim[:, sl].astype(BF16), cim_ref[q]))
    y = jnp.concatenate(ys, axis=1) + d_ref[...] * u
    y = _gelu_tanh(y)
    y = y * _sigmoid(_mm(y.astype(BF16), wglu_ref[...]))
    y_ref[...] = y.astype(BF16)


def _const_spec(shape):
    zeros = (0,) * len(shape)
    return pl.BlockSpec(shape, lambda *a: zeros)


def _s5_param_specs():
    return [_const_spec((S5_Q, LANES, S5_QW)), _const_spec((S5_Q, LANES, S5_QW)),
            _const_spec((S5_Q, S5_QW, LANES)), _const_spec((S5_Q, S5_QW, LANES)),
            _const_spec((3, SUBLANES, S5_NSTATE)), _const_spec((3, SUBLANES, S5_NSTATE)),
            _const_spec((SUBLANES, S5_NSTATE)), _const_spec((SUBLANES, S5_NSTATE)),
            _const_spec((1, S5_WIDTH)), _const_spec((S5_WIDTH, S5_WIDTH))]


def _s5_prompt(proj, sp, nb, seq, tm):
    nt = seq // tm
    cb = P_S5 // S5_WIDTH
    y, xre, xim = pl.pallas_call(
        functools.partial(_s5_kernel, chained=True, tm=tm),
        grid=(nb, nt),
        in_specs=[pl.BlockSpec((tm, S5_WIDTH), lambda b, t: (b * nt + t, cb))] + _s5_param_specs(),
        out_specs=[pl.BlockSpec((tm, S5_WIDTH), lambda b, t: (b * nt + t, 0)),
                   pl.BlockSpec((1, 1, S5_NSTATE), lambda b, t: (b, 0, 0)),
                   pl.BlockSpec((1, 1, S5_NSTATE), lambda b, t: (b, 0, 0))],
        out_shape=[jax.ShapeDtypeStruct((nb * seq, S5_WIDTH), BF16),
                   jax.ShapeDtypeStruct((nb, 1, S5_NSTATE), F32),
                   jax.ShapeDtypeStruct((nb, 1, S5_NSTATE), F32)],
        scratch_shapes=[pltpu.VMEM((tm, S5_NSTATE), F32), pltpu.VMEM((tm, S5_NSTATE), F32),
                        pltpu.VMEM((1, S5_NSTATE), F32), pltpu.VMEM((1, S5_NSTATE), F32)],
        compiler_params=_cparams(("parallel", "arbitrary")),
        name="s5_prompt",
    )(proj, *sp)
    return y, xre.reshape(nb, S5_GROUPS, S5_STATE), xim.reshape(nb, S5_GROUPS, S5_STATE)


def _s5_sample(proj, x0re, x0im, sp, row0, nseq, tm):
    ns = tm // SUBLANES
    cb = P_S5 // S5_WIDTH
    r0 = row0 // tm
    st = pl.BlockSpec((ns, S5_NSTATE), lambda i: (i, 0))
    y, xre, xim = pl.pallas_call(
        functools.partial(_s5_kernel, chained=False, tm=tm),
        grid=(nseq // ns,),
        in_specs=[pl.BlockSpec((tm, S5_WIDTH), lambda i: (r0 + i, cb)), st, st] + _s5_param_specs(),
        out_specs=[pl.BlockSpec((tm, S5_WIDTH), lambda i: (i, 0)), st, st],
        out_shape=[jax.ShapeDtypeStruct((nseq * SUBLANES, S5_WIDTH), BF16),
                   jax.ShapeDtypeStruct((nseq, S5_NSTATE), F32),
                   jax.ShapeDtypeStruct((nseq, S5_NSTATE), F32)],
        scratch_shapes=[pltpu.VMEM((tm, S5_NSTATE), F32), pltpu.VMEM((tm, S5_NSTATE), F32)],
        compiler_params=_cparams(("parallel",)),
        name="s5_sample",
    )(proj, x0re.reshape(nseq, S5_NSTATE), x0im.reshape(nseq, S5_NSTATE), *sp)
    return y, xre.reshape(nseq, S5_GROUPS, S5_STATE), xim.reshape(nseq, S5_GROUPS, S5_STATE)


def _s5_params(lam_re, lam_im, b_re, b_im, c_re, c_im, d, log_step, w_glu):
    lam = lax.complex(lam_re, lam_im)
    step = jnp.exp(log_step)[:, None]
    lam_bar = jnp.exp(lam * step)
    b_bar = ((lam_bar - 1.0) / lam)[..., None] * lax.complex(b_re, b_im)

    def bblk(bm):
        bm = bm.reshape(S5_Q, 8, S5_STATE, S5_GROUP)
        eye = jnp.eye(8, dtype=F32)
        out = jnp.einsum('qgpc,gh->qgchp', bm, eye)
        return out.reshape(S5_Q, LANES, S5_QW).astype(BF16)

    def cblk(cm):
        cm = cm.reshape(S5_Q, 8, S5_GROUP, S5_STATE)
        eye = jnp.eye(8, dtype=F32)
        out = jnp.einsum('qgcp,gh->qgphc', cm, eye)
        return out.reshape(S5_Q, S5_QW, LANES).astype(BF16)

    lam_flat = lam_bar.reshape(1, S5_NSTATE)
    pows = [lam_flat]
    for _ in range(SUBLANES - 1):
        pows.append(pows[-1] * lam_flat)
    row = jnp.arange(SUBLANES)[:, None]
    m = jnp.stack([jnp.where(row >= dd, pows[dd - 1], 0.0) for dd in (1, 2, 4)])
    p = jnp.concatenate(pows, axis=0)
    return (bblk(b_bar.real), bblk(b_bar.imag), cblk(c_re), cblk(c_im),
            m.real, m.imag, p.real, p.imag, d.reshape(1, S5_WIDTH), w_glu.astype(BF16))


def _seg64_sum(x):
    outs = []
    for t in range(x.shape[1] // LANES):
        xt = x[:, LANES * t:LANES * (t + 1)]
        lo = lax.broadcasted_iota(jnp.int32, xt.shape, 1) < RW_HD
        s_lo = jnp.sum(jnp.where(lo, xt, 0.0), axis=-1, keepdims=True)
        s_hi = jnp.sum(jnp.where(lo, 0.0, xt), axis=-1, keepdims=True)
        outs.append(jnp.where(lo, s_lo, s_hi))
    return jnp.concatenate(outs, axis=1)


def _rwkv_kernel(*refs, per_seq, tm):
    if per_seq:
        (h_ref, sh_ref, si_ref, mu_ref, w0_ref, w2_ref, a0_ref, a2_ref, g2_ref, kk_ref, ka_ref, rk_ref,
         lnw_ref, lnb_ref, y_ref, so_ref, r_s, k_s, v_s, a_s, b_s, lw_s, y_s) = refs
    else:
        (h_ref, mu_ref, w0_ref, w2_ref, a0_ref, a2_ref, g2_ref, kk_ref, ka_ref, rk_ref, lnw_ref, lnb_ref,
         y_ref, so_ref, r_s, k_s, v_s, a_s, b_s, lw_s, y_s, prev_s, st_s) = refs
    c, w, hd_w = CHUNK, RW_WIDTH, RW_HD

    h = h_ref[...]
    if per_seq:
        prev = _shift_seq8(h, 1, sh_ref[...])
    else:
        @pl.when(pl.program_id(1) == 0)
        def _():
            prev_s[...] = jnp.zeros_like(prev_s)
            st_s[...] = jnp.zeros_like(st_s)
        prev = _shift_chain(h, 1, prev_s[...])
        prev_s[...] = h[tm - SUBLANES:, :]
    hs = h + (prev - h) * mu_ref[...]
    r = hs[:, 0:w]
    k = hs[:, w:2 * w]
    v = hs[:, 2 * w:3 * w]
    wd = hs[:, 3 * w:3 * w + LANES]
    ad = hs[:, 3 * w + LANES:3 * w + 2 * LANES]
    gd = hs[:, 3 * w + 2 * LANES:]
    w_log = -_softplus(-(w0_ref[...] + _mm(jnp.tanh(wd).astype(BF16), w2_ref[...]))) - 0.5
    lw_s[...] = -jnp.exp(w_log)
    a_ic = _sigmoid(a0_ref[...] + _mm(ad.astype(BF16), a2_ref[...]))
    gate = _mm(_sigmoid(gd).astype(BF16), g2_ref[...])
    kx = k * kk_ref[...]
    kkn = kx * lax.rsqrt(_seg64_sum(kx * kx) + 1e-6)
    k2 = k * (1.0 + (a_ic - 1.0) * ka_ref[...])
    r_s[...] = r
    k_s[...] = k2
    v_s[...] = v
    a_s[...] = -kkn
    b_s[...] = kkn * a_ic

    strict, causal = _chunk_masks(per_seq)
    if per_seq:
        big_mask = _seq_block_mask(2 * c, hd_w)

    heads = range(RW_HEADS)
    sls = [slice(hd_w * hd, hd_w * (hd + 1)) for hd in heads]
    group = _chunks_per_iteration(tm // c)

    def chunk_group(gidx, carry):
        pre = []
        for t in range(group):
            idx = gidx * group + t
            rows = pl.ds(pl.multiple_of(idx * c, c), c)
            lw = lw_s[rows, :]
            cum = _cumsum_rows(lw, groupwise=per_seq)
            cl = _group_last(cum) if per_seq else cum[c - 1:c, :]
            e_pos, e_neg, e_end = jnp.exp(cum), jnp.exp(-cum), jnp.exp(cl - cum)
            rr, kc, vc, ac, bc = r_s[rows, :], k_s[rows, :], v_s[rows, :], a_s[rows, :], b_s[rows, :]
            pre.append(dict(rows=rows, seqs=pl.ds(idx * SEQ_PER_CHUNK, SEQ_PER_CHUNK), vc=vc,
                            rt=rr * e_pos, kt=kc * e_neg, bt=bc * e_neg, at=ac * jnp.exp(cum - lw),
                            kh=kc * e_end, bh=bc * e_end, wc=jnp.exp(cl)))
        pairs = [(t, hd) for t in range(group) for hd in heads]
        ath = [pre[t]['at'][:, sls[hd]] for t, hd in pairs]
        vh = [pre[t]['vc'][:, sls[hd]] for t, hd in pairs]
        rth = [pre[t]['rt'][:, sls[hd]] for t, hd in pairs]
        ab = [_mm_nt(jnp.concatenate([ath[p], rth[p]], axis=0).astype(BF16),
                     jnp.concatenate([pre[t]['kt'][:, sls[hd]], pre[t]['bt'][:, sls[hd]]], axis=0).astype(BF16))
              for p, (t, hd) in enumerate(pairs)]
        a_ak = [jnp.where(strict, m[:c, :c], 0.0) for m in ab]
        n_ab = [jnp.where(strict, -m[:c, c:], 0.0) for m in ab]
        a_r = [jnp.concatenate([jnp.where(causal, m[c:, :c], 0.0), jnp.where(causal, m[c:, c:], 0.0)],
                               axis=1).astype(BF16) for m in ab]
        akv = [_mm(a_ak[p].astype(BF16), vh[p].astype(BF16)) for p in range(len(pairs))]
        sol = _solve_unit_lower_each(
            n_ab, [jnp.concatenate([ath[p], akv[p]], axis=1) for p in range(len(pairs))], per_seq)
        kb = [jnp.concatenate([pre[t]['kh'][:, sls[hd]], pre[t]['bh'][:, sls[hd]]], axis=0).astype(BF16)
              for t, hd in pairs]
        for t in range(group):
            rows, seqs, wc = pre[t]['rows'], pre[t]['seqs'], pre[t]['wc']
            ps = [t * RW_HEADS + hd for hd in heads]
            if per_seq:
                st = [si_ref[seqs, hd].reshape(SEQ_PER_CHUNK * hd_w, hd_w) for hd in heads]
            else:
                st = [st_s[hd] for hd in heads]
            pr = [_mm_nt(jnp.concatenate([sol[p][:, :hd_w], rth[p]], axis=0).astype(BF16), st[hd].astype(BF16))
                  for hd, p in zip(heads, ps)]
            if per_seq:
                u = [_pick_own_seq(pr[hd][:c], hd_w) + sol[p][:, hd_w:] for hd, p in zip(heads, ps)]
                yst = [_pick_own_seq(pr[hd][c:], hd_w) for hd in heads]
            else:
                u = [pr[hd][:c] + sol[p][:, hd_w:] for hd, p in zip(heads, ps)]
                yst = [pr[hd][c:] for hd in heads]
            vu = [jnp.concatenate([vh[p], u[hd]], axis=0) for hd, p in zip(heads, ps)]
            ya = [_mm(a_r[p], vu[hd].astype(BF16)) for hd, p in zip(heads, ps)]
            if per_seq:
                upd = [_mm_tn(jnp.where(big_mask, jnp.concatenate([vu[hd]] * SEQ_PER_CHUNK, axis=1), 0.0
                                        ).astype(BF16), kb[p]) for hd, p in zip(heads, ps)]
            else:
                upd = [_mm_tn(vu[hd].astype(BF16), kb[p]) for hd, p in zip(heads, ps)]
            for hd in heads:
                if per_seq:
                    wcb = jnp.concatenate(
                        [jnp.broadcast_to(wc[SUBLANES * s:SUBLANES * s + 1, sls[hd]], (hd_w, hd_w))
                         for s in range(SEQ_PER_CHUNK)], axis=0)
                    so_ref[seqs, hd] = (st[hd] * wcb + upd[hd]).reshape(SEQ_PER_CHUNK, hd_w, hd_w)
                else:
                    st_s[hd] = st[hd] * wc[:, sls[hd]] + upd[hd]
                y_s[rows, sls[hd]] = yst[hd] + ya[hd]
        return carry

    lax.fori_loop(0, tm // (c * group), chunk_group, 0)

    if not per_seq:
        so_ref[0] = st_s[...]

    y = y_s[...]
    mean = _seg64_sum(y) * (1.0 / hd_w)
    yc = y - mean
    var = _seg64_sum(yc * yc) * (1.0 / hd_w)
    yn = yc * lax.rsqrt(var + RW_GN_EPS) * lnw_ref[...] + lnb_ref[...]
    bonus = _seg64_sum(r_s[...] * k_s[...] * rk_ref[...]) * v_s[...]
    y_ref[...] = ((yn + bonus) * gate).astype(BF16)


def _rwkv_param_specs():
    row = _const_spec((1, RW_WIDTH))
    return [_const_spec((1, RW_PCOLS)), row, _const_spec((LANES, RW_WIDTH)), row, _const_spec((LANES, RW_WIDTH)),
            _const_spec((RW_GATE, RW_WIDTH)), row, row, row, row, row]


def _rwkv_scratch(tm):
    return [pltpu.VMEM((tm, RW_WIDTH), F32) for _ in range(7)]


def _rwkv_prompt(proj, rp, nb, seq, tm):
    nt = seq // tm
    cb = P_RW // RW_PCOLS
    y, st = pl.pallas_call(
        functools.partial(_rwkv_kernel, per_seq=False, tm=tm),
        grid=(nb, nt),
        in_specs=[pl.BlockSpec((tm, RW_PCOLS), lambda b, t: (b * nt + t, cb))] + _rwkv_param_specs(),
        out_specs=[pl.BlockSpec((tm, RW_WIDTH), lambda b, t: (b * nt + t, 0)),
                   pl.BlockSpec((1, RW_HEADS, RW_HD, RW_HD), lambda b, t: (b, 0, 0, 0))],
        out_shape=[jax.ShapeDtypeStruct((nb * seq, RW_WIDTH), BF16),
                   jax.ShapeDtypeStruct((nb, RW_HEADS, RW_HD, RW_HD), F32)],
        scratch_shapes=_rwkv_scratch(tm) + [pltpu.VMEM((SUBLANES, RW_PCOLS), F32),
                                            pltpu.VMEM((RW_HEADS, RW_HD, RW_HD), F32)],
        compiler_params=_cparams(("parallel", "arbitrary")),
        name="rwkv_prompt",
    )(proj, *rp)
    return y, st


def _rwkv_sample(proj, shift0, st0, layer, rp, row0, nseq, tm):
    ns = tm // SUBLANES
    cb = P_RW // RW_PCOLS
    r0 = row0 // tm
    st_spec = pl.BlockSpec((ns, RW_HEADS, RW_HD, RW_HD), lambda i: (i, 0, 0, 0))
    st_in = _layer_spec((ns, RW_HEADS, RW_HD, RW_HD), lambda i: (i, 0, 0, 0), layer)
    y, st = pl.pallas_call(
        functools.partial(_rwkv_kernel, per_seq=True, tm=tm),
        grid=(nseq // ns,),
        in_specs=[pl.BlockSpec((tm, RW_PCOLS), lambda i: (r0 + i, cb)),
                  pl.BlockSpec((ns, 1, RW_PCOLS), lambda i: (i, 0, 0)), st_in] + _rwkv_param_specs(),
        out_specs=[pl.BlockSpec((tm, RW_WIDTH), lambda i: (i, 0)), st_spec],
        out_shape=[jax.ShapeDtypeStruct((nseq * SUBLANES, RW_WIDTH), BF16),
                   jax.ShapeDtypeStruct((nseq, RW_HEADS, RW_HD, RW_HD), F32)],
        scratch_shapes=_rwkv_scratch(tm),
        compiler_params=_cparams(("parallel",)),
        name="rwkv_sample",
    )(proj, shift0, st0, *rp)
    return y, st


def _rw_pad_cols(x):
    z = jnp.zeros(x.shape[:-1] + (LANES - RW_LORA,), x.dtype)
    a, b = 3 * RW_WIDTH, 3 * RW_WIDTH + RW_LORA
    return jnp.concatenate([x[..., :a], x[..., a:b], z, x[..., b:b + RW_LORA], z, x[..., b + RW_LORA:]], axis=-1)


def _rw_unpad_cols(x):
    a = 3 * RW_WIDTH
    return jnp.concatenate([x[..., :a + RW_LORA], x[..., a + LANES:a + LANES + RW_LORA], x[..., a + 2 * LANES:]],
                           axis=-1)


def _rwkv_params(mu, w0, w2, a0, a2, g2, k_k, k_a, r_k, ln_w, ln_b):
    row = lambda t: t.reshape(1, RW_WIDTH)
    padr = lambda t: jnp.concatenate([t, jnp.zeros((LANES - RW_LORA, RW_WIDTH), t.dtype)], axis=0).astype(BF16)
    return (_rw_pad_cols(mu).reshape(1, RW_PCOLS), row(w0), padr(w2), row(a0), padr(a2), g2.astype(BF16),
            row(k_k), row(k_a), row(r_k), row(ln_w), row(ln_b))


def _gdn_kernel(*refs, per_seq, tm):
    if per_seq:
        (q_ref, k_ref, v_ref, z_ref, ba_ref, fq_ref, fk_ref, fv_ref, si_ref, cwq_ref, cwk_ref, cwv_ref,
         al_ref, dt_ref, ng_ref, y_ref, so_ref, q_s, k_s, v_s, ba_s, o_s) = refs
        conv_state = (fq_ref, fk_ref, fv_ref)
    else:
        (q_ref, k_ref, v_ref, z_ref, ba_ref, cwq_ref, cwk_ref, cwv_ref, al_ref, dt_ref, ng_ref,
         y_ref, so_ref, q_s, k_s, v_s, ba_s, o_s, pq_s, pk_s, pv_s, st_s) = refs
        prevs = (pq_s, pk_s, pv_s)

        @pl.when(pl.program_id(1) == 0)
        def _():
            for p in prevs:
                p[...] = jnp.zeros_like(p)
            st_s[...] = jnp.zeros_like(st_s)
    c, hw = CHUNK, GD_HD

    def conv_act(idx, u_ref, cw_ref):
        u = u_ref[...]
        if per_seq:
            f = conv_state[idx][...]
            sh = [_shift_seq8(u, d, f) for d in (1, 2, 3)]
        else:
            p = prevs[idx][...]
            sh = [_shift_chain(u, d, p) for d in (1, 2, 3)]
            prevs[idx][...] = u[tm - SUBLANES:, :]
        x = cw_ref[3:4, :] * u + cw_ref[2:3, :] * sh[0] + cw_ref[1:2, :] * sh[1] + cw_ref[0:1, :] * sh[2]
        return x * _sigmoid(x)

    def l2n(x):
        outs = []
        for hd in range(GD_HEADS):
            xh = x[:, hw * hd:hw * (hd + 1)]
            outs.append(xh * lax.rsqrt(jnp.sum(xh * xh, axis=-1, keepdims=True) + 1e-6))
        return jnp.concatenate(outs, axis=1)

    q_s[...] = l2n(conv_act(0, q_ref, cwq_ref)) * (hw ** -0.5)
    k_s[...] = l2n(conv_act(1, k_ref, cwk_ref))
    v_s[...] = conv_act(2, v_ref, cwv_ref)
    ba = ba_ref[...]
    lane = lax.broadcasted_iota(jnp.int32, ba.shape, 1)
    g_all = -jnp.exp(al_ref[...]) * _softplus(ba + dt_ref[...])
    ba_s[...] = jnp.where(lane < GD_HEADS, _sigmoid(ba), g_all)

    strict, causal = _chunk_masks(per_seq)
    if per_seq:
        big_mask = _seq_block_mask(c, hw)

    heads = range(GD_HEADS)
    sls = [slice(hw * hd, hw * (hd + 1)) for hd in heads]
    group = _chunks_per_iteration(tm // c)

    def chunk_group(gidx, carry):
        pre = []
        for t in range(group):
            idx = gidx * group + t
            rows = pl.ds(pl.multiple_of(idx * c, c), c)
            bg = ba_s[rows, :]
            gc_all = _cumsum_rows(bg, groupwise=per_seq)
            pre.append(dict(rows=rows, seqs=pl.ds(idx * SEQ_PER_CHUNK, SEQ_PER_CHUNK), bg=bg, gc_all=gc_all,
                            gc_t=gc_all.T, gl_all=_group_last(gc_all) if per_seq else gc_all[c - 1:c, :],
                            qa=q_s[rows, :], ka=k_s[rows, :], va=v_s[rows, :]))
        pairs = [(t, hd) for t in range(group) for hd in heads]
        npair = len(pairs)
        beta = [pre[t]['bg'][:, hd:hd + 1] for t, hd in pairs]
        gc = [pre[t]['gc_all'][:, GD_HEADS + hd:GD_HEADS + hd + 1] for t, hd in pairs]
        gl = [pre[t]['gl_all'][:, GD_HEADS + hd:GD_HEADS + hd + 1] for t, hd in pairs]
        dec = [jnp.exp(jnp.where(causal, gc[p] - pre[t]['gc_t'][GD_HEADS + hd:GD_HEADS + hd + 1, :], -1e30))
               for p, (t, hd) in enumerate(pairs)]
        qh = [pre[t]['qa'][:, sls[hd]] for t, hd in pairs]
        kh = [pre[t]['ka'][:, sls[hd]] for t, hd in pairs]
        kb = [kh[p] * beta[p] for p in range(npair)]
        eg = [jnp.exp(g) for g in gc]
        kq = [_mm_nt(jnp.concatenate([kb[p], qh[p]], axis=0).astype(BF16), kh[p].astype(BF16))
              for p in range(npair)]
        low = [jnp.where(strict, kq[p][:c] * dec[p], 0.0) for p in range(npair)]
        intra = [(kq[p][c:] * dec[p]).astype(BF16) for p in range(npair)]
        sol = _solve_unit_lower_each(
            low, [jnp.concatenate([pre[t]['va'][:, sls[hd]] * beta[p], kb[p] * eg[p]], axis=1)
                  for p, (t, hd) in enumerate(pairs)], per_seq)
        wq = [jnp.concatenate([sol[p][:, hw:], qh[p] * eg[p]], axis=0).astype(BF16) for p in range(npair)]
        kdec = [kh[p] * jnp.exp(gl[p] - gc[p]) for p in range(npair)]
        for t in range(group):
            rows, seqs = pre[t]['rows'], pre[t]['seqs']
            ps = [t * GD_HEADS + hd for hd in heads]
            if per_seq:
                st3 = [si_ref[seqs, hd] for hd in heads]
                st_rhs = [jnp.concatenate([s3[s] for s in range(SEQ_PER_CHUNK)], axis=1).astype(BF16)
                          for s3 in st3]
            else:
                st = [st_s[hd] for hd in heads]
                st_rhs = [s.astype(BF16) for s in st]
            ws = [_mm(wq[p], st_rhs[hd]) for hd, p in zip(heads, ps)]
            if per_seq:
                v_new = [sol[p][:, :hw] - _pick_own_seq(ws[hd][:c], hw) for hd, p in zip(heads, ps)]
                o_st = [_pick_own_seq(ws[hd][c:], hw) for hd in heads]
            else:
                v_new = [sol[p][:, :hw] - ws[hd][:c] for hd, p in zip(heads, ps)]
                o_st = [ws[hd][c:] for hd in heads]
            vnb = [v.astype(BF16) for v in v_new]
            o_in = [_mm(intra[p], vnb[hd]) for hd, p in zip(heads, ps)]
            if per_seq:
                upd = [_mm_tn(jnp.where(big_mask, jnp.concatenate([kdec[p]] * SEQ_PER_CHUNK, axis=1), 0.0
                                        ).astype(BF16), vnb[hd]) for hd, p in zip(heads, ps)]
            else:
                upd = [_mm_tn(kdec[p].astype(BF16), vnb[hd]) for hd, p in zip(heads, ps)]
            for hd, p in zip(heads, ps):
                if per_seq:
                    egl = jnp.exp(gl[p])
                    mult = jnp.concatenate([jnp.broadcast_to(egl[SUBLANES * s:SUBLANES * s + 1, :], (hw, hw))
                                            for s in range(SEQ_PER_CHUNK)], axis=0)
                    st_new = st3[hd].reshape(SEQ_PER_CHUNK * hw, hw) * mult + upd[hd]
                    so_ref[seqs, hd] = st_new.reshape(SEQ_PER_CHUNK, hw, hw)
                else:
                    st_s[hd] = st[hd] * jnp.exp(gl[p]) + upd[hd]
                o_s[rows, sls[hd]] = o_st[hd] + o_in[hd]
        return carry

    lax.fori_loop(0, tm // (c * group), chunk_group, 0)

    if not per_seq:
        so_ref[0] = st_s[...]

    outs = []
    for hd in range(GD_HEADS):
        sl = slice(hw * hd, hw * (hd + 1))
        o = o_s[:, sl]
        o = o * lax.rsqrt(jnp.mean(o * o, axis=-1, keepdims=True) + NORM_EPS) * ng_ref[...]
        z = z_ref[:, sl]
        outs.append(o * (z * _sigmoid(z)))
    y_ref[...] = jnp.concatenate(outs, axis=1).astype(BF16)


def _gdn_param_specs():
    c2 = lambda shape, j=0: pl.BlockSpec(shape, (lambda *a: (0, j)))
    return [c2((GD_CONV, GD_WIDTH), 0), c2((GD_CONV, GD_WIDTH), 1), c2((GD_CONV, GD_WIDTH), 2),
            c2((1, LANES)), c2((1, LANES)), c2((1, GD_HD))]


def _gdn_scratch(tm):
    return ([pltpu.VMEM((tm, GD_WIDTH), F32) for _ in range(3)] + [pltpu.VMEM((tm, LANES), F32),
                                                                  pltpu.VMEM((tm, GD_WIDTH), F32)])


def _gdn_prompt(proj, gp, nb, seq, tm):
    nt = seq // tm
    qb = P_Q // GD_WIDTH
    blk = lambda j: pl.BlockSpec((tm, GD_WIDTH), lambda b, t: (b * nt + t, j))
    y, st = pl.pallas_call(
        functools.partial(_gdn_kernel, per_seq=False, tm=tm),
        grid=(nb, nt),
        in_specs=[blk(qb), blk(qb + 1), blk(qb + 2), blk(P_Z // GD_WIDTH),
                  pl.BlockSpec((tm, LANES), lambda b, t: (b * nt + t, P_BA // LANES))] + _gdn_param_specs(),
        out_specs=[pl.BlockSpec((tm, GD_WIDTH), lambda b, t: (b * nt + t, 0)),
                   pl.BlockSpec((1, GD_HEADS, GD_HD, GD_HD), lambda b, t: (b, 0, 0, 0))],
        out_shape=[jax.ShapeDtypeStruct((nb * seq, GD_WIDTH), BF16),
                   jax.ShapeDtypeStruct((nb, GD_HEADS, GD_HD, GD_HD), F32)],
        scratch_shapes=_gdn_scratch(tm) + [pltpu.VMEM((SUBLANES, GD_WIDTH), F32) for _ in range(3)]
        + [pltpu.VMEM((GD_HEADS, GD_HD, GD_HD), F32)],
        compiler_params=_cparams(("parallel", "arbitrary")),
        name="gdn_prompt",
    )(proj, proj, proj, proj, proj, gp[0], gp[0], gp[0], *gp[1:])
    return y, st


def _gdn_sample(proj, conv0, st0, layer, gp, row0, nseq, tm):
    ns = tm // SUBLANES
    qb = P_Q // GD_WIDTH
    r0 = row0 // tm
    blk = lambda j: pl.BlockSpec((tm, GD_WIDTH), lambda i: (r0 + i, j))
    frm = lambda j: _layer_spec((ns, GD_CONV - 1, GD_WIDTH), lambda i: (i, 0, j), layer)
    st_spec = pl.BlockSpec((ns, GD_HEADS, GD_HD, GD_HD), lambda i: (i, 0, 0, 0))
    st_in = _layer_spec((ns, GD_HEADS, GD_HD, GD_HD), lambda i: (i, 0, 0, 0), layer)
    y, st = pl.pallas_call(
        functools.partial(_gdn_kernel, per_seq=True, tm=tm),
        grid=(nseq // ns,),
        in_specs=[blk(qb), blk(qb + 1), blk(qb + 2), blk(P_Z // GD_WIDTH),
                  pl.BlockSpec((tm, LANES), lambda i: (r0 + i, P_BA // LANES)),
                  frm(0), frm(1), frm(2), st_in] + _gdn_param_specs(),
        out_specs=[pl.BlockSpec((tm, GD_WIDTH), lambda i: (i, 0)), st_spec],
        out_shape=[jax.ShapeDtypeStruct((nseq * SUBLANES, GD_WIDTH), BF16),
                   jax.ShapeDtypeStruct((nseq, GD_HEADS, GD_HD, GD_HD), F32)],
        scratch_shapes=_gdn_scratch(tm),
        compiler_params=_cparams(("parallel",)),
        name="gdn_sample",
    )(proj, proj, proj, proj, proj, conv0, conv0, conv0, st0, gp[0], gp[0], gp[0], *gp[1:])
    return y, st


def _gdn_params(conv_w, a_log, dt_bias, norm_g):
    pad = lambda t: jnp.concatenate([jnp.zeros((GD_HEADS,), F32), t, jnp.zeros((LANES - 2 * GD_HEADS,), F32)]
                                    ).reshape(1, LANES)
    return (conv_w, pad(a_log), pad(dt_bias), norm_g.reshape(1, GD_HD))


def _permute_w_in(w):
    o_rw, o_qkv = S5_WIDTH, S5_WIDTH + RW_COLS
    o_z = o_qkv + 3 * GD_WIDTH
    o_b = o_z + GD_WIDTH
    o_g = o_b + 2 * GD_HEADS
    w = w.astype(BF16)
    parts = [w[..., o_g:], _rw_pad_cols(w[..., o_rw:o_qkv]), w[..., o_qkv:o_z], w[..., o_z:o_b], w[..., :S5_WIDTH],
             w[..., o_b:o_g], jnp.zeros(w.shape[:-1] + (P_COLS - P_BA - 2 * GD_HEADS,), w.dtype)]
    return jnp.concatenate(parts, axis=-1)


def _tail_rows(a, cols, nrows, mp, seq):
    a3 = a.reshape(a.shape[0] // SUBLANES, SUBLANES, a.shape[1])
    g = seq // SUBLANES
    lo = SUBLANES - nrows
    return (a3[g - 1:mp // SUBLANES:g, lo:, cols[0]:cols[1]], a3[mp // SUBLANES:, lo:, cols[0]:cols[1]])


def kernel(x_prompt, x_sample, state_s5_re, state_s5_im, state_rwkv_shift, state_rwkv_wkv, state_gdn_conv, state_gdn, state_ffn_conv, norm1_g, norm2_g, final_norm_g, w_in, s5_lambda_re, s5_lambda_im, s5_b_re, s5_b_im, s5_c_re, s5_c_im, s5_d, s5_log_step, s5_w_glu, rwkv_mu, rwkv_w0, rwkv_w2, rwkv_a0, rwkv_a2, rwkv_g2, rwkv_k_k, rwkv_k_a, rwkv_r_k, rwkv_ln_w, rwkv_ln_b, gdn_conv_w, gdn_a_log, gdn_dt_bias, gdn_norm_g, w_br_s5, w_br_rwkv, w_br_gdn, w_out, ffn_w_up, ffn_conv_w, ffn_conv_b, ffn_w_down):
    nb, seq, d = x_prompt.shape
    ns, sl, _ = x_sample.shape
    assert sl == SUBLANES and d == D_MODEL
    mp, ms = nb * seq, ns * sl
    m = mp + ms
    tm_proj, tm_out = 1024, 512
    x_groups = (x_prompt.reshape(mp, d), x_sample.reshape(ms, d), 0)

    w_in_p = _permute_w_in(w_in)
    w_br = (w_br_s5.astype(BF16), w_br_rwkv.astype(BF16), w_br_gdn.astype(BF16))
    w_out_b, w_down_b = w_out.astype(BF16), ffn_w_down.astype(BF16)
    conv_b = ffn_conv_b.reshape(DEPTH, 1, 2 * D_FF)
    shift0 = _rw_pad_cols(state_rwkv_shift)[:, :, None, :]

    new_p = [[] for _ in range(7)]
    new_s = [[] for _ in range(7)]
    for l in range(DEPTH):
        xp, xs, s_off = x_groups
        proj = _norm_matmul(xp, xs, s_off * (mp // tm_proj), m, mp // tm_proj, norm1_g[l], w_in_p, l,
                            tm=tm_proj, tn=1024)

        sp = _s5_params(s5_lambda_re[l], s5_lambda_im[l], s5_b_re[l], s5_b_im[l], s5_c_re[l], s5_c_im[l],
                        s5_d[l], s5_log_step[l], s5_w_glu[l])
        ys5_p, p_re, p_im = _s5_prompt(proj, sp, nb, seq, tm=256)
        ys5_s, s_re, s_im = _s5_sample(proj, state_s5_re[l], state_s5_im[l], sp, mp, ns, tm=128)

        rp = _rwkv_params(rwkv_mu[l], rwkv_w0[l], rwkv_w2[l], rwkv_a0[l], rwkv_a2[l], rwkv_g2[l], rwkv_k_k[l],
                          rwkv_k_a[l], rwkv_r_k[l].reshape(RW_WIDTH), rwkv_ln_w[l], rwkv_ln_b[l])
        yrw_p, p_wkv = _rwkv_prompt(proj, rp, nb, seq, tm=256)
        yrw_s, s_wkv = _rwkv_sample(proj, shift0[l], state_rwkv_wkv, l, rp, mp, ns, tm=128)
        p_shift, s_shift = [_rw_unpad_cols(t[:, 0]) for t in _tail_rows(proj, (P_RW, P_RW + RW_PCOLS), 1, mp, seq)]

        gp = _gdn_params(gdn_conv_w[l], gdn_a_log[l], gdn_dt_bias[l], gdn_norm_g[l])
        ygd_p, p_gdn = _gdn_prompt(proj, gp, nb, seq, tm=256)
        ygd_s, s_gdn = _gdn_sample(proj, state_gdn_conv, state_gdn, l, gp, mp, ns, tm=64)
        p_gconv, s_gconv = _tail_rows(proj, (P_Q, P_Q + 3 * GD_WIDTH), GD_CONV - 1, mp, seq)

        merged = _merge((ys5_p, yrw_p, ygd_p), (ys5_s, yrw_s, ygd_s), w_br, l, proj, tm=512, tn=512)
        x, xn = _out_norm(merged, w_out_b, l, xp, xs, s_off * (mp // tm_out), mp // tm_out, norm2_g[l], tm=tm_out)

        h, p_ffn, s_ffn = _ffn_up(xn, ffn_w_up, ffn_conv_w, conv_b, state_ffn_conv, l, nb, seq, ns,
                                  tm=1024, tn=512)
        x = _matmul_res(h, w_down_b, l, x, tm=1024, tn=512)
        x_groups = (x, x, 1)

        for lst, vals in ((new_p, (p_re, p_im, p_shift, p_wkv, p_gconv, p_gdn, p_ffn)),
                          (new_s, (s_re, s_im, s_shift, s_wkv, s_gconv, s_gdn, s_ffn))):
            for acc, val in zip(lst, vals):
                acc.append(val)

    y_p, y_s = _final_norm(x, final_norm_g, mp, tm=512)
    stack = lambda lst: tuple(jnp.stack(v) for v in lst)
    return (y_p.reshape(nb, seq, d), y_s.reshape(ns, sl, d)) + stack(new_p) + stack(new_s)
```

```python
import functools
import math

import jax
import jax.numpy as jnp
from jax import lax
from jax.experimental import pallas as pl
from jax.experimental.pallas import tpu as pltpu

F32 = jnp.float32
BF16 = jnp.bfloat16

SUBLANES = 8
LANES = 128
VMEM_LIMIT = 52 * 1024 * 1024
CHUNK = 64
SEQ_PER_CHUNK = CHUNK // SUBLANES

D_MODEL = 2048
DEPTH = 2
S5_WIDTH = 512
S5_GROUP = 16
S5_GROUPS = 32
S5_STATE = 64
S5_NSTATE = S5_GROUPS * S5_STATE
RW_WIDTH = 512
RW_HEADS = 8
RW_HD = 64
RW_LORA = 96
RW_GATE = 256
RW_COLS = 3 * RW_WIDTH + 2 * RW_LORA + RW_GATE
RW_PCOLS = 2048
RW_GN_EPS = 64e-5
GD_WIDTH = 1024
GD_HEADS = 8
GD_HD = 128
GD_CONV = 4
D_FF = 5632
FFN_CONV = 3
NORM_EPS = 1e-6
P_GATES, P_RW, P_Q, P_Z, P_S5, P_BA, P_SB_COLS = 0, 0, 0, 3072, 0, 512, 1024


def _cparams(sem):
    return pltpu.CompilerParams(dimension_semantics=sem, vmem_limit_bytes=VMEM_LIMIT)


def _mm(a, b):
    return jnp.dot(a, b, preferred_element_type=F32)


def _mm_nt(a, b):
    return lax.dot_general(a, b, (((1,), (1,)), ((), ())), preferred_element_type=F32)


def _mm_tn(a, b):
    return lax.dot_general(a, b, (((0,), (0,)), ((), ())), preferred_element_type=F32)


def _sigmoid(x):
    return 1.0 / (1.0 + jnp.exp(-x))


def _softplus(x):
    return jnp.maximum(x, 0.0) + jnp.log1p(jnp.exp(-jnp.abs(x)))


def _gelu_tanh(x):
    return 0.5 * x * (1.0 + jnp.tanh(math.sqrt(2.0 / math.pi) * (x + 0.044715 * (x * x * x))))


def _row_iota(shape):
    return lax.broadcasted_iota(jnp.int32, shape, 0)


def _shift_chain(x, d, prev8):
    rolled = pltpu.roll(x, d, 0)
    row = _row_iota(prev8.shape)
    head = jnp.where(row < d, pltpu.roll(prev8, d, 0), rolled[:SUBLANES])
    return jnp.concatenate([head, rolled[SUBLANES:]], axis=0)


def _shift_seq8(x, d, state):
    n, ch = x.shape
    ns, w1, _ = state.shape
    out = pltpu.roll(x, d, 0).reshape(ns, SUBLANES, ch)
    row = lax.broadcasted_iota(jnp.int32, out.shape, 1)
    for r in range(d):
        out = jnp.where(row == r, state[:, w1 - d + r:w1 - d + r + 1, :], out)
    return out.reshape(n, ch)


def _group_last(x):
    c, n = x.shape
    x3 = x.reshape(c // SUBLANES, SUBLANES, n)
    return jnp.broadcast_to(x3[:, SUBLANES - 1:SUBLANES, :], x3.shape).reshape(c, n)


def _cumsum_rows(x, groupwise=False):
    c = x.shape[0]
    row = _row_iota(x.shape) % SUBLANES
    for d in (1, 2, 4):
        x = x + jnp.where(row >= d, pltpu.roll(x, d, 0), 0.0)
    if c > SUBLANES and not groupwise:
        blocks = [x[SUBLANES * i:SUBLANES * (i + 1)] for i in range(c // SUBLANES)]
        for i in range(1, len(blocks)):
            blocks[i] = blocks[i] + blocks[i - 1][SUBLANES - 1:SUBLANES, :]
        x = jnp.concatenate(blocks, axis=0)
    return x


def _split2(x):
    hi = x.astype(BF16)
    return hi, (x - hi.astype(F32)).astype(BF16)


def _mm_hilo_each(a_list, b_list):
    sa = [_split2(a) for a in a_list]
    sb = [_split2(b) for b in b_list]
    hh = [_mm(a[0], b[0]) for a, b in zip(sa, sb)]
    hl = [_mm(a[0], b[1]) for a, b in zip(sa, sb)]
    lh = [_mm(a[1], b[0]) for a, b in zip(sa, sb)]
    return [x + (y + z) for x, y, z in zip(hh, hl, lh)]


def _solve_unit_lower_each(lows, rhss, diag_only):
    c = lows[0].shape[0]
    nb = c // SUBLANES
    nh = len(lows)
    n = rhss[0].shape[1]
    blk = lambda a, i: a[SUBLANES * i:SUBLANES * (i + 1)]
    rb = [[blk(r, i) for i in range(nb)] for r in rhss]
    with_off = nb > 1 and not diag_only
    same = (lax.broadcasted_iota(jnp.int32, (c, c), 0) // SUBLANES
            == lax.broadcasted_iota(jnp.int32, (c, c), 1) // SUBLANES)
    if with_off:
        ob = [[blk(jnp.where(same, 0.0, low), i) for i in range(nb)] for low in lows]
    sel = (lax.broadcasted_iota(jnp.int32, (c, (SUBLANES - 1) * LANES), 0) % SUBLANES
           == lax.broadcasted_iota(jnp.int32, (c, (SUBLANES - 1) * LANES), 1) // LANES)
    sel = jnp.where(sel, 1.0, 0.0).astype(BF16)
    diag = [_split2(jnp.where(same, low, 0.0)) if not diag_only else _split2(low) for low in lows]
    cols_hi = [_mm(d[0], sel) for d in diag]
    cols_lo = [_mm(d[1], sel) for d in diag]
    cols = [a + b for a, b in zip(cols_hi, cols_lo)]
    reps = n // LANES
    for j in range(SUBLANES - 1):
        for h in range(nh):
            cj = cols[h][:, LANES * j:LANES * (j + 1)]
            cj_n = jnp.concatenate([cj] * reps, axis=1) if reps > 1 else cj
            for i in range(nb):
                rb[h][i] = rb[h][i] - blk(cj_n, i) * rb[h][i][j:j + 1, :]
                if with_off and i > 0:
                    ob[h][i] = ob[h][i] - blk(cj, i)[:, :c] * ob[h][i][j:j + 1, :]
    r1 = [jnp.concatenate(b, axis=0) if nb > 1 else b[0] for b in rb]
    if not with_off:
        return r1
    n1 = [jnp.concatenate(b, axis=0) for b in ob]
    n2 = _mm_hilo_each(n1, n1)
    t = _mm_hilo_each(n1 + n2, r1 + n2)
    y = [r - nr for r, nr in zip(r1, t[:nh])]
    n4 = t[nh:]
    y = [a + b for a, b in zip(y, _mm_hilo_each(n2, y))]
    return [a + b for a, b in zip(y, _mm_hilo_each(n4, y))]


def _chunks_per_iteration(n_chunks):
    return 2 if n_chunks % 2 == 0 else 1


def _chunk_masks(per_seq):
    ri = lax.broadcasted_iota(jnp.int32, (CHUNK, CHUNK), 0)
    ci = lax.broadcasted_iota(jnp.int32, (CHUNK, CHUNK), 1)
    strict, causal = ri > ci, ri >= ci
    if per_seq:
        same = (ri // SUBLANES) == (ci // SUBLANES)
        strict, causal = jnp.logical_and(strict, same), jnp.logical_and(causal, same)
    return strict, causal


def _seq_block_mask(rows, width):
    r = lax.broadcasted_iota(jnp.int32, (rows, SEQ_PER_CHUNK * width), 0)
    l = lax.broadcasted_iota(jnp.int32, (rows, SEQ_PER_CHUNK * width), 1)
    return ((r % CHUNK) // SUBLANES) == (l // width)


def _pick_own_seq(a, width):
    return jnp.concatenate([a[SUBLANES * s:SUBLANES * (s + 1), width * s:width * (s + 1)]
                            for s in range(SEQ_PER_CHUNK)], axis=0)


def _rmsnorm_rows(x_ref, g_ref, o_ref, rows, chunk=128):
    def body(i, carry):
        r0 = pl.multiple_of(i * chunk, chunk)
        x = x_ref[pl.ds(r0, chunk), :]
        ms = jnp.mean(x * x, axis=-1, keepdims=True)
        o_ref[pl.ds(r0, chunk), :] = (x * lax.rsqrt(ms + NORM_EPS) * g_ref[...]).astype(o_ref.dtype)
        return carry
    lax.fori_loop(0, rows // chunk, body, 0)


def _two_group_specs(tm, width, prompt_tiles, sample_off, pipeline_mode=None):
    p = pl.BlockSpec((tm, width), lambda i, *_: (jnp.minimum(i, prompt_tiles - 1), 0), pipeline_mode=pipeline_mode)
    s = pl.BlockSpec((tm, width), lambda i, *_: (jnp.maximum(i - prompt_tiles, 0) + sample_off, 0),
                     pipeline_mode=pipeline_mode)
    return p, s


def _layer_spec(shape, index_map, layer):
    return pl.BlockSpec((None,) + shape, lambda *a: (layer,) + tuple(index_map(*a)))


def _norm_cast_kernel(xp_ref, xs_ref, g_ref, o_ref, *, rows, prompt_tiles):
    i = pl.program_id(0)

    @pl.when(i < prompt_tiles)
    def _():
        _rmsnorm_rows(xp_ref, g_ref, o_ref, rows)

    @pl.when(i >= prompt_tiles)
    def _():
        _rmsnorm_rows(xs_ref, g_ref, o_ref, rows)


def _norm_cast(xp, xs, sample_off, m, prompt_tiles, g, tm):
    k = xp.shape[1]
    psp, ssp = _two_group_specs(tm, k, prompt_tiles, sample_off)
    return pl.pallas_call(
        functools.partial(_norm_cast_kernel, rows=tm, prompt_tiles=prompt_tiles),
        grid=(m // tm,),
        in_specs=[psp, ssp, pl.BlockSpec((1, k), lambda i: (0, 0))],
        out_specs=pl.BlockSpec((tm, k), lambda i: (i, 0)),
        out_shape=jax.ShapeDtypeStruct((m, k), BF16),
        compiler_params=_cparams(("arbitrary",)),
        name="norm_cast",
    )(xp, xs, g.reshape(1, k))


def _matmul_kernel(a_ref, w_ref, o_ref):
    o_ref[...] = _mm(a_ref[...], w_ref[...])


def _matmul(a, w, layer, tm, tn):
    m, k = a.shape
    n = w.shape[-1]
    return pl.pallas_call(
        _matmul_kernel,
        grid=(m // tm, n // tn),
        in_specs=[pl.BlockSpec((tm, k), lambda i, j: (i, 0), pipeline_mode=pl.Buffered(1)),
                  _layer_spec((k, tn), lambda i, j: (0, j), layer)],
        out_specs=pl.BlockSpec((tm, tn), lambda i, j: (i, j)),
        out_shape=jax.ShapeDtypeStruct((m, n), F32),
        compiler_params=_cparams(("parallel", "arbitrary")),
        name="proj_matmul",
    )(a, w)


def _matmul_res_kernel(a_ref, w_ref, r_ref, o_ref):
    o_ref[...] = r_ref[...] + _mm(a_ref[...], w_ref[...])


def _matmul_res(a, w, layer, res, tm, tn):
    m, k = a.shape
    n = w.shape[-1]
    return pl.pallas_call(
        _matmul_res_kernel,
        grid=(m // tm, n // tn),
        in_specs=[pl.BlockSpec((tm, k), lambda i, j: (i, 0), pipeline_mode=pl.Buffered(1)),
                  _layer_spec((k, tn), lambda i, j: (0, j), layer),
                  pl.BlockSpec((tm, tn), lambda i, j: (i, j))],
        out_specs=pl.BlockSpec((tm, tn), lambda i, j: (i, j)),
        out_shape=jax.ShapeDtypeStruct((m, n), F32),
        compiler_params=_cparams(("parallel", "arbitrary")),
        name="matmul_res",
    )(a, w, res)


def _out_norm_kernel(a_ref, w_ref, rp_ref, rs_ref, g_ref, x_ref, xn_ref, *, rows, prompt_tiles):
    res = jnp.where(pl.program_id(0) < prompt_tiles, rp_ref[...], rs_ref[...])
    x_ref[...] = res + _mm(a_ref[...], w_ref[...])
    _rmsnorm_rows(x_ref, g_ref, xn_ref, rows)


def _out_norm(a, w, layer, rp, rs, sample_off, prompt_tiles, g, tm):
    m, k = a.shape
    n = w.shape[-1]
    psp, ssp = _two_group_specs(tm, n, prompt_tiles, sample_off)
    return pl.pallas_call(
        functools.partial(_out_norm_kernel, rows=tm, prompt_tiles=prompt_tiles),
        grid=(m // tm,),
        in_specs=[pl.BlockSpec((tm, k), lambda i: (i, 0)),
                  _layer_spec((k, n), lambda i: (0, 0), layer),
                  psp, ssp, pl.BlockSpec((1, n), lambda i: (0, 0))],
        out_specs=[pl.BlockSpec((tm, n), lambda i: (i, 0)), pl.BlockSpec((tm, n), lambda i: (i, 0))],
        out_shape=[jax.ShapeDtypeStruct((m, n), F32), jax.ShapeDtypeStruct((m, n), BF16)],
        compiler_params=_cparams(("parallel",)),
        name="out_norm",
    )(a, w, rp, rs, g.reshape(1, n))


def _merge_kernel(y1p, y2p, y3p, y1s, y2s, y3s, w1_ref, w2_ref, w3_ref, g1_ref, g2_ref, g3_ref, o_ref, *,
                  prompt_tiles, tm, rc):
    is_p = pl.program_id(0) < prompt_tiles
    for r in range(tm // rc):
        rows = slice(r * rc, (r + 1) * rc)
        pick = lambda p, s: jnp.where(is_p, p[rows, :], s[rows, :])
        acc = _sigmoid(g1_ref[rows, :]) * _mm(pick(y1p, y1s), w1_ref[...])
        acc = acc + _sigmoid(g2_ref[rows, :]) * _mm(pick(y2p, y2s), w2_ref[...])
        acc = acc + _sigmoid(g3_ref[rows, :]) * _mm(pick(y3p, y3s), w3_ref[...])
        o_ref[rows, :] = acc.astype(BF16)


def _merge(ys_p, ys_s, ws, layer, proj, tm, tn):
    mp, ms = ys_p[0].shape[0], ys_s[0].shape[0]
    n = ws[0].shape[-1]
    nj = n // tn
    pt = mp // tm
    gate_spec = lambda b: pl.BlockSpec((tm, tn), lambda i, j: (i, P_GATES // tn + b * nj + j))
    y_specs = [_two_group_specs(tm, y.shape[1], pt, 0) for y in ys_p]
    w_spec = lambda w: _layer_spec((w.shape[1], tn), lambda i, j: (0, j), layer)
    return pl.pallas_call(
        functools.partial(_merge_kernel, prompt_tiles=pt, tm=tm, rc=128),
        grid=((mp + ms) // tm, nj),
        in_specs=[s[0] for s in y_specs] + [s[1] for s in y_specs] + [w_spec(w) for w in ws]
        + [gate_spec(0), gate_spec(1), gate_spec(2)],
        out_specs=pl.BlockSpec((tm, tn), lambda i, j: (i, j)),
        out_shape=jax.ShapeDtypeStruct((mp + ms, n), BF16),
        compiler_params=_cparams(("parallel", "arbitrary")),
        name="merge",
    )(*ys_p, *ys_s, *ws, proj, proj, proj)


def _final_norm_kernel(x_ref, g_ref, yp_ref, ys_ref, *, prompt_tiles):
    x = x_ref[...]
    ms = jnp.mean(x * x, axis=-1, keepdims=True)
    y = x * lax.rsqrt(ms + NORM_EPS) * g_ref[...]
    i = pl.program_id(0)

    @pl.when(i < prompt_tiles)
    def _():
        yp_ref[...] = y

    @pl.when(i >= prompt_tiles)
    def _():
        ys_ref[...] = y


def _final_norm(x, g, mp, tm):
    m, k = x.shape
    pt = mp // tm
    psp, ssp = _two_group_specs(tm, k, pt, 0)
    return pl.pallas_call(
        functools.partial(_final_norm_kernel, prompt_tiles=pt),
        grid=(m // tm,),
        in_specs=[pl.BlockSpec((tm, k), lambda i: (i, 0)), pl.BlockSpec((1, k), lambda i: (0, 0))],
        out_specs=[psp, ssp],
        out_shape=[jax.ShapeDtypeStruct((mp, k), F32), jax.ShapeDtypeStruct((m - mp, k), F32)],
        compiler_params=_cparams(("arbitrary",)),
        name="final_norm",
    )(x, g.reshape(1, k))


def _ffn_up_kernel(xn_ref, wg_ref, wv_ref, cwg_ref, cwv_ref, bg_ref, bv_ref, sg_ref, sv_ref,
                   h_ref, pg_ref, pv_ref, og_ref, ov_ref, wgb, wvb, cg, cv, *, tm, rc, prompt_tiles, tiles_per_seq):
    i = pl.program_id(1)
    w1 = FFN_CONV - 1

    @pl.when(i == 0)
    def _():
        wgb[...] = wg_ref[...].astype(BF16)
        wvb[...] = wv_ref[...].astype(BF16)

    def conv(u, s1, s2, cw_ref, b_ref):
        return cw_ref[2:3, :] * u + cw_ref[1:2, :] * s1 + cw_ref[0:1, :] * s2 + b_ref[...]

    def up(rows):
        x = xn_ref[rows, :]
        return _mm(x, wgb[...]), _mm(x, wvb[...])

    @pl.when(i < prompt_tiles)
    def _():
        first = (i % tiles_per_seq) == 0
        pg = jnp.where(first, 0.0, cg[...])
        pv = jnp.where(first, 0.0, cv[...])
        for r in range(tm // rc):
            rows = slice(r * rc, (r + 1) * rc)
            ug, uv = up(rows)
            gate = conv(ug, _shift_chain(ug, 1, pg), _shift_chain(ug, 2, pg), cwg_ref, bg_ref)
            val = conv(uv, _shift_chain(uv, 1, pv), _shift_chain(uv, 2, pv), cwv_ref, bv_ref)
            h_ref[rows, :] = (gate * _sigmoid(gate) * val).astype(BF16)
            pg, pv = ug[rc - SUBLANES:, :], uv[rc - SUBLANES:, :]
        cg[...] = pg
        cv[...] = pv
        pg_ref[0] = pg[SUBLANES - w1:, :]
        pv_ref[0] = pv[SUBLANES - w1:, :]

    @pl.when(i >= prompt_tiles)
    def _():
        nsr = rc // SUBLANES
        for r in range(tm // rc):
            rows = slice(r * rc, (r + 1) * rc)
            seqs = slice(r * nsr, (r + 1) * nsr)
            ug, uv = up(rows)
            sg, sv = sg_ref[seqs], sv_ref[seqs]
            gate = conv(ug, _shift_seq8(ug, 1, sg), _shift_seq8(ug, 2, sg), cwg_ref, bg_ref)
            val = conv(uv, _shift_seq8(uv, 1, sv), _shift_seq8(uv, 2, sv), cwv_ref, bv_ref)
            h_ref[rows, :] = (gate * _sigmoid(gate) * val).astype(BF16)
            og_ref[seqs] = ug.reshape(nsr, SUBLANES, ug.shape[1])[:, SUBLANES - w1:, :]
            ov_ref[seqs] = uv.reshape(nsr, SUBLANES, uv.shape[1])[:, SUBLANES - w1:, :]


def _ffn_up(xn, w_up, conv_w, conv_b, state, layer, nb, seq, ns, tm, tn):
    m, k = xn.shape
    mp = nb * seq
    assert ns * SUBLANES == tm and (m - mp) == tm and seq % tm == 0
    nj, pt, tps = D_FF // tn, mp // tm, seq // tm
    w1 = FFN_CONV - 1
    wsp = lambda off: _layer_spec((k, tn), lambda j, i: (0, off + j), layer)
    par = lambda rows, off: _layer_spec((rows, tn), lambda j, i: (0, off + j), layer)
    stsp = lambda off: _layer_spec((ns, w1, tn), lambda j, i: (0, 0, off + j), layer)
    psp = pl.BlockSpec((1, w1, tn), lambda j, i: (jnp.minimum(i, pt - 1) // tps, 0, j))
    osp = pl.BlockSpec((ns, w1, tn), lambda j, i: (0, 0, j))
    h, pg, pv, og, ov = pl.pallas_call(
        functools.partial(_ffn_up_kernel, tm=tm, rc=256, prompt_tiles=pt, tiles_per_seq=tps),
        grid=(nj, m // tm),
        in_specs=[pl.BlockSpec((tm, k), lambda j, i: (i, 0)), wsp(0), wsp(nj),
                  par(FFN_CONV, 0), par(FFN_CONV, nj), par(1, 0), par(1, nj), stsp(0), stsp(nj)],
        out_specs=[pl.BlockSpec((tm, tn), lambda j, i: (i, j)), psp, psp, osp, osp],
        out_shape=[jax.ShapeDtypeStruct((m, D_FF), BF16),
                   jax.ShapeDtypeStruct((nb, w1, D_FF), F32), jax.ShapeDtypeStruct((nb, w1, D_FF), F32),
                   jax.ShapeDtypeStruct((ns, w1, D_FF), F32), jax.ShapeDtypeStruct((ns, w1, D_FF), F32)],
        scratch_shapes=[pltpu.VMEM((k, tn), BF16), pltpu.VMEM((k, tn), BF16),
                        pltpu.VMEM((SUBLANES, tn), F32), pltpu.VMEM((SUBLANES, tn), F32)],
        compiler_params=_cparams(("arbitrary", "arbitrary")),
        name="ffn_up",
    )(xn, w_up, w_up, conv_w, conv_w, conv_b, conv_b, state, state)
    return h, jnp.concatenate([pg, pv], axis=-1), jnp.concatenate([og, ov], axis=-1)


S5_Q = 4
S5_QW = S5_NSTATE // S5_Q


def _s5_kernel(*refs, chained, tm):
    if chained:
        (u_ref, bre_ref, bim_ref, cre_ref, cim_ref, mre_ref, mim_ref, pre_ref, pim_ref, d_ref, wglu_ref,
         y_ref, xlre_ref, xlim_ref, sre, sim, car_re, car_im) = refs
    else:
        (u_ref, x0re_ref, x0im_ref, bre_ref, bim_ref, cre_ref, cim_ref, mre_ref, mim_ref, pre_ref, pim_ref,
         d_ref, wglu_ref, y_ref, xlre_ref, xlim_ref, sre, sim) = refs

    u = u_ref[...]
    ub = u.astype(BF16)
    for q in range(S5_Q):
        uq = ub[:, LANES * q:LANES * (q + 1)]
        sre[:, S5_QW * q:S5_QW * (q + 1)] = _mm(uq, bre_ref[q])
        sim[:, S5_QW * q:S5_QW * (q + 1)] = _mm(uq, bim_ref[q])

    if chained:
        @pl.when(pl.program_id(1) == 0)
        def _():
            car_re[...] = jnp.zeros_like(car_re)
            car_im[...] = jnp.zeros_like(car_im)

    def blk(i, carry):
        r0 = pl.multiple_of(i * SUBLANES, SUBLANES)
        for q in range(S5_Q):
            sl = slice(S5_QW * q, S5_QW * (q + 1))
            xr = sre[pl.ds(r0, SUBLANES), sl]
            xi = sim[pl.ds(r0, SUBLANES), sl]
            for li, d in enumerate((1, 2, 4)):
                mr, mi = mre_ref[li, :, sl], mim_ref[li, :, sl]
                sr, si = pltpu.roll(xr, d, 0), pltpu.roll(xi, d, 0)
                xr, xi = xr + (mr * sr - mi * si), xi + (mr * si + mi * sr)
            if chained:
                cr, ci = car_re[:, sl], car_im[:, sl]
            else:
                cr, ci = x0re_ref[pl.ds(i, 1), sl], x0im_ref[pl.ds(i, 1), sl]
            pr, pi_ = pre_ref[:, sl], pim_ref[:, sl]
            xr, xi = xr + (pr * cr - pi_ * ci), xi + (pr * ci + pi_ * cr)
            sre[pl.ds(r0, SUBLANES), sl] = xr
            sim[pl.ds(r0, SUBLANES), sl] = xi
            if chained:
                car_re[:, sl] = xr[SUBLANES - 1:SUBLANES, :]
                car_im[:, sl] = xi[SUBLANES - 1:SUBLANES, :]
            else:
                xlre_ref[pl.ds(i, 1), sl] = xr[SUBLANES - 1:SUBLANES, :]
                xlim_ref[pl.ds(i, 1), sl] = xi[SUBLANES - 1:SUBLANES, :]
        return carry

    lax.fori_loop(0, tm // SUBLANES, blk, 0)

    if chained:
        xlre_ref[0] = car_re[...]
        xlim_ref[0] = car_im[...]

    ys = []
    for q in range(S5_Q):
        sl = slice(S5_QW * q, S5_QW * (q + 1))
        ys.append(_mm(sre[:, sl].astype(BF16), cre_ref[q]) - _mm(sim[:, sl].astype(BF16), cim_ref[q]))
    y = jnp.concatenate(ys, axis=1) + d_ref[...] * u
    y = _gelu_tanh(y)
    y = y * _sigmoid(_mm(y.astype(BF16), wglu_ref[...]))
    y_ref[...] = y.astype(BF16)


def _const_spec(shape):
    zeros = (0,) * len(shape)
    return pl.BlockSpec(shape, lambda *a: zeros)


def _s5_param_specs():
    return [_const_spec((S5_Q, LANES, S5_QW)), _const_spec((S5_Q, LANES, S5_QW)),
            _const_spec((S5_Q, S5_QW, LANES)), _const_spec((S5_Q, S5_QW, LANES)),
            _const_spec((3, SUBLANES, S5_NSTATE)), _const_spec((3, SUBLANES, S5_NSTATE)),
            _const_spec((SUBLANES, S5_NSTATE)), _const_spec((SUBLANES, S5_NSTATE)),
            _const_spec((1, S5_WIDTH)), _const_spec((S5_WIDTH, S5_WIDTH))]


def _s5_prompt(proj, sp, nb, seq, tm):
    nt = seq // tm
    cb = P_S5 // S5_WIDTH
    y, xre, xim = pl.pallas_call(
        functools.partial(_s5_kernel, chained=True, tm=tm),
        grid=(nb, nt),
        in_specs=[pl.BlockSpec((tm, S5_WIDTH), lambda b, t: (b * nt + t, cb))] + _s5_param_specs(),
        out_specs=[pl.BlockSpec((tm, S5_WIDTH), lambda b, t: (b * nt + t, 0)),
                   pl.BlockSpec((1, 1, S5_NSTATE), lambda b, t: (b, 0, 0)),
                   pl.BlockSpec((1, 1, S5_NSTATE), lambda b, t: (b, 0, 0))],
        out_shape=[jax.ShapeDtypeStruct((nb * seq, S5_WIDTH), BF16),
                   jax.ShapeDtypeStruct((nb, 1, S5_NSTATE), F32),
                   jax.ShapeDtypeStruct((nb, 1, S5_NSTATE), F32)],
        scratch_shapes=[pltpu.VMEM((tm, S5_NSTATE), F32), pltpu.VMEM((tm, S5_NSTATE), F32),
                        pltpu.VMEM((1, S5_NSTATE), F32), pltpu.VMEM((1, S5_NSTATE), F32)],
        compiler_params=_cparams(("parallel", "arbitrary")),
        name="s5_prompt",
    )(proj, *sp)
    return y, xre.reshape(nb, S5_GROUPS, S5_STATE), xim.reshape(nb, S5_GROUPS, S5_STATE)


def _s5_sample(proj, x0re, x0im, sp, row0, nseq, tm):
    ns = tm // SUBLANES
    cb = P_S5 // S5_WIDTH
    r0 = row0 // tm
    st = pl.BlockSpec((ns, S5_NSTATE), lambda i: (i, 0))
    y, xre, xim = pl.pallas_call(
        functools.partial(_s5_kernel, chained=False, tm=tm),
        grid=(nseq // ns,),
        in_specs=[pl.BlockSpec((tm, S5_WIDTH), lambda i: (r0 + i, cb)), st, st] + _s5_param_specs(),
        out_specs=[pl.BlockSpec((tm, S5_WIDTH), lambda i: (i, 0)), st, st],
        out_shape=[jax.ShapeDtypeStruct((nseq * SUBLANES, S5_WIDTH), BF16),
                   jax.ShapeDtypeStruct((nseq, S5_NSTATE), F32),
                   jax.ShapeDtypeStruct((nseq, S5_NSTATE), F32)],
        scratch_shapes=[pltpu.VMEM((tm, S5_NSTATE), F32), pltpu.VMEM((tm, S5_NSTATE), F32)],
        compiler_params=_cparams(("parallel",)),
        name="s5_sample",
    )(proj, x0re.reshape(nseq, S5_NSTATE), x0im.reshape(nseq, S5_NSTATE), *sp)
    return y, xre.reshape(nseq, S5_GROUPS, S5_STATE), xim.reshape(nseq, S5_GROUPS, S5_STATE)


def _s5_params(lam_re, lam_im, b_re, b_im, c_re, c_im, d, log_step, w_glu):
    lam = lax.complex(lam_re, lam_im)
    step = jnp.exp(log_step)[:, None]
    lam_bar = jnp.exp(lam * step)
    b_bar = ((lam_bar - 1.0) / lam)[..., None] * lax.complex(b_re, b_im)

    def bblk(bm):
        bm = bm.reshape(S5_Q, 8, S5_STATE, S5_GROUP)
        eye = jnp.eye(8, dtype=F32)
        out = jnp.einsum('qgpc,gh->qgchp', bm, eye)
        return out.reshape(S5_Q, LANES, S5_QW).astype(BF16)

    def cblk(cm):
        cm = cm.reshape(S5_Q, 8, S5_GROUP, S5_STATE)
        eye = jnp.eye(8, dtype=F32)
        out = jnp.einsum('qgcp,gh->qgphc', cm, eye)
        return out.reshape(S5_Q, S5_QW, LANES).astype(BF16)

    lam_flat = lam_bar.reshape(1, S5_NSTATE)
    pows = [lam_flat]
    for _ in range(SUBLANES - 1):
        pows.append(pows[-1] * lam_flat)
    row = jnp.arange(SUBLANES)[:, None]
    m = jnp.stack([jnp.where(row >= dd, pows[dd - 1], 0.0) for dd in (1, 2, 4)])
    p = jnp.concatenate(pows, axis=0)
    return (bblk(b_bar.real), bblk(b_bar.imag), cblk(c_re), cblk(c_im),
            m.real, m.imag, p.real, p.imag, d.reshape(1, S5_WIDTH), w_glu.astype(BF16))


def _seg64_sum(x):
    outs = []
    for t in range(x.shape[1] // LANES):
        xt = x[:, LANES * t:LANES * (t + 1)]
        lo = lax.broadcasted_iota(jnp.int32, xt.shape, 1) < RW_HD
        s_lo = jnp.sum(jnp.where(lo, xt, 0.0), axis=-1, keepdims=True)
        s_hi = jnp.sum(jnp.where(lo, 0.0, xt), axis=-1, keepdims=True)
        outs.append(jnp.where(lo, s_lo, s_hi))
    return jnp.concatenate(outs, axis=1)


def _rwkv_kernel(*refs, per_seq, tm):
    if per_seq:
        (h_ref, sh_ref, si_ref, mu_ref, w0_ref, w2_ref, a0_ref, a2_ref, g2_ref, kk_ref, ka_ref, rk_ref,
         lnw_ref, lnb_ref, y_ref, so_ref, r_s, k_s, v_s, a_s, b_s, lw_s, y_s) = refs
    else:
        (h_ref, mu_ref, w0_ref, w2_ref, a0_ref, a2_ref, g2_ref, kk_ref, ka_ref, rk_ref, lnw_ref, lnb_ref,
         y_ref, so_ref, r_s, k_s, v_s, a_s, b_s, lw_s, y_s, prev_s, st_s) = refs
    c, w, hd_w = CHUNK, RW_WIDTH, RW_HD

    h = h_ref[...]
    if per_seq:
        prev = _shift_seq8(h, 1, sh_ref[...])
    else:
        @pl.when(pl.program_id(1) == 0)
        def _():
            prev_s[...] = jnp.zeros_like(prev_s)
            st_s[...] = jnp.zeros_like(st_s)
        prev = _shift_chain(h, 1, prev_s[...])
        prev_s[...] = h[tm - SUBLANES:, :]
    hs = h + (prev - h) * mu_ref[...]
    r = hs[:, 0:w]
    k = hs[:, w:2 * w]
    v = hs[:, 2 * w:3 * w]
    wd = hs[:, 3 * w:3 * w + LANES]
    ad = hs[:, 3 * w + LANES:3 * w + 2 * LANES]
    gd = hs[:, 3 * w + 2 * LANES:]
    w_log = -_softplus(-(w0_ref[...] + _mm(jnp.tanh(wd).astype(BF16), w2_ref[...]))) - 0.5
    lw_s[...] = -jnp.exp(w_log)
    a_ic = _sigmoid(a0_ref[...] + _mm(ad.astype(BF16), a2_ref[...]))
    gate = _mm(_sigmoid(gd).astype(BF16), g2_ref[...])
    kx = k * kk_ref[...]
    kkn = kx * lax.rsqrt(_seg64_sum(kx * kx) + 1e-6)
    k2 = k * (1.0 + (a_ic - 1.0) * ka_ref[...])
    r_s[...] = r
    k_s[...] = k2
    v_s[...] = v
    a_s[...] = -kkn
    b_s[...] = kkn * a_ic

    strict, causal = _chunk_masks(per_seq)
    if per_seq:
        big_mask = _seq_block_mask(2 * c, hd_w)

    heads = range(RW_HEADS)
    sls = [slice(hd_w * hd, hd_w * (hd + 1)) for hd in heads]
    group = _chunks_per_iteration(tm // c)

    def chunk_group(gidx, carry):
        pre = []
        for t in range(group):
            idx = gidx * group + t
            rows = pl.ds(pl.multiple_of(idx * c, c), c)
            lw = lw_s[rows, :]
            cum = _cumsum_rows(lw, groupwise=per_seq)
            cl = _group_last(cum) if per_seq else cum[c - 1:c, :]
            e_pos, e_neg, e_end = jnp.exp(cum), jnp.exp(-cum), jnp.exp(cl - cum)
            rr, kc, vc, ac, bc = r_s[rows, :], k_s[rows, :], v_s[rows, :], a_s[rows, :], b_s[rows, :]
            pre.append(dict(rows=rows, seqs=pl.ds(idx * SEQ_PER_CHUNK, SEQ_PER_CHUNK), vc=vc,
                            rt=rr * e_pos, kt=kc * e_neg, bt=bc * e_neg, at=ac * jnp.exp(cum - lw),
                            kh=kc * e_end, bh=bc * e_end, wc=jnp.exp(cl)))
        pairs = [(t, hd) for t in range(group) for hd in heads]
        ath = [pre[t]['at'][:, sls[hd]] for t, hd in pairs]
        vh = [pre[t]['vc'][:, sls[hd]] for t, hd in pairs]
        rth = [pre[t]['rt'][:, sls[hd]] for t, hd in pairs]
        ab = [_mm_nt(jnp.concatenate([ath[p], rth[p]], axis=0).astype(BF16),
                     jnp.concatenate([pre[t]['kt'][:, sls[hd]], pre[t]['bt'][:, sls[hd]]], axis=0).astype(BF16))
              for p, (t, hd) in enumerate(pairs)]
        a_ak = [jnp.where(strict, m[:c, :c], 0.0) for m in ab]
        n_ab = [jnp.where(strict, -m[:c, c:], 0.0) for m in ab]
        a_r = [jnp.concatenate([jnp.where(causal, m[c:, :c], 0.0), jnp.where(causal, m[c:, c:], 0.0)],
                               axis=1).astype(BF16) for m in ab]
        akv = [_mm(a_ak[p].astype(BF16), vh[p].astype(BF16)) for p in range(len(pairs))]
        sol = _solve_unit_lower_each(
            n_ab, [jnp.concatenate([ath[p], akv[p]], axis=1) for p in range(len(pairs))], per_seq)
        kb = [jnp.concatenate([pre[t]['kh'][:, sls[hd]], pre[t]['bh'][:, sls[hd]]], axis=0).astype(BF16)
              for t, hd in pairs]
        for t in range(group):
            rows, seqs, wc = pre[t]['rows'], pre[t]['seqs'], pre[t]['wc']
            ps = [t * RW_HEADS + hd for hd in heads]
            if per_seq:
                st = [si_ref[seqs, hd].reshape(SEQ_PER_CHUNK * hd_w, hd_w) for hd in heads]
            else:
                st = [st_s[hd] for hd in heads]
            pr = [_mm_nt(jnp.concatenate([sol[p][:, :hd_w], rth[p]], axis=0).astype(BF16), st[hd].astype(BF16))
                  for hd, p in zip(heads, ps)]
            if per_seq:
                u = [_pick_own_seq(pr[hd][:c], hd_w) + sol[p][:, hd_w:] for hd, p in zip(heads, ps)]
                yst = [_pick_own_seq(pr[hd][c:], hd_w) for hd in heads]
            else:
                u = [pr[hd][:c] + sol[p][:, hd_w:] for hd, p in zip(heads, ps)]
                yst = [pr[hd][c:] for hd in heads]
            vu = [jnp.concatenate([vh[p], u[hd]], axis=0) for hd, p in zip(heads, ps)]
            ya = [_mm(a_r[p], vu[hd].astype(BF16)) for hd, p in zip(heads, ps)]
            if per_seq:
                upd = [_mm_tn(jnp.where(big_mask, jnp.concatenate([vu[hd]] * SEQ_PER_CHUNK, axis=1), 0.0
                                        ).astype(BF16), kb[p]) for hd, p in zip(heads, ps)]
            else:
                upd = [_mm_tn(vu[hd].astype(BF16), kb[p]) for hd, p in zip(heads, ps)]
            for hd in heads:
                if per_seq:
                    wcb = jnp.concatenate(
                        [jnp.broadcast_to(wc[SUBLANES * s:SUBLANES * s + 1, sls[hd]], (hd_w, hd_w))
                         for s in range(SEQ_PER_CHUNK)], axis=0)
                    so_ref[seqs, hd] = (st[hd] * wcb + upd[hd]).reshape(SEQ_PER_CHUNK, hd_w, hd_w)
                else:
                    st_s[hd] = st[hd] * wc[:, sls[hd]] + upd[hd]
                y_s[rows, sls[hd]] = yst[hd] + ya[hd]
        return carry

    lax.fori_loop(0, tm // (c * group), chunk_group, 0)

    if not per_seq:
        so_ref[0] = st_s[...]

    y = y_s[...]
    mean = _seg64_sum(y) * (1.0 / hd_w)
    yc = y - mean
    var = _seg64_sum(yc * yc) * (1.0 / hd_w)
    yn = yc * lax.rsqrt(var + RW_GN_EPS) * lnw_ref[...] + lnb_ref[...]
    bonus = _seg64_sum(r_s[...] * k_s[...] * rk_ref[...]) * v_s[...]
    y_ref[...] = ((yn + bonus) * gate).astype(BF16)


def _rwkv_param_specs():
    row = _const_spec((1, RW_WIDTH))
    return [_const_spec((1, RW_PCOLS)), row, _const_spec((LANES, RW_WIDTH)), row, _const_spec((LANES, RW_WIDTH)),
            _const_spec((RW_GATE, RW_WIDTH)), row, row, row, row, row]


def _rwkv_scratch(tm):
    return [pltpu.VMEM((tm, RW_WIDTH), F32) for _ in range(7)]


def _rwkv_prompt(proj, rp, nb, seq, tm):
    nt = seq // tm
    cb = P_RW // RW_PCOLS
    y, st = pl.pallas_call(
        functools.partial(_rwkv_kernel, per_seq=False, tm=tm),
        grid=(nb, nt),
        in_specs=[pl.BlockSpec((tm, RW_PCOLS), lambda b, t: (b * nt + t, cb))] + _rwkv_param_specs(),
        out_specs=[pl.BlockSpec((tm, RW_WIDTH), lambda b, t: (b * nt + t, 0)),
                   pl.BlockSpec((1, RW_HEADS, RW_HD, RW_HD), lambda b, t: (b, 0, 0, 0))],
        out_shape=[jax.ShapeDtypeStruct((nb * seq, RW_WIDTH), BF16),
                   jax.ShapeDtypeStruct((nb, RW_HEADS, RW_HD, RW_HD), F32)],
        scratch_shapes=_rwkv_scratch(tm) + [pltpu.VMEM((SUBLANES, RW_PCOLS), F32),
                                            pltpu.VMEM((RW_HEADS, RW_HD, RW_HD), F32)],
        compiler_params=_cparams(("parallel", "arbitrary")),
        name="rwkv_prompt",
    )(proj, *rp)
    return y, st


def _rwkv_sample(proj, shift0, st0, layer, rp, row0, nseq, tm):
    ns = tm // SUBLANES
    cb = P_RW // RW_PCOLS
    r0 = row0 // tm
    st_spec = pl.BlockSpec((ns, RW_HEADS, RW_HD, RW_HD), lambda i: (i, 0, 0, 0))
    st_in = _layer_spec((ns, RW_HEADS, RW_HD, RW_HD), lambda i: (i, 0, 0, 0), layer)
    y, st = pl.pallas_call(
        functools.partial(_rwkv_kernel, per_seq=True, tm=tm),
        grid=(nseq // ns,),
        in_specs=[pl.BlockSpec((tm, RW_PCOLS), lambda i: (r0 + i, cb)),
                  pl.BlockSpec((ns, 1, RW_PCOLS), lambda i: (i, 0, 0)), st_in] + _rwkv_param_specs(),
        out_specs=[pl.BlockSpec((tm, RW_WIDTH), lambda i: (i, 0)), st_spec],
        out_shape=[jax.ShapeDtypeStruct((nseq * SUBLANES, RW_WIDTH), BF16),
                   jax.ShapeDtypeStruct((nseq, RW_HEADS, RW_HD, RW_HD), F32)],
        scratch_shapes=_rwkv_scratch(tm),
        compiler_params=_cparams(("parallel",)),
        name="rwkv_sample",
    )(proj, shift0, st0, *rp)
    return y, st


def _rw_pad_cols(x):
    z = jnp.zeros(x.shape[:-1] + (LANES - RW_LORA,), x.dtype)
    a, b = 3 * RW_WIDTH, 3 * RW_WIDTH + RW_LORA
    return jnp.concatenate([x[..., :a], x[..., a:b], z, x[..., b:b + RW_LORA], z, x[..., b + RW_LORA:]], axis=-1)


def _rw_unpad_cols(x):
    a = 3 * RW_WIDTH
    return jnp.concatenate([x[..., :a + RW_LORA], x[..., a + LANES:a + LANES + RW_LORA], x[..., a + 2 * LANES:]],
                           axis=-1)


def _rwkv_params(mu, w0, w2, a0, a2, g2, k_k, k_a, r_k, ln_w, ln_b):
    row = lambda t: t.reshape(1, RW_WIDTH)
    padr = lambda t: jnp.concatenate([t, jnp.zeros((LANES - RW_LORA, RW_WIDTH), t.dtype)], axis=0).astype(BF16)
    return (_rw_pad_cols(mu).reshape(1, RW_PCOLS), row(w0), padr(w2), row(a0), padr(a2), g2.astype(BF16),
            row(k_k), row(k_a), row(r_k), row(ln_w), row(ln_b))


def _gdn_kernel(*refs, per_seq, tm):
    if per_seq:
        (q_ref, k_ref, v_ref, z_ref, ba_ref, fq_ref, fk_ref, fv_ref, si_ref, cwq_ref, cwk_ref, cwv_ref,
         al_ref, dt_ref, ng_ref, y_ref, so_ref, q_s, k_s, v_s, ba_s, o_s) = refs
        conv_state = (fq_ref, fk_ref, fv_ref)
    else:
        (q_ref, k_ref, v_ref, z_ref, ba_ref, cwq_ref, cwk_ref, cwv_ref, al_ref, dt_ref, ng_ref,
         y_ref, so_ref, q_s, k_s, v_s, ba_s, o_s, pq_s, pk_s, pv_s, st_s) = refs
        prevs = (pq_s, pk_s, pv_s)

        @pl.when(pl.program_id(1) == 0)
        def _():
            for p in prevs:
                p[...] = jnp.zeros_like(p)
            st_s[...] = jnp.zeros_like(st_s)
    c, hw = CHUNK, GD_HD

    def conv_act(idx, u_ref, cw_ref):
        u = u_ref[...]
        if per_seq:
            f = conv_state[idx][...]
            sh = [_shift_seq8(u, d, f) for d in (1, 2, 3)]
        else:
            p = prevs[idx][...]
            sh = [_shift_chain(u, d, p) for d in (1, 2, 3)]
            prevs[idx][...] = u[tm - SUBLANES:, :]
        x = cw_ref[3:4, :] * u + cw_ref[2:3, :] * sh[0] + cw_ref[1:2, :] * sh[1] + cw_ref[0:1, :] * sh[2]
        return x * _sigmoid(x)

    def l2n(x):
        outs = []
        for hd in range(GD_HEADS):
            xh = x[:, hw * hd:hw * (hd + 1)]
            outs.append(xh * lax.rsqrt(jnp.sum(xh * xh, axis=-1, keepdims=True) + 1e-6))
        return jnp.concatenate(outs, axis=1)

    q_s[...] = l2n(conv_act(0, q_ref, cwq_ref)) * (hw ** -0.5)
    k_s[...] = l2n(conv_act(1, k_ref, cwk_ref))
    v_s[...] = conv_act(2, v_ref, cwv_ref)
    ba = ba_ref[...]
    lane = lax.broadcasted_iota(jnp.int32, ba.shape, 1)
    g_all = -jnp.exp(al_ref[...]) * _softplus(ba + dt_ref[...])
    ba_s[...] = jnp.where(lane < GD_HEADS, _sigmoid(ba), g_all)

    strict, causal = _chunk_masks(per_seq)
    if per_seq:
        big_mask = _seq_block_mask(c, hw)

    heads = range(GD_HEADS)
    sls = [slice(hw * hd, hw * (hd + 1)) for hd in heads]
    group = _chunks_per_iteration(tm // c)

    def chunk_group(gidx, carry):
        pre = []
        for t in range(group):
            idx = gidx * group + t
            rows = pl.ds(pl.multiple_of(idx * c, c), c)
            bg = ba_s[rows, :]
            gc_all = _cumsum_rows(bg, groupwise=per_seq)
            pre.append(dict(rows=rows, seqs=pl.ds(idx * SEQ_PER_CHUNK, SEQ_PER_CHUNK), bg=bg, gc_all=gc_all,
                            gc_t=gc_all.T, gl_all=_group_last(gc_all) if per_seq else gc_all[c - 1:c, :],
                            qa=q_s[rows, :], ka=k_s[rows, :], va=v_s[rows, :]))
        pairs = [(t, hd) for t in range(group) for hd in heads]
        npair = len(pairs)
        beta = [pre[t]['bg'][:, hd:hd + 1] for t, hd in pairs]
        gc = [pre[t]['gc_all'][:, GD_HEADS + hd:GD_HEADS + hd + 1] for t, hd in pairs]
        gl = [pre[t]['gl_all'][:, GD_HEADS + hd:GD_HEADS + hd + 1] for t, hd in pairs]
        dec = [jnp.exp(jnp.where(causal, gc[p] - pre[t]['gc_t'][GD_HEADS + hd:GD_HEADS + hd + 1, :], -1e30))
               for p, (t, hd) in enumerate(pairs)]
        qh = [pre[t]['qa'][:, sls[hd]] for t, hd in pairs]
        kh = [pre[t]['ka'][:, sls[hd]] for t, hd in pairs]
        kb = [kh[p] * beta[p] for p in range(npair)]
        eg = [jnp.exp(g) for g in gc]
        kq = [_mm_nt(jnp.concatenate([kb[p], qh[p]], axis=0).astype(BF16), kh[p].astype(BF16))
              for p in range(npair)]
        low = [jnp.where(strict, kq[p][:c] * dec[p], 0.0) for p in range(npair)]
        intra = [(kq[p][c:] * dec[p]).astype(BF16) for p in range(npair)]
        sol = _solve_unit_lower_each(
            low, [jnp.concatenate([pre[t]['va'][:, sls[hd]] * beta[p], kb[p] * eg[p]], axis=1)
                  for p, (t, hd) in enumerate(pairs)], per_seq)
        wq = [jnp.concatenate([sol[p][:, hw:], qh[p] * eg[p]], axis=0).astype(BF16) for p in range(npair)]
        kdec = [kh[p] * jnp.exp(gl[p] - gc[p]) for p in range(npair)]
        for t in range(group):
            rows, seqs = pre[t]['rows'], pre[t]['seqs']
            ps = [t * GD_HEADS + hd for hd in heads]
            if per_seq:
                st3 = [si_ref[seqs, hd] for hd in heads]
                st_rhs = [jnp.concatenate([s3[s] for s in range(SEQ_PER_CHUNK)], axis=1).astype(BF16)
                          for s3 in st3]
            else:
                st = [st_s[hd] for hd in heads]
                st_rhs = [s.astype(BF16) for s in st]
            ws = [_mm(wq[p], st_rhs[hd]) for hd, p in zip(heads, ps)]
            if per_seq:
                v_new = [sol[p][:, :hw] - _pick_own_seq(ws[hd][:c], hw) for hd, p in zip(heads, ps)]
                o_st = [_pick_own_seq(ws[hd][c:], hw) for hd in heads]
            else:
                v_new = [sol[p][:, :hw] - ws[hd][:c] for hd, p in zip(heads, ps)]
                o_st = [ws[hd][c:] for hd in heads]
            vnb = [v.astype(BF16) for v in v_new]
            o_in = [_mm(intra[p], vnb[hd]) for hd, p in zip(heads, ps)]
            if per_seq:
                upd = [_mm_tn(jnp.where(big_mask, jnp.concatenate([kdec[p]] * SEQ_PER_CHUNK, axis=1), 0.0
                                        ).astype(BF16), vnb[hd]) for hd, p in zip(heads, ps)]
            else:
                upd = [_mm_tn(kdec[p].astype(BF16), vnb[hd]) for hd, p in zip(heads, ps)]
            for hd, p in zip(heads, ps):
                if per_seq:
                    egl = jnp.exp(gl[p])
                    mult = jnp.concatenate([jnp.broadcast_to(egl[SUBLANES * s:SUBLANES * s + 1, :], (hw, hw))
                                            for s in range(SEQ_PER_CHUNK)], axis=0)
                    st_new = st3[hd].reshape(SEQ_PER_CHUNK * hw, hw) * mult + upd[hd]
                    so_ref[seqs, hd] = st_new.reshape(SEQ_PER_CHUNK, hw, hw)
                else:
                    st_s[hd] = st[hd] * jnp.exp(gl[p]) + upd[hd]
                o_s[rows, sls[hd]] = o_st[hd] + o_in[hd]
        return carry

    lax.fori_loop(0, tm // (c * group), chunk_group, 0)

    if not per_seq:
        so_ref[0] = st_s[...]

    outs = []
    for hd in range(GD_HEADS):
        sl = slice(hw * hd, hw * (hd + 1))
        o = o_s[:, sl]
        o = o * lax.rsqrt(jnp.mean(o * o, axis=-1, keepdims=True) + NORM_EPS) * ng_ref[...]
        z = z_ref[:, sl]
        outs.append(o * (z * _sigmoid(z)))
    y_ref[...] = jnp.concatenate(outs, axis=1).astype(BF16)


def _gdn_param_specs():
    c2 = lambda shape, j=0: pl.BlockSpec(shape, (lambda *a: (0, j)))
    return [c2((GD_CONV, GD_WIDTH), 0), c2((GD_CONV, GD_WIDTH), 1), c2((GD_CONV, GD_WIDTH), 2),
            c2((1, LANES)), c2((1, LANES)), c2((1, GD_HD))]


def _gdn_scratch(tm):
    return ([pltpu.VMEM((tm, GD_WIDTH), F32) for _ in range(3)] + [pltpu.VMEM((tm, LANES), F32),
                                                                  pltpu.VMEM((tm, GD_WIDTH), F32)])


def _gdn_prompt(proj, proj_sb, gp, nb, seq, tm):
    nt = seq // tm
    qb = P_Q // GD_WIDTH
    blk = lambda j: pl.BlockSpec((tm, GD_WIDTH), lambda b, t: (b * nt + t, j))
    y, st = pl.pallas_call(
        functools.partial(_gdn_kernel, per_seq=False, tm=tm),
        grid=(nb, nt),
        in_specs=[blk(qb), blk(qb + 1), blk(qb + 2), blk(P_Z // GD_WIDTH),
                  pl.BlockSpec((tm, LANES), lambda b, t: (b * nt + t, P_BA // LANES))] + _gdn_param_specs(),
        out_specs=[pl.BlockSpec((tm, GD_WIDTH), lambda b, t: (b * nt + t, 0)),
                   pl.BlockSpec((1, GD_HEADS, GD_HD, GD_HD), lambda b, t: (b, 0, 0, 0))],
        out_shape=[jax.ShapeDtypeStruct((nb * seq, GD_WIDTH), BF16),
                   jax.ShapeDtypeStruct((nb, GD_HEADS, GD_HD, GD_HD), F32)],
        scratch_shapes=_gdn_scratch(tm) + [pltpu.VMEM((SUBLANES, GD_WIDTH), F32) for _ in range(3)]
        + [pltpu.VMEM((GD_HEADS, GD_HD, GD_HD), F32)],
        compiler_params=_cparams(("parallel", "arbitrary")),
        name="gdn_prompt",
    )(proj, proj, proj, proj, proj_sb, gp[0], gp[0], gp[0], *gp[1:])
    return y, st


def _gdn_sample(proj, proj_sb, conv0, st0, layer, gp, row0, nseq, tm):
    ns = tm // SUBLANES
    qb = P_Q // GD_WIDTH
    r0 = row0 // tm
    blk = lambda j: pl.BlockSpec((tm, GD_WIDTH), lambda i: (r0 + i, j))
    frm = lambda j: _layer_spec((ns, GD_CONV - 1, GD_WIDTH), lambda i: (i, 0, j), layer)
    st_spec = pl.BlockSpec((ns, GD_HEADS, GD_HD, GD_HD), lambda i: (i, 0, 0, 0))
    st_in = _layer_spec((ns, GD_HEADS, GD_HD, GD_HD), lambda i: (i, 0, 0, 0), layer)
    y, st = pl.pallas_call(
        functools.partial(_gdn_kernel, per_seq=True, tm=tm),
        grid=(nseq // ns,),
        in_specs=[blk(qb), blk(qb + 1), blk(qb + 2), blk(P_Z // GD_WIDTH),
                  pl.BlockSpec((tm, LANES), lambda i: (r0 + i, P_BA // LANES)),
                  frm(0), frm(1), frm(2), st_in] + _gdn_param_specs(),
        out_specs=[pl.BlockSpec((tm, GD_WIDTH), lambda i: (i, 0)), st_spec],
        out_shape=[jax.ShapeDtypeStruct((nseq * SUBLANES, GD_WIDTH), BF16),
                   jax.ShapeDtypeStruct((nseq, GD_HEADS, GD_HD, GD_HD), F32)],
        scratch_shapes=_gdn_scratch(tm),
        compiler_params=_cparams(("parallel",)),
        name="gdn_sample",
    )(proj, proj, proj, proj, proj_sb, conv0, conv0, conv0, st0, gp[0], gp[0], gp[0], *gp[1:])
    return y, st


def _gdn_params(conv_w, a_log, dt_bias, norm_g):
    pad = lambda t: jnp.concatenate([jnp.zeros((GD_HEADS,), F32), t, jnp.zeros((LANES - 2 * GD_HEADS,), F32)]
                                    ).reshape(1, LANES)
    return (conv_w, pad(a_log), pad(dt_bias), norm_g.reshape(1, GD_HD))


def _split_w_in(w):
    o_rw, o_qkv = S5_WIDTH, S5_WIDTH + RW_COLS
    o_b = o_qkv + 4 * GD_WIDTH
    o_g = o_b + 2 * GD_HEADS
    w_sb = jnp.concatenate([w[..., :S5_WIDTH], w[..., o_b:o_g],
                            jnp.zeros(w.shape[:-1] + (P_SB_COLS - P_BA - 2 * GD_HEADS,), w.dtype)], axis=-1)
    return (w[..., o_g:].astype(BF16), _rw_pad_cols(w[..., o_rw:o_qkv]).astype(BF16),
            w[..., o_qkv:o_b].astype(BF16), w_sb.astype(BF16))


def _tail_rows(a, cols, nrows, mp, seq):
    a3 = a.reshape(a.shape[0] // SUBLANES, SUBLANES, a.shape[1])
    g = seq // SUBLANES
    lo = SUBLANES - nrows
    return (a3[g - 1:mp // SUBLANES:g, lo:, cols[0]:cols[1]], a3[mp // SUBLANES:, lo:, cols[0]:cols[1]])


def kernel(x_prompt, x_sample, state_s5_re, state_s5_im, state_rwkv_shift, state_rwkv_wkv, state_gdn_conv, state_gdn, state_ffn_conv, norm1_g, norm2_g, final_norm_g, w_in, s5_lambda_re, s5_lambda_im, s5_b_re, s5_b_im, s5_c_re, s5_c_im, s5_d, s5_log_step, s5_w_glu, rwkv_mu, rwkv_w0, rwkv_w2, rwkv_a0, rwkv_a2, rwkv_g2, rwkv_k_k, rwkv_k_a, rwkv_r_k, rwkv_ln_w, rwkv_ln_b, gdn_conv_w, gdn_a_log, gdn_dt_bias, gdn_norm_g, w_br_s5, w_br_rwkv, w_br_gdn, w_out, ffn_w_up, ffn_conv_w, ffn_conv_b, ffn_w_down):
    nb, seq, d = x_prompt.shape
    ns, sl, _ = x_sample.shape
    assert sl == SUBLANES and d == D_MODEL
    mp, ms = nb * seq, ns * sl
    m = mp + ms
    tm_norm, tm_proj, tm_out = 512, 3072, 512
    x_groups = (x_prompt.reshape(mp, d), x_sample.reshape(ms, d), 0)

    w_gates, w_rw, w_qkvz, w_sb = _split_w_in(w_in)
    w_br = (w_br_s5.astype(BF16), w_br_rwkv.astype(BF16), w_br_gdn.astype(BF16))
    w_out_b, w_down_b = w_out.astype(BF16), ffn_w_down.astype(BF16)
    conv_b = ffn_conv_b.reshape(DEPTH, 1, 2 * D_FF)
    shift0 = _rw_pad_cols(state_rwkv_shift)[:, :, None, :]

    new_p = [[] for _ in range(7)]
    new_s = [[] for _ in range(7)]
    for l in range(DEPTH):
        xp, xs, s_off = x_groups
        xn1 = _norm_cast(xp, xs, s_off * (mp // tm_norm), m, mp // tm_norm, norm1_g[l], tm=tm_norm)
        proj_g = _matmul(xn1, w_gates, l, tm=tm_proj, tn=512)
        proj_rw = _matmul(xn1, w_rw, l, tm=tm_proj, tn=512)
        proj_qkvz = _matmul(xn1, w_qkvz, l, tm=tm_proj, tn=512)
        proj_sb = _matmul(xn1, w_sb, l, tm=tm_proj, tn=512)

        sp = _s5_params(s5_lambda_re[l], s5_lambda_im[l], s5_b_re[l], s5_b_im[l], s5_c_re[l], s5_c_im[l],
                        s5_d[l], s5_log_step[l], s5_w_glu[l])
        ys5_p, p_re, p_im = _s5_prompt(proj_sb, sp, nb, seq, tm=256)
        ys5_s, s_re, s_im = _s5_sample(proj_sb, state_s5_re[l], state_s5_im[l], sp, mp, ns, tm=128)

        rp = _rwkv_params(rwkv_mu[l], rwkv_w0[l], rwkv_w2[l], rwkv_a0[l], rwkv_a2[l], rwkv_g2[l], rwkv_k_k[l],
                          rwkv_k_a[l], rwkv_r_k[l].reshape(RW_WIDTH), rwkv_ln_w[l], rwkv_ln_b[l])
        yrw_p, p_wkv = _rwkv_prompt(proj_rw, rp, nb, seq, tm=256)
        yrw_s, s_wkv = _rwkv_sample(proj_rw, shift0[l], state_rwkv_wkv, l, rp, mp, ns, tm=128)
        p_shift, s_shift = [_rw_unpad_cols(t[:, 0])
                            for t in _tail_rows(proj_rw, (P_RW, P_RW + RW_PCOLS), 1, mp, seq)]

        gp = _gdn_params(gdn_conv_w[l], gdn_a_log[l], gdn_dt_bias[l], gdn_norm_g[l])
        ygd_p, p_gdn = _gdn_prompt(proj_qkvz, proj_sb, gp, nb, seq, tm=256)
        ygd_s, s_gdn = _gdn_sample(proj_qkvz, proj_sb, state_gdn_conv, state_gdn, l, gp, mp, ns, tm=64)
        p_gconv, s_gconv = _tail_rows(proj_qkvz, (P_Q, P_Q + 3 * GD_WIDTH), GD_CONV - 1, mp, seq)

        merged = _merge((ys5_p, yrw_p, ygd_p), (ys5_s, yrw_s, ygd_s), w_br, l, proj_g, tm=512, tn=512)
        x, xn = _out_norm(merged, w_out_b, l, xp, xs, s_off * (mp // tm_out), mp // tm_out, norm2_g[l], tm=tm_out)

        h, p_ffn, s_ffn = _ffn_up(xn, ffn_w_up, ffn_conv_w, conv_b, state_ffn_conv, l, nb, seq, ns,
                                  tm=1024, tn=512)
        x = _matmul_res(h, w_down_b, l, x, tm=1536, tn=512)
        x_groups = (x, x, 1)

        for lst, vals in ((new_p, (p_re, p_im, p_shift, p_wkv, p_gconv, p_gdn, p_ffn)),
                          (new_s, (s_re, s_im, s_shift, s_wkv, s_gconv, s_gdn, s_ffn))):
            for acc, val in zip(lst, vals):
                acc.append(val)

    y_p, y_s = _final_norm(x, final_norm_g, mp, tm=512)
    stack = lambda lst: tuple(jnp.stack(v) for v in lst)
    return (y_p.reshape(nb, seq, d), y_s.reshape(ns, sl, d)) + stack(new_p) + stack(new_s)
```

```python
import functools
import math

import jax
import jax.numpy as jnp
from jax import lax
from jax.experimental import pallas as pl
from jax.experimental.pallas import tpu as pltpu

F32 = jnp.float32
BF16 = jnp.bfloat16

SUBLANES = 8
LANES = 128
VMEM_LIMIT = 52 * 1024 * 1024
CHUNK = 64
SEQ_PER_CHUNK = CHUNK // SUBLANES

D_MODEL = 2048
DEPTH = 2
S5_WIDTH = 512
S5_GROUP = 16
S5_GROUPS = 32
S5_STATE = 64
S5_NSTATE = S5_GROUPS * S5_STATE
RW_WIDTH = 512
RW_HEADS = 8
RW_HD = 64
RW_LORA = 96
RW_GATE = 256
RW_COLS = 3 * RW_WIDTH + 2 * RW_LORA + RW_GATE
RW_PCOLS = 2048
RW_GN_EPS = 64e-5
GD_WIDTH = 1024
GD_HEADS = 8
GD_HD = 128
GD_CONV = 4
D_FF = 5632
FFN_CONV = 3
NORM_EPS = 1e-6
P_GATES, P_RW, P_Q, P_Z, P_S5, P_BA, P_SB_COLS = 0, 0, 0, 3072, 0, 512, 1024


def _cparams(sem):
    return pltpu.CompilerParams(dimension_semantics=sem, vmem_limit_bytes=VMEM_LIMIT)


def _mm(a, b):
    return jnp.dot(a, b, preferred_element_type=F32)


def _mm_nt(a, b):
    return lax.dot_general(a, b, (((1,), (1,)), ((), ())), preferred_element_type=F32)


def _mm_tn(a, b):
    return lax.dot_general(a, b, (((0,), (0,)), ((), ())), preferred_element_type=F32)


def _sigmoid(x):
    return 1.0 / (1.0 + jnp.exp(-x))


def _softplus(x):
    return jnp.maximum(x, 0.0) + jnp.log1p(jnp.exp(-jnp.abs(x)))


def _gelu_tanh(x):
    return 0.5 * x * (1.0 + jnp.tanh(math.sqrt(2.0 / math.pi) * (x + 0.044715 * (x * x * x))))


def _row_iota(shape):
    return lax.broadcasted_iota(jnp.int32, shape, 0)


def _shift_chain(x, d, prev8):
    n, ch = x.shape
    x3 = x.reshape(n // SUBLANES, SUBLANES, ch)
    rot = pltpu.roll(x3, d, 1)
    before = jnp.concatenate([pltpu.roll(prev8, d, 0)[None], rot[:-1]], axis=0)
    row = lax.broadcasted_iota(jnp.int32, x3.shape, 1)
    return jnp.where(row < d, before, rot).reshape(n, ch)


def _shift_seq8(x, d, state):
    n, ch = x.shape
    ns, w1, _ = state.shape
    out = pltpu.roll(x.reshape(ns, SUBLANES, ch), d, 1)
    row = lax.broadcasted_iota(jnp.int32, out.shape, 1)
    for r in range(d):
        out = jnp.where(row == r, state[:, w1 - d + r:w1 - d + r + 1, :], out)
    return out.reshape(n, ch)


def _group_last(x):
    c, n = x.shape
    x3 = x.reshape(c // SUBLANES, SUBLANES, n)
    return jnp.broadcast_to(x3[:, SUBLANES - 1:SUBLANES, :], x3.shape).reshape(c, n)


def _cumsum_rows(x, groupwise=False):
    c = x.shape[0]
    row = _row_iota(x.shape) % SUBLANES
    for d in (1, 2, 4):
        x = x + jnp.where(row >= d, pltpu.roll(x, d, 0), 0.0)
    if c > SUBLANES and not groupwise:
        blocks = [x[SUBLANES * i:SUBLANES * (i + 1)] for i in range(c // SUBLANES)]
        for i in range(1, len(blocks)):
            blocks[i] = blocks[i] + blocks[i - 1][SUBLANES - 1:SUBLANES, :]
        x = jnp.concatenate(blocks, axis=0)
    return x


def _split2(x):
    hi = x.astype(BF16)
    return hi, (x - hi.astype(F32)).astype(BF16)


def _mm_hilo_each(a_list, b_list):
    sa = [_split2(a) for a in a_list]
    sb = [_split2(b) for b in b_list]
    hh = [_mm(a[0], b[0]) for a, b in zip(sa, sb)]
    hl = [_mm(a[0], b[1]) for a, b in zip(sa, sb)]
    lh = [_mm(a[1], b[0]) for a, b in zip(sa, sb)]
    return [x + (y + z) for x, y, z in zip(hh, hl, lh)]


def _solve_unit_lower_each(lows, rhss, diag_only):
    c = lows[0].shape[0]
    nb = c // SUBLANES
    nh = len(lows)
    n = rhss[0].shape[1]
    blk = lambda a, i: a[SUBLANES * i:SUBLANES * (i + 1)]
    rb = [[blk(r, i) for i in range(nb)] for r in rhss]
    with_off = nb > 1 and not diag_only
    same = (lax.broadcasted_iota(jnp.int32, (c, c), 0) // SUBLANES
            == lax.broadcasted_iota(jnp.int32, (c, c), 1) // SUBLANES)
    if with_off:
        ob = [[blk(jnp.where(same, 0.0, low), i) for i in range(nb)] for low in lows]
    sel = (lax.broadcasted_iota(jnp.int32, (c, (SUBLANES - 1) * LANES), 0) % SUBLANES
           == lax.broadcasted_iota(jnp.int32, (c, (SUBLANES - 1) * LANES), 1) // LANES)
    sel = jnp.where(sel, 1.0, 0.0).astype(BF16)
    diag = [_split2(jnp.where(same, low, 0.0)) if not diag_only else _split2(low) for low in lows]
    cols_hi = [_mm(d[0], sel) for d in diag]
    cols_lo = [_mm(d[1], sel) for d in diag]
    cols = [a + b for a, b in zip(cols_hi, cols_lo)]
    reps = n // LANES
    for j in range(SUBLANES - 1):
        for h in range(nh):
            cj = cols[h][:, LANES * j:LANES * (j + 1)]
            cj_n = jnp.concatenate([cj] * reps, axis=1) if reps > 1 else cj
            for i in range(nb):
                rb[h][i] = rb[h][i] - blk(cj_n, i) * rb[h][i][j:j + 1, :]
                if with_off and i > 0:
                    ob[h][i] = ob[h][i] - blk(cj, i)[:, :c] * ob[h][i][j:j + 1, :]
    r1 = [jnp.concatenate(b, axis=0) if nb > 1 else b[0] for b in rb]
    if not with_off:
        return r1
    n1 = [jnp.concatenate(b, axis=0) for b in ob]
    n2 = _mm_hilo_each(n1, n1)
    t = _mm_hilo_each(n1 + n2, r1 + n2)
    y = [r - nr for r, nr in zip(r1, t[:nh])]
    n4 = t[nh:]
    y = [a + b for a, b in zip(y, _mm_hilo_each(n2, y))]
    return [a + b for a, b in zip(y, _mm_hilo_each(n4, y))]


def _chunks_per_iteration(n_chunks):
    return 2 if n_chunks % 2 == 0 else 1


def _chunk_masks(per_seq):
    ri = lax.broadcasted_iota(jnp.int32, (CHUNK, CHUNK), 0)
    ci = lax.broadcasted_iota(jnp.int32, (CHUNK, CHUNK), 1)
    strict, causal = ri > ci, ri >= ci
    if per_seq:
        same = (ri // SUBLANES) == (ci // SUBLANES)
        strict, causal = jnp.logical_and(strict, same), jnp.logical_and(causal, same)
    return strict, causal


def _seq_block_mask(rows, width):
    r = lax.broadcasted_iota(jnp.int32, (rows, SEQ_PER_CHUNK * width), 0)
    l = lax.broadcasted_iota(jnp.int32, (rows, SEQ_PER_CHUNK * width), 1)
    return ((r % CHUNK) // SUBLANES) == (l // width)


def _pick_own_seq(a, width):
    return jnp.concatenate([a[SUBLANES * s:SUBLANES * (s + 1), width * s:width * (s + 1)]
                            for s in range(SEQ_PER_CHUNK)], axis=0)


def _rmsnorm_rows(x_ref, g_ref, o_ref, rows, chunk=128):
    def body(i, carry):
        r0 = pl.multiple_of(i * chunk, chunk)
        x = x_ref[pl.ds(r0, chunk), :]
        ms = jnp.mean(x * x, axis=-1, keepdims=True)
        o_ref[pl.ds(r0, chunk), :] = (x * lax.rsqrt(ms + NORM_EPS) * g_ref[...]).astype(o_ref.dtype)
        return carry
    lax.fori_loop(0, rows // chunk, body, 0)


def _two_group_specs(tm, width, prompt_tiles, sample_off, pipeline_mode=None):
    p = pl.BlockSpec((tm, width), lambda i, *_: (jnp.minimum(i, prompt_tiles - 1), 0), pipeline_mode=pipeline_mode)
    s = pl.BlockSpec((tm, width), lambda i, *_: (jnp.maximum(i - prompt_tiles, 0) + sample_off, 0),
                     pipeline_mode=pipeline_mode)
    return p, s


def _layer_spec(shape, index_map, layer):
    return pl.BlockSpec((None,) + shape, lambda *a: (layer,) + tuple(index_map(*a)))


def _norm_cast_kernel(xp_ref, xs_ref, g_ref, o_ref, *, rows, prompt_tiles):
    i = pl.program_id(0)

    @pl.when(i < prompt_tiles)
    def _():
        _rmsnorm_rows(xp_ref, g_ref, o_ref, rows)

    @pl.when(i >= prompt_tiles)
    def _():
        _rmsnorm_rows(xs_ref, g_ref, o_ref, rows)


def _norm_cast(xp, xs, sample_off, m, prompt_tiles, g, tm):
    k = xp.shape[1]
    psp, ssp = _two_group_specs(tm, k, prompt_tiles, sample_off)
    return pl.pallas_call(
        functools.partial(_norm_cast_kernel, rows=tm, prompt_tiles=prompt_tiles),
        grid=(m // tm,),
        in_specs=[psp, ssp, pl.BlockSpec((1, k), lambda i: (0, 0))],
        out_specs=pl.BlockSpec((tm, k), lambda i: (i, 0)),
        out_shape=jax.ShapeDtypeStruct((m, k), BF16),
        compiler_params=_cparams(("arbitrary",)),
        name="norm_cast",
    )(xp, xs, g.reshape(1, k))


def _matmul_kernel(a_ref, w_ref, o_ref):
    o_ref[...] = _mm(a_ref[...], w_ref[...])


def _matmul(a, w, layer, tm, tn):
    m, k = a.shape
    n = w.shape[-1]
    return pl.pallas_call(
        _matmul_kernel,
        grid=(m // tm, n // tn),
        in_specs=[pl.BlockSpec((tm, k), lambda i, j: (i, 0), pipeline_mode=pl.Buffered(1)),
                  _layer_spec((k, tn), lambda i, j: (0, j), layer)],
        out_specs=pl.BlockSpec((tm, tn), lambda i, j: (i, j)),
        out_shape=jax.ShapeDtypeStruct((m, n), F32),
        compiler_params=_cparams(("parallel", "arbitrary")),
        name="proj_matmul",
    )(a, w)


def _matmul_res_kernel(a_ref, w_ref, r_ref, o_ref):
    o_ref[...] = r_ref[...] + _mm(a_ref[...], w_ref[...])


def _matmul_res(a, w, layer, res, tm, tn):
    m, k = a.shape
    n = w.shape[-1]
    return pl.pallas_call(
        _matmul_res_kernel,
        grid=(m // tm, n // tn),
        in_specs=[pl.BlockSpec((tm, k), lambda i, j: (i, 0), pipeline_mode=pl.Buffered(1)),
                  _layer_spec((k, tn), lambda i, j: (0, j), layer),
                  pl.BlockSpec((tm, tn), lambda i, j: (i, j))],
        out_specs=pl.BlockSpec((tm, tn), lambda i, j: (i, j)),
        out_shape=jax.ShapeDtypeStruct((m, n), F32),
        compiler_params=_cparams(("parallel", "arbitrary")),
        name="matmul_res",
    )(a, w, res)


def _out_norm_kernel(a_ref, w_ref, rp_ref, rs_ref, g_ref, x_ref, xn_ref, *, rows, prompt_tiles):
    res = jnp.where(pl.program_id(0) < prompt_tiles, rp_ref[...], rs_ref[...])
    x_ref[...] = res + _mm(a_ref[...], w_ref[...])
    _rmsnorm_rows(x_ref, g_ref, xn_ref, rows)


def _out_norm(a, w, layer, rp, rs, sample_off, prompt_tiles, g, tm):
    m, k = a.shape
    n = w.shape[-1]
    psp, ssp = _two_group_specs(tm, n, prompt_tiles, sample_off)
    return pl.pallas_call(
        functools.partial(_out_norm_kernel, rows=tm, prompt_tiles=prompt_tiles),
        grid=(m // tm,),
        in_specs=[pl.BlockSpec((tm, k), lambda i: (i, 0)),
                  _layer_spec((k, n), lambda i: (0, 0), layer),
                  psp, ssp, pl.BlockSpec((1, n), lambda i: (0, 0))],
        out_specs=[pl.BlockSpec((tm, n), lambda i: (i, 0)), pl.BlockSpec((tm, n), lambda i: (i, 0))],
        out_shape=[jax.ShapeDtypeStruct((m, n), F32), jax.ShapeDtypeStruct((m, n), BF16)],
        compiler_params=_cparams(("parallel",)),
        name="out_norm",
    )(a, w, rp, rs, g.reshape(1, n))


def _merge_kernel(y1p, y2p, y3p, y1s, y2s, y3s, w1_ref, w2_ref, w3_ref, g1_ref, g2_ref, g3_ref, o_ref, *,
                  prompt_tiles, tm, rc):
    is_p = pl.program_id(0) < prompt_tiles
    for r in range(tm // rc):
        rows = slice(r * rc, (r + 1) * rc)
        pick = lambda p, s: jnp.where(is_p, p[rows, :], s[rows, :])
        acc = _sigmoid(g1_ref[rows, :]) * _mm(pick(y1p, y1s), w1_ref[...])
        acc = acc + _sigmoid(g2_ref[rows, :]) * _mm(pick(y2p, y2s), w2_ref[...])
        acc = acc + _sigmoid(g3_ref[rows, :]) * _mm(pick(y3p, y3s), w3_ref[...])
        o_ref[rows, :] = acc.astype(BF16)


def _merge(ys_p, ys_s, ws, layer, proj, tm, tn):
    mp, ms = ys_p[0].shape[0], ys_s[0].shape[0]
    n = ws[0].shape[-1]
    nj = n // tn
    pt = mp // tm
    gate_spec = lambda b: pl.BlockSpec((tm, tn), lambda i, j: (i, P_GATES // tn + b * nj + j))
    y_specs = [_two_group_specs(tm, y.shape[1], pt, 0) for y in ys_p]
    w_spec = lambda w: _layer_spec((w.shape[1], tn), lambda i, j: (0, j), layer)
    return pl.pallas_call(
        functools.partial(_merge_kernel, prompt_tiles=pt, tm=tm, rc=128),
        grid=((mp + ms) // tm, nj),
        in_specs=[s[0] for s in y_specs] + [s[1] for s in y_specs] + [w_spec(w) for w in ws]
        + [gate_spec(0), gate_spec(1), gate_spec(2)],
        out_specs=pl.BlockSpec((tm, tn), lambda i, j: (i, j)),
        out_shape=jax.ShapeDtypeStruct((mp + ms, n), BF16),
        compiler_params=_cparams(("parallel", "arbitrary")),
        name="merge",
    )(*ys_p, *ys_s, *ws, proj, proj, proj)


def _final_norm_kernel(x_ref, g_ref, yp_ref, ys_ref, *, prompt_tiles):
    x = x_ref[...]
    ms = jnp.mean(x * x, axis=-1, keepdims=True)
    y = x * lax.rsqrt(ms + NORM_EPS) * g_ref[...]
    i = pl.program_id(0)

    @pl.when(i < prompt_tiles)
    def _():
        yp_ref[...] = y

    @pl.when(i >= prompt_tiles)
    def _():
        ys_ref[...] = y


def _final_norm(x, g, mp, tm):
    m, k = x.shape
    pt = mp // tm
    psp, ssp = _two_group_specs(tm, k, pt, 0)
    return pl.pallas_call(
        functools.partial(_final_norm_kernel, prompt_tiles=pt),
        grid=(m // tm,),
        in_specs=[pl.BlockSpec((tm, k), lambda i: (i, 0)), pl.BlockSpec((1, k), lambda i: (0, 0))],
        out_specs=[psp, ssp],
        out_shape=[jax.ShapeDtypeStruct((mp, k), F32), jax.ShapeDtypeStruct((m - mp, k), F32)],
        compiler_params=_cparams(("arbitrary",)),
        name="final_norm",
    )(x, g.reshape(1, k))


def _ffn_up_kernel(xn_ref, wg_ref, wv_ref, cwg_ref, cwv_ref, bg_ref, bv_ref, sg_ref, sv_ref,
                   h_ref, pg_ref, pv_ref, og_ref, ov_ref, wgb, wvb, cg, cv, *, tm, rc, prompt_tiles, tiles_per_seq):
    i = pl.program_id(1)
    w1 = FFN_CONV - 1

    @pl.when(i == 0)
    def _():
        wgb[...] = wg_ref[...].astype(BF16)
        wvb[...] = wv_ref[...].astype(BF16)

    def conv(u, s1, s2, cw_ref, b_ref):
        return cw_ref[2:3, :] * u + cw_ref[1:2, :] * s1 + cw_ref[0:1, :] * s2 + b_ref[...]

    def up(rows):
        x = xn_ref[rows, :]
        return _mm(x, wgb[...]), _mm(x, wvb[...])

    @pl.when(i < prompt_tiles)
    def _():
        first = (i % tiles_per_seq) == 0
        pg = jnp.where(first, 0.0, cg[...])
        pv = jnp.where(first, 0.0, cv[...])
        for r in range(tm // rc):
            rows = slice(r * rc, (r + 1) * rc)
            ug, uv = up(rows)
            gate = conv(ug, _shift_chain(ug, 1, pg), _shift_chain(ug, 2, pg), cwg_ref, bg_ref)
            val = conv(uv, _shift_chain(uv, 1, pv), _shift_chain(uv, 2, pv), cwv_ref, bv_ref)
            h_ref[rows, :] = (gate * _sigmoid(gate) * val).astype(BF16)
            pg, pv = ug[rc - SUBLANES:, :], uv[rc - SUBLANES:, :]
        cg[...] = pg
        cv[...] = pv
        pg_ref[0] = pg[SUBLANES - w1:, :]
        pv_ref[0] = pv[SUBLANES - w1:, :]

    @pl.when(i >= prompt_tiles)
    def _():
        nsr = rc // SUBLANES
        for r in range(tm // rc):
            rows = slice(r * rc, (r + 1) * rc)
            seqs = slice(r * nsr, (r + 1) * nsr)
            ug, uv = up(rows)
            sg, sv = sg_ref[seqs], sv_ref[seqs]
            gate = conv(ug, _shift_seq8(ug, 1, sg), _shift_seq8(ug, 2, sg), cwg_ref, bg_ref)
            val = conv(uv, _shift_seq8(uv, 1, sv), _shift_seq8(uv, 2, sv), cwv_ref, bv_ref)
            h_ref[rows, :] = (gate * _sigmoid(gate) * val).astype(BF16)
            og_ref[seqs] = ug.reshape(nsr, SUBLANES, ug.shape[1])[:, SUBLANES - w1:, :]
            ov_ref[seqs] = uv.reshape(nsr, SUBLANES, uv.shape[1])[:, SUBLANES - w1:, :]


def _ffn_up(xn, w_up, conv_w, conv_b, state, layer, nb, seq, ns, tm, tn):
    m, k = xn.shape
    mp = nb * seq
    assert ns * SUBLANES == tm and (m - mp) == tm and seq % tm == 0
    nj, pt, tps = D_FF // tn, mp // tm, seq // tm
    w1 = FFN_CONV - 1
    wsp = lambda off: _layer_spec((k, tn), lambda j, i: (0, off + j), layer)
    par = lambda rows, off: _layer_spec((rows, tn), lambda j, i: (0, off + j), layer)
    stsp = lambda off: _layer_spec((ns, w1, tn), lambda j, i: (0, 0, off + j), layer)
    psp = pl.BlockSpec((1, w1, tn), lambda j, i: (jnp.minimum(i, pt - 1) // tps, 0, j))
    osp = pl.BlockSpec((ns, w1, tn), lambda j, i: (0, 0, j))
    h, pg, pv, og, ov = pl.pallas_call(
        functools.partial(_ffn_up_kernel, tm=tm, rc=256, prompt_tiles=pt, tiles_per_seq=tps),
        grid=(nj, m // tm),
        in_specs=[pl.BlockSpec((tm, k), lambda j, i: (i, 0)), wsp(0), wsp(nj),
                  par(FFN_CONV, 0), par(FFN_CONV, nj), par(1, 0), par(1, nj), stsp(0), stsp(nj)],
        out_specs=[pl.BlockSpec((tm, tn), lambda j, i: (i, j)), psp, psp, osp, osp],
        out_shape=[jax.ShapeDtypeStruct((m, D_FF), BF16),
                   jax.ShapeDtypeStruct((nb, w1, D_FF), F32), jax.ShapeDtypeStruct((nb, w1, D_FF), F32),
                   jax.ShapeDtypeStruct((ns, w1, D_FF), F32), jax.ShapeDtypeStruct((ns, w1, D_FF), F32)],
        scratch_shapes=[pltpu.VMEM((k, tn), BF16), pltpu.VMEM((k, tn), BF16),
                        pltpu.VMEM((SUBLANES, tn), F32), pltpu.VMEM((SUBLANES, tn), F32)],
        compiler_params=_cparams(("arbitrary", "arbitrary")),
        name="ffn_up",
    )(xn, w_up, w_up, conv_w, conv_w, conv_b, conv_b, state, state)
    return h, jnp.concatenate([pg, pv], axis=-1), jnp.concatenate([og, ov], axis=-1)


S5_Q = 4
S5_QW = S5_NSTATE // S5_Q


def _s5_kernel(*refs, chained, tm):
    if chained:
        (u_ref, bre_ref, bim_ref, cre_ref, cim_ref, mre_ref, mim_ref, pre_ref, pim_ref, d_ref, wglu_ref,
         y_ref, xlre_ref, xlim_ref, sre, sim, car_re, car_im) = refs
    else:
        (u_ref, x0re_ref, x0im_ref, bre_ref, bim_ref, cre_ref, cim_ref, mre_ref, mim_ref, pre_ref, pim_ref,
         d_ref, wglu_ref, y_ref, xlre_ref, xlim_ref, sre, sim) = refs

    u = u_ref[...]
    ub = u.astype(BF16)
    for q in range(S5_Q):
        uq = ub[:, LANES * q:LANES * (q + 1)]
        sre[:, S5_QW * q:S5_QW * (q + 1)] = _mm(uq, bre_ref[q])
        sim[:, S5_QW * q:S5_QW * (q + 1)] = _mm(uq, bim_ref[q])

    if chained:
        @pl.when(pl.program_id(1) == 0)
        def _():
            car_re[...] = jnp.zeros_like(car_re)
            car_im[...] = jnp.zeros_like(car_im)

    def blk(i, carry):
        r0 = pl.multiple_of(i * SUBLANES, SUBLANES)
        for q in range(S5_Q):
            sl = slice(S5_QW * q, S5_QW * (q + 1))
            xr = sre[pl.ds(r0, SUBLANES), sl]
            xi = sim[pl.ds(r0, SUBLANES), sl]
            for li, d in enumerate((1, 2, 4)):
                mr, mi = mre_ref[li, :, sl], mim_ref[li, :, sl]
                sr, si = pltpu.roll(xr, d, 0), pltpu.roll(xi, d, 0)
                xr, xi = xr + (mr * sr - mi * si), xi + (mr * si + mi * sr)
            if chained:
                cr, ci = car_re[:, sl], car_im[:, sl]
            else:
                cr, ci = x0re_ref[pl.ds(i, 1), sl], x0im_ref[pl.ds(i, 1), sl]
            pr, pi_ = pre_ref[:, sl], pim_ref[:, sl]
            xr, xi = xr + (pr * cr - pi_ * ci), xi + (pr * ci + pi_ * cr)
            sre[pl.ds(r0, SUBLANES), sl] = xr
            sim[pl.ds(r0, SUBLANES), sl] = xi
            if chained:
                car_re[:, sl] = xr[SUBLANES - 1:SUBLANES, :]
                car_im[:, sl] = xi[SUBLANES - 1:SUBLANES, :]
            else:
                xlre_ref[pl.ds(i, 1), sl] = xr[SUBLANES - 1:SUBLANES, :]
                xlim_ref[pl.ds(i, 1), sl] = xi[SUBLANES - 1:SUBLANES, :]
        return carry

    lax.fori_loop(0, tm // SUBLANES, blk, 0)

    if chained:
        xlre_ref[0] = car_re[...]
        xlim_ref[0] = car_im[...]

    ys = []
    for q in range(S5_Q):
        sl = slice(S5_QW * q, S5_QW * (q + 1))
        ys.append(_mm(sre[:, sl].astype(BF16), cre_ref[q]) - _mm(sim[:, sl].astype(BF16), cim_ref[q]))
    y = jnp.concatenate(ys, axis=1) + d_ref[...] * u
    y = _gelu_tanh(y)
    y = y * _sigmoid(_mm(y.astype(BF16), wglu_ref[...]))
    y_ref[...] = y.astype(BF16)


def _const_spec(shape):
    zeros = (0,) * len(shape)
    return pl.BlockSpec(shape, lambda *a: zeros)


def _s5_param_specs():
    return [_const_spec((S5_Q, LANES, S5_QW)), _const_spec((S5_Q, LANES, S5_QW)),
            _const_spec((S5_Q, S5_QW, LANES)), _const_spec((S5_Q, S5_QW, LANES)),
            _const_spec((3, SUBLANES, S5_NSTATE)), _const_spec((3, SUBLANES, S5_NSTATE)),
            _const_spec((SUBLANES, S5_NSTATE)), _const_spec((SUBLANES, S5_NSTATE)),
            _const_spec((1, S5_WIDTH)), _const_spec((S5_WIDTH, S5_WIDTH))]


def _s5_prompt(proj, sp, nb, seq, tm):
    nt = seq // tm
    cb = P_S5 // S5_WIDTH
    y, xre, xim = pl.pallas_call(
        functools.partial(_s5_kernel, chained=True, tm=tm),
        grid=(nb, nt),
        in_specs=[pl.BlockSpec((tm, S5_WIDTH), lambda b, t: (b * nt + t, cb))] + _s5_param_specs(),
        out_specs=[pl.BlockSpec((tm, S5_WIDTH), lambda b, t: (b * nt + t, 0)),
                   pl.BlockSpec((1, 1, S5_NSTATE), lambda b, t: (b, 0, 0)),
                   pl.BlockSpec((1, 1, S5_NSTATE), lambda b, t: (b, 0, 0))],
        out_shape=[jax.ShapeDtypeStruct((nb * seq, S5_WIDTH), BF16),
                   jax.ShapeDtypeStruct((nb, 1, S5_NSTATE), F32),
                   jax.ShapeDtypeStruct((nb, 1, S5_NSTATE), F32)],
        scratch_shapes=[pltpu.VMEM((tm, S5_NSTATE), F32), pltpu.VMEM((tm, S5_NSTATE), F32),
                        pltpu.VMEM((1, S5_NSTATE), F32), pltpu.VMEM((1, S5_NSTATE), F32)],
        compiler_params=_cparams(("parallel", "arbitrary")),
        name="s5_prompt",
    )(proj, *sp)
    return y, xre.reshape(nb, S5_GROUPS, S5_STATE), xim.reshape(nb, S5_GROUPS, S5_STATE)


def _s5_sample(proj, x0re, x0im, sp, row0, nseq, tm):
    ns = tm // SUBLANES
    cb = P_S5 // S5_WIDTH
    r0 = row0 // tm
    st = pl.BlockSpec((ns, S5_NSTATE), lambda i: (i, 0))
    y, xre, xim = pl.pallas_call(
        functools.partial(_s5_kernel, chained=False, tm=tm),
        grid=(nseq // ns,),
        in_specs=[pl.BlockSpec((tm, S5_WIDTH), lambda i: (r0 + i, cb)), st, st] + _s5_param_specs(),
        out_specs=[pl.BlockSpec((tm, S5_WIDTH), lambda i: (i, 0)), st, st],
        out_shape=[jax.ShapeDtypeStruct((nseq * SUBLANES, S5_WIDTH), BF16),
                   jax.ShapeDtypeStruct((nseq, S5_NSTATE), F32),
                   jax.ShapeDtypeStruct((nseq, S5_NSTATE), F32)],
        scratch_shapes=[pltpu.VMEM((tm, S5_NSTATE), F32), pltpu.VMEM((tm, S5_NSTATE), F32)],
        compiler_params=_cparams(("parallel",)),
        name="s5_sample",
    )(proj, x0re.reshape(nseq, S5_NSTATE), x0im.reshape(nseq, S5_NSTATE), *sp)
    return y, xre.reshape(nseq, S5_GROUPS, S5_STATE), xim.reshape(nseq, S5_GROUPS, S5_STATE)


def _s5_params(lam_re, lam_im, b_re, b_im, c_re, c_im, d, log_step, w_glu):
    lam = lax.complex(lam_re, lam_im)
    step = jnp.exp(log_step)[:, None]
    lam_bar = jnp.exp(lam * step)
    b_bar = ((lam_bar - 1.0) / lam)[..., None] * lax.complex(b_re, b_im)

    def bblk(bm):
        bm = bm.reshape(S5_Q, 8, S5_STATE, S5_GROUP)
        eye = jnp.eye(8, dtype=F32)
        out = jnp.einsum('qgpc,gh->qgchp', bm, eye)
        return out.reshape(S5_Q, LANES, S5_QW).astype(BF16)

    def cblk(cm):
        cm = cm.reshape(S5_Q, 8, S5_GROUP, S5_STATE)
        eye = jnp.eye(8, dtype=F32)
        out = jnp.einsum('qgcp,gh->qgphc', cm, eye)
        return out.reshape(S5_Q, S5_QW, LANES).astype(BF16)

    lam_flat = lam_bar.reshape(1, S5_NSTATE)
    pows = [lam_flat]
    for _ in range(SUBLANES - 1):
        pows.append(pows[-1] * lam_flat)
    row = jnp.arange(SUBLANES)[:, None]
    m = jnp.stack([jnp.where(row >= dd, pows[dd - 1], 0.0) for dd in (1, 2, 4)])
    p = jnp.concatenate(pows, axis=0)
    return (bblk(b_bar.real), bblk(b_bar.imag), cblk(c_re), cblk(c_im),
            m.real, m.imag, p.real, p.imag, d.reshape(1, S5_WIDTH), w_glu.astype(BF16))


def _seg64_sum(x):
    outs = []
    for t in range(x.shape[1] // LANES):
        xt = x[:, LANES * t:LANES * (t + 1)]
        lo = lax.broadcasted_iota(jnp.int32, xt.shape, 1) < RW_HD
        s_lo = jnp.sum(jnp.where(lo, xt, 0.0), axis=-1, keepdims=True)
        s_hi = jnp.sum(jnp.where(lo, 0.0, xt), axis=-1, keepdims=True)
        outs.append(jnp.where(lo, s_lo, s_hi))
    return jnp.concatenate(outs, axis=1)


def _rwkv_kernel(*refs, per_seq, tm, aliased=False):
    if aliased:
        refs = refs[1:]
    if per_seq:
        (h_ref, sh_ref, si_ref, mu_ref, w0_ref, w2_ref, a0_ref, a2_ref, g2_ref, kk_ref, ka_ref, rk_ref,
         lnw_ref, lnb_ref, y_ref, so_ref, r_s, k_s, v_s, a_s, b_s, lw_s, y_s) = refs
    else:
        (h_ref, mu_ref, w0_ref, w2_ref, a0_ref, a2_ref, g2_ref, kk_ref, ka_ref, rk_ref, lnw_ref, lnb_ref,
         y_ref, so_ref, r_s, k_s, v_s, a_s, b_s, lw_s, y_s, prev_s, st_s) = refs
    c, w, hd_w = CHUNK, RW_WIDTH, RW_HD

    h = h_ref[...]
    if per_seq:
        prev = _shift_seq8(h, 1, sh_ref[...])
    else:
        @pl.when(pl.program_id(1) == 0)
        def _():
            prev_s[...] = jnp.zeros_like(prev_s)
            st_s[...] = jnp.zeros_like(st_s)
        prev = _shift_chain(h, 1, prev_s[...])
        prev_s[...] = h[tm - SUBLANES:, :]
    hs = h + (prev - h) * mu_ref[...]
    r = hs[:, 0:w]
    k = hs[:, w:2 * w]
    v = hs[:, 2 * w:3 * w]
    wd = hs[:, 3 * w:3 * w + LANES]
    ad = hs[:, 3 * w + LANES:3 * w + 2 * LANES]
    gd = hs[:, 3 * w + 2 * LANES:]
    w_log = -_softplus(-(w0_ref[...] + _mm(jnp.tanh(wd).astype(BF16), w2_ref[...]))) - 0.5
    lw_s[...] = -jnp.exp(w_log)
    a_ic = _sigmoid(a0_ref[...] + _mm(ad.astype(BF16), a2_ref[...]))
    gate = _mm(_sigmoid(gd).astype(BF16), g2_ref[...])
    kx = k * kk_ref[...]
    kkn = kx * lax.rsqrt(_seg64_sum(kx * kx) + 1e-6)
    k2 = k * (1.0 + (a_ic - 1.0) * ka_ref[...])
    r_s[...] = r
    k_s[...] = k2
    v_s[...] = v
    a_s[...] = -kkn
    b_s[...] = kkn * a_ic

    strict, causal = _chunk_masks(per_seq)
    if per_seq:
        big_mask = _seq_block_mask(2 * c, hd_w)

    heads = range(RW_HEADS)
    sls = [slice(hd_w * hd, hd_w * (hd + 1)) for hd in heads]
    group = _chunks_per_iteration(tm // c)

    def chunk_group(gidx, carry):
        pre = []
        for t in range(group):
            idx = gidx * group + t
            rows = pl.ds(pl.multiple_of(idx * c, c), c)
            lw = lw_s[rows, :]
            cum = _cumsum_rows(lw, groupwise=per_seq)
            cl = _group_last(cum) if per_seq else cum[c - 1:c, :]
            e_pos, e_neg, e_end = jnp.exp(cum), jnp.exp(-cum), jnp.exp(cl - cum)
            rr, kc, vc, ac, bc = r_s[rows, :], k_s[rows, :], v_s[rows, :], a_s[rows, :], b_s[rows, :]
            pre.append(dict(rows=rows, seqs=pl.ds(idx * SEQ_PER_CHUNK, SEQ_PER_CHUNK), vc=vc,
                            rt=rr * e_pos, kt=kc * e_neg, bt=bc * e_neg, at=ac * jnp.exp(cum - lw),
                            kh=kc * e_end, bh=bc * e_end, wc=jnp.exp(cl)))
        pairs = [(t, hd) for t in range(group) for hd in heads]
        ath = [pre[t]['at'][:, sls[hd]] for t, hd in pairs]
        vh = [pre[t]['vc'][:, sls[hd]] for t, hd in pairs]
        rth = [pre[t]['rt'][:, sls[hd]] for t, hd in pairs]
        ab = [_mm_nt(jnp.concatenate([ath[p], rth[p]], axis=0).astype(BF16),
                     jnp.concatenate([pre[t]['kt'][:, sls[hd]], pre[t]['bt'][:, sls[hd]]], axis=0).astype(BF16))
              for p, (t, hd) in enumerate(pairs)]
        a_ak = [jnp.where(strict, m[:c, :c], 0.0) for m in ab]
        n_ab = [jnp.where(strict, -m[:c, c:], 0.0) for m in ab]
        a_r = [jnp.concatenate([jnp.where(causal, m[c:, :c], 0.0), jnp.where(causal, m[c:, c:], 0.0)],
                               axis=1).astype(BF16) for m in ab]
        akv = [_mm(a_ak[p].astype(BF16), vh[p].astype(BF16)) for p in range(len(pairs))]
        sol = _solve_unit_lower_each(
            n_ab, [jnp.concatenate([ath[p], akv[p]], axis=1) for p in range(len(pairs))], per_seq)
        kb = [jnp.concatenate([pre[t]['kh'][:, sls[hd]], pre[t]['bh'][:, sls[hd]]], axis=0).astype(BF16)
              for t, hd in pairs]
        for t in range(group):
            rows, seqs, wc = pre[t]['rows'], pre[t]['seqs'], pre[t]['wc']
            ps = [t * RW_HEADS + hd for hd in heads]
            if per_seq:
                st = [si_ref[seqs, hd].reshape(SEQ_PER_CHUNK * hd_w, hd_w) for hd in heads]
            else:
                st = [st_s[hd] for hd in heads]
            pr = [_mm_nt(jnp.concatenate([sol[p][:, :hd_w], rth[p]], axis=0).astype(BF16), st[hd].astype(BF16))
                  for hd, p in zip(heads, ps)]
            if per_seq:
                u = [_pick_own_seq(pr[hd][:c], hd_w) + sol[p][:, hd_w:] for hd, p in zip(heads, ps)]
                yst = [_pick_own_seq(pr[hd][c:], hd_w) for hd in heads]
            else:
                u = [pr[hd][:c] + sol[p][:, hd_w:] for hd, p in zip(heads, ps)]
                yst = [pr[hd][c:] for hd in heads]
            vu = [jnp.concatenate([vh[p], u[hd]], axis=0) for hd, p in zip(heads, ps)]
            ya = [_mm(a_r[p], vu[hd].astype(BF16)) for hd, p in zip(heads, ps)]
            if per_seq:
                upd = [_mm_tn(jnp.where(big_mask, jnp.concatenate([vu[hd]] * SEQ_PER_CHUNK, axis=1), 0.0
                                        ).astype(BF16), kb[p]) for hd, p in zip(heads, ps)]
            else:
                upd = [_mm_tn(vu[hd].astype(BF16), kb[p]) for hd, p in zip(heads, ps)]
            for hd in heads:
                if per_seq:
                    wcb = jnp.concatenate(
                        [jnp.broadcast_to(wc[SUBLANES * s:SUBLANES * s + 1, sls[hd]], (hd_w, hd_w))
                         for s in range(SEQ_PER_CHUNK)], axis=0)
                    so_ref[seqs, hd] = (st[hd] * wcb + upd[hd]).reshape(SEQ_PER_CHUNK, hd_w, hd_w)
                else:
                    st_s[hd] = st[hd] * wc[:, sls[hd]] + upd[hd]
                y_s[rows, sls[hd]] = yst[hd] + ya[hd]
        return carry

    lax.fori_loop(0, tm // (c * group), chunk_group, 0)

    if not per_seq:
        so_ref[0] = st_s[...]

    y = y_s[...]
    mean = _seg64_sum(y) * (1.0 / hd_w)
    yc = y - mean
    var = _seg64_sum(yc * yc) * (1.0 / hd_w)
    yn = yc * lax.rsqrt(var + RW_GN_EPS) * lnw_ref[...] + lnb_ref[...]
    bonus = _seg64_sum(r_s[...] * k_s[...] * rk_ref[...]) * v_s[...]
    y_ref[...] = ((yn + bonus) * gate).astype(BF16)


def _rwkv_param_specs():
    row = _const_spec((1, RW_WIDTH))
    return [_const_spec((1, RW_PCOLS)), row, _const_spec((LANES, RW_WIDTH)), row, _const_spec((LANES, RW_WIDTH)),
            _const_spec((RW_GATE, RW_WIDTH)), row, row, row, row, row]


def _rwkv_scratch(tm):
    return [pltpu.VMEM((tm, RW_WIDTH), F32) for _ in range(7)]


def _rwkv_prompt(proj, rp, nb, seq, tm):
    nt = seq // tm
    cb = P_RW // RW_PCOLS
    y, st = pl.pallas_call(
        functools.partial(_rwkv_kernel, per_seq=False, tm=tm),
        grid=(nb, nt),
        in_specs=[pl.BlockSpec((tm, RW_PCOLS), lambda b, t: (b * nt + t, cb))] + _rwkv_param_specs(),
        out_specs=[pl.BlockSpec((tm, RW_WIDTH), lambda b, t: (b * nt + t, 0)),
                   pl.BlockSpec((1, RW_HEADS, RW_HD, RW_HD), lambda b, t: (b, 0, 0, 0))],
        out_shape=[jax.ShapeDtypeStruct((nb * seq, RW_WIDTH), BF16),
                   jax.ShapeDtypeStruct((nb, RW_HEADS, RW_HD, RW_HD), F32)],
        scratch_shapes=_rwkv_scratch(tm) + [pltpu.VMEM((SUBLANES, RW_PCOLS), F32),
                                            pltpu.VMEM((RW_HEADS, RW_HD, RW_HD), F32)],
        compiler_params=_cparams(("parallel", "arbitrary")),
        name="rwkv_prompt",
    )(proj, *rp)
    return y, st


def _stacked_out(prev):
    if prev is None:
        return (), [], {}
    return (prev,), [pl.BlockSpec(memory_space=pl.ANY)], {0: 1}


def _rwkv_sample(proj, shift0, st0, layer, prev_st, rp, row0, nseq, tm):
    ns = tm // SUBLANES
    cb = P_RW // RW_PCOLS
    r0 = row0 // tm
    st_io = _layer_spec((ns, RW_HEADS, RW_HD, RW_HD), lambda i: (i, 0, 0, 0), layer)
    extra, extra_specs, aliases = _stacked_out(prev_st)
    y, st = pl.pallas_call(
        functools.partial(_rwkv_kernel, per_seq=True, tm=tm, aliased=prev_st is not None),
        grid=(nseq // ns,),
        in_specs=extra_specs + [pl.BlockSpec((tm, RW_PCOLS), lambda i: (r0 + i, cb)),
                                pl.BlockSpec((ns, 1, RW_PCOLS), lambda i: (i, 0, 0)), st_io] + _rwkv_param_specs(),
        out_specs=[pl.BlockSpec((tm, RW_WIDTH), lambda i: (i, 0)), st_io],
        out_shape=[jax.ShapeDtypeStruct((nseq * SUBLANES, RW_WIDTH), BF16),
                   jax.ShapeDtypeStruct(st0.shape, F32)],
        input_output_aliases=aliases,
        scratch_shapes=_rwkv_scratch(tm),
        compiler_params=_cparams(("parallel",)),
        name="rwkv_sample",
    )(*extra, proj, shift0, st0, *rp)
    return y, st


def _rw_pad_cols(x):
    z = jnp.zeros(x.shape[:-1] + (LANES - RW_LORA,), x.dtype)
    a, b = 3 * RW_WIDTH, 3 * RW_WIDTH + RW_LORA
    return jnp.concatenate([x[..., :a], x[..., a:b], z, x[..., b:b + RW_LORA], z, x[..., b + RW_LORA:]], axis=-1)


def _rw_unpad_cols(x):
    a = 3 * RW_WIDTH
    return jnp.concatenate([x[..., :a + RW_LORA], x[..., a + LANES:a + LANES + RW_LORA], x[..., a + 2 * LANES:]],
                           axis=-1)


def _rwkv_params(mu, w0, w2, a0, a2, g2, k_k, k_a, r_k, ln_w, ln_b):
    row = lambda t: t.reshape(1, RW_WIDTH)
    padr = lambda t: jnp.concatenate([t, jnp.zeros((LANES - RW_LORA, RW_WIDTH), t.dtype)], axis=0).astype(BF16)
    return (_rw_pad_cols(mu).reshape(1, RW_PCOLS), row(w0), padr(w2), row(a0), padr(a2), g2.astype(BF16),
            row(k_k), row(k_a), row(r_k), row(ln_w), row(ln_b))


def _gdn_kernel(*refs, per_seq, tm, aliased=False):
    if aliased:
        refs = refs[1:]
    if per_seq:
        (q_ref, k_ref, v_ref, z_ref, ba_ref, fq_ref, fk_ref, fv_ref, si_ref, cwq_ref, cwk_ref, cwv_ref,
         al_ref, dt_ref, ng_ref, y_ref, so_ref, q_s, k_s, v_s, ba_s, o_s) = refs
        conv_state = (fq_ref, fk_ref, fv_ref)
    else:
        (q_ref, k_ref, v_ref, z_ref, ba_ref, cwq_ref, cwk_ref, cwv_ref, al_ref, dt_ref, ng_ref,
         y_ref, so_ref, q_s, k_s, v_s, ba_s, o_s, pq_s, pk_s, pv_s, st_s) = refs
        prevs = (pq_s, pk_s, pv_s)

        @pl.when(pl.program_id(1) == 0)
        def _():
            for p in prevs:
                p[...] = jnp.zeros_like(p)
            st_s[...] = jnp.zeros_like(st_s)
    c, hw = CHUNK, GD_HD

    def conv_act(idx, u_ref, cw_ref):
        u = u_ref[...]
        if per_seq:
            f = conv_state[idx][...]
            sh = [_shift_seq8(u, d, f) for d in (1, 2, 3)]
        else:
            p = prevs[idx][...]
            sh = [_shift_chain(u, d, p) for d in (1, 2, 3)]
            prevs[idx][...] = u[tm - SUBLANES:, :]
        x = cw_ref[3:4, :] * u + cw_ref[2:3, :] * sh[0] + cw_ref[1:2, :] * sh[1] + cw_ref[0:1, :] * sh[2]
        return x * _sigmoid(x)

    def l2n(x):
        outs = []
        for hd in range(GD_HEADS):
            xh = x[:, hw * hd:hw * (hd + 1)]
            outs.append(xh * lax.rsqrt(jnp.sum(xh * xh, axis=-1, keepdims=True) + 1e-6))
        return jnp.concatenate(outs, axis=1)

    q_s[...] = l2n(conv_act(0, q_ref, cwq_ref)) * (hw ** -0.5)
    k_s[...] = l2n(conv_act(1, k_ref, cwk_ref))
    v_s[...] = conv_act(2, v_ref, cwv_ref)
    ba = ba_ref[...]
    lane = lax.broadcasted_iota(jnp.int32, ba.shape, 1)
    g_all = -jnp.exp(al_ref[...]) * _softplus(ba + dt_ref[...])
    ba_s[...] = jnp.where(lane < GD_HEADS, _sigmoid(ba), g_all)

    strict, causal = _chunk_masks(per_seq)
    if per_seq:
        big_mask = _seq_block_mask(c, hw)

    heads = range(GD_HEADS)
    sls = [slice(hw * hd, hw * (hd + 1)) for hd in heads]
    group = _chunks_per_iteration(tm // c)

    def chunk_group(gidx, carry):
        pre = []
        for t in range(group):
            idx = gidx * group + t
            rows = pl.ds(pl.multiple_of(idx * c, c), c)
            bg = ba_s[rows, :]
            gc_all = _cumsum_rows(bg, groupwise=per_seq)
            pre.append(dict(rows=rows, seqs=pl.ds(idx * SEQ_PER_CHUNK, SEQ_PER_CHUNK), bg=bg, gc_all=gc_all,
                            gc_t=gc_all.T, gl_all=_group_last(gc_all) if per_seq else gc_all[c - 1:c, :],
                            qa=q_s[rows, :], ka=k_s[rows, :], va=v_s[rows, :]))
        pairs = [(t, hd) for t in range(group) for hd in heads]
        npair = len(pairs)
        beta = [pre[t]['bg'][:, hd:hd + 1] for t, hd in pairs]
        gc = [pre[t]['gc_all'][:, GD_HEADS + hd:GD_HEADS + hd + 1] for t, hd in pairs]
        gl = [pre[t]['gl_all'][:, GD_HEADS + hd:GD_HEADS + hd + 1] for t, hd in pairs]
        dec = [jnp.exp(jnp.where(causal, gc[p] - pre[t]['gc_t'][GD_HEADS + hd:GD_HEADS + hd + 1, :], -1e30))
               for p, (t, hd) in enumerate(pairs)]
        qh = [pre[t]['qa'][:, sls[hd]] for t, hd in pairs]
        kh = [pre[t]['ka'][:, sls[hd]] for t, hd in pairs]
        kb = [kh[p] * beta[p] for p in range(npair)]
        eg = [jnp.exp(g) for g in gc]
        kq = [_mm_nt(jnp.concatenate([kb[p], qh[p]], axis=0).astype(BF16), kh[p].astype(BF16))
              for p in range(npair)]
        low = [jnp.where(strict, kq[p][:c] * dec[p], 0.0) for p in range(npair)]
        intra = [(kq[p][c:] * dec[p]).astype(BF16) for p in range(npair)]
        sol = _solve_unit_lower_each(
            low, [jnp.concatenate([pre[t]['va'][:, sls[hd]] * beta[p], kb[p] * eg[p]], axis=1)
                  for p, (t, hd) in enumerate(pairs)], per_seq)
        wq = [jnp.concatenate([sol[p][:, hw:], qh[p] * eg[p]], axis=0).astype(BF16) for p in range(npair)]
        kdec = [kh[p] * jnp.exp(gl[p] - gc[p]) for p in range(npair)]
        for t in range(group):
            rows, seqs = pre[t]['rows'], pre[t]['seqs']
            ps = [t * GD_HEADS + hd for hd in heads]
            if per_seq:
                st3 = [si_ref[seqs, hd] for hd in heads]
                st_rhs = [jnp.concatenate([s3[s] for s in range(SEQ_PER_CHUNK)], axis=1).astype(BF16)
                          for s3 in st3]
            else:
                st = [st_s[hd] for hd in heads]
                st_rhs = [s.astype(BF16) for s in st]
            ws = [_mm(wq[p], st_rhs[hd]) for hd, p in zip(heads, ps)]
            if per_seq:
                v_new = [sol[p][:, :hw] - _pick_own_seq(ws[hd][:c], hw) for hd, p in zip(heads, ps)]
                o_st = [_pick_own_seq(ws[hd][c:], hw) for hd in heads]
            else:
                v_new = [sol[p][:, :hw] - ws[hd][:c] for hd, p in zip(heads, ps)]
                o_st = [ws[hd][c:] for hd in heads]
            vnb = [v.astype(BF16) for v in v_new]
            o_in = [_mm(intra[p], vnb[hd]) for hd, p in zip(heads, ps)]
            if per_seq:
                upd = [_mm_tn(jnp.where(big_mask, jnp.concatenate([kdec[p]] * SEQ_PER_CHUNK, axis=1), 0.0
                                        ).astype(BF16), vnb[hd]) for hd, p in zip(heads, ps)]
            else:
                upd = [_mm_tn(kdec[p].astype(BF16), vnb[hd]) for hd, p in zip(heads, ps)]
            for hd, p in zip(heads, ps):
                if per_seq:
                    egl = jnp.exp(gl[p])
                    mult = jnp.concatenate([jnp.broadcast_to(egl[SUBLANES * s:SUBLANES * s + 1, :], (hw, hw))
                                            for s in range(SEQ_PER_CHUNK)], axis=0)
                    st_new = st3[hd].reshape(SEQ_PER_CHUNK * hw, hw) * mult + upd[hd]
                    so_ref[seqs, hd] = st_new.reshape(SEQ_PER_CHUNK, hw, hw)
                else:
                    st_s[hd] = st[hd] * jnp.exp(gl[p]) + upd[hd]
                o_s[rows, sls[hd]] = o_st[hd] + o_in[hd]
        return carry

    lax.fori_loop(0, tm // (c * group), chunk_group, 0)

    if not per_seq:
        so_ref[0] = st_s[...]

    outs = []
    for hd in range(GD_HEADS):
        sl = slice(hw * hd, hw * (hd + 1))
        o = o_s[:, sl]
        o = o * lax.rsqrt(jnp.mean(o * o, axis=-1, keepdims=True) + NORM_EPS) * ng_ref[...]
        z = z_ref[:, sl]
        outs.append(o * (z * _sigmoid(z)))
    y_ref[...] = jnp.concatenate(outs, axis=1).astype(BF16)


def _gdn_param_specs():
    c2 = lambda shape, j=0: pl.BlockSpec(shape, (lambda *a: (0, j)))
    return [c2((GD_CONV, GD_WIDTH), 0), c2((GD_CONV, GD_WIDTH), 1), c2((GD_CONV, GD_WIDTH), 2),
            c2((1, LANES)), c2((1, LANES)), c2((1, GD_HD))]


def _gdn_scratch(tm):
    return ([pltpu.VMEM((tm, GD_WIDTH), F32) for _ in range(3)] + [pltpu.VMEM((tm, LANES), F32),
                                                                  pltpu.VMEM((tm, GD_WIDTH), F32)])


def _gdn_prompt(proj, proj_sb, gp, nb, seq, tm):
    nt = seq // tm
    qb = P_Q // GD_WIDTH
    blk = lambda j: pl.BlockSpec((tm, GD_WIDTH), lambda b, t: (b * nt + t, j))
    y, st = pl.pallas_call(
        functools.partial(_gdn_kernel, per_seq=False, tm=tm),
        grid=(nb, nt),
        in_specs=[blk(qb), blk(qb + 1), blk(qb + 2), blk(P_Z // GD_WIDTH),
                  pl.BlockSpec((tm, LANES), lambda b, t: (b * nt + t, P_BA // LANES))] + _gdn_param_specs(),
        out_specs=[pl.BlockSpec((tm, GD_WIDTH), lambda b, t: (b * nt + t, 0)),
                   pl.BlockSpec((1, GD_HEADS, GD_HD, GD_HD), lambda b, t: (b, 0, 0, 0))],
        out_shape=[jax.ShapeDtypeStruct((nb * seq, GD_WIDTH), BF16),
                   jax.ShapeDtypeStruct((nb, GD_HEADS, GD_HD, GD_HD), F32)],
        scratch_shapes=_gdn_scratch(tm) + [pltpu.VMEM((SUBLANES, GD_WIDTH), F32) for _ in range(3)]
        + [pltpu.VMEM((GD_HEADS, GD_HD, GD_HD), F32)],
        compiler_params=_cparams(("parallel", "arbitrary")),
        name="gdn_prompt",
    )(proj, proj, proj, proj, proj_sb, gp[0], gp[0], gp[0], *gp[1:])
    return y, st


def _gdn_sample(proj, proj_sb, conv0, st0, layer, prev_st, gp, row0, nseq, tm):
    ns = tm // SUBLANES
    qb = P_Q // GD_WIDTH
    r0 = row0 // tm
    blk = lambda j: pl.BlockSpec((tm, GD_WIDTH), lambda i: (r0 + i, j))
    frm = lambda j: _layer_spec((ns, GD_CONV - 1, GD_WIDTH), lambda i: (i, 0, j), layer)
    st_io = _layer_spec((ns, GD_HEADS, GD_HD, GD_HD), lambda i: (i, 0, 0, 0), layer)
    extra, extra_specs, aliases = _stacked_out(prev_st)
    y, st = pl.pallas_call(
        functools.partial(_gdn_kernel, per_seq=True, tm=tm, aliased=prev_st is not None),
        grid=(nseq // ns,),
        in_specs=extra_specs + [blk(qb), blk(qb + 1), blk(qb + 2), blk(P_Z // GD_WIDTH),
                                pl.BlockSpec((tm, LANES), lambda i: (r0 + i, P_BA // LANES)),
                                frm(0), frm(1), frm(2), st_io] + _gdn_param_specs(),
        out_specs=[pl.BlockSpec((tm, GD_WIDTH), lambda i: (i, 0)), st_io],
        out_shape=[jax.ShapeDtypeStruct((nseq * SUBLANES, GD_WIDTH), BF16),
                   jax.ShapeDtypeStruct(st0.shape, F32)],
        input_output_aliases=aliases,
        scratch_shapes=_gdn_scratch(tm),
        compiler_params=_cparams(("parallel",)),
        name="gdn_sample",
    )(*extra, proj, proj, proj, proj, proj_sb, conv0, conv0, conv0, st0, gp[0], gp[0], gp[0], *gp[1:])
    return y, st


def _gdn_params(conv_w, a_log, dt_bias, norm_g):
    pad = lambda t: jnp.concatenate([jnp.zeros((GD_HEADS,), F32), t, jnp.zeros((LANES - 2 * GD_HEADS,), F32)]
                                    ).reshape(1, LANES)
    return (conv_w, pad(a_log), pad(dt_bias), norm_g.reshape(1, GD_HD))


def _split_w_in(w):
    o_rw, o_qkv = S5_WIDTH, S5_WIDTH + RW_COLS
    o_b = o_qkv + 4 * GD_WIDTH
    o_g = o_b + 2 * GD_HEADS
    w_sb = jnp.concatenate([w[..., :S5_WIDTH], w[..., o_b:o_g],
                            jnp.zeros(w.shape[:-1] + (P_SB_COLS - P_BA - 2 * GD_HEADS,), w.dtype)], axis=-1)
    return (w[..., o_g:].astype(BF16), _rw_pad_cols(w[..., o_rw:o_qkv]).astype(BF16),
            w[..., o_qkv:o_b].astype(BF16), w_sb.astype(BF16))


def _tail_rows(a, cols, nrows, mp, seq):
    a3 = a.reshape(a.shape[0] // SUBLANES, SUBLANES, a.shape[1])
    g = seq // SUBLANES
    lo = SUBLANES - nrows
    return (a3[g - 1:mp // SUBLANES:g, lo:, cols[0]:cols[1]], a3[mp // SUBLANES:, lo:, cols[0]:cols[1]])


def kernel(x_prompt, x_sample, state_s5_re, state_s5_im, state_rwkv_shift, state_rwkv_wkv, state_gdn_conv, state_gdn, state_ffn_conv, norm1_g, norm2_g, final_norm_g, w_in, s5_lambda_re, s5_lambda_im, s5_b_re, s5_b_im, s5_c_re, s5_c_im, s5_d, s5_log_step, s5_w_glu, rwkv_mu, rwkv_w0, rwkv_w2, rwkv_a0, rwkv_a2, rwkv_g2, rwkv_k_k, rwkv_k_a, rwkv_r_k, rwkv_ln_w, rwkv_ln_b, gdn_conv_w, gdn_a_log, gdn_dt_bias, gdn_norm_g, w_br_s5, w_br_rwkv, w_br_gdn, w_out, ffn_w_up, ffn_conv_w, ffn_conv_b, ffn_w_down):
    nb, seq, d = x_prompt.shape
    ns, sl, _ = x_sample.shape
    assert sl == SUBLANES and d == D_MODEL
    mp, ms = nb * seq, ns * sl
    m = mp + ms
    tm_norm, tm_proj, tm_out = 512, 3072, 512
    x_groups = (x_prompt.reshape(mp, d), x_sample.reshape(ms, d), 0)

    w_gates, w_rw, w_qkvz, w_sb = _split_w_in(w_in)
    w_br = (w_br_s5.astype(BF16), w_br_rwkv.astype(BF16), w_br_gdn.astype(BF16))
    w_out_b, w_down_b = w_out.astype(BF16), ffn_w_down.astype(BF16)
    conv_b = ffn_conv_b.reshape(DEPTH, 1, 2 * D_FF)
    shift0 = _rw_pad_cols(state_rwkv_shift)[:, :, None, :]

    new_p = [[] for _ in range(7)]
    new_s = [[] for _ in range(7)]
    s_wkv = s_gdn = None
    for l in range(DEPTH):
        xp, xs, s_off = x_groups
        xn1 = _norm_cast(xp, xs, s_off * (mp // tm_norm), m, mp // tm_norm, norm1_g[l], tm=tm_norm)
        proj_g = _matmul(xn1, w_gates, l, tm=tm_proj, tn=512)
        proj_rw = _matmul(xn1, w_rw, l, tm=tm_proj, tn=512)
        proj_qkvz = _matmul(xn1, w_qkvz, l, tm=tm_proj, tn=512)
        proj_sb = _matmul(xn1, w_sb, l, tm=tm_proj, tn=512)

        sp = _s5_params(s5_lambda_re[l], s5_lambda_im[l], s5_b_re[l], s5_b_im[l], s5_c_re[l], s5_c_im[l],
                        s5_d[l], s5_log_step[l], s5_w_glu[l])
        ys5_p, p_re, p_im = _s5_prompt(proj_sb, sp, nb, seq, tm=256)
        ys5_s, s_re, s_im = _s5_sample(proj_sb, state_s5_re[l], state_s5_im[l], sp, mp, ns, tm=128)

        rp = _rwkv_params(rwkv_mu[l], rwkv_w0[l], rwkv_w2[l], rwkv_a0[l], rwkv_a2[l], rwkv_g2[l], rwkv_k_k[l],
                          rwkv_k_a[l], rwkv_r_k[l].reshape(RW_WIDTH), rwkv_ln_w[l], rwkv_ln_b[l])
        yrw_p, p_wkv = _rwkv_prompt(proj_rw, rp, nb, seq, tm=256)
        yrw_s, s_wkv = _rwkv_sample(proj_rw, shift0[l], state_rwkv_wkv, l, s_wkv, rp, mp, ns, tm=128)
        p_shift, s_shift = [_rw_unpad_cols(t[:, 0])
                            for t in _tail_rows(proj_rw, (P_RW, P_RW + RW_PCOLS), 1, mp, seq)]

        gp = _gdn_params(gdn_conv_w[l], gdn_a_log[l], gdn_dt_bias[l], gdn_norm_g[l])
        ygd_p, p_gdn = _gdn_prompt(proj_qkvz, proj_sb, gp, nb, seq, tm=256)
        ygd_s, s_gdn = _gdn_sample(proj_qkvz, proj_sb, state_gdn_conv, state_gdn, l, s_gdn, gp, mp, ns, tm=64)
        p_gconv, s_gconv = _tail_rows(proj_qkvz, (P_Q, P_Q + 3 * GD_WIDTH), GD_CONV - 1, mp, seq)

        merged = _merge((ys5_p, yrw_p, ygd_p), (ys5_s, yrw_s, ygd_s), w_br, l, proj_g, tm=512, tn=1024)
        x, xn = _out_norm(merged, w_out_b, l, xp, xs, s_off * (mp // tm_out), mp // tm_out, norm2_g[l], tm=tm_out)

        h, p_ffn, s_ffn = _ffn_up(xn, ffn_w_up, ffn_conv_w, conv_b, state_ffn_conv, l, nb, seq, ns,
                                  tm=1024, tn=512)
        x = _matmul_res(h, w_down_b, l, x, tm=1536, tn=512)
        x_groups = (x, x, 1)

        for lst, vals in ((new_p, (p_re, p_im, p_shift, p_wkv, p_gconv, p_gdn, p_ffn)),
                          (new_s, (s_re, s_im, s_shift, None, s_gconv, None, s_ffn))):
            for acc, val in zip(lst, vals):
                acc.append(val)

    y_p, y_s = _final_norm(x, final_norm_g, mp, tm=512)
    stack = lambda lst: [jnp.stack(v) for v in lst]
    out_p, out_s = stack(new_p), stack([v for v in new_s if v[0] is not None])
    out_s = out_s[:3] + [s_wkv, out_s[3], s_gdn, out_s[4]]
    return (y_p.reshape(nb, seq, d), y_s.reshape(ns, sl, d)) + tuple(out_p) + tuple(out_s)
```

```python
import functools
import math

import jax
import jax.numpy as jnp
from jax import lax
from jax.experimental import pallas as pl
from jax.experimental.pallas import tpu as pltpu

F32 = jnp.float32
BF16 = jnp.bfloat16

SUBLANES = 8
LANES = 128
VMEM_LIMIT = 52 * 1024 * 1024
CHUNK = 64
SEQ_PER_CHUNK = CHUNK // SUBLANES

D_MODEL = 2048
DEPTH = 2
S5_WIDTH = 512
S5_GROUP = 16
S5_GROUPS = 32
S5_STATE = 64
S5_NSTATE = S5_GROUPS * S5_STATE
RW_WIDTH = 512
RW_HEADS = 8
RW_HD = 64
RW_LORA = 96
RW_GATE = 256
RW_COLS = 3 * RW_WIDTH + 2 * RW_LORA + RW_GATE
RW_PCOLS = 2048
RW_GN_EPS = 64e-5
GD_WIDTH = 1024
GD_HEADS = 8
GD_HD = 128
GD_CONV = 4
D_FF = 5632
FFN_CONV = 3
NORM_EPS = 1e-6
P_GATES, P_RW, P_Q, P_Z, P_S5, P_BA, P_SB_COLS = 0, 0, 0, 3072, 0, 512, 1024


def _cparams(sem):
    return pltpu.CompilerParams(dimension_semantics=sem, vmem_limit_bytes=VMEM_LIMIT)


def _mm(a, b):
    return jnp.dot(a, b, preferred_element_type=F32)


def _mm_nt(a, b):
    return lax.dot_general(a, b, (((1,), (1,)), ((), ())), preferred_element_type=F32)


def _mm_tn(a, b):
    return lax.dot_general(a, b, (((0,), (0,)), ((), ())), preferred_element_type=F32)


def _sigmoid(x):
    return 1.0 / (1.0 + jnp.exp(-x))


def _softplus(x):
    return jnp.maximum(x, 0.0) + jnp.log1p(jnp.exp(-jnp.abs(x)))


def _gelu_tanh(x):
    return 0.5 * x * (1.0 + jnp.tanh(math.sqrt(2.0 / math.pi) * (x + 0.044715 * (x * x * x))))


def _row_iota(shape):
    return lax.broadcasted_iota(jnp.int32, shape, 0)


def _shift_chain(x, d, prev8):
    n, ch = x.shape
    x3 = x.reshape(n // SUBLANES, SUBLANES, ch)
    rot = pltpu.roll(x3, d, 1)
    before = jnp.concatenate([pltpu.roll(prev8, d, 0)[None], rot[:-1]], axis=0)
    row = lax.broadcasted_iota(jnp.int32, x3.shape, 1)
    return jnp.where(row < d, before, rot).reshape(n, ch)


def _shift_seq8(x, d, state):
    n, ch = x.shape
    ns, w1, _ = state.shape
    out = pltpu.roll(x.reshape(ns, SUBLANES, ch), d, 1)
    row = lax.broadcasted_iota(jnp.int32, out.shape, 1)
    for r in range(d):
        out = jnp.where(row == r, state[:, w1 - d + r:w1 - d + r + 1, :], out)
    return out.reshape(n, ch)


def _group_last(x):
    c, n = x.shape
    x3 = x.reshape(c // SUBLANES, SUBLANES, n)
    return jnp.broadcast_to(x3[:, SUBLANES - 1:SUBLANES, :], x3.shape).reshape(c, n)


def _cumsum_rows(x, groupwise=False):
    c = x.shape[0]
    row = _row_iota(x.shape) % SUBLANES
    for d in (1, 2, 4):
        x = x + jnp.where(row >= d, pltpu.roll(x, d, 0), 0.0)
    if c > SUBLANES and not groupwise:
        blocks = [x[SUBLANES * i:SUBLANES * (i + 1)] for i in range(c // SUBLANES)]
        for i in range(1, len(blocks)):
            blocks[i] = blocks[i] + blocks[i - 1][SUBLANES - 1:SUBLANES, :]
        x = jnp.concatenate(blocks, axis=0)
    return x


def _split2(x):
    hi = x.astype(BF16)
    return hi, (x - hi.astype(F32)).astype(BF16)


def _mm_hilo_each(a_list, b_list):
    sa = [_split2(a) for a in a_list]
    sb = [_split2(b) for b in b_list]
    hh = [_mm(a[0], b[0]) for a, b in zip(sa, sb)]
    hl = [_mm(a[0], b[1]) for a, b in zip(sa, sb)]
    lh = [_mm(a[1], b[0]) for a, b in zip(sa, sb)]
    return [x + (y + z) for x, y, z in zip(hh, hl, lh)]


def _solve_unit_lower_each(lows, rhss, diag_only):
    c = lows[0].shape[0]
    nb = c // SUBLANES
    nh = len(lows)
    n = rhss[0].shape[1]
    blk = lambda a, i: a[SUBLANES * i:SUBLANES * (i + 1)]
    rb = [[blk(r, i) for i in range(nb)] for r in rhss]
    with_off = nb > 1 and not diag_only
    same = (lax.broadcasted_iota(jnp.int32, (c, c), 0) // SUBLANES
            == lax.broadcasted_iota(jnp.int32, (c, c), 1) // SUBLANES)
    if with_off:
        ob = [[blk(jnp.where(same, 0.0, low), i) for i in range(nb)] for low in lows]
    sel = (lax.broadcasted_iota(jnp.int32, (c, (SUBLANES - 1) * LANES), 0) % SUBLANES
           == lax.broadcasted_iota(jnp.int32, (c, (SUBLANES - 1) * LANES), 1) // LANES)
    sel = jnp.where(sel, 1.0, 0.0).astype(BF16)
    diag = [_split2(jnp.where(same, low, 0.0)) if not diag_only else _split2(low) for low in lows]
    cols_hi = [_mm(d[0], sel) for d in diag]
    cols_lo = [_mm(d[1], sel) for d in diag]
    cols = [a + b for a, b in zip(cols_hi, cols_lo)]
    reps = n // LANES
    for j in range(SUBLANES - 1):
        for h in range(nh):
            cj = cols[h][:, LANES * j:LANES * (j + 1)]
            cj_n = jnp.concatenate([cj] * reps, axis=1) if reps > 1 else cj
            for i in range(nb):
                rb[h][i] = rb[h][i] - blk(cj_n, i) * rb[h][i][j:j + 1, :]
                if with_off and i > 0:
                    ob[h][i] = ob[h][i] - blk(cj, i)[:, :c] * ob[h][i][j:j + 1, :]
    r1 = [jnp.concatenate(b, axis=0) if nb > 1 else b[0] for b in rb]
    if not with_off:
        return r1
    n1 = [jnp.concatenate(b, axis=0) for b in ob]
    n2 = _mm_hilo_each(n1, n1)
    t = _mm_hilo_each(n1 + n2, r1 + n2)
    y = [r - nr for r, nr in zip(r1, t[:nh])]
    n4 = t[nh:]
    y = [a + b for a, b in zip(y, _mm_hilo_each(n2, y))]
    return [a + b for a, b in zip(y, _mm_hilo_each(n4, y))]


def _chunks_per_iteration(n_chunks):
    return 2 if n_chunks % 2 == 0 else 1


def _chunk_masks(per_seq):
    ri = lax.broadcasted_iota(jnp.int32, (CHUNK, CHUNK), 0)
    ci = lax.broadcasted_iota(jnp.int32, (CHUNK, CHUNK), 1)
    strict, causal = ri > ci, ri >= ci
    if per_seq:
        same = (ri // SUBLANES) == (ci // SUBLANES)
        strict, causal = jnp.logical_and(strict, same), jnp.logical_and(causal, same)
    return strict, causal


def _seq_block_mask(rows, width):
    r = lax.broadcasted_iota(jnp.int32, (rows, SEQ_PER_CHUNK * width), 0)
    l = lax.broadcasted_iota(jnp.int32, (rows, SEQ_PER_CHUNK * width), 1)
    return ((r % CHUNK) // SUBLANES) == (l // width)


def _pick_own_seq(a, width):
    return jnp.concatenate([a[SUBLANES * s:SUBLANES * (s + 1), width * s:width * (s + 1)]
                            for s in range(SEQ_PER_CHUNK)], axis=0)


def _rmsnorm_rows(x_ref, g_ref, o_ref, rows, chunk=128):
    def body(i, carry):
        r0 = pl.multiple_of(i * chunk, chunk)
        x = x_ref[pl.ds(r0, chunk), :]
        ms = jnp.mean(x * x, axis=-1, keepdims=True)
        o_ref[pl.ds(r0, chunk), :] = (x * lax.rsqrt(ms + NORM_EPS) * g_ref[...]).astype(o_ref.dtype)
        return carry
    lax.fori_loop(0, rows // chunk, body, 0)


def _two_group_specs(tm, width, prompt_tiles, sample_off, pipeline_mode=None):
    p = pl.BlockSpec((tm, width), lambda i, *_: (jnp.minimum(i, prompt_tiles - 1), 0), pipeline_mode=pipeline_mode)
    s = pl.BlockSpec((tm, width), lambda i, *_: (jnp.maximum(i - prompt_tiles, 0) + sample_off, 0),
                     pipeline_mode=pipeline_mode)
    return p, s


def _layer_spec(shape, index_map, layer):
    return pl.BlockSpec((None,) + shape, lambda *a: (layer,) + tuple(index_map(*a)))


def _norm_cast_kernel(xp_ref, xs_ref, g_ref, o_ref, *, rows, prompt_tiles):
    i = pl.program_id(0)

    @pl.when(i < prompt_tiles)
    def _():
        _rmsnorm_rows(xp_ref, g_ref, o_ref, rows)

    @pl.when(i >= prompt_tiles)
    def _():
        _rmsnorm_rows(xs_ref, g_ref, o_ref, rows)


def _norm_cast(xp, xs, sample_off, m, prompt_tiles, g, tm):
    k = xp.shape[1]
    psp, ssp = _two_group_specs(tm, k, prompt_tiles, sample_off)
    return pl.pallas_call(
        functools.partial(_norm_cast_kernel, rows=tm, prompt_tiles=prompt_tiles),
        grid=(m // tm,),
        in_specs=[psp, ssp, pl.BlockSpec((1, k), lambda i: (0, 0))],
        out_specs=pl.BlockSpec((tm, k), lambda i: (i, 0)),
        out_shape=jax.ShapeDtypeStruct((m, k), BF16),
        compiler_params=_cparams(("arbitrary",)),
        name="norm_cast",
    )(xp, xs, g.reshape(1, k))


def _matmul_nt_kernel(a_ref, w_ref, o_ref):
    o_ref[...] = _mm_nt(a_ref[...], w_ref[0].astype(BF16))


def _matmul_nt(a, w_t, layer, row0, n, tm, tn):
    m, k = a.shape
    assert row0 % SUBLANES == 0 and n % tn == 0
    return pl.pallas_call(
        _matmul_nt_kernel,
        grid=(m // tm, n // tn),
        in_specs=[pl.BlockSpec((tm, k), lambda i, j: (i, 0), pipeline_mode=pl.Buffered(1)),
                  pl.BlockSpec((pl.Element(1), pl.Element(tn), pl.Element(k)),
                               lambda i, j: (layer, pl.multiple_of(row0 + j * tn, SUBLANES), 0))],
        out_specs=pl.BlockSpec((tm, tn), lambda i, j: (i, j)),
        out_shape=jax.ShapeDtypeStruct((m, n), F32),
        compiler_params=_cparams(("parallel", "arbitrary")),
        name="proj_matmul",
    )(a, w_t)


def _matmul_res_kernel(a_ref, w_ref, r_ref, o_ref):
    o_ref[...] = r_ref[...] + _mm(a_ref[...], w_ref[...])


def _matmul_res(a, w, layer, res, tm, tn):
    m, k = a.shape
    n = w.shape[-1]
    return pl.pallas_call(
        _matmul_res_kernel,
        grid=(m // tm, n // tn),
        in_specs=[pl.BlockSpec((tm, k), lambda i, j: (i, 0), pipeline_mode=pl.Buffered(1)),
                  _layer_spec((k, tn), lambda i, j: (0, j), layer),
                  pl.BlockSpec((tm, tn), lambda i, j: (i, j))],
        out_specs=pl.BlockSpec((tm, tn), lambda i, j: (i, j)),
        out_shape=jax.ShapeDtypeStruct((m, n), F32),
        compiler_params=_cparams(("parallel", "arbitrary")),
        name="matmul_res",
    )(a, w, res)


def _out_norm_kernel(a_ref, w_ref, rp_ref, rs_ref, g_ref, x_ref, xn_ref, *, rows, prompt_tiles):
    res = jnp.where(pl.program_id(0) < prompt_tiles, rp_ref[...], rs_ref[...])
    x_ref[...] = res + _mm(a_ref[...], w_ref[...])
    _rmsnorm_rows(x_ref, g_ref, xn_ref, rows)


def _out_norm(a, w, layer, rp, rs, sample_off, prompt_tiles, g, tm):
    m, k = a.shape
    n = w.shape[-1]
    psp, ssp = _two_group_specs(tm, n, prompt_tiles, sample_off)
    return pl.pallas_call(
        functools.partial(_out_norm_kernel, rows=tm, prompt_tiles=prompt_tiles),
        grid=(m // tm,),
        in_specs=[pl.BlockSpec((tm, k), lambda i: (i, 0)),
                  _layer_spec((k, n), lambda i: (0, 0), layer),
                  psp, ssp, pl.BlockSpec((1, n), lambda i: (0, 0))],
        out_specs=[pl.BlockSpec((tm, n), lambda i: (i, 0)), pl.BlockSpec((tm, n), lambda i: (i, 0))],
        out_shape=[jax.ShapeDtypeStruct((m, n), F32), jax.ShapeDtypeStruct((m, n), BF16)],
        compiler_params=_cparams(("parallel",)),
        name="out_norm",
    )(a, w, rp, rs, g.reshape(1, n))


def _merge_kernel(y1p, y2p, y3p, y1s, y2s, y3s, w1_ref, w2_ref, w3_ref, g1_ref, g2_ref, g3_ref, o_ref, *,
                  prompt_tiles, tm, rc):
    is_p = pl.program_id(0) < prompt_tiles
    for r in range(tm // rc):
        rows = slice(r * rc, (r + 1) * rc)
        pick = lambda p, s: jnp.where(is_p, p[rows, :], s[rows, :])
        acc = _sigmoid(g1_ref[rows, :]) * _mm(pick(y1p, y1s), w1_ref[...])
        acc = acc + _sigmoid(g2_ref[rows, :]) * _mm(pick(y2p, y2s), w2_ref[...])
        acc = acc + _sigmoid(g3_ref[rows, :]) * _mm(pick(y3p, y3s), w3_ref[...])
        o_ref[rows, :] = acc.astype(BF16)


def _merge(ys_p, ys_s, ws, layer, proj, tm, tn):
    mp, ms = ys_p[0].shape[0], ys_s[0].shape[0]
    n = ws[0].shape[-1]
    nj = n // tn
    pt = mp // tm
    gate_spec = lambda b: pl.BlockSpec((tm, tn), lambda i, j: (i, P_GATES // tn + b * nj + j))
    y_specs = [_two_group_specs(tm, y.shape[1], pt, 0) for y in ys_p]
    w_spec = lambda w: _layer_spec((w.shape[1], tn), lambda i, j: (0, j), layer)
    return pl.pallas_call(
        functools.partial(_merge_kernel, prompt_tiles=pt, tm=tm, rc=128),
        grid=((mp + ms) // tm, nj),
        in_specs=[s[0] for s in y_specs] + [s[1] for s in y_specs] + [w_spec(w) for w in ws]
        + [gate_spec(0), gate_spec(1), gate_spec(2)],
        out_specs=pl.BlockSpec((tm, tn), lambda i, j: (i, j)),
        out_shape=jax.ShapeDtypeStruct((mp + ms, n), BF16),
        compiler_params=_cparams(("parallel", "arbitrary")),
        name="merge",
    )(*ys_p, *ys_s, *ws, proj, proj, proj)


def _final_norm_kernel(x_ref, g_ref, yp_ref, ys_ref, *, prompt_tiles):
    x = x_ref[...]
    ms = jnp.mean(x * x, axis=-1, keepdims=True)
    y = x * lax.rsqrt(ms + NORM_EPS) * g_ref[...]
    i = pl.program_id(0)

    @pl.when(i < prompt_tiles)
    def _():
        yp_ref[...] = y

    @pl.when(i >= prompt_tiles)
    def _():
        ys_ref[...] = y


def _final_norm(x, g, mp, tm):
    m, k = x.shape
    pt = mp // tm
    psp, ssp = _two_group_specs(tm, k, pt, 0)
    return pl.pallas_call(
        functools.partial(_final_norm_kernel, prompt_tiles=pt),
        grid=(m // tm,),
        in_specs=[pl.BlockSpec((tm, k), lambda i: (i, 0)), pl.BlockSpec((1, k), lambda i: (0, 0))],
        out_specs=[psp, ssp],
        out_shape=[jax.ShapeDtypeStruct((mp, k), F32), jax.ShapeDtypeStruct((m - mp, k), F32)],
        compiler_params=_cparams(("arbitrary",)),
        name="final_norm",
    )(x, g.reshape(1, k))


def _ffn_up_kernel(xn_ref, wg_ref, wv_ref, cwg_ref, cwv_ref, bg_ref, bv_ref, sg_ref, sv_ref,
                   h_ref, pg_ref, pv_ref, og_ref, ov_ref, wgb, wvb, cg, cv, *, tm, rc, prompt_tiles, tiles_per_seq):
    i = pl.program_id(1)
    w1 = FFN_CONV - 1

    @pl.when(i == 0)
    def _():
        wgb[...] = wg_ref[...].astype(BF16)
        wvb[...] = wv_ref[...].astype(BF16)

    def conv(u, s1, s2, cw_ref, b_ref):
        return cw_ref[2:3, :] * u + cw_ref[1:2, :] * s1 + cw_ref[0:1, :] * s2 + b_ref[...]

    def up(rows):
        x = xn_ref[rows, :]
        return _mm(x, wgb[...]), _mm(x, wvb[...])

    @pl.when(i < prompt_tiles)
    def _():
        first = (i % tiles_per_seq) == 0
        pg = jnp.where(first, 0.0, cg[...])
        pv = jnp.where(first, 0.0, cv[...])
        for r in range(tm // rc):
            rows = slice(r * rc, (r + 1) * rc)
            ug, uv = up(rows)
            gate = conv(ug, _shift_chain(ug, 1, pg), _shift_chain(ug, 2, pg), cwg_ref, bg_ref)
            val = conv(uv, _shift_chain(uv, 1, pv), _shift_chain(uv, 2, pv), cwv_ref, bv_ref)
            h_ref[rows, :] = (gate * _sigmoid(gate) * val).astype(BF16)
            pg, pv = ug[rc - SUBLANES:, :], uv[rc - SUBLANES:, :]
        cg[...] = pg
        cv[...] = pv
        pg_ref[0] = pg[SUBLANES - w1:, :]
        pv_ref[0] = pv[SUBLANES - w1:, :]

    @pl.when(i >= prompt_tiles)
    def _():
        nsr = rc // SUBLANES
        for r in range(tm // rc):
            rows = slice(r * rc, (r + 1) * rc)
            seqs = slice(r * nsr, (r + 1) * nsr)
            ug, uv = up(rows)
            sg, sv = sg_ref[seqs], sv_ref[seqs]
            gate = conv(ug, _shift_seq8(ug, 1, sg), _shift_seq8(ug, 2, sg), cwg_ref, bg_ref)
            val = conv(uv, _shift_seq8(uv, 1, sv), _shift_seq8(uv, 2, sv), cwv_ref, bv_ref)
            h_ref[rows, :] = (gate * _sigmoid(gate) * val).astype(BF16)
            og_ref[seqs] = ug.reshape(nsr, SUBLANES, ug.shape[1])[:, SUBLANES - w1:, :]
            ov_ref[seqs] = uv.reshape(nsr, SUBLANES, uv.shape[1])[:, SUBLANES - w1:, :]


def _ffn_up(xn, w_up, conv_w, conv_b, state, layer, nb, seq, ns, tm, tn):
    m, k = xn.shape
    mp = nb * seq
    assert ns * SUBLANES == tm and (m - mp) == tm and seq % tm == 0
    nj, pt, tps = D_FF // tn, mp // tm, seq // tm
    w1 = FFN_CONV - 1
    wsp = lambda off: _layer_spec((k, tn), lambda j, i: (0, off + j), layer)
    par = lambda rows, off: _layer_spec((rows, tn), lambda j, i: (0, off + j), layer)
    stsp = lambda off: _layer_spec((ns, w1, tn), lambda j, i: (0, 0, off + j), layer)
    psp = pl.BlockSpec((1, w1, tn), lambda j, i: (jnp.minimum(i, pt - 1) // tps, 0, j))
    osp = pl.BlockSpec((ns, w1, tn), lambda j, i: (0, 0, j))
    h, pg, pv, og, ov = pl.pallas_call(
        functools.partial(_ffn_up_kernel, tm=tm, rc=128, prompt_tiles=pt, tiles_per_seq=tps),
        grid=(nj, m // tm),
        in_specs=[pl.BlockSpec((tm, k), lambda j, i: (i, 0)), wsp(0), wsp(nj),
                  par(FFN_CONV, 0), par(FFN_CONV, nj), par(1, 0), par(1, nj), stsp(0), stsp(nj)],
        out_specs=[pl.BlockSpec((tm, tn), lambda j, i: (i, j)), psp, psp, osp, osp],
        out_shape=[jax.ShapeDtypeStruct((m, D_FF), BF16),
                   jax.ShapeDtypeStruct((nb, w1, D_FF), F32), jax.ShapeDtypeStruct((nb, w1, D_FF), F32),
                   jax.ShapeDtypeStruct((ns, w1, D_FF), F32), jax.ShapeDtypeStruct((ns, w1, D_FF), F32)],
        scratch_shapes=[pltpu.VMEM((k, tn), BF16), pltpu.VMEM((k, tn), BF16),
                        pltpu.VMEM((SUBLANES, tn), F32), pltpu.VMEM((SUBLANES, tn), F32)],
        compiler_params=_cparams(("arbitrary", "arbitrary")),
        name="ffn_up",
    )(xn, w_up, w_up, conv_w, conv_w, conv_b, conv_b, state, state)
    return h, jnp.concatenate([pg, pv], axis=-1), jnp.concatenate([og, ov], axis=-1)


S5_Q = 4
S5_QW = S5_NSTATE // S5_Q


def _s5_kernel(*refs, chained, tm):
    if chained:
        (u_ref, bre_ref, bim_ref, cre_ref, cim_ref, mre_ref, mim_ref, pre_ref, pim_ref, d_ref, wglu_ref,
         y_ref, xlre_ref, xlim_ref, sre, sim, car_re, car_im) = refs
    else:
        (u_ref, x0re_ref, x0im_ref, bre_ref, bim_ref, cre_ref, cim_ref, mre_ref, mim_ref, pre_ref, pim_ref,
         d_ref, wglu_ref, y_ref, xlre_ref, xlim_ref, sre, sim) = refs

    u = u_ref[...]
    ub = u.astype(BF16)
    for q in range(S5_Q):
        uq = ub[:, LANES * q:LANES * (q + 1)]
        sre[:, S5_QW * q:S5_QW * (q + 1)] = _mm(uq, bre_ref[q])
        sim[:, S5_QW * q:S5_QW * (q + 1)] = _mm(uq, bim_ref[q])

    if chained:
        @pl.when(pl.program_id(1) == 0)
        def _():
            car_re[...] = jnp.zeros_like(car_re)
            car_im[...] = jnp.zeros_like(car_im)

    def blk(i, carry):
        r0 = pl.multiple_of(i * SUBLANES, SUBLANES)
        for q in range(S5_Q):
            sl = slice(S5_QW * q, S5_QW * (q + 1))
            xr = sre[pl.ds(r0, SUBLANES), sl]
            xi = sim[pl.ds(r0, SUBLANES), sl]
            for li, d in enumerate((1, 2, 4)):
                mr, mi = mre_ref[li, :, sl], mim_ref[li, :, sl]
                sr, si = pltpu.roll(xr, d, 0), pltpu.roll(xi, d, 0)
                xr, xi = xr + (mr * sr - mi * si), xi + (mr * si + mi * sr)
            if chained:
                cr, ci = car_re[:, sl], car_im[:, sl]
            else:
                cr, ci = x0re_ref[pl.ds(i, 1), sl], x0im_ref[pl.ds(i, 1), sl]
            pr, pi_ = pre_ref[:, sl], pim_ref[:, sl]
            xr, xi = xr + (pr * cr - pi_ * ci), xi + (pr * ci + pi_ * cr)
            sre[pl.ds(r0, SUBLANES), sl] = xr
            sim[pl.ds(r0, SUBLANES), sl] = xi
            if chained:
                car_re[:, sl] = xr[SUBLANES - 1:SUBLANES, :]
                car_im[:, sl] = xi[SUBLANES - 1:SUBLANES, :]
            else:
                xlre_ref[pl.ds(i, 1), sl] = xr[SUBLANES - 1:SUBLANES, :]
                xlim_ref[pl.ds(i, 1), sl] = xi[SUBLANES - 1:SUBLANES, :]
        return carry

    lax.fori_loop(0, tm // SUBLANES, blk, 0)

    if chained:
        xlre_ref[0] = car_re[...]
        xlim_ref[0] = car_im[...]

    ys = []
    for q in range(S5_Q):
        sl = slice(S5_QW * q, S5_QW * (q + 1))
        ys.append(_mm(sre[:, sl].astype(BF16), cre_ref[q]) - _mm(sim[:, sl].astype(BF16), cim_ref[q]))
    y = jnp.concatenate(ys, axis=1) + d_ref[...] * u
    y = _gelu_tanh(y)
    y = y * _sigmoid(_mm(y.astype(BF16), wglu_ref[...]))
    y_ref[...] = y.astype(BF16)


def _const_spec(shape):
    zeros = (0,) * len(shape)
    return pl.BlockSpec(shape, lambda *a: zeros)


def _s5_param_specs():
    return [_const_spec((S5_Q, LANES, S5_QW)), _const_spec((S5_Q, LANES, S5_QW)),
            _const_spec((S5_Q, S5_QW, LANES)), _const_spec((S5_Q, S5_QW, LANES)),
            _const_spec((3, SUBLANES, S5_NSTATE)), _const_spec((3, SUBLANES, S5_NSTATE)),
            _const_spec((SUBLANES, S5_NSTATE)), _const_spec((SUBLANES, S5_NSTATE)),
            _const_spec((1, S5_WIDTH)), _const_spec((S5_WIDTH, S5_WIDTH))]


def _s5_prompt(proj, sp, nb, seq, tm):
    nt = seq // tm
    cb = P_S5 // S5_WIDTH
    y, xre, xim = pl.pallas_call(
        functools.partial(_s5_kernel, chained=True, tm=tm),
        grid=(nb, nt),
        in_specs=[pl.BlockSpec((tm, S5_WIDTH), lambda b, t: (b * nt + t, cb))] + _s5_param_specs(),
        out_specs=[pl.BlockSpec((tm, S5_WIDTH), lambda b, t: (b * nt + t, 0)),
                   pl.BlockSpec((1, 1, S5_NSTATE), lambda b, t: (b, 0, 0)),
                   pl.BlockSpec((1, 1, S5_NSTATE), lambda b, t: (b, 0, 0))],
        out_shape=[jax.ShapeDtypeStruct((nb * seq, S5_WIDTH), BF16),
                   jax.ShapeDtypeStruct((nb, 1, S5_NSTATE), F32),
                   jax.ShapeDtypeStruct((nb, 1, S5_NSTATE), F32)],
        scratch_shapes=[pltpu.VMEM((tm, S5_NSTATE), F32), pltpu.VMEM((tm, S5_NSTATE), F32),
                        pltpu.VMEM((1, S5_NSTATE), F32), pltpu.VMEM((1, S5_NSTATE), F32)],
        compiler_params=_cparams(("parallel", "arbitrary")),
        name="s5_prompt",
    )(proj, *sp)
    return y, xre.reshape(nb, S5_GROUPS, S5_STATE), xim.reshape(nb, S5_GROUPS, S5_STATE)


def _s5_sample(proj, x0re, x0im, sp, row0, nseq, tm):
    ns = tm // SUBLANES
    cb = P_S5 // S5_WIDTH
    r0 = row0 // tm
    st = pl.BlockSpec((ns, S5_NSTATE), lambda i: (i, 0))
    y, xre, xim = pl.pallas_call(
        functools.partial(_s5_kernel, chained=False, tm=tm),
        grid=(nseq // ns,),
        in_specs=[pl.BlockSpec((tm, S5_WIDTH), lambda i: (r0 + i, cb)), st, st] + _s5_param_specs(),
        out_specs=[pl.BlockSpec((tm, S5_WIDTH), lambda i: (i, 0)), st, st],
        out_shape=[jax.ShapeDtypeStruct((nseq * SUBLANES, S5_WIDTH), BF16),
                   jax.ShapeDtypeStruct((nseq, S5_NSTATE), F32),
                   jax.ShapeDtypeStruct((nseq, S5_NSTATE), F32)],
        scratch_shapes=[pltpu.VMEM((tm, S5_NSTATE), F32), pltpu.VMEM((tm, S5_NSTATE), F32)],
        compiler_params=_cparams(("parallel",)),
        name="s5_sample",
    )(proj, x0re.reshape(nseq, S5_NSTATE), x0im.reshape(nseq, S5_NSTATE), *sp)
    return y, xre.reshape(nseq, S5_GROUPS, S5_STATE), xim.reshape(nseq, S5_GROUPS, S5_STATE)


def _s5_params(lam_re, lam_im, b_re, b_im, c_re, c_im, d, log_step, w_glu):
    lam = lax.complex(lam_re, lam_im)
    step = jnp.exp(log_step)[:, None]
    lam_bar = jnp.exp(lam * step)
    b_bar = ((lam_bar - 1.0) / lam)[..., None] * lax.complex(b_re, b_im)

    def bblk(bm):
        bm = bm.reshape(S5_Q, 8, S5_STATE, S5_GROUP)
        eye = jnp.eye(8, dtype=F32)
        out = jnp.einsum('qgpc,gh->qgchp', bm, eye)
        return out.reshape(S5_Q, LANES, S5_QW).astype(BF16)

    def cblk(cm):
        cm = cm.reshape(S5_Q, 8, S5_GROUP, S5_STATE)
        eye = jnp.eye(8, dtype=F32)
        out = jnp.einsum('qgcp,gh->qgphc', cm, eye)
        return out.reshape(S5_Q, S5_QW, LANES).astype(BF16)

    lam_flat = lam_bar.reshape(1, S5_NSTATE)
    pows = [lam_flat]
    for _ in range(SUBLANES - 1):
        pows.append(pows[-1] * lam_flat)
    row = jnp.arange(SUBLANES)[:, None]
    m = jnp.stack([jnp.where(row >= dd, pows[dd - 1], 0.0) for dd in (1, 2, 4)])
    p = jnp.concatenate(pows, axis=0)
    return (bblk(b_bar.real), bblk(b_bar.imag), cblk(c_re), cblk(c_im),
            m.real, m.imag, p.real, p.imag, d.reshape(1, S5_WIDTH), w_glu.astype(BF16))


def _seg64_sum(x):
    outs = []
    for t in range(x.shape[1] // LANES):
        xt = x[:, LANES * t:LANES * (t + 1)]
        lo = lax.broadcasted_iota(jnp.int32, xt.shape, 1) < RW_HD
        s_lo = jnp.sum(jnp.where(lo, xt, 0.0), axis=-1, keepdims=True)
        s_hi = jnp.sum(jnp.where(lo, 0.0, xt), axis=-1, keepdims=True)
        outs.append(jnp.where(lo, s_lo, s_hi))
    return jnp.concatenate(outs, axis=1)


def _rwkv_kernel(*refs, per_seq, tm, aliased=False):
    if aliased:
        refs = refs[1:]
    if per_seq:
        (h_ref, sh_ref, si_ref, mu_ref, w0_ref, w2_ref, a0_ref, a2_ref, g2_ref, kk_ref, ka_ref, rk_ref,
         lnw_ref, lnb_ref, y_ref, so_ref, r_s, k_s, v_s, a_s, b_s, lw_s, y_s) = refs
    else:
        (h_ref, mu_ref, w0_ref, w2_ref, a0_ref, a2_ref, g2_ref, kk_ref, ka_ref, rk_ref, lnw_ref, lnb_ref,
         y_ref, so_ref, r_s, k_s, v_s, a_s, b_s, lw_s, y_s, prev_s, st_s) = refs
    c, w, hd_w = CHUNK, RW_WIDTH, RW_HD

    h = h_ref[...]
    if per_seq:
        prev = _shift_seq8(h, 1, sh_ref[...])
    else:
        @pl.when(pl.program_id(1) == 0)
        def _():
            prev_s[...] = jnp.zeros_like(prev_s)
            st_s[...] = jnp.zeros_like(st_s)
        prev = _shift_chain(h, 1, prev_s[...])
        prev_s[...] = h[tm - SUBLANES:, :]
    hs = h + (prev - h) * mu_ref[...]
    r = hs[:, 0:w]
    k = hs[:, w:2 * w]
    v = hs[:, 2 * w:3 * w]
    wd = hs[:, 3 * w:3 * w + LANES]
    ad = hs[:, 3 * w + LANES:3 * w + 2 * LANES]
    gd = hs[:, 3 * w + 2 * LANES:]
    w_log = -_softplus(-(w0_ref[...] + _mm(jnp.tanh(wd).astype(BF16), w2_ref[...]))) - 0.5
    lw_s[...] = -jnp.exp(w_log)
    a_ic = _sigmoid(a0_ref[...] + _mm(ad.astype(BF16), a2_ref[...]))
    gate = _mm(_sigmoid(gd).astype(BF16), g2_ref[...])
    kx = k * kk_ref[...]
    kkn = kx * lax.rsqrt(_seg64_sum(kx * kx) + 1e-6)
    k2 = k * (1.0 + (a_ic - 1.0) * ka_ref[...])
    r_s[...] = r
    k_s[...] = k2
    v_s[...] = v
    a_s[...] = -kkn
    b_s[...] = kkn * a_ic

    strict, causal = _chunk_masks(per_seq)
    if per_seq:
        big_mask = _seq_block_mask(2 * c, hd_w)

    heads = range(RW_HEADS)
    sls = [slice(hd_w * hd, hd_w * (hd + 1)) for hd in heads]
    group = _chunks_per_iteration(tm // c)

    def chunk_group(gidx, carry):
        pre = []
        for t in range(group):
            idx = gidx * group + t
            rows = pl.ds(pl.multiple_of(idx * c, c), c)
            lw = lw_s[rows, :]
            cum = _cumsum_rows(lw, groupwise=per_seq)
            cl = _group_last(cum) if per_seq else cum[c - 1:c, :]
            e_pos, e_neg, e_end = jnp.exp(cum), jnp.exp(-cum), jnp.exp(cl - cum)
            rr, kc, vc, ac, bc = r_s[rows, :], k_s[rows, :], v_s[rows, :], a_s[rows, :], b_s[rows, :]
            pre.append(dict(rows=rows, seqs=pl.ds(idx * SEQ_PER_CHUNK, SEQ_PER_CHUNK), vc=vc,
                            rt=rr * e_pos, kt=kc * e_neg, bt=bc * e_neg, at=ac * jnp.exp(cum - lw),
                            kh=kc * e_end, bh=bc * e_end, wc=jnp.exp(cl)))
        pairs = [(t, hd) for t in range(group) for hd in heads]
        ath = [pre[t]['at'][:, sls[hd]] for t, hd in pairs]
        vh = [pre[t]['vc'][:, sls[hd]] for t, hd in pairs]
        rth = [pre[t]['rt'][:, sls[hd]] for t, hd in pairs]
        ab = [_mm_nt(jnp.concatenate([ath[p], rth[p]], axis=0).astype(BF16),
                     jnp.concatenate([pre[t]['kt'][:, sls[hd]], pre[t]['bt'][:, sls[hd]]], axis=0).astype(BF16))
              for p, (t, hd) in enumerate(pairs)]
        a_ak = [jnp.where(strict, m[:c, :c], 0.0) for m in ab]
        n_ab = [jnp.where(strict, -m[:c, c:], 0.0) for m in ab]
        a_r = [jnp.concatenate([jnp.where(causal, m[c:, :c], 0.0), jnp.where(causal, m[c:, c:], 0.0)],
                               axis=1).astype(BF16) for m in ab]
        akv = [_mm(a_ak[p].astype(BF16), vh[p].astype(BF16)) for p in range(len(pairs))]
        sol = _solve_unit_lower_each(
            n_ab, [jnp.concatenate([ath[p], akv[p]], axis=1) for p in range(len(pairs))], per_seq)
        kb = [jnp.concatenate([pre[t]['kh'][:, sls[hd]], pre[t]['bh'][:, sls[hd]]], axis=0).astype(BF16)
              for t, hd in pairs]
        for t in range(group):
            rows, seqs, wc = pre[t]['rows'], pre[t]['seqs'], pre[t]['wc']
            ps = [t * RW_HEADS + hd for hd in heads]
            if per_seq:
                st = [si_ref[seqs, hd].reshape(SEQ_PER_CHUNK * hd_w, hd_w) for hd in heads]
            else:
                st = [st_s[hd] for hd in heads]
            pr = [_mm_nt(jnp.concatenate([sol[p][:, :hd_w], rth[p]], axis=0).astype(BF16), st[hd].astype(BF16))
                  for hd, p in zip(heads, ps)]
            if per_seq:
                u = [_pick_own_seq(pr[hd][:c], hd_w) + sol[p][:, hd_w:] for hd, p in zip(heads, ps)]
                yst = [_pick_own_seq(pr[hd][c:], hd_w) for hd in heads]
            else:
                u = [pr[hd][:c] + sol[p][:, hd_w:] for hd, p in zip(heads, ps)]
                yst = [pr[hd][c:] for hd in heads]
            vu = [jnp.concatenate([vh[p], u[hd]], axis=0) for hd, p in zip(heads, ps)]
            ya = [_mm(a_r[p], vu[hd].astype(BF16)) for hd, p in zip(heads, ps)]
            if per_seq:
                upd = [_mm_tn(jnp.where(big_mask, jnp.concatenate([vu[hd]] * SEQ_PER_CHUNK, axis=1), 0.0
                                        ).astype(BF16), kb[p]) for hd, p in zip(heads, ps)]
            else:
                upd = [_mm_tn(vu[hd].astype(BF16), kb[p]) for hd, p in zip(heads, ps)]
            for hd in heads:
                if per_seq:
                    wcb = jnp.concatenate(
                        [jnp.broadcast_to(wc[SUBLANES * s:SUBLANES * s + 1, sls[hd]], (hd_w, hd_w))
                         for s in range(SEQ_PER_CHUNK)], axis=0)
                    so_ref[seqs, hd] = (st[hd] * wcb + upd[hd]).reshape(SEQ_PER_CHUNK, hd_w, hd_w)
                else:
                    st_s[hd] = st[hd] * wc[:, sls[hd]] + upd[hd]
                y_s[rows, sls[hd]] = yst[hd] + ya[hd]
        return carry

    lax.fori_loop(0, tm // (c * group), chunk_group, 0)

    if not per_seq:
        so_ref[0] = st_s[...]

    y = y_s[...]
    mean = _seg64_sum(y) * (1.0 / hd_w)
    yc = y - mean
    var = _seg64_sum(yc * yc) * (1.0 / hd_w)
    yn = yc * lax.rsqrt(var + RW_GN_EPS) * lnw_ref[...] + lnb_ref[...]
    bonus = _seg64_sum(r_s[...] * k_s[...] * rk_ref[...]) * v_s[...]
    y_ref[...] = ((yn + bonus) * gate).astype(BF16)


def _rwkv_param_specs():
    row = _const_spec((1, RW_WIDTH))
    return [_const_spec((1, RW_PCOLS)), row, _const_spec((LANES, RW_WIDTH)), row, _const_spec((LANES, RW_WIDTH)),
            _const_spec((RW_GATE, RW_WIDTH)), row, row, row, row, row]


def _rwkv_scratch(tm):
    return [pltpu.VMEM((tm, RW_WIDTH), F32) for _ in range(7)]


def _rwkv_prompt(proj, rp, nb, seq, tm):
    nt = seq // tm
    cb = P_RW // RW_PCOLS
    y, st = pl.pallas_call(
        functools.partial(_rwkv_kernel, per_seq=False, tm=tm),
        grid=(nb, nt),
        in_specs=[pl.BlockSpec((tm, RW_PCOLS), lambda b, t: (b * nt + t, cb))] + _rwkv_param_specs(),
        out_specs=[pl.BlockSpec((tm, RW_WIDTH), lambda b, t: (b * nt + t, 0)),
                   pl.BlockSpec((1, RW_HEADS, RW_HD, RW_HD), lambda b, t: (b, 0, 0, 0))],
        out_shape=[jax.ShapeDtypeStruct((nb * seq, RW_WIDTH), BF16),
                   jax.ShapeDtypeStruct((nb, RW_HEADS, RW_HD, RW_HD), F32)],
        scratch_shapes=_rwkv_scratch(tm) + [pltpu.VMEM((SUBLANES, RW_PCOLS), F32),
                                            pltpu.VMEM((RW_HEADS, RW_HD, RW_HD), F32)],
        compiler_params=_cparams(("parallel", "arbitrary")),
        name="rwkv_prompt",
    )(proj, *rp)
    return y, st


def _stacked_out(prev):
    if prev is None:
        return (), [], {}
    return (prev,), [pl.BlockSpec(memory_space=pl.ANY)], {0: 1}


def _rwkv_sample(proj, shift0, st0, layer, prev_st, rp, row0, nseq, tm):
    ns = tm // SUBLANES
    cb = P_RW // RW_PCOLS
    r0 = row0 // tm
    st_io = _layer_spec((ns, RW_HEADS, RW_HD, RW_HD), lambda i: (i, 0, 0, 0), layer)
    extra, extra_specs, aliases = _stacked_out(prev_st)
    y, st = pl.pallas_call(
        functools.partial(_rwkv_kernel, per_seq=True, tm=tm, aliased=prev_st is not None),
        grid=(nseq // ns,),
        in_specs=extra_specs + [pl.BlockSpec((tm, RW_PCOLS), lambda i: (r0 + i, cb)),
                                pl.BlockSpec((ns, 1, RW_PCOLS), lambda i: (i, 0, 0)), st_io] + _rwkv_param_specs(),
        out_specs=[pl.BlockSpec((tm, RW_WIDTH), lambda i: (i, 0)), st_io],
        out_shape=[jax.ShapeDtypeStruct((nseq * SUBLANES, RW_WIDTH), BF16),
                   jax.ShapeDtypeStruct(st0.shape, F32)],
        input_output_aliases=aliases,
        scratch_shapes=_rwkv_scratch(tm),
        compiler_params=_cparams(("parallel",)),
        name="rwkv_sample",
    )(*extra, proj, shift0, st0, *rp)
    return y, st


def _rw_pad_cols(x):
    z = jnp.zeros(x.shape[:-1] + (LANES - RW_LORA,), x.dtype)
    a, b = 3 * RW_WIDTH, 3 * RW_WIDTH + RW_LORA
    return jnp.concatenate([x[..., :a], x[..., a:b], z, x[..., b:b + RW_LORA], z, x[..., b + RW_LORA:]], axis=-1)


def _rw_unpad_cols(x):
    a = 3 * RW_WIDTH
    return jnp.concatenate([x[..., :a + RW_LORA], x[..., a + LANES:a + LANES + RW_LORA], x[..., a + 2 * LANES:]],
                           axis=-1)


def _rwkv_params(mu, w0, w2, a0, a2, g2, k_k, k_a, r_k, ln_w, ln_b):
    row = lambda t: t.reshape(1, RW_WIDTH)
    padr = lambda t: jnp.concatenate([t, jnp.zeros((LANES - RW_LORA, RW_WIDTH), t.dtype)], axis=0).astype(BF16)
    return (_rw_pad_cols(mu).reshape(1, RW_PCOLS), row(w0), padr(w2), row(a0), padr(a2), g2.astype(BF16),
            row(k_k), row(k_a), row(r_k), row(ln_w), row(ln_b))


def _gdn_kernel(*refs, per_seq, tm, aliased=False):
    if aliased:
        refs = refs[1:]
    if per_seq:
        (q_ref, k_ref, v_ref, z_ref, ba_ref, fq_ref, fk_ref, fv_ref, si_ref, cwq_ref, cwk_ref, cwv_ref,
         al_ref, dt_ref, ng_ref, y_ref, so_ref, q_s, k_s, v_s, ba_s, o_s) = refs
        conv_state = (fq_ref, fk_ref, fv_ref)
    else:
        (q_ref, k_ref, v_ref, z_ref, ba_ref, cwq_ref, cwk_ref, cwv_ref, al_ref, dt_ref, ng_ref,
         y_ref, so_ref, q_s, k_s, v_s, ba_s, o_s, pq_s, pk_s, pv_s, st_s) = refs
        prevs = (pq_s, pk_s, pv_s)

        @pl.when(pl.program_id(1) == 0)
        def _():
            for p in prevs:
                p[...] = jnp.zeros_like(p)
            st_s[...] = jnp.zeros_like(st_s)
    c, hw = CHUNK, GD_HD

    def conv_act(idx, u_ref, cw_ref):
        u = u_ref[...]
        if per_seq:
            f = conv_state[idx][...]
            sh = [_shift_seq8(u, d, f) for d in (1, 2, 3)]
        else:
            p = prevs[idx][...]
            sh = [_shift_chain(u, d, p) for d in (1, 2, 3)]
            prevs[idx][...] = u[tm - SUBLANES:, :]
        x = cw_ref[3:4, :] * u + cw_ref[2:3, :] * sh[0] + cw_ref[1:2, :] * sh[1] + cw_ref[0:1, :] * sh[2]
        return x * _sigmoid(x)

    def l2n(x):
        outs = []
        for hd in range(GD_HEADS):
            xh = x[:, hw * hd:hw * (hd + 1)]
            outs.append(xh * lax.rsqrt(jnp.sum(xh * xh, axis=-1, keepdims=True) + 1e-6))
        return jnp.concatenate(outs, axis=1)

    q_s[...] = l2n(conv_act(0, q_ref, cwq_ref)) * (hw ** -0.5)
    k_s[...] = l2n(conv_act(1, k_ref, cwk_ref))
    v_s[...] = conv_act(2, v_ref, cwv_ref)
    ba = ba_ref[...]
    lane = lax.broadcasted_iota(jnp.int32, ba.shape, 1)
    g_all = -jnp.exp(al_ref[...]) * _softplus(ba + dt_ref[...])
    ba_s[...] = jnp.where(lane < GD_HEADS, _sigmoid(ba), g_all)

    strict, causal = _chunk_masks(per_seq)
    if per_seq:
        big_mask = _seq_block_mask(c, hw)

    heads = range(GD_HEADS)
    sls = [slice(hw * hd, hw * (hd + 1)) for hd in heads]
    group = _chunks_per_iteration(tm // c)

    def chunk_group(gidx, carry):
        pre = []
        for t in range(group):
            idx = gidx * group + t
            rows = pl.ds(pl.multiple_of(idx * c, c), c)
            bg = ba_s[rows, :]
            gc_all = _cumsum_rows(bg, groupwise=per_seq)
            pre.append(dict(rows=rows, seqs=pl.ds(idx * SEQ_PER_CHUNK, SEQ_PER_CHUNK), bg=bg, gc_all=gc_all,
                            gc_t=gc_all.T, gl_all=_group_last(gc_all) if per_seq else gc_all[c - 1:c, :],
                            qa=q_s[rows, :], ka=k_s[rows, :], va=v_s[rows, :]))
        pairs = [(t, hd) for t in range(group) for hd in heads]
        npair = len(pairs)
        beta = [pre[t]['bg'][:, hd:hd + 1] for t, hd in pairs]
        gc = [pre[t]['gc_all'][:, GD_HEADS + hd:GD_HEADS + hd + 1] for t, hd in pairs]
        gl = [pre[t]['gl_all'][:, GD_HEADS + hd:GD_HEADS + hd + 1] for t, hd in pairs]
        dec = [jnp.exp(jnp.where(causal, gc[p] - pre[t]['gc_t'][GD_HEADS + hd:GD_HEADS + hd + 1, :], -1e30))
               for p, (t, hd) in enumerate(pairs)]
        qh = [pre[t]['qa'][:, sls[hd]] for t, hd in pairs]
        kh = [pre[t]['ka'][:, sls[hd]] for t, hd in pairs]
        kb = [kh[p] * beta[p] for p in range(npair)]
        eg = [jnp.exp(g) for g in gc]
        kq = [_mm_nt(jnp.concatenate([kb[p], qh[p]], axis=0).astype(BF16), kh[p].astype(BF16))
              for p in range(npair)]
        low = [jnp.where(strict, kq[p][:c] * dec[p], 0.0) for p in range(npair)]
        intra = [(kq[p][c:] * dec[p]).astype(BF16) for p in range(npair)]
        sol = _solve_unit_lower_each(
            low, [jnp.concatenate([pre[t]['va'][:, sls[hd]] * beta[p], kb[p] * eg[p]], axis=1)
                  for p, (t, hd) in enumerate(pairs)], per_seq)
        wq = [jnp.concatenate([sol[p][:, hw:], qh[p] * eg[p]], axis=0).astype(BF16) for p in range(npair)]
        kdec = [kh[p] * jnp.exp(gl[p] - gc[p]) for p in range(npair)]
        for t in range(group):
            rows, seqs = pre[t]['rows'], pre[t]['seqs']
            ps = [t * GD_HEADS + hd for hd in heads]
            if per_seq:
                st3 = [si_ref[seqs, hd] for hd in heads]
                st_rhs = [jnp.concatenate([s3[s] for s in range(SEQ_PER_CHUNK)], axis=1).astype(BF16)
                          for s3 in st3]
            else:
                st = [st_s[hd] for hd in heads]
                st_rhs = [s.astype(BF16) for s in st]
            ws = [_mm(wq[p], st_rhs[hd]) for hd, p in zip(heads, ps)]
            if per_seq:
                v_new = [sol[p][:, :hw] - _pick_own_seq(ws[hd][:c], hw) for hd, p in zip(heads, ps)]
                o_st = [_pick_own_seq(ws[hd][c:], hw) for hd in heads]
            else:
                v_new = [sol[p][:, :hw] - ws[hd][:c] for hd, p in zip(heads, ps)]
                o_st = [ws[hd][c:] for hd in heads]
            vnb = [v.astype(BF16) for v in v_new]
            o_in = [_mm(intra[p], vnb[hd]) for hd, p in zip(heads, ps)]
            if per_seq:
                upd = [_mm_tn(jnp.where(big_mask, jnp.concatenate([kdec[p]] * SEQ_PER_CHUNK, axis=1), 0.0
                                        ).astype(BF16), vnb[hd]) for hd, p in zip(heads, ps)]
            else:
                upd = [_mm_tn(kdec[p].astype(BF16), vnb[hd]) for hd, p in zip(heads, ps)]
            for hd, p in zip(heads, ps):
                if per_seq:
                    egl = jnp.exp(gl[p])
                    mult = jnp.concatenate([jnp.broadcast_to(egl[SUBLANES * s:SUBLANES * s + 1, :], (hw, hw))
                                            for s in range(SEQ_PER_CHUNK)], axis=0)
                    st_new = st3[hd].reshape(SEQ_PER_CHUNK * hw, hw) * mult + upd[hd]
                    so_ref[seqs, hd] = st_new.reshape(SEQ_PER_CHUNK, hw, hw)
                else:
                    st_s[hd] = st[hd] * jnp.exp(gl[p]) + upd[hd]
                o_s[rows, sls[hd]] = o_st[hd] + o_in[hd]
        return carry

    lax.fori_loop(0, tm // (c * group), chunk_group, 0)

    if not per_seq:
        so_ref[0] = st_s[...]

    outs = []
    for hd in range(GD_HEADS):
        sl = slice(hw * hd, hw * (hd + 1))
        o = o_s[:, sl]
        o = o * lax.rsqrt(jnp.mean(o * o, axis=-1, keepdims=True) + NORM_EPS) * ng_ref[...]
        z = z_ref[:, sl]
        outs.append(o * (z * _sigmoid(z)))
    y_ref[...] = jnp.concatenate(outs, axis=1).astype(BF16)


def _gdn_param_specs():
    c2 = lambda shape, j=0: pl.BlockSpec(shape, (lambda *a: (0, j)))
    return [c2((GD_CONV, GD_WIDTH), 0), c2((GD_CONV, GD_WIDTH), 1), c2((GD_CONV, GD_WIDTH), 2),
            c2((1, LANES)), c2((1, LANES)), c2((1, GD_HD))]


def _gdn_scratch(tm):
    return ([pltpu.VMEM((tm, GD_WIDTH), F32) for _ in range(3)] + [pltpu.VMEM((tm, LANES), F32),
                                                                  pltpu.VMEM((tm, GD_WIDTH), F32)])


def _gdn_prompt(proj, proj_sb, gp, nb, seq, tm):
    nt = seq // tm
    qb = P_Q // GD_WIDTH
    blk = lambda j: pl.BlockSpec((tm, GD_WIDTH), lambda b, t: (b * nt + t, j))
    y, st = pl.pallas_call(
        functools.partial(_gdn_kernel, per_seq=False, tm=tm),
        grid=(nb, nt),
        in_specs=[blk(qb), blk(qb + 1), blk(qb + 2), blk(P_Z // GD_WIDTH),
                  pl.BlockSpec((tm, LANES), lambda b, t: (b * nt + t, P_BA // LANES))] + _gdn_param_specs(),
        out_specs=[pl.BlockSpec((tm, GD_WIDTH), lambda b, t: (b * nt + t, 0)),
                   pl.BlockSpec((1, GD_HEADS, GD_HD, GD_HD), lambda b, t: (b, 0, 0, 0))],
        out_shape=[jax.ShapeDtypeStruct((nb * seq, GD_WIDTH), BF16),
                   jax.ShapeDtypeStruct((nb, GD_HEADS, GD_HD, GD_HD), F32)],
        scratch_shapes=_gdn_scratch(tm) + [pltpu.VMEM((SUBLANES, GD_WIDTH), F32) for _ in range(3)]
        + [pltpu.VMEM((GD_HEADS, GD_HD, GD_HD), F32)],
        compiler_params=_cparams(("parallel", "arbitrary")),
        name="gdn_prompt",
    )(proj, proj, proj, proj, proj_sb, gp[0], gp[0], gp[0], *gp[1:])
    return y, st


def _gdn_sample(proj, proj_sb, conv0, st0, layer, prev_st, gp, row0, nseq, tm):
    ns = tm // SUBLANES
    qb = P_Q // GD_WIDTH
    r0 = row0 // tm
    blk = lambda j: pl.BlockSpec((tm, GD_WIDTH), lambda i: (r0 + i, j))
    frm = lambda j: _layer_spec((ns, GD_CONV - 1, GD_WIDTH), lambda i: (i, 0, j), layer)
    st_io = _layer_spec((ns, GD_HEADS, GD_HD, GD_HD), lambda i: (i, 0, 0, 0), layer)
    extra, extra_specs, aliases = _stacked_out(prev_st)
    y, st = pl.pallas_call(
        functools.partial(_gdn_kernel, per_seq=True, tm=tm, aliased=prev_st is not None),
        grid=(nseq // ns,),
        in_specs=extra_specs + [blk(qb), blk(qb + 1), blk(qb + 2), blk(P_Z // GD_WIDTH),
                                pl.BlockSpec((tm, LANES), lambda i: (r0 + i, P_BA // LANES)),
                                frm(0), frm(1), frm(2), st_io] + _gdn_param_specs(),
        out_specs=[pl.BlockSpec((tm, GD_WIDTH), lambda i: (i, 0)), st_io],
        out_shape=[jax.ShapeDtypeStruct((nseq * SUBLANES, GD_WIDTH), BF16),
                   jax.ShapeDtypeStruct(st0.shape, F32)],
        input_output_aliases=aliases,
        scratch_shapes=_gdn_scratch(tm),
        compiler_params=_cparams(("parallel",)),
        name="gdn_sample",
    )(*extra, proj, proj, proj, proj, proj_sb, conv0, conv0, conv0, st0, gp[0], gp[0], gp[0], *gp[1:])
    return y, st


def _gdn_params(conv_w, a_log, dt_bias, norm_g):
    pad = lambda t: jnp.concatenate([jnp.zeros((GD_HEADS,), F32), t, jnp.zeros((LANES - 2 * GD_HEADS,), F32)]
                                    ).reshape(1, LANES)
    return (conv_w, pad(a_log), pad(dt_bias), norm_g.reshape(1, GD_HD))


W_IN_RW = S5_WIDTH
W_IN_QKVZ = W_IN_RW + RW_COLS
W_IN_BA = W_IN_QKVZ + 4 * GD_WIDTH
W_IN_GATES = W_IN_BA + 2 * GD_HEADS


def _split_w_in(w):
    w_t = jnp.swapaxes(w, 1, 2)
    zeros = lambda rows: jnp.zeros((w_t.shape[0], rows, w_t.shape[2]), w_t.dtype)
    a = W_IN_RW + 3 * RW_WIDTH + RW_LORA
    w_rw_t = jnp.concatenate([w_t[:, W_IN_RW:a], zeros(LANES - RW_LORA), w_t[:, a:a + RW_LORA],
                              zeros(LANES - RW_LORA), w_t[:, a + RW_LORA:W_IN_QKVZ]], axis=1)
    w_sb_t = jnp.concatenate([w_t[:, :S5_WIDTH], w_t[:, W_IN_BA:W_IN_GATES],
                              zeros(P_SB_COLS - P_BA - 2 * GD_HEADS)], axis=1)
    return w_t, w_rw_t, w_sb_t


def _tail_rows(a, cols, nrows, mp, seq):
    a3 = a.reshape(a.shape[0] // SUBLANES, SUBLANES, a.shape[1])
    g = seq // SUBLANES
    lo = SUBLANES - nrows
    return (a3[g - 1:mp // SUBLANES:g, lo:, cols[0]:cols[1]], a3[mp // SUBLANES:, lo:, cols[0]:cols[1]])


def kernel(x_prompt, x_sample, state_s5_re, state_s5_im, state_rwkv_shift, state_rwkv_wkv, state_gdn_conv, state_gdn, state_ffn_conv, norm1_g, norm2_g, final_norm_g, w_in, s5_lambda_re, s5_lambda_im, s5_b_re, s5_b_im, s5_c_re, s5_c_im, s5_d, s5_log_step, s5_w_glu, rwkv_mu, rwkv_w0, rwkv_w2, rwkv_a0, rwkv_a2, rwkv_g2, rwkv_k_k, rwkv_k_a, rwkv_r_k, rwkv_ln_w, rwkv_ln_b, gdn_conv_w, gdn_a_log, gdn_dt_bias, gdn_norm_g, w_br_s5, w_br_rwkv, w_br_gdn, w_out, ffn_w_up, ffn_conv_w, ffn_conv_b, ffn_w_down):
    nb, seq, d = x_prompt.shape
    ns, sl, _ = x_sample.shape
    assert sl == SUBLANES and d == D_MODEL
    mp, ms = nb * seq, ns * sl
    m = mp + ms
    tm_norm, tm_proj, tm_out = 512, 3072, 512
    x_groups = (x_prompt.reshape(mp, d), x_sample.reshape(ms, d), 0)

    w_t, w_rw_t, w_sb_t = _split_w_in(w_in)
    w_br = (w_br_s5.astype(BF16), w_br_rwkv.astype(BF16), w_br_gdn.astype(BF16))
    w_out_b, w_down_b = w_out.astype(BF16), ffn_w_down.astype(BF16)
    conv_b = ffn_conv_b.reshape(DEPTH, 1, 2 * D_FF)
    shift0 = _rw_pad_cols(state_rwkv_shift)[:, :, None, :]

    new_p = [[] for _ in range(7)]
    new_s = [[] for _ in range(7)]
    s_wkv = s_gdn = None
    for l in range(DEPTH):
        xp, xs, s_off = x_groups
        xn1 = _norm_cast(xp, xs, s_off * (mp // tm_norm), m, mp // tm_norm, norm1_g[l], tm=tm_norm)
        proj_g = _matmul_nt(xn1, w_t, l, W_IN_GATES, 3 * D_MODEL, tm=tm_proj, tn=512)
        proj_rw = _matmul_nt(xn1, w_rw_t, l, 0, RW_PCOLS, tm=tm_proj, tn=512)
        proj_qkvz = _matmul_nt(xn1, w_t, l, W_IN_QKVZ, 4 * GD_WIDTH, tm=tm_proj, tn=512)
        proj_sb = _matmul_nt(xn1, w_sb_t, l, 0, P_SB_COLS, tm=tm_proj, tn=512)

        sp = _s5_params(s5_lambda_re[l], s5_lambda_im[l], s5_b_re[l], s5_b_im[l], s5_c_re[l], s5_c_im[l],
                        s5_d[l], s5_log_step[l], s5_w_glu[l])
        ys5_p, p_re, p_im = _s5_prompt(proj_sb, sp, nb, seq, tm=256)
        ys5_s, s_re, s_im = _s5_sample(proj_sb, state_s5_re[l], state_s5_im[l], sp, mp, ns, tm=128)

        rp = _rwkv_params(rwkv_mu[l], rwkv_w0[l], rwkv_w2[l], rwkv_a0[l], rwkv_a2[l], rwkv_g2[l], rwkv_k_k[l],
                          rwkv_k_a[l], rwkv_r_k[l].reshape(RW_WIDTH), rwkv_ln_w[l], rwkv_ln_b[l])
        yrw_p, p_wkv = _rwkv_prompt(proj_rw, rp, nb, seq, tm=256)
        yrw_s, s_wkv = _rwkv_sample(proj_rw, shift0[l], state_rwkv_wkv, l, s_wkv, rp, mp, ns, tm=128)
        p_shift, s_shift = [_rw_unpad_cols(t[:, 0])
                            for t in _tail_rows(proj_rw, (P_RW, P_RW + RW_PCOLS), 1, mp, seq)]

        gp = _gdn_params(gdn_conv_w[l], gdn_a_log[l], gdn_dt_bias[l], gdn_norm_g[l])
        ygd_p, p_gdn = _gdn_prompt(proj_qkvz, proj_sb, gp, nb, seq, tm=256)
        ygd_s, s_gdn = _gdn_sample(proj_qkvz, proj_sb, state_gdn_conv, state_gdn, l, s_gdn, gp, mp, ns, tm=64)
        p_gconv, s_gconv = _tail_rows(proj_qkvz, (P_Q, P_Q + 3 * GD_WIDTH), GD_CONV - 1, mp, seq)

        merged = _merge((ys5_p, yrw_p, ygd_p), (ys5_s, yrw_s, ygd_s), w_br, l, proj_g, tm=512, tn=1024)
        x, xn = _out_norm(merged, w_out_b, l, xp, xs, s_off * (mp // tm_out), mp // tm_out, norm2_g[l], tm=tm_out)

        h, p_ffn, s_ffn = _ffn_up(xn, ffn_w_up, ffn_conv_w, conv_b, state_ffn_conv, l, nb, seq, ns,
                                  tm=1024, tn=512)
        x = _matmul_res(h, w_down_b, l, x, tm=1536, tn=512)
        x_groups = (x, x, 1)

        for lst, vals in ((new_p, (p_re, p_im, p_shift, p_wkv, p_gconv, p_gdn, p_ffn)),
                          (new_s, (s_re, s_im, s_shift, None, s_gconv, None, s_ffn))):
            for acc, val in zip(lst, vals):
                acc.append(val)

    y_p, y_s = _final_norm(x, final_norm_g, mp, tm=512)
    stack = lambda lst: [jnp.stack(v) for v in lst]
    out_p, out_s = stack(new_p), stack([v for v in new_s if v[0] is not None])
    out_s = out_s[:3] + [s_wkv, out_s[3], s_gdn, out_s[4]]
    return (y_p.reshape(nb, seq, d), y_s.reshape(ns, sl, d)) + tuple(out_p) + tuple(out_s)
```

```python
import functools
import math

import jax
import jax.numpy as jnp
from jax import lax
from jax.experimental import pallas as pl
from jax.experimental.pallas import tpu as pltpu

F32 = jnp.float32
BF16 = jnp.bfloat16

SUBLANES = 8
LANES = 128
VMEM_LIMIT = 52 * 1024 * 1024
CHUNK = 64
SEQ_PER_CHUNK = CHUNK // SUBLANES

D_MODEL = 2048
DEPTH = 2
S5_WIDTH = 512
S5_GROUP = 16
S5_GROUPS = 32
S5_STATE = 64
S5_NSTATE = S5_GROUPS * S5_STATE
RW_WIDTH = 512
RW_HEADS = 8
RW_HD = 64
RW_LORA = 96
RW_GATE = 256
RW_COLS = 3 * RW_WIDTH + 2 * RW_LORA + RW_GATE
RW_PCOLS = 2048
RW_GN_EPS = 64e-5
GD_WIDTH = 1024
GD_HEADS = 8
GD_HD = 128
GD_CONV = 4
D_FF = 5632
FFN_CONV = 3
NORM_EPS = 1e-6
P_GATES, P_RW, P_Q, P_Z, P_S5, P_BA, P_SB_COLS = 0, 0, 0, 3072, 0, 512, 1024


def _cparams(sem):
    return pltpu.CompilerParams(dimension_semantics=sem, vmem_limit_bytes=VMEM_LIMIT)


def _mm(a, b):
    return jnp.dot(a, b, preferred_element_type=F32)


def _mm_nt(a, b):
    return lax.dot_general(a, b, (((1,), (1,)), ((), ())), preferred_element_type=F32)


def _mm_tn(a, b):
    return lax.dot_general(a, b, (((0,), (0,)), ((), ())), preferred_element_type=F32)


def _sigmoid(x):
    return 1.0 / (1.0 + jnp.exp(-x))


def _softplus(x):
    return jnp.maximum(x, 0.0) + jnp.log1p(jnp.exp(-jnp.abs(x)))


def _gelu_tanh(x):
    return 0.5 * x * (1.0 + jnp.tanh(math.sqrt(2.0 / math.pi) * (x + 0.044715 * (x * x * x))))


def _row_iota(shape):
    return lax.broadcasted_iota(jnp.int32, shape, 0)


def _shift_chain(x, d, prev8):
    n, ch = x.shape
    x3 = x.reshape(n // SUBLANES, SUBLANES, ch)
    rot = pltpu.roll(x3, d, 1)
    before = jnp.concatenate([pltpu.roll(prev8, d, 0)[None], rot[:-1]], axis=0)
    row = lax.broadcasted_iota(jnp.int32, x3.shape, 1)
    return jnp.where(row < d, before, rot).reshape(n, ch)


def _shift_seq8(x, d, state):
    n, ch = x.shape
    ns, w1, _ = state.shape
    out = pltpu.roll(x.reshape(ns, SUBLANES, ch), d, 1)
    row = lax.broadcasted_iota(jnp.int32, out.shape, 1)
    for r in range(d):
        out = jnp.where(row == r, state[:, w1 - d + r:w1 - d + r + 1, :], out)
    return out.reshape(n, ch)


def _group_last(x):
    c, n = x.shape
    x3 = x.reshape(c // SUBLANES, SUBLANES, n)
    return jnp.broadcast_to(x3[:, SUBLANES - 1:SUBLANES, :], x3.shape).reshape(c, n)


def _cumsum_rows(x, groupwise=False):
    c = x.shape[0]
    row = _row_iota(x.shape) % SUBLANES
    for d in (1, 2, 4):
        x = x + jnp.where(row >= d, pltpu.roll(x, d, 0), 0.0)
    if c > SUBLANES and not groupwise:
        blocks = [x[SUBLANES * i:SUBLANES * (i + 1)] for i in range(c // SUBLANES)]
        for i in range(1, len(blocks)):
            blocks[i] = blocks[i] + blocks[i - 1][SUBLANES - 1:SUBLANES, :]
        x = jnp.concatenate(blocks, axis=0)
    return x


def _split2(x):
    hi = x.astype(BF16)
    return hi, (x - hi.astype(F32)).astype(BF16)


def _hi_lo_f32(x):
    hi = x.astype(BF16).astype(F32)
    return hi, x - hi


def _mm_hilo_each(a_list, b_list):
    lhs, rhs = [], []
    for a, b in zip(a_list, b_list):
        ah, al = _hi_lo_f32(a)
        bh, bl = _hi_lo_f32(b)
        lhs.append(jnp.concatenate([ah, ah, al], axis=1).astype(BF16))
        rhs.append(jnp.concatenate([bh, bl, bh], axis=0).astype(BF16))
    return [_mm(l, r) for l, r in zip(lhs, rhs)]


def _solve_unit_lower_each(lows, rhss, diag_only):
    c = lows[0].shape[0]
    nb = c // SUBLANES
    nh = len(lows)
    n = rhss[0].shape[1]
    blk = lambda a, i: a[SUBLANES * i:SUBLANES * (i + 1)]
    rb = [[blk(r, i) for i in range(nb)] for r in rhss]
    with_off = nb > 1 and not diag_only
    same = (lax.broadcasted_iota(jnp.int32, (c, c), 0) // SUBLANES
            == lax.broadcasted_iota(jnp.int32, (c, c), 1) // SUBLANES)
    if with_off:
        ob = [[blk(jnp.where(same, 0.0, low), i) for i in range(nb)] for low in lows]
    sel = (lax.broadcasted_iota(jnp.int32, (c, (SUBLANES - 1) * LANES), 0) % SUBLANES
           == lax.broadcasted_iota(jnp.int32, (c, (SUBLANES - 1) * LANES), 1) // LANES)
    sel = jnp.where(sel, 1.0, 0.0).astype(BF16)
    sel2 = jnp.concatenate([sel, sel], axis=0)
    diag = [_hi_lo_f32(jnp.where(same, low, 0.0)) if not diag_only else _hi_lo_f32(low) for low in lows]
    cols = [_mm(jnp.concatenate(d, axis=1).astype(BF16), sel2) for d in diag]
    reps = n // LANES
    for j in range(SUBLANES - 1):
        for h in range(nh):
            cj = cols[h][:, LANES * j:LANES * (j + 1)]
            cj_n = jnp.concatenate([cj] * reps, axis=1) if reps > 1 else cj
            for i in range(nb):
                rb[h][i] = rb[h][i] - blk(cj_n, i) * rb[h][i][j:j + 1, :]
                if with_off and i > 0:
                    ob[h][i] = ob[h][i] - blk(cj, i)[:, :c] * ob[h][i][j:j + 1, :]
    r1 = [jnp.concatenate(b, axis=0) if nb > 1 else b[0] for b in rb]
    if not with_off:
        return r1
    n1 = [jnp.concatenate(b, axis=0) for b in ob]
    n2 = _mm_hilo_each(n1, n1)
    t = _mm_hilo_each(n1 + n2, r1 + n2)
    y = [r - nr for r, nr in zip(r1, t[:nh])]
    n4 = t[nh:]
    y = [a + b for a, b in zip(y, _mm_hilo_each(n2, y))]
    return [a + b for a, b in zip(y, _mm_hilo_each(n4, y))]


def _chunks_per_iteration(n_chunks):
    return 2 if n_chunks % 2 == 0 else 1


def _chunk_masks(per_seq):
    ri = lax.broadcasted_iota(jnp.int32, (CHUNK, CHUNK), 0)
    ci = lax.broadcasted_iota(jnp.int32, (CHUNK, CHUNK), 1)
    strict, causal = ri > ci, ri >= ci
    if per_seq:
        same = (ri // SUBLANES) == (ci // SUBLANES)
        strict, causal = jnp.logical_and(strict, same), jnp.logical_and(causal, same)
    return strict, causal


def _seq_block_mask(rows, width):
    r = lax.broadcasted_iota(jnp.int32, (rows, SEQ_PER_CHUNK * width), 0)
    l = lax.broadcasted_iota(jnp.int32, (rows, SEQ_PER_CHUNK * width), 1)
    return ((r % CHUNK) // SUBLANES) == (l // width)


def _pick_own_seq(a, width):
    return jnp.concatenate([a[SUBLANES * s:SUBLANES * (s + 1), width * s:width * (s + 1)]
                            for s in range(SEQ_PER_CHUNK)], axis=0)


def _rmsnorm_rows(x_ref, g_ref, o_ref, rows, chunk=128):
    def body(i, carry):
        r0 = pl.multiple_of(i * chunk, chunk)
        x = x_ref[pl.ds(r0, chunk), :]
        ms = jnp.mean(x * x, axis=-1, keepdims=True)
        o_ref[pl.ds(r0, chunk), :] = (x * lax.rsqrt(ms + NORM_EPS) * g_ref[...]).astype(o_ref.dtype)
        return carry
    lax.fori_loop(0, rows // chunk, body, 0)


def _two_group_specs(tm, width, prompt_tiles, sample_off, pipeline_mode=None):
    p = pl.BlockSpec((tm, width), lambda i, *_: (jnp.minimum(i, prompt_tiles - 1), 0), pipeline_mode=pipeline_mode)
    s = pl.BlockSpec((tm, width), lambda i, *_: (jnp.maximum(i - prompt_tiles, 0) + sample_off, 0),
                     pipeline_mode=pipeline_mode)
    return p, s


def _layer_spec(shape, index_map, layer):
    return pl.BlockSpec((None,) + shape, lambda *a: (layer,) + tuple(index_map(*a)))


def _norm_cast_kernel(xp_ref, xs_ref, g_ref, o_ref, *, rows, prompt_tiles):
    i = pl.program_id(0)

    @pl.when(i < prompt_tiles)
    def _():
        _rmsnorm_rows(xp_ref, g_ref, o_ref, rows)

    @pl.when(i >= prompt_tiles)
    def _():
        _rmsnorm_rows(xs_ref, g_ref, o_ref, rows)


def _norm_cast(xp, xs, sample_off, m, prompt_tiles, g, tm):
    k = xp.shape[1]
    psp, ssp = _two_group_specs(tm, k, prompt_tiles, sample_off)
    return pl.pallas_call(
        functools.partial(_norm_cast_kernel, rows=tm, prompt_tiles=prompt_tiles),
        grid=(m // tm,),
        in_specs=[psp, ssp, pl.BlockSpec((1, k), lambda i: (0, 0))],
        out_specs=pl.BlockSpec((tm, k), lambda i: (i, 0)),
        out_shape=jax.ShapeDtypeStruct((m, k), BF16),
        compiler_params=_cparams(("arbitrary",)),
        name="norm_cast",
    )(xp, xs, g.reshape(1, k))


def _matmul_nt_kernel(a_ref, w_ref, o_ref):
    o_ref[...] = _mm_nt(a_ref[...], w_ref[0].astype(BF16))


def _matmul_nt(a, w_t, layer, row0, n, tm, tn):
    m, k = a.shape
    assert row0 % SUBLANES == 0 and n % tn == 0
    return pl.pallas_call(
        _matmul_nt_kernel,
        grid=(m // tm, n // tn),
        in_specs=[pl.BlockSpec((tm, k), lambda i, j: (i, 0), pipeline_mode=pl.Buffered(1)),
                  pl.BlockSpec((pl.Element(1), pl.Element(tn), pl.Element(k)),
                               lambda i, j: (layer, pl.multiple_of(row0 + j * tn, SUBLANES), 0))],
        out_specs=pl.BlockSpec((tm, tn), lambda i, j: (i, j)),
        out_shape=jax.ShapeDtypeStruct((m, n), F32),
        compiler_params=_cparams(("parallel", "arbitrary")),
        name="proj_matmul",
    )(a, w_t)


def _matmul_res_kernel(a_ref, w_ref, r_ref, o_ref):
    o_ref[...] = r_ref[...] + _mm(a_ref[...], w_ref[...])


def _matmul_res(a, w, layer, res, tm, tn):
    m, k = a.shape
    n = w.shape[-1]
    return pl.pallas_call(
        _matmul_res_kernel,
        grid=(m // tm, n // tn),
        in_specs=[pl.BlockSpec((tm, k), lambda i, j: (i, 0), pipeline_mode=pl.Buffered(1)),
                  _layer_spec((k, tn), lambda i, j: (0, j), layer),
                  pl.BlockSpec((tm, tn), lambda i, j: (i, j))],
        out_specs=pl.BlockSpec((tm, tn), lambda i, j: (i, j)),
        out_shape=jax.ShapeDtypeStruct((m, n), F32),
        compiler_params=_cparams(("parallel", "arbitrary")),
        name="matmul_res",
    )(a, w, res)


def _out_norm_kernel(a_ref, w_ref, rp_ref, rs_ref, g_ref, x_ref, xn_ref, *, rows, prompt_tiles):
    res = jnp.where(pl.program_id(0) < prompt_tiles, rp_ref[...], rs_ref[...])
    x_ref[...] = res + _mm(a_ref[...], w_ref[...])
    _rmsnorm_rows(x_ref, g_ref, xn_ref, rows)


def _out_norm(a, w, layer, rp, rs, sample_off, prompt_tiles, g, tm):
    m, k = a.shape
    n = w.shape[-1]
    psp, ssp = _two_group_specs(tm, n, prompt_tiles, sample_off)
    return pl.pallas_call(
        functools.partial(_out_norm_kernel, rows=tm, prompt_tiles=prompt_tiles),
        grid=(m // tm,),
        in_specs=[pl.BlockSpec((tm, k), lambda i: (i, 0)),
                  _layer_spec((k, n), lambda i: (0, 0), layer),
                  psp, ssp, pl.BlockSpec((1, n), lambda i: (0, 0))],
        out_specs=[pl.BlockSpec((tm, n), lambda i: (i, 0)), pl.BlockSpec((tm, n), lambda i: (i, 0))],
        out_shape=[jax.ShapeDtypeStruct((m, n), F32), jax.ShapeDtypeStruct((m, n), BF16)],
        compiler_params=_cparams(("parallel",)),
        name="out_norm",
    )(a, w, rp, rs, g.reshape(1, n))


def _merge_kernel(y1p, y2p, y3p, y1s, y2s, y3s, w1_ref, w2_ref, w3_ref, g1_ref, g2_ref, g3_ref, o_ref, *,
                  prompt_tiles, tm, rc):
    is_p = pl.program_id(0) < prompt_tiles
    for r in range(tm // rc):
        rows = slice(r * rc, (r + 1) * rc)
        pick = lambda p, s: jnp.where(is_p, p[rows, :], s[rows, :])
        acc = _sigmoid(g1_ref[rows, :]) * _mm(pick(y1p, y1s), w1_ref[...])
        acc = acc + _sigmoid(g2_ref[rows, :]) * _mm(pick(y2p, y2s), w2_ref[...])
        acc = acc + _sigmoid(g3_ref[rows, :]) * _mm(pick(y3p, y3s), w3_ref[...])
        o_ref[rows, :] = acc.astype(BF16)


def _merge(ys_p, ys_s, ws, layer, proj, tm, tn):
    mp, ms = ys_p[0].shape[0], ys_s[0].shape[0]
    n = ws[0].shape[-1]
    nj = n // tn
    pt = mp // tm
    gate_spec = lambda b: pl.BlockSpec((tm, tn), lambda i, j: (i, P_GATES // tn + b * nj + j))
    y_specs = [_two_group_specs(tm, y.shape[1], pt, 0) for y in ys_p]
    w_spec = lambda w: _layer_spec((w.shape[1], tn), lambda i, j: (0, j), layer)
    return pl.pallas_call(
        functools.partial(_merge_kernel, prompt_tiles=pt, tm=tm, rc=128),
        grid=((mp + ms) // tm, nj),
        in_specs=[s[0] for s in y_specs] + [s[1] for s in y_specs] + [w_spec(w) for w in ws]
        + [gate_spec(0), gate_spec(1), gate_spec(2)],
        out_specs=pl.BlockSpec((tm, tn), lambda i, j: (i, j)),
        out_shape=jax.ShapeDtypeStruct((mp + ms, n), BF16),
        compiler_params=_cparams(("parallel", "arbitrary")),
        name="merge",
    )(*ys_p, *ys_s, *ws, proj, proj, proj)


def _final_norm_kernel(x_ref, g_ref, yp_ref, ys_ref, *, prompt_tiles):
    x = x_ref[...]
    ms = jnp.mean(x * x, axis=-1, keepdims=True)
    y = x * lax.rsqrt(ms + NORM_EPS) * g_ref[...]
    i = pl.program_id(0)

    @pl.when(i < prompt_tiles)
    def _():
        yp_ref[...] = y

    @pl.when(i >= prompt_tiles)
    def _():
        ys_ref[...] = y


def _final_norm(x, g, mp, tm):
    m, k = x.shape
    pt = mp // tm
    psp, ssp = _two_group_specs(tm, k, pt, 0)
    return pl.pallas_call(
        functools.partial(_final_norm_kernel, prompt_tiles=pt),
        grid=(m // tm,),
        in_specs=[pl.BlockSpec((tm, k), lambda i: (i, 0)), pl.BlockSpec((1, k), lambda i: (0, 0))],
        out_specs=[psp, ssp],
        out_shape=[jax.ShapeDtypeStruct((mp, k), F32), jax.ShapeDtypeStruct((m - mp, k), F32)],
        compiler_params=_cparams(("arbitrary",)),
        name="final_norm",
    )(x, g.reshape(1, k))


def _ffn_up_kernel(xn_ref, wg_ref, wv_ref, cwg_ref, cwv_ref, bg_ref, bv_ref, sg_ref, sv_ref,
                   h_ref, pg_ref, pv_ref, og_ref, ov_ref, wgb, wvb, cg, cv, *, tm, rc, prompt_tiles, tiles_per_seq):
    i = pl.program_id(1)
    w1 = FFN_CONV - 1

    @pl.when(i == 0)
    def _():
        wgb[...] = wg_ref[...].astype(BF16)
        wvb[...] = wv_ref[...].astype(BF16)

    def conv(u, s1, s2, cw_ref, b_ref):
        return cw_ref[2:3, :] * u + cw_ref[1:2, :] * s1 + cw_ref[0:1, :] * s2 + b_ref[...]

    def up(rows):
        x = xn_ref[rows, :]
        return _mm(x, wgb[...]), _mm(x, wvb[...])

    @pl.when(i < prompt_tiles)
    def _():
        first = (i % tiles_per_seq) == 0
        pg = jnp.where(first, 0.0, cg[...])
        pv = jnp.where(first, 0.0, cv[...])
        for r in range(tm // rc):
            rows = slice(r * rc, (r + 1) * rc)
            ug, uv = up(rows)
            gate = conv(ug, _shift_chain(ug, 1, pg), _shift_chain(ug, 2, pg), cwg_ref, bg_ref)
            val = conv(uv, _shift_chain(uv, 1, pv), _shift_chain(uv, 2, pv), cwv_ref, bv_ref)
            h_ref[rows, :] = (gate * _sigmoid(gate) * val).astype(BF16)
            pg, pv = ug[rc - SUBLANES:, :], uv[rc - SUBLANES:, :]
        cg[...] = pg
        cv[...] = pv
        pg_ref[0] = pg[SUBLANES - w1:, :]
        pv_ref[0] = pv[SUBLANES - w1:, :]

    @pl.when(i >= prompt_tiles)
    def _():
        nsr = rc // SUBLANES
        for r in range(tm // rc):
            rows = slice(r * rc, (r + 1) * rc)
            seqs = slice(r * nsr, (r + 1) * nsr)
            ug, uv = up(rows)
            sg, sv = sg_ref[seqs], sv_ref[seqs]
            gate = conv(ug, _shift_seq8(ug, 1, sg), _shift_seq8(ug, 2, sg), cwg_ref, bg_ref)
            val = conv(uv, _shift_seq8(uv, 1, sv), _shift_seq8(uv, 2, sv), cwv_ref, bv_ref)
            h_ref[rows, :] = (gate * _sigmoid(gate) * val).astype(BF16)
            og_ref[seqs] = ug.reshape(nsr, SUBLANES, ug.shape[1])[:, SUBLANES - w1:, :]
            ov_ref[seqs] = uv.reshape(nsr, SUBLANES, uv.shape[1])[:, SUBLANES - w1:, :]


def _ffn_up(xn, w_up, conv_w, conv_b, state, layer, nb, seq, ns, tm, tn):
    m, k = xn.shape
    mp = nb * seq
    assert ns * SUBLANES == tm and (m - mp) == tm and seq % tm == 0
    nj, pt, tps = D_FF // tn, mp // tm, seq // tm
    w1 = FFN_CONV - 1
    wsp = lambda off: _layer_spec((k, tn), lambda j, i: (0, off + j), layer)
    par = lambda rows, off: _layer_spec((rows, tn), lambda j, i: (0, off + j), layer)
    stsp = lambda off: _layer_spec((ns, w1, tn), lambda j, i: (0, 0, off + j), layer)
    psp = pl.BlockSpec((1, w1, tn), lambda j, i: (jnp.minimum(i, pt - 1) // tps, 0, j))
    osp = pl.BlockSpec((ns, w1, tn), lambda j, i: (0, 0, j))
    h, pg, pv, og, ov = pl.pallas_call(
        functools.partial(_ffn_up_kernel, tm=tm, rc=128, prompt_tiles=pt, tiles_per_seq=tps),
        grid=(nj, m // tm),
        in_specs=[pl.BlockSpec((tm, k), lambda j, i: (i, 0)), wsp(0), wsp(nj),
                  par(FFN_CONV, 0), par(FFN_CONV, nj), par(1, 0), par(1, nj), stsp(0), stsp(nj)],
        out_specs=[pl.BlockSpec((tm, tn), lambda j, i: (i, j)), psp, psp, osp, osp],
        out_shape=[jax.ShapeDtypeStruct((m, D_FF), BF16),
                   jax.ShapeDtypeStruct((nb, w1, D_FF), F32), jax.ShapeDtypeStruct((nb, w1, D_FF), F32),
                   jax.ShapeDtypeStruct((ns, w1, D_FF), F32), jax.ShapeDtypeStruct((ns, w1, D_FF), F32)],
        scratch_shapes=[pltpu.VMEM((k, tn), BF16), pltpu.VMEM((k, tn), BF16),
                        pltpu.VMEM((SUBLANES, tn), F32), pltpu.VMEM((SUBLANES, tn), F32)],
        compiler_params=_cparams(("arbitrary", "arbitrary")),
        name="ffn_up",
    )(xn, w_up, w_up, conv_w, conv_w, conv_b, conv_b, state, state)
    return h, jnp.concatenate([pg, pv], axis=-1), jnp.concatenate([og, ov], axis=-1)


S5_Q = 4
S5_QW = S5_NSTATE // S5_Q


def _s5_kernel(*refs, chained, tm):
    if chained:
        (u_ref, bre_ref, bim_ref, cre_ref, cim_ref, mre_ref, mim_ref, pre_ref, pim_ref, d_ref, wglu_ref,
         y_ref, xlre_ref, xlim_ref, sre, sim, car_re, car_im) = refs
    else:
        (u_ref, x0re_ref, x0im_ref, bre_ref, bim_ref, cre_ref, cim_ref, mre_ref, mim_ref, pre_ref, pim_ref,
         d_ref, wglu_ref, y_ref, xlre_ref, xlim_ref, sre, sim) = refs

    u = u_ref[...]
    ub = u.astype(BF16)
    for q in range(S5_Q):
        uq = ub[:, LANES * q:LANES * (q + 1)]
        sre[:, S5_QW * q:S5_QW * (q + 1)] = _mm(uq, bre_ref[q])
        sim[:, S5_QW * q:S5_QW * (q + 1)] = _mm(uq, bim_ref[q])

    if chained:
        @pl.when(pl.program_id(1) == 0)
        def _():
            car_re[...] = jnp.zeros_like(car_re)
            car_im[...] = jnp.zeros_like(car_im)

    def blk(i, carry):
        r0 = pl.multiple_of(i * SUBLANES, SUBLANES)
        for q in range(S5_Q):
            sl = slice(S5_QW * q, S5_QW * (q + 1))
            xr = sre[pl.ds(r0, SUBLANES), sl]
            xi = sim[pl.ds(r0, SUBLANES), sl]
            for li, d in enumerate((1, 2, 4)):
                mr, mi = mre_ref[li, :, sl], mim_ref[li, :, sl]
                sr, si = pltpu.roll(xr, d, 0), pltpu.roll(xi, d, 0)
                xr, xi = xr + (mr * sr - mi * si), xi + (mr * si + mi * sr)
            if chained:
                cr, ci = car_re[:, sl], car_im[:, sl]
            else:
                cr, ci = x0re_ref[pl.ds(i, 1), sl], x0im_ref[pl.ds(i, 1), sl]
            pr, pi_ = pre_ref[:, sl], pim_ref[:, sl]
            xr, xi = xr + (pr * cr - pi_ * ci), xi + (pr * ci + pi_ * cr)
            sre[pl.ds(r0, SUBLANES), sl] = xr
            sim[pl.ds(r0, SUBLANES), sl] = xi
            if chained:
                car_re[:, sl] = xr[SUBLANES - 1:SUBLANES, :]
                car_im[:, sl] = xi[SUBLANES - 1:SUBLANES, :]
            else:
                xlre_ref[pl.ds(i, 1), sl] = xr[SUBLANES - 1:SUBLANES, :]
                xlim_ref[pl.ds(i, 1), sl] = xi[SUBLANES - 1:SUBLANES, :]
        return carry

    lax.fori_loop(0, tm // SUBLANES, blk, 0)

    if chained:
        xlre_ref[0] = car_re[...]
        xlim_ref[0] = car_im[...]

    ys = []
    for q in range(S5_Q):
        sl = slice(S5_QW * q, S5_QW * (q + 1))
        ys.append(_mm(sre[:, sl].astype(BF16), cre_ref[q]) - _mm(sim[:, sl].astype(BF16), cim_ref[q]))
    y = jnp.concatenate(ys, axis=1) + d_ref[...] * u
    y = _gelu_tanh(y)
    y = y * _sigmoid(_mm(y.astype(BF16), wglu_ref[...]))
    y_ref[...] = y.astype(BF16)


def _const_spec(shape):
    zeros = (0,) * len(shape)
    return pl.BlockSpec(shape, lambda *a: zeros)


def _s5_param_specs():
    return [_const_spec((S5_Q, LANES, S5_QW)), _const_spec((S5_Q, LANES, S5_QW)),
            _const_spec((S5_Q, S5_QW, LANES)), _const_spec((S5_Q, S5_QW, LANES)),
            _const_spec((3, SUBLANES, S5_NSTATE)), _const_spec((3, SUBLANES, S5_NSTATE)),
            _const_spec((SUBLANES, S5_NSTATE)), _const_spec((SUBLANES, S5_NSTATE)),
            _const_spec((1, S5_WIDTH)), _const_spec((S5_WIDTH, S5_WIDTH))]


def _s5_prompt(proj, sp, nb, seq, tm):
    nt = seq // tm
    cb = P_S5 // S5_WIDTH
    y, xre, xim = pl.pallas_call(
        functools.partial(_s5_kernel, chained=True, tm=tm),
        grid=(nb, nt),
        in_specs=[pl.BlockSpec((tm, S5_WIDTH), lambda b, t: (b * nt + t, cb))] + _s5_param_specs(),
        out_specs=[pl.BlockSpec((tm, S5_WIDTH), lambda b, t: (b * nt + t, 0)),
                   pl.BlockSpec((1, 1, S5_NSTATE), lambda b, t: (b, 0, 0)),
                   pl.BlockSpec((1, 1, S5_NSTATE), lambda b, t: (b, 0, 0))],
        out_shape=[jax.ShapeDtypeStruct((nb * seq, S5_WIDTH), BF16),
                   jax.ShapeDtypeStruct((nb, 1, S5_NSTATE), F32),
                   jax.ShapeDtypeStruct((nb, 1, S5_NSTATE), F32)],
        scratch_shapes=[pltpu.VMEM((tm, S5_NSTATE), F32), pltpu.VMEM((tm, S5_NSTATE), F32),
                        pltpu.VMEM((1, S5_NSTATE), F32), pltpu.VMEM((1, S5_NSTATE), F32)],
        compiler_params=_cparams(("parallel", "arbitrary")),
        name="s5_prompt",
    )(proj, *sp)
    return y, xre.reshape(nb, S5_GROUPS, S5_STATE), xim.reshape(nb, S5_GROUPS, S5_STATE)


def _s5_sample(proj, x0re, x0im, sp, row0, nseq, tm):
    ns = tm // SUBLANES
    cb = P_S5 // S5_WIDTH
    r0 = row0 // tm
    st = pl.BlockSpec((ns, S5_NSTATE), lambda i: (i, 0))
    y, xre, xim = pl.pallas_call(
        functools.partial(_s5_kernel, chained=False, tm=tm),
        grid=(nseq // ns,),
        in_specs=[pl.BlockSpec((tm, S5_WIDTH), lambda i: (r0 + i, cb)), st, st] + _s5_param_specs(),
        out_specs=[pl.BlockSpec((tm, S5_WIDTH), lambda i: (i, 0)), st, st],
        out_shape=[jax.ShapeDtypeStruct((nseq * SUBLANES, S5_WIDTH), BF16),
                   jax.ShapeDtypeStruct((nseq, S5_NSTATE), F32),
                   jax.ShapeDtypeStruct((nseq, S5_NSTATE), F32)],
        scratch_shapes=[pltpu.VMEM((tm, S5_NSTATE), F32), pltpu.VMEM((tm, S5_NSTATE), F32)],
        compiler_params=_cparams(("parallel",)),
        name="s5_sample",
    )(proj, x0re.reshape(nseq, S5_NSTATE), x0im.reshape(nseq, S5_NSTATE), *sp)
    return y, xre.reshape(nseq, S5_GROUPS, S5_STATE), xim.reshape(nseq, S5_GROUPS, S5_STATE)


def _s5_params(lam_re, lam_im, b_re, b_im, c_re, c_im, d, log_step, w_glu):
    lam = lax.complex(lam_re, lam_im)
    step = jnp.exp(log_step)[:, None]
    lam_bar = jnp.exp(lam * step)
    b_bar = ((lam_bar - 1.0) / lam)[..., None] * lax.complex(b_re, b_im)

    def bblk(bm):
        bm = bm.reshape(S5_Q, 8, S5_STATE, S5_GROUP)
        eye = jnp.eye(8, dtype=F32)
        out = jnp.einsum('qgpc,gh->qgchp', bm, eye)
        return out.reshape(S5_Q, LANES, S5_QW).astype(BF16)

    def cblk(cm):
        cm = cm.reshape(S5_Q, 8, S5_GROUP, S5_STATE)
        eye = jnp.eye(8, dtype=F32)
        out = jnp.einsum('qgcp,gh->qgphc', cm, eye)
        return out.reshape(S5_Q, S5_QW, LANES).astype(BF16)

    lam_flat = lam_bar.reshape(1, S5_NSTATE)
    pows = [lam_flat]
    for _ in range(SUBLANES - 1):
        pows.append(pows[-1] * lam_flat)
    row = jnp.arange(SUBLANES)[:, None]
    m = jnp.stack([jnp.where(row >= dd, pows[dd - 1], 0.0) for dd in (1, 2, 4)])
    p = jnp.concatenate(pows, axis=0)
    return (bblk(b_bar.real), bblk(b_bar.imag), cblk(c_re), cblk(c_im),
            m.real, m.imag, p.real, p.imag, d.reshape(1, S5_WIDTH), w_glu.astype(BF16))


def _seg64_sum(x):
    outs = []
    for t in range(x.shape[1] // LANES):
        xt = x[:, LANES * t:LANES * (t + 1)]
        lo = lax.broadcasted_iota(jnp.int32, xt.shape, 1) < RW_HD
        s_lo = jnp.sum(jnp.where(lo, xt, 0.0), axis=-1, keepdims=True)
        s_hi = jnp.sum(jnp.where(lo, 0.0, xt), axis=-1, keepdims=True)
        outs.append(jnp.where(lo, s_lo, s_hi))
    return jnp.concatenate(outs, axis=1)


def _rwkv_kernel(*refs, per_seq, tm, aliased=False):
    if aliased:
        refs = refs[1:]
    if per_seq:
        (h_ref, sh_ref, si_ref, mu_ref, w0_ref, w2_ref, a0_ref, a2_ref, g2_ref, kk_ref, ka_ref, rk_ref,
         lnw_ref, lnb_ref, y_ref, so_ref, r_s, k_s, v_s, a_s, b_s, lw_s, y_s) = refs
    else:
        (h_ref, mu_ref, w0_ref, w2_ref, a0_ref, a2_ref, g2_ref, kk_ref, ka_ref, rk_ref, lnw_ref, lnb_ref,
         y_ref, so_ref, r_s, k_s, v_s, a_s, b_s, lw_s, y_s, prev_s, st_s) = refs
    c, w, hd_w = CHUNK, RW_WIDTH, RW_HD

    h = h_ref[...]
    if per_seq:
        prev = _shift_seq8(h, 1, sh_ref[...])
    else:
        @pl.when(pl.program_id(1) == 0)
        def _():
            prev_s[...] = jnp.zeros_like(prev_s)
            st_s[...] = jnp.zeros_like(st_s)
        prev = _shift_chain(h, 1, prev_s[...])
        prev_s[...] = h[tm - SUBLANES:, :]
    hs = h + (prev - h) * mu_ref[...]
    r = hs[:, 0:w]
    k = hs[:, w:2 * w]
    v = hs[:, 2 * w:3 * w]
    wd = hs[:, 3 * w:3 * w + LANES]
    ad = hs[:, 3 * w + LANES:3 * w + 2 * LANES]
    gd = hs[:, 3 * w + 2 * LANES:]
    w_log = -_softplus(-(w0_ref[...] + _mm(jnp.tanh(wd).astype(BF16), w2_ref[...]))) - 0.5
    lw_s[...] = -jnp.exp(w_log)
    a_ic = _sigmoid(a0_ref[...] + _mm(ad.astype(BF16), a2_ref[...]))
    gate = _mm(_sigmoid(gd).astype(BF16), g2_ref[...])
    kx = k * kk_ref[...]
    kkn = kx * lax.rsqrt(_seg64_sum(kx * kx) + 1e-6)
    k2 = k * (1.0 + (a_ic - 1.0) * ka_ref[...])
    r_s[...] = r
    k_s[...] = k2
    v_s[...] = v
    a_s[...] = -kkn
    b_s[...] = kkn * a_ic

    strict, causal = _chunk_masks(per_seq)
    if per_seq:
        big_mask = _seq_block_mask(2 * c, hd_w)

    heads = range(RW_HEADS)
    sls = [slice(hd_w * hd, hd_w * (hd + 1)) for hd in heads]
    group = _chunks_per_iteration(tm // c)

    def chunk_group(gidx, carry):
        pre = []
        for t in range(group):
            idx = gidx * group + t
            rows = pl.ds(pl.multiple_of(idx * c, c), c)
            lw = lw_s[rows, :]
            cum = _cumsum_rows(lw, groupwise=per_seq)
            cl = _group_last(cum) if per_seq else cum[c - 1:c, :]
            e_pos, e_neg, e_end = jnp.exp(cum), jnp.exp(-cum), jnp.exp(cl - cum)
            rr, kc, vc, ac, bc = r_s[rows, :], k_s[rows, :], v_s[rows, :], a_s[rows, :], b_s[rows, :]
            pre.append(dict(rows=rows, seqs=pl.ds(idx * SEQ_PER_CHUNK, SEQ_PER_CHUNK), vc=vc,
                            rt=rr * e_pos, kt=kc * e_neg, bt=bc * e_neg, at=ac * jnp.exp(cum - lw),
                            kh=kc * e_end, bh=bc * e_end, wc=jnp.exp(cl)))
        pairs = [(t, hd) for t in range(group) for hd in heads]
        ath = [pre[t]['at'][:, sls[hd]] for t, hd in pairs]
        vh = [pre[t]['vc'][:, sls[hd]] for t, hd in pairs]
        rth = [pre[t]['rt'][:, sls[hd]] for t, hd in pairs]
        ab = [_mm_nt(jnp.concatenate([ath[p], rth[p]], axis=0).astype(BF16),
                     jnp.concatenate([pre[t]['kt'][:, sls[hd]], pre[t]['bt'][:, sls[hd]]], axis=0).astype(BF16))
              for p, (t, hd) in enumerate(pairs)]
        a_ak = [jnp.where(strict, m[:c, :c], 0.0) for m in ab]
        n_ab = [jnp.where(strict, -m[:c, c:], 0.0) for m in ab]
        a_r = [jnp.concatenate([jnp.where(causal, m[c:, :c], 0.0), jnp.where(causal, m[c:, c:], 0.0)],
                               axis=1).astype(BF16) for m in ab]
        akv = [_mm(a_ak[p].astype(BF16), vh[p].astype(BF16)) for p in range(len(pairs))]
        sol = _solve_unit_lower_each(
            n_ab, [jnp.concatenate([ath[p], akv[p]], axis=1) for p in range(len(pairs))], per_seq)
        kb = [jnp.concatenate([pre[t]['kh'][:, sls[hd]], pre[t]['bh'][:, sls[hd]]], axis=0).astype(BF16)
              for t, hd in pairs]
        for t in range(group):
            rows, seqs, wc = pre[t]['rows'], pre[t]['seqs'], pre[t]['wc']
            ps = [t * RW_HEADS + hd for hd in heads]
            if per_seq:
                st = [si_ref[seqs, hd].reshape(SEQ_PER_CHUNK * hd_w, hd_w) for hd in heads]
            else:
                st = [st_s[hd] for hd in heads]
            pr = [_mm_nt(jnp.concatenate([sol[p][:, :hd_w], rth[p]], axis=0).astype(BF16), st[hd].astype(BF16))
                  for hd, p in zip(heads, ps)]
            if per_seq:
                u = [_pick_own_seq(pr[hd][:c], hd_w) + sol[p][:, hd_w:] for hd, p in zip(heads, ps)]
                yst = [_pick_own_seq(pr[hd][c:], hd_w) for hd in heads]
            else:
                u = [pr[hd][:c] + sol[p][:, hd_w:] for hd, p in zip(heads, ps)]
                yst = [pr[hd][c:] for hd in heads]
            vu = [jnp.concatenate([vh[p], u[hd]], axis=0) for hd, p in zip(heads, ps)]
            ya = [_mm(a_r[p], vu[hd].astype(BF16)) for hd, p in zip(heads, ps)]
            if per_seq:
                upd = [_mm_tn(jnp.where(big_mask, jnp.concatenate([vu[hd]] * SEQ_PER_CHUNK, axis=1), 0.0
                                        ).astype(BF16), kb[p]) for hd, p in zip(heads, ps)]
            else:
                upd = [_mm_tn(vu[hd].astype(BF16), kb[p]) for hd, p in zip(heads, ps)]
            for hd in heads:
                if per_seq:
                    wcb = jnp.concatenate(
                        [jnp.broadcast_to(wc[SUBLANES * s:SUBLANES * s + 1, sls[hd]], (hd_w, hd_w))
                         for s in range(SEQ_PER_CHUNK)], axis=0)
                    so_ref[seqs, hd] = (st[hd] * wcb + upd[hd]).reshape(SEQ_PER_CHUNK, hd_w, hd_w)
                else:
                    st_s[hd] = st[hd] * wc[:, sls[hd]] + upd[hd]
                y_s[rows, sls[hd]] = yst[hd] + ya[hd]
        return carry

    lax.fori_loop(0, tm // (c * group), chunk_group, 0)

    if not per_seq:
        so_ref[0] = st_s[...]

    y = y_s[...]
    mean = _seg64_sum(y) * (1.0 / hd_w)
    yc = y - mean
    var = _seg64_sum(yc * yc) * (1.0 / hd_w)
    yn = yc * lax.rsqrt(var + RW_GN_EPS) * lnw_ref[...] + lnb_ref[...]
    bonus = _seg64_sum(r_s[...] * k_s[...] * rk_ref[...]) * v_s[...]
    y_ref[...] = ((yn + bonus) * gate).astype(BF16)


def _rwkv_param_specs():
    row = _const_spec((1, RW_WIDTH))
    return [_const_spec((1, RW_PCOLS)), row, _const_spec((LANES, RW_WIDTH)), row, _const_spec((LANES, RW_WIDTH)),
            _const_spec((RW_GATE, RW_WIDTH)), row, row, row, row, row]


def _rwkv_scratch(tm):
    return [pltpu.VMEM((tm, RW_WIDTH), F32) for _ in range(7)]


def _rwkv_prompt(proj, rp, nb, seq, tm):
    nt = seq // tm
    cb = P_RW // RW_PCOLS
    y, st = pl.pallas_call(
        functools.partial(_rwkv_kernel, per_seq=False, tm=tm),
        grid=(nb, nt),
        in_specs=[pl.BlockSpec((tm, RW_PCOLS), lambda b, t: (b * nt + t, cb))] + _rwkv_param_specs(),
        out_specs=[pl.BlockSpec((tm, RW_WIDTH), lambda b, t: (b * nt + t, 0)),
                   pl.BlockSpec((1, RW_HEADS, RW_HD, RW_HD), lambda b, t: (b, 0, 0, 0))],
        out_shape=[jax.ShapeDtypeStruct((nb * seq, RW_WIDTH), BF16),
                   jax.ShapeDtypeStruct((nb, RW_HEADS, RW_HD, RW_HD), F32)],
        scratch_shapes=_rwkv_scratch(tm) + [pltpu.VMEM((SUBLANES, RW_PCOLS), F32),
                                            pltpu.VMEM((RW_HEADS, RW_HD, RW_HD), F32)],
        compiler_params=_cparams(("parallel", "arbitrary")),
        name="rwkv_prompt",
    )(proj, *rp)
    return y, st


def _stacked_out(prev):
    if prev is None:
        return (), [], {}
    return (prev,), [pl.BlockSpec(memory_space=pl.ANY)], {0: 1}


def _rwkv_sample(proj, shift0, st0, layer, prev_st, rp, row0, nseq, tm):
    ns = tm // SUBLANES
    cb = P_RW // RW_PCOLS
    r0 = row0 // tm
    st_io = _layer_spec((ns, RW_HEADS, RW_HD, RW_HD), lambda i: (i, 0, 0, 0), layer)
    extra, extra_specs, aliases = _stacked_out(prev_st)
    y, st = pl.pallas_call(
        functools.partial(_rwkv_kernel, per_seq=True, tm=tm, aliased=prev_st is not None),
        grid=(nseq // ns,),
        in_specs=extra_specs + [pl.BlockSpec((tm, RW_PCOLS), lambda i: (r0 + i, cb)),
                                pl.BlockSpec((ns, 1, RW_PCOLS), lambda i: (i, 0, 0)), st_io] + _rwkv_param_specs(),
        out_specs=[pl.BlockSpec((tm, RW_WIDTH), lambda i: (i, 0)), st_io],
        out_shape=[jax.ShapeDtypeStruct((nseq * SUBLANES, RW_WIDTH), BF16),
                   jax.ShapeDtypeStruct(st0.shape, F32)],
        input_output_aliases=aliases,
        scratch_shapes=_rwkv_scratch(tm),
        compiler_params=_cparams(("parallel",)),
        name="rwkv_sample",
    )(*extra, proj, shift0, st0, *rp)
    return y, st


def _rw_pad_cols(x):
    z = jnp.zeros(x.shape[:-1] + (LANES - RW_LORA,), x.dtype)
    a, b = 3 * RW_WIDTH, 3 * RW_WIDTH + RW_LORA
    return jnp.concatenate([x[..., :a], x[..., a:b], z, x[..., b:b + RW_LORA], z, x[..., b + RW_LORA:]], axis=-1)


def _rw_unpad_cols(x):
    a = 3 * RW_WIDTH
    return jnp.concatenate([x[..., :a + RW_LORA], x[..., a + LANES:a + LANES + RW_LORA], x[..., a + 2 * LANES:]],
                           axis=-1)


def _rwkv_params(mu, w0, w2, a0, a2, g2, k_k, k_a, r_k, ln_w, ln_b):
    row = lambda t: t.reshape(1, RW_WIDTH)
    padr = lambda t: jnp.concatenate([t, jnp.zeros((LANES - RW_LORA, RW_WIDTH), t.dtype)], axis=0).astype(BF16)
    return (_rw_pad_cols(mu).reshape(1, RW_PCOLS), row(w0), padr(w2), row(a0), padr(a2), g2.astype(BF16),
            row(k_k), row(k_a), row(r_k), row(ln_w), row(ln_b))


def _gdn_kernel(*refs, per_seq, tm, aliased=False):
    if aliased:
        refs = refs[1:]
    if per_seq:
        (q_ref, k_ref, v_ref, z_ref, ba_ref, fq_ref, fk_ref, fv_ref, si_ref, cwq_ref, cwk_ref, cwv_ref,
         al_ref, dt_ref, ng_ref, y_ref, so_ref, q_s, k_s, v_s, ba_s, o_s) = refs
        conv_state = (fq_ref, fk_ref, fv_ref)
    else:
        (q_ref, k_ref, v_ref, z_ref, ba_ref, cwq_ref, cwk_ref, cwv_ref, al_ref, dt_ref, ng_ref,
         y_ref, so_ref, q_s, k_s, v_s, ba_s, o_s, pq_s, pk_s, pv_s, st_s) = refs
        prevs = (pq_s, pk_s, pv_s)

        @pl.when(pl.program_id(1) == 0)
        def _():
            for p in prevs:
                p[...] = jnp.zeros_like(p)
            st_s[...] = jnp.zeros_like(st_s)
    c, hw = CHUNK, GD_HD

    def conv_act(idx, u_ref, cw_ref):
        u = u_ref[...]
        if per_seq:
            f = conv_state[idx][...]
            sh = [_shift_seq8(u, d, f) for d in (1, 2, 3)]
        else:
            p = prevs[idx][...]
            sh = [_shift_chain(u, d, p) for d in (1, 2, 3)]
            prevs[idx][...] = u[tm - SUBLANES:, :]
        x = cw_ref[3:4, :] * u + cw_ref[2:3, :] * sh[0] + cw_ref[1:2, :] * sh[1] + cw_ref[0:1, :] * sh[2]
        return x * _sigmoid(x)

    def l2n(x):
        outs = []
        for hd in range(GD_HEADS):
            xh = x[:, hw * hd:hw * (hd + 1)]
            outs.append(xh * lax.rsqrt(jnp.sum(xh * xh, axis=-1, keepdims=True) + 1e-6))
        return jnp.concatenate(outs, axis=1)

    q_s[...] = l2n(conv_act(0, q_ref, cwq_ref)) * (hw ** -0.5)
    k_s[...] = l2n(conv_act(1, k_ref, cwk_ref))
    v_s[...] = conv_act(2, v_ref, cwv_ref)
    ba = ba_ref[...]
    lane = lax.broadcasted_iota(jnp.int32, ba.shape, 1)
    g_all = -jnp.exp(al_ref[...]) * _softplus(ba + dt_ref[...])
    ba_s[...] = jnp.where(lane < GD_HEADS, _sigmoid(ba), g_all)

    strict, causal = _chunk_masks(per_seq)
    if per_seq:
        big_mask = _seq_block_mask(c, hw)

    heads = range(GD_HEADS)
    sls = [slice(hw * hd, hw * (hd + 1)) for hd in heads]
    group = _chunks_per_iteration(tm // c)

    def chunk_group(gidx, carry):
        pre = []
        for t in range(group):
            idx = gidx * group + t
            rows = pl.ds(pl.multiple_of(idx * c, c), c)
            bg = ba_s[rows, :]
            gc_all = _cumsum_rows(bg, groupwise=per_seq)
            pre.append(dict(rows=rows, seqs=pl.ds(idx * SEQ_PER_CHUNK, SEQ_PER_CHUNK), bg=bg, gc_all=gc_all,
                            gc_t=gc_all.T, gl_all=_group_last(gc_all) if per_seq else gc_all[c - 1:c, :],
                            qa=q_s[rows, :], ka=k_s[rows, :], va=v_s[rows, :]))
        pairs = [(t, hd) for t in range(group) for hd in heads]
        npair = len(pairs)
        beta = [pre[t]['bg'][:, hd:hd + 1] for t, hd in pairs]
        gc = [pre[t]['gc_all'][:, GD_HEADS + hd:GD_HEADS + hd + 1] for t, hd in pairs]
        gl = [pre[t]['gl_all'][:, GD_HEADS + hd:GD_HEADS + hd + 1] for t, hd in pairs]
        dec = [jnp.exp(jnp.where(causal, gc[p] - pre[t]['gc_t'][GD_HEADS + hd:GD_HEADS + hd + 1, :], -1e30))
               for p, (t, hd) in enumerate(pairs)]
        qh = [pre[t]['qa'][:, sls[hd]] for t, hd in pairs]
        kh = [pre[t]['ka'][:, sls[hd]] for t, hd in pairs]
        kb = [kh[p] * beta[p] for p in range(npair)]
        eg = [jnp.exp(g) for g in gc]
        kq = [_mm_nt(jnp.concatenate([kb[p], qh[p]], axis=0).astype(BF16), kh[p].astype(BF16))
              for p in range(npair)]
        low = [jnp.where(strict, kq[p][:c] * dec[p], 0.0) for p in range(npair)]
        intra = [(kq[p][c:] * dec[p]).astype(BF16) for p in range(npair)]
        sol = _solve_unit_lower_each(
            low, [jnp.concatenate([pre[t]['va'][:, sls[hd]] * beta[p], kb[p] * eg[p]], axis=1)
                  for p, (t, hd) in enumerate(pairs)], per_seq)
        wq = [jnp.concatenate([sol[p][:, hw:], qh[p] * eg[p]], axis=0).astype(BF16) for p in range(npair)]
        kdec = [kh[p] * jnp.exp(gl[p] - gc[p]) for p in range(npair)]
        for t in range(group):
            rows, seqs = pre[t]['rows'], pre[t]['seqs']
            ps = [t * GD_HEADS + hd for hd in heads]
            if per_seq:
                st3 = [si_ref[seqs, hd] for hd in heads]
                st_rhs = [jnp.concatenate([s3[s] for s in range(SEQ_PER_CHUNK)], axis=1).astype(BF16)
                          for s3 in st3]
            else:
                st = [st_s[hd] for hd in heads]
                st_rhs = [s.astype(BF16) for s in st]
            ws = [_mm(wq[p], st_rhs[hd]) for hd, p in zip(heads, ps)]
            if per_seq:
                v_new = [sol[p][:, :hw] - _pick_own_seq(ws[hd][:c], hw) for hd, p in zip(heads, ps)]
                o_st = [_pick_own_seq(ws[hd][c:], hw) for hd in heads]
            else:
                v_new = [sol[p][:, :hw] - ws[hd][:c] for hd, p in zip(heads, ps)]
                o_st = [ws[hd][c:] for hd in heads]
            vnb = [v.astype(BF16) for v in v_new]
            o_in = [_mm(intra[p], vnb[hd]) for hd, p in zip(heads, ps)]
            if per_seq:
                upd = [_mm_tn(jnp.where(big_mask, jnp.concatenate([kdec[p]] * SEQ_PER_CHUNK, axis=1), 0.0
                                        ).astype(BF16), vnb[hd]) for hd, p in zip(heads, ps)]
            else:
                upd = [_mm_tn(kdec[p].astype(BF16), vnb[hd]) for hd, p in zip(heads, ps)]
            for hd, p in zip(heads, ps):
                if per_seq:
                    egl = jnp.exp(gl[p])
                    mult = jnp.concatenate([jnp.broadcast_to(egl[SUBLANES * s:SUBLANES * s + 1, :], (hw, hw))
                                            for s in range(SEQ_PER_CHUNK)], axis=0)
                    st_new = st3[hd].reshape(SEQ_PER_CHUNK * hw, hw) * mult + upd[hd]
                    so_ref[seqs, hd] = st_new.reshape(SEQ_PER_CHUNK, hw, hw)
                else:
                    st_s[hd] = st[hd] * jnp.exp(gl[p]) + upd[hd]
                o_s[rows, sls[hd]] = o_st[hd] + o_in[hd]
        return carry

    lax.fori_loop(0, tm // (c * group), chunk_group, 0)

    if not per_seq:
        so_ref[0] = st_s[...]

    outs = []
    for hd in range(GD_HEADS):
        sl = slice(hw * hd, hw * (hd + 1))
        o = o_s[:, sl]
        o = o * lax.rsqrt(jnp.mean(o * o, axis=-1, keepdims=True) + NORM_EPS) * ng_ref[...]
        z = z_ref[:, sl]
        outs.append(o * (z * _sigmoid(z)))
    y_ref[...] = jnp.concatenate(outs, axis=1).astype(BF16)


def _gdn_param_specs():
    c2 = lambda shape, j=0: pl.BlockSpec(shape, (lambda *a: (0, j)))
    return [c2((GD_CONV, GD_WIDTH), 0), c2((GD_CONV, GD_WIDTH), 1), c2((GD_CONV, GD_WIDTH), 2),
            c2((1, LANES)), c2((1, LANES)), c2((1, GD_HD))]


def _gdn_scratch(tm):
    return ([pltpu.VMEM((tm, GD_WIDTH), F32) for _ in range(3)] + [pltpu.VMEM((tm, LANES), F32),
                                                                  pltpu.VMEM((tm, GD_WIDTH), F32)])


def _gdn_prompt(proj, proj_sb, gp, nb, seq, tm):
    nt = seq // tm
    qb = P_Q // GD_WIDTH
    blk = lambda j: pl.BlockSpec((tm, GD_WIDTH), lambda b, t: (b * nt + t, j))
    y, st = pl.pallas_call(
        functools.partial(_gdn_kernel, per_seq=False, tm=tm),
        grid=(nb, nt),
        in_specs=[blk(qb), blk(qb + 1), blk(qb + 2), blk(P_Z // GD_WIDTH),
                  pl.BlockSpec((tm, LANES), lambda b, t: (b * nt + t, P_BA // LANES))] + _gdn_param_specs(),
        out_specs=[pl.BlockSpec((tm, GD_WIDTH), lambda b, t: (b * nt + t, 0)),
                   pl.BlockSpec((1, GD_HEADS, GD_HD, GD_HD), lambda b, t: (b, 0, 0, 0))],
        out_shape=[jax.ShapeDtypeStruct((nb * seq, GD_WIDTH), BF16),
                   jax.ShapeDtypeStruct((nb, GD_HEADS, GD_HD, GD_HD), F32)],
        scratch_shapes=_gdn_scratch(tm) + [pltpu.VMEM((SUBLANES, GD_WIDTH), F32) for _ in range(3)]
        + [pltpu.VMEM((GD_HEADS, GD_HD, GD_HD), F32)],
        compiler_params=_cparams(("parallel", "arbitrary")),
        name="gdn_prompt",
    )(proj, proj, proj, proj, proj_sb, gp[0], gp[0], gp[0], *gp[1:])
    return y, st


def _gdn_sample(proj, proj_sb, conv0, st0, layer, prev_st, gp, row0, nseq, tm):
    ns = tm // SUBLANES
    qb = P_Q // GD_WIDTH
    r0 = row0 // tm
    blk = lambda j: pl.BlockSpec((tm, GD_WIDTH), lambda i: (r0 + i, j))
    frm = lambda j: _layer_spec((ns, GD_CONV - 1, GD_WIDTH), lambda i: (i, 0, j), layer)
    st_io = _layer_spec((ns, GD_HEADS, GD_HD, GD_HD), lambda i: (i, 0, 0, 0), layer)
    extra, extra_specs, aliases = _stacked_out(prev_st)
    y, st = pl.pallas_call(
        functools.partial(_gdn_kernel, per_seq=True, tm=tm, aliased=prev_st is not None),
        grid=(nseq // ns,),
        in_specs=extra_specs + [blk(qb), blk(qb + 1), blk(qb + 2), blk(P_Z // GD_WIDTH),
                                pl.BlockSpec((tm, LANES), lambda i: (r0 + i, P_BA // LANES)),
                                frm(0), frm(1), frm(2), st_io] + _gdn_param_specs(),
        out_specs=[pl.BlockSpec((tm, GD_WIDTH), lambda i: (i, 0)), st_io],
        out_shape=[jax.ShapeDtypeStruct((nseq * SUBLANES, GD_WIDTH), BF16),
                   jax.ShapeDtypeStruct(st0.shape, F32)],
        input_output_aliases=aliases,
        scratch_shapes=_gdn_scratch(tm),
        compiler_params=_cparams(("parallel",)),
        name="gdn_sample",
    )(*extra, proj, proj, proj, proj, proj_sb, conv0, conv0, conv0, st0, gp[0], gp[0], gp[0], *gp[1:])
    return y, st


def _gdn_params(conv_w, a_log, dt_bias, norm_g):
    pad = lambda t: jnp.concatenate([jnp.zeros((GD_HEADS,), F32), t, jnp.zeros((LANES - 2 * GD_HEADS,), F32)]
                                    ).reshape(1, LANES)
    return (conv_w, pad(a_log), pad(dt_bias), norm_g.reshape(1, GD_HD))


W_IN_RW = S5_WIDTH
W_IN_QKVZ = W_IN_RW + RW_COLS
W_IN_BA = W_IN_QKVZ + 4 * GD_WIDTH
W_IN_GATES = W_IN_BA + 2 * GD_HEADS


def _split_w_in(w):
    w_t = jnp.swapaxes(w, 1, 2)
    zeros = lambda rows: jnp.zeros((w_t.shape[0], rows, w_t.shape[2]), w_t.dtype)
    a = W_IN_RW + 3 * RW_WIDTH + RW_LORA
    w_rw_t = jnp.concatenate([w_t[:, W_IN_RW:a], zeros(LANES - RW_LORA), w_t[:, a:a + RW_LORA],
                              zeros(LANES - RW_LORA), w_t[:, a + RW_LORA:W_IN_QKVZ]], axis=1)
    w_sb_t = jnp.concatenate([w_t[:, :S5_WIDTH], w_t[:, W_IN_BA:W_IN_GATES],
                              zeros(P_SB_COLS - P_BA - 2 * GD_HEADS)], axis=1)
    return w_t, w_rw_t, w_sb_t


def _tail_rows(a, cols, nrows, mp, seq):
    a3 = a.reshape(a.shape[0] // SUBLANES, SUBLANES, a.shape[1])
    g = seq // SUBLANES
    lo = SUBLANES - nrows
    return (a3[g - 1:mp // SUBLANES:g, lo:, cols[0]:cols[1]], a3[mp // SUBLANES:, lo:, cols[0]:cols[1]])


def kernel(x_prompt, x_sample, state_s5_re, state_s5_im, state_rwkv_shift, state_rwkv_wkv, state_gdn_conv, state_gdn, state_ffn_conv, norm1_g, norm2_g, final_norm_g, w_in, s5_lambda_re, s5_lambda_im, s5_b_re, s5_b_im, s5_c_re, s5_c_im, s5_d, s5_log_step, s5_w_glu, rwkv_mu, rwkv_w0, rwkv_w2, rwkv_a0, rwkv_a2, rwkv_g2, rwkv_k_k, rwkv_k_a, rwkv_r_k, rwkv_ln_w, rwkv_ln_b, gdn_conv_w, gdn_a_log, gdn_dt_bias, gdn_norm_g, w_br_s5, w_br_rwkv, w_br_gdn, w_out, ffn_w_up, ffn_conv_w, ffn_conv_b, ffn_w_down):
    nb, seq, d = x_prompt.shape
    ns, sl, _ = x_sample.shape
    assert sl == SUBLANES and d == D_MODEL
    mp, ms = nb * seq, ns * sl
    m = mp + ms
    tm_norm, tm_proj, tm_out = 512, 3072, 512
    x_groups = (x_prompt.reshape(mp, d), x_sample.reshape(ms, d), 0)

    w_t, w_rw_t, w_sb_t = _split_w_in(w_in)
    w_br = (w_br_s5.astype(BF16), w_br_rwkv.astype(BF16), w_br_gdn.astype(BF16))
    w_out_b, w_down_b = w_out.astype(BF16), ffn_w_down.astype(BF16)
    conv_b = ffn_conv_b.reshape(DEPTH, 1, 2 * D_FF)
    shift0 = _rw_pad_cols(state_rwkv_shift)[:, :, None, :]

    new_p = [[] for _ in range(7)]
    new_s = [[] for _ in range(7)]
    s_wkv = s_gdn = None
    for l in range(DEPTH):
        xp, xs, s_off = x_groups
        xn1 = _norm_cast(xp, xs, s_off * (mp // tm_norm), m, mp // tm_norm, norm1_g[l], tm=tm_norm)
        proj_g = _matmul_nt(xn1, w_t, l, W_IN_GATES, 3 * D_MODEL, tm=tm_proj, tn=512)
        proj_rw = _matmul_nt(xn1, w_rw_t, l, 0, RW_PCOLS, tm=tm_proj, tn=512)
        proj_qkvz = _matmul_nt(xn1, w_t, l, W_IN_QKVZ, 4 * GD_WIDTH, tm=tm_proj, tn=512)
        proj_sb = _matmul_nt(xn1, w_sb_t, l, 0, P_SB_COLS, tm=tm_proj, tn=512)

        sp = _s5_params(s5_lambda_re[l], s5_lambda_im[l], s5_b_re[l], s5_b_im[l], s5_c_re[l], s5_c_im[l],
                        s5_d[l], s5_log_step[l], s5_w_glu[l])
        ys5_p, p_re, p_im = _s5_prompt(proj_sb, sp, nb, seq, tm=256)
        ys5_s, s_re, s_im = _s5_sample(proj_sb, state_s5_re[l], state_s5_im[l], sp, mp, ns, tm=128)

        rp = _rwkv_params(rwkv_mu[l], rwkv_w0[l], rwkv_w2[l], rwkv_a0[l], rwkv_a2[l], rwkv_g2[l], rwkv_k_k[l],
                          rwkv_k_a[l], rwkv_r_k[l].reshape(RW_WIDTH), rwkv_ln_w[l], rwkv_ln_b[l])
        yrw_p, p_wkv = _rwkv_prompt(proj_rw, rp, nb, seq, tm=256)
        yrw_s, s_wkv = _rwkv_sample(proj_rw, shift0[l], state_rwkv_wkv, l, s_wkv, rp, mp, ns, tm=128)
        p_shift, s_shift = [_rw_unpad_cols(t[:, 0])
                            for t in _tail_rows(proj_rw, (P_RW, P_RW + RW_PCOLS), 1, mp, seq)]

        gp = _gdn_params(gdn_conv_w[l], gdn_a_log[l], gdn_dt_bias[l], gdn_norm_g[l])
        ygd_p, p_gdn = _gdn_prompt(proj_qkvz, proj_sb, gp, nb, seq, tm=256)
        ygd_s, s_gdn = _gdn_sample(proj_qkvz, proj_sb, state_gdn_conv, state_gdn, l, s_gdn, gp, mp, ns, tm=64)
        p_gconv, s_gconv = _tail_rows(proj_qkvz, (P_Q, P_Q + 3 * GD_WIDTH), GD_CONV - 1, mp, seq)

        merged = _merge((ys5_p, yrw_p, ygd_p), (ys5_s, yrw_s, ygd_s), w_br, l, proj_g, tm=512, tn=1024)
        x, xn = _out_norm(merged, w_out_b, l, xp, xs, s_off * (mp // tm_out), mp // tm_out, norm2_g[l], tm=tm_out)

        h, p_ffn, s_ffn = _ffn_up(xn, ffn_w_up, ffn_conv_w, conv_b, state_ffn_conv, l, nb, seq, ns,
                                  tm=1024, tn=512)
        x = _matmul_res(h, w_down_b, l, x, tm=1536, tn=512)
        x_groups = (x, x, 1)

        for lst, vals in ((new_p, (p_re, p_im, p_shift, p_wkv, p_gconv, p_gdn, p_ffn)),
                          (new_s, (s_re, s_im, s_shift, None, s_gconv, None, s_ffn))):
            for acc, val in zip(lst, vals):
                acc.append(val)

    y_p, y_s = _final_norm(x, final_norm_g, mp, tm=512)
    stack = lambda lst: [jnp.stack(v) for v in lst]
    out_p, out_s = stack(new_p), stack([v for v in new_s if v[0] is not None])
    out_s = out_s[:3] + [s_wkv, out_s[3], s_gdn, out_s[4]]
    return (y_p.reshape(nb, seq, d), y_s.reshape(ns, sl, d)) + tuple(out_p) + tuple(out_s)
```

```python
import functools
import math

import jax
import jax.numpy as jnp
from jax import lax
from jax.experimental import pallas as pl
from jax.experimental.pallas import tpu as pltpu

F32 = jnp.float32
BF16 = jnp.bfloat16

SUBLANES = 8
LANES = 128
VMEM_LIMIT = 52 * 1024 * 1024
CHUNK = 64
SEQ_PER_CHUNK = CHUNK // SUBLANES

D_MODEL = 2048
DEPTH = 2
S5_WIDTH = 512
S5_GROUP = 16
S5_GROUPS = 32
S5_STATE = 64
S5_NSTATE = S5_GROUPS * S5_STATE
RW_WIDTH = 512
RW_HEADS = 8
RW_HD = 64
RW_LORA = 96
RW_GATE = 256
RW_COLS = 3 * RW_WIDTH + 2 * RW_LORA + RW_GATE
RW_PCOLS = 2048
RW_GN_EPS = 64e-5
GD_WIDTH = 1024
GD_HEADS = 8
GD_HD = 128
GD_CONV = 4
D_FF = 5632
FFN_CONV = 3
NORM_EPS = 1e-6
P_GATES, P_RW, P_Q, P_Z, P_S5, P_BA, P_SB_COLS = 0, 0, 0, 3072, 0, 512, 1024


def _cparams(sem):
    return pltpu.CompilerParams(dimension_semantics=sem, vmem_limit_bytes=VMEM_LIMIT)


def _mm(a, b):
    return jnp.dot(a, b, preferred_element_type=F32)


def _mm_nt(a, b):
    return lax.dot_general(a, b, (((1,), (1,)), ((), ())), preferred_element_type=F32)


def _mm_tn(a, b):
    return lax.dot_general(a, b, (((0,), (0,)), ((), ())), preferred_element_type=F32)


def _sigmoid(x):
    return 1.0 / (1.0 + jnp.exp(-x))


def _softplus(x):
    return jnp.maximum(x, 0.0) + jnp.log1p(jnp.exp(-jnp.abs(x)))


def _gelu_tanh(x):
    return 0.5 * x * (1.0 + jnp.tanh(math.sqrt(2.0 / math.pi) * (x + 0.044715 * (x * x * x))))


def _row_iota(shape):
    return lax.broadcasted_iota(jnp.int32, shape, 0)


def _shift_chain(x, d, prev8):
    n, ch = x.shape
    x3 = x.reshape(n // SUBLANES, SUBLANES, ch)
    rot = pltpu.roll(x3, d, 1)
    before = jnp.concatenate([pltpu.roll(prev8, d, 0)[None], rot[:-1]], axis=0)
    row = lax.broadcasted_iota(jnp.int32, x3.shape, 1)
    return jnp.where(row < d, before, rot).reshape(n, ch)


def _shift_seq8(x, d, state):
    n, ch = x.shape
    ns, w1, _ = state.shape
    out = pltpu.roll(x.reshape(ns, SUBLANES, ch), d, 1)
    row = lax.broadcasted_iota(jnp.int32, out.shape, 1)
    for r in range(d):
        out = jnp.where(row == r, state[:, w1 - d + r:w1 - d + r + 1, :], out)
    return out.reshape(n, ch)


def _group_last(x):
    c, n = x.shape
    x3 = x.reshape(c // SUBLANES, SUBLANES, n)
    return jnp.broadcast_to(x3[:, SUBLANES - 1:SUBLANES, :], x3.shape).reshape(c, n)


def _cumsum_rows(x, groupwise=False):
    c = x.shape[0]
    row = _row_iota(x.shape) % SUBLANES
    for d in (1, 2, 4):
        x = x + jnp.where(row >= d, pltpu.roll(x, d, 0), 0.0)
    if c > SUBLANES and not groupwise:
        blocks = [x[SUBLANES * i:SUBLANES * (i + 1)] for i in range(c // SUBLANES)]
        for i in range(1, len(blocks)):
            blocks[i] = blocks[i] + blocks[i - 1][SUBLANES - 1:SUBLANES, :]
        x = jnp.concatenate(blocks, axis=0)
    return x


def _split2(x):
    hi = x.astype(BF16)
    return hi, (x - hi.astype(F32)).astype(BF16)


def _hi_lo_f32(x):
    hi = x.astype(BF16).astype(F32)
    return hi, x - hi


def _mm_hilo_each(a_list, b_list):
    lhs, rhs = [], []
    for a, b in zip(a_list, b_list):
        ah, al = _hi_lo_f32(a)
        bh, bl = _hi_lo_f32(b)
        lhs.append(jnp.concatenate([ah, ah, al], axis=1).astype(BF16))
        rhs.append(jnp.concatenate([bh, bl, bh], axis=0).astype(BF16))
    return [_mm(l, r) for l, r in zip(lhs, rhs)]


def _solve_unit_lower_each(lows, rhss, diag_only):
    c = lows[0].shape[0]
    nb = c // SUBLANES
    nh = len(lows)
    n = rhss[0].shape[1]
    blk = lambda a, i: a[SUBLANES * i:SUBLANES * (i + 1)]
    rb = [[blk(r, i) for i in range(nb)] for r in rhss]
    with_off = nb > 1 and not diag_only
    same = (lax.broadcasted_iota(jnp.int32, (c, c), 0) // SUBLANES
            == lax.broadcasted_iota(jnp.int32, (c, c), 1) // SUBLANES)
    if with_off:
        ob = [[blk(jnp.where(same, 0.0, low), i) for i in range(nb)] for low in lows]
    sel = (lax.broadcasted_iota(jnp.int32, (c, (SUBLANES - 1) * LANES), 0) % SUBLANES
           == lax.broadcasted_iota(jnp.int32, (c, (SUBLANES - 1) * LANES), 1) // LANES)
    sel = jnp.where(sel, 1.0, 0.0).astype(BF16)
    sel2 = jnp.concatenate([sel, sel], axis=0)
    diag = [_hi_lo_f32(jnp.where(same, low, 0.0)) if not diag_only else _hi_lo_f32(low) for low in lows]
    cols = [_mm(jnp.concatenate(d, axis=1).astype(BF16), sel2) for d in diag]
    reps = n // LANES
    for j in range(SUBLANES - 1):
        for h in range(nh):
            cj = cols[h][:, LANES * j:LANES * (j + 1)]
            cj_n = jnp.concatenate([cj] * reps, axis=1) if reps > 1 else cj
            for i in range(nb):
                rb[h][i] = rb[h][i] - blk(cj_n, i) * rb[h][i][j:j + 1, :]
                if with_off and i > 0:
                    ob[h][i] = ob[h][i] - blk(cj, i)[:, :c] * ob[h][i][j:j + 1, :]
    r1 = [jnp.concatenate(b, axis=0) if nb > 1 else b[0] for b in rb]
    if not with_off:
        return r1
    n1 = [jnp.concatenate(b, axis=0) for b in ob]
    n2 = _mm_hilo_each(n1, n1)
    t = _mm_hilo_each(n1 + n2, r1 + n2)
    y = [r - nr for r, nr in zip(r1, t[:nh])]
    n4 = t[nh:]
    y = [a + b for a, b in zip(y, _mm_hilo_each(n2, y))]
    return [a + b for a, b in zip(y, _mm_hilo_each(n4, y))]


def _chunks_per_iteration(n_chunks):
    return 2 if n_chunks % 2 == 0 else 1


def _chunk_masks(per_seq):
    ri = lax.broadcasted_iota(jnp.int32, (CHUNK, CHUNK), 0)
    ci = lax.broadcasted_iota(jnp.int32, (CHUNK, CHUNK), 1)
    strict, causal = ri > ci, ri >= ci
    if per_seq:
        same = (ri // SUBLANES) == (ci // SUBLANES)
        strict, causal = jnp.logical_and(strict, same), jnp.logical_and(causal, same)
    return strict, causal


def _seq_block_mask(rows, width):
    r = lax.broadcasted_iota(jnp.int32, (rows, SEQ_PER_CHUNK * width), 0)
    l = lax.broadcasted_iota(jnp.int32, (rows, SEQ_PER_CHUNK * width), 1)
    return ((r % CHUNK) // SUBLANES) == (l // width)


def _pick_own_seq(a, width):
    return jnp.concatenate([a[SUBLANES * s:SUBLANES * (s + 1), width * s:width * (s + 1)]
                            for s in range(SEQ_PER_CHUNK)], axis=0)


def _rmsnorm_rows(x_ref, g_ref, o_ref, rows, chunk=128):
    def body(i, carry):
        r0 = pl.multiple_of(i * chunk, chunk)
        x = x_ref[pl.ds(r0, chunk), :]
        ms = jnp.mean(x * x, axis=-1, keepdims=True)
        o_ref[pl.ds(r0, chunk), :] = (x * lax.rsqrt(ms + NORM_EPS) * g_ref[...]).astype(o_ref.dtype)
        return carry
    lax.fori_loop(0, rows // chunk, body, 0)


def _two_group_specs(tm, width, prompt_tiles, sample_off, pipeline_mode=None):
    p = pl.BlockSpec((tm, width), lambda i, *_: (jnp.minimum(i, prompt_tiles - 1), 0), pipeline_mode=pipeline_mode)
    s = pl.BlockSpec((tm, width), lambda i, *_: (jnp.maximum(i - prompt_tiles, 0) + sample_off, 0),
                     pipeline_mode=pipeline_mode)
    return p, s


def _layer_spec(shape, index_map, layer):
    return pl.BlockSpec((None,) + shape, lambda *a: (layer,) + tuple(index_map(*a)))


def _norm_cast_kernel(xp_ref, xs_ref, g_ref, o_ref, *, rows, prompt_tiles):
    i = pl.program_id(0)

    @pl.when(i < prompt_tiles)
    def _():
        _rmsnorm_rows(xp_ref, g_ref, o_ref, rows)

    @pl.when(i >= prompt_tiles)
    def _():
        _rmsnorm_rows(xs_ref, g_ref, o_ref, rows)


def _norm_cast(xp, xs, sample_off, m, prompt_tiles, g, tm):
    k = xp.shape[1]
    psp, ssp = _two_group_specs(tm, k, prompt_tiles, sample_off)
    return pl.pallas_call(
        functools.partial(_norm_cast_kernel, rows=tm, prompt_tiles=prompt_tiles),
        grid=(m // tm,),
        in_specs=[psp, ssp, pl.BlockSpec((1, k), lambda i: (0, 0))],
        out_specs=pl.BlockSpec((tm, k), lambda i: (i, 0)),
        out_shape=jax.ShapeDtypeStruct((m, k), BF16),
        compiler_params=_cparams(("arbitrary",)),
        name="norm_cast",
    )(xp, xs, g.reshape(1, k))


def _matmul_nt_kernel(a_ref, w_ref, o_ref):
    o_ref[...] = _mm_nt(a_ref[...], w_ref[0].astype(BF16))


def _matmul_nt(a, w_t, layer, row0, n, tm, tn):
    m, k = a.shape
    assert row0 % SUBLANES == 0 and n % tn == 0
    return pl.pallas_call(
        _matmul_nt_kernel,
        grid=(m // tm, n // tn),
        in_specs=[pl.BlockSpec((tm, k), lambda i, j: (i, 0), pipeline_mode=pl.Buffered(1)),
                  pl.BlockSpec((pl.Element(1), pl.Element(tn), pl.Element(k)),
                               lambda i, j: (layer, pl.multiple_of(row0 + j * tn, SUBLANES), 0))],
        out_specs=pl.BlockSpec((tm, tn), lambda i, j: (i, j)),
        out_shape=jax.ShapeDtypeStruct((m, n), F32),
        compiler_params=_cparams(("parallel", "arbitrary")),
        name="proj_matmul",
    )(a, w_t)


def _matmul_res_kernel(a_ref, w_ref, r_ref, o_ref):
    o_ref[...] = r_ref[...] + _mm(a_ref[...], w_ref[...])


def _matmul_res(a, w, layer, res, tm, tn):
    m, k = a.shape
    n = w.shape[-1]
    return pl.pallas_call(
        _matmul_res_kernel,
        grid=(m // tm, n // tn),
        in_specs=[pl.BlockSpec((tm, k), lambda i, j: (i, 0), pipeline_mode=pl.Buffered(1)),
                  _layer_spec((k, tn), lambda i, j: (0, j), layer),
                  pl.BlockSpec((tm, tn), lambda i, j: (i, j))],
        out_specs=pl.BlockSpec((tm, tn), lambda i, j: (i, j)),
        out_shape=jax.ShapeDtypeStruct((m, n), F32),
        compiler_params=_cparams(("parallel", "arbitrary")),
        name="matmul_res",
    )(a, w, res)


def _out_norm_kernel(a_ref, w_ref, rp_ref, rs_ref, g_ref, x_ref, xn_ref, *, rows, prompt_tiles):
    res = jnp.where(pl.program_id(0) < prompt_tiles, rp_ref[...], rs_ref[...])
    x_ref[...] = res + _mm(a_ref[...], w_ref[...])
    _rmsnorm_rows(x_ref, g_ref, xn_ref, rows)


def _out_norm(a, w, layer, rp, rs, sample_off, prompt_tiles, g, tm):
    m, k = a.shape
    n = w.shape[-1]
    psp, ssp = _two_group_specs(tm, n, prompt_tiles, sample_off)
    return pl.pallas_call(
        functools.partial(_out_norm_kernel, rows=tm, prompt_tiles=prompt_tiles),
        grid=(m // tm,),
        in_specs=[pl.BlockSpec((tm, k), lambda i: (i, 0)),
                  _layer_spec((k, n), lambda i: (0, 0), layer),
                  psp, ssp, pl.BlockSpec((1, n), lambda i: (0, 0))],
        out_specs=[pl.BlockSpec((tm, n), lambda i: (i, 0)), pl.BlockSpec((tm, n), lambda i: (i, 0))],
        out_shape=[jax.ShapeDtypeStruct((m, n), F32), jax.ShapeDtypeStruct((m, n), BF16)],
        compiler_params=_cparams(("parallel",)),
        name="out_norm",
    )(a, w, rp, rs, g.reshape(1, n))


def _merge_kernel(y1p, y2p, y3p, y1s, y2s, y3s, w1_ref, w2_ref, w3_ref, g1_ref, g2_ref, g3_ref, o_ref, *,
                  prompt_tiles, tm, rc):
    is_p = pl.program_id(0) < prompt_tiles
    for r in range(tm // rc):
        rows = slice(r * rc, (r + 1) * rc)
        pick = lambda p, s: jnp.where(is_p, p[rows, :], s[rows, :])
        acc = _sigmoid(g1_ref[rows, :]) * _mm(pick(y1p, y1s), w1_ref[...])
        acc = acc + _sigmoid(g2_ref[rows, :]) * _mm(pick(y2p, y2s), w2_ref[...])
        acc = acc + _sigmoid(g3_ref[rows, :]) * _mm(pick(y3p, y3s), w3_ref[...])
        o_ref[rows, :] = acc.astype(BF16)


def _merge(ys_p, ys_s, ws, layer, proj, tm, tn):
    mp, ms = ys_p[0].shape[0], ys_s[0].shape[0]
    n = ws[0].shape[-1]
    nj = n // tn
    pt = mp // tm
    gate_spec = lambda b: pl.BlockSpec((tm, tn), lambda i, j: (i, P_GATES // tn + b * nj + j))
    y_specs = [_two_group_specs(tm, y.shape[1], pt, 0) for y in ys_p]
    w_spec = lambda w: _layer_spec((w.shape[1], tn), lambda i, j: (0, j), layer)
    return pl.pallas_call(
        functools.partial(_merge_kernel, prompt_tiles=pt, tm=tm, rc=128),
        grid=((mp + ms) // tm, nj),
        in_specs=[s[0] for s in y_specs] + [s[1] for s in y_specs] + [w_spec(w) for w in ws]
        + [gate_spec(0), gate_spec(1), gate_spec(2)],
        out_specs=pl.BlockSpec((tm, tn), lambda i, j: (i, j)),
        out_shape=jax.ShapeDtypeStruct((mp + ms, n), BF16),
        compiler_params=_cparams(("parallel", "arbitrary")),
        name="merge",
    )(*ys_p, *ys_s, *ws, proj, proj, proj)


def _final_norm_kernel(x_ref, g_ref, yp_ref, ys_ref, *, prompt_tiles):
    x = x_ref[...]
    ms = jnp.mean(x * x, axis=-1, keepdims=True)
    y = x * lax.rsqrt(ms + NORM_EPS) * g_ref[...]
    i = pl.program_id(0)

    @pl.when(i < prompt_tiles)
    def _():
        yp_ref[...] = y

    @pl.when(i >= prompt_tiles)
    def _():
        ys_ref[...] = y


def _final_norm(x, g, mp, tm):
    m, k = x.shape
    pt = mp // tm
    psp, ssp = _two_group_specs(tm, k, pt, 0)
    return pl.pallas_call(
        functools.partial(_final_norm_kernel, prompt_tiles=pt),
        grid=(m // tm,),
        in_specs=[pl.BlockSpec((tm, k), lambda i: (i, 0)), pl.BlockSpec((1, k), lambda i: (0, 0))],
        out_specs=[psp, ssp],
        out_shape=[jax.ShapeDtypeStruct((mp, k), F32), jax.ShapeDtypeStruct((m - mp, k), F32)],
        compiler_params=_cparams(("arbitrary",)),
        name="final_norm",
    )(x, g.reshape(1, k))


def _ffn_up_kernel(xn_ref, wg_ref, wv_ref, cwg_ref, cwv_ref, bg_ref, bv_ref, sg_ref, sv_ref,
                   h_ref, pg_ref, pv_ref, og_ref, ov_ref, wgb, wvb, cg, cv, *, tm, rc, prompt_tiles, tiles_per_seq):
    i = pl.program_id(1)
    w1 = FFN_CONV - 1

    @pl.when(i == 0)
    def _():
        wgb[...] = wg_ref[...].astype(BF16)
        wvb[...] = wv_ref[...].astype(BF16)

    def conv(u, s1, s2, cw_ref, b_ref):
        return cw_ref[2:3, :] * u + cw_ref[1:2, :] * s1 + cw_ref[0:1, :] * s2 + b_ref[...]

    def up(rows):
        x = xn_ref[rows, :]
        return _mm(x, wgb[...]), _mm(x, wvb[...])

    @pl.when(i < prompt_tiles)
    def _():
        first = (i % tiles_per_seq) == 0
        pg = jnp.where(first, 0.0, cg[...])
        pv = jnp.where(first, 0.0, cv[...])
        for r in range(tm // rc):
            rows = slice(r * rc, (r + 1) * rc)
            ug, uv = up(rows)
            gate = conv(ug, _shift_chain(ug, 1, pg), _shift_chain(ug, 2, pg), cwg_ref, bg_ref)
            val = conv(uv, _shift_chain(uv, 1, pv), _shift_chain(uv, 2, pv), cwv_ref, bv_ref)
            h_ref[rows, :] = (gate * _sigmoid(gate) * val).astype(BF16)
            pg, pv = ug[rc - SUBLANES:, :], uv[rc - SUBLANES:, :]
        cg[...] = pg
        cv[...] = pv
        pg_ref[0] = pg[SUBLANES - w1:, :]
        pv_ref[0] = pv[SUBLANES - w1:, :]

    @pl.when(i >= prompt_tiles)
    def _():
        nsr = rc // SUBLANES
        for r in range(tm // rc):
            rows = slice(r * rc, (r + 1) * rc)
            seqs = slice(r * nsr, (r + 1) * nsr)
            ug, uv = up(rows)
            sg, sv = sg_ref[seqs], sv_ref[seqs]
            gate = conv(ug, _shift_seq8(ug, 1, sg), _shift_seq8(ug, 2, sg), cwg_ref, bg_ref)
            val = conv(uv, _shift_seq8(uv, 1, sv), _shift_seq8(uv, 2, sv), cwv_ref, bv_ref)
            h_ref[rows, :] = (gate * _sigmoid(gate) * val).astype(BF16)
            og_ref[seqs] = ug.reshape(nsr, SUBLANES, ug.shape[1])[:, SUBLANES - w1:, :]
            ov_ref[seqs] = uv.reshape(nsr, SUBLANES, uv.shape[1])[:, SUBLANES - w1:, :]


def _ffn_up(xn, w_up, conv_w, conv_b, state, layer, nb, seq, ns, tm, tn):
    m, k = xn.shape
    mp = nb * seq
    assert ns * SUBLANES == tm and (m - mp) == tm and seq % tm == 0
    nj, pt, tps = D_FF // tn, mp // tm, seq // tm
    w1 = FFN_CONV - 1
    wsp = lambda off: _layer_spec((k, tn), lambda j, i: (0, off + j), layer)
    par = lambda rows, off: _layer_spec((rows, tn), lambda j, i: (0, off + j), layer)
    stsp = lambda off: _layer_spec((ns, w1, tn), lambda j, i: (0, 0, off + j), layer)
    psp = pl.BlockSpec((1, w1, tn), lambda j, i: (jnp.minimum(i, pt - 1) // tps, 0, j))
    osp = pl.BlockSpec((ns, w1, tn), lambda j, i: (0, 0, j))
    h, pg, pv, og, ov = pl.pallas_call(
        functools.partial(_ffn_up_kernel, tm=tm, rc=256, prompt_tiles=pt, tiles_per_seq=tps),
        grid=(nj, m // tm),
        in_specs=[pl.BlockSpec((tm, k), lambda j, i: (i, 0)), wsp(0), wsp(nj),
                  par(FFN_CONV, 0), par(FFN_CONV, nj), par(1, 0), par(1, nj), stsp(0), stsp(nj)],
        out_specs=[pl.BlockSpec((tm, tn), lambda j, i: (i, j)), psp, psp, osp, osp],
        out_shape=[jax.ShapeDtypeStruct((m, D_FF), BF16),
                   jax.ShapeDtypeStruct((nb, w1, D_FF), F32), jax.ShapeDtypeStruct((nb, w1, D_FF), F32),
                   jax.ShapeDtypeStruct((ns, w1, D_FF), F32), jax.ShapeDtypeStruct((ns, w1, D_FF), F32)],
        scratch_shapes=[pltpu.VMEM((k, tn), BF16), pltpu.VMEM((k, tn), BF16),
                        pltpu.VMEM((SUBLANES, tn), F32), pltpu.VMEM((SUBLANES, tn), F32)],
        compiler_params=_cparams(("arbitrary", "arbitrary")),
        name="ffn_up",
    )(xn, w_up, w_up, conv_w, conv_w, conv_b, conv_b, state, state)
    return h, jnp.concatenate([pg, pv], axis=-1), jnp.concatenate([og, ov], axis=-1)


S5_Q = 4
S5_QW = S5_NSTATE // S5_Q


def _s5_kernel(*refs, chained, tm):
    if chained:
        (u_ref, bre_ref, bim_ref, cre_ref, cim_ref, mre_ref, mim_ref, pre_ref, pim_ref, d_ref, wglu_ref,
         y_ref, xlre_ref, xlim_ref, sre, sim, car_re, car_im) = refs
    else:
        (u_ref, x0re_ref, x0im_ref, bre_ref, bim_ref, cre_ref, cim_ref, mre_ref, mim_ref, pre_ref, pim_ref,
         d_ref, wglu_ref, y_ref, xlre_ref, xlim_ref, sre, sim) = refs

    u = u_ref[...]
    ub = u.astype(BF16)
    for q in range(S5_Q):
        uq = ub[:, LANES * q:LANES * (q + 1)]
        sre[:, S5_QW * q:S5_QW * (q + 1)] = _mm(uq, bre_ref[q])
        sim[:, S5_QW * q:S5_QW * (q + 1)] = _mm(uq, bim_ref[q])

    if chained:
        @pl.when(pl.program_id(1) == 0)
        def _():
            car_re[...] = jnp.zeros_like(car_re)
            car_im[...] = jnp.zeros_like(car_im)

    def blk(i, carry):
        r0 = pl.multiple_of(i * SUBLANES, SUBLANES)
        for q in range(S5_Q):
            sl = slice(S5_QW * q, S5_QW * (q + 1))
            xr = sre[pl.ds(r0, SUBLANES), sl]
            xi = sim[pl.ds(r0, SUBLANES), sl]
            for li, d in enumerate((1, 2, 4)):
                mr, mi = mre_ref[li, :, sl], mim_ref[li, :, sl]
                sr, si = pltpu.roll(xr, d, 0), pltpu.roll(xi, d, 0)
                xr, xi = xr + (mr * sr - mi * si), xi + (mr * si + mi * sr)
            if chained:
                cr, ci = car_re[:, sl], car_im[:, sl]
            else:
                cr, ci = x0re_ref[pl.ds(i, 1), sl], x0im_ref[pl.ds(i, 1), sl]
            pr, pi_ = pre_ref[:, sl], pim_ref[:, sl]
            xr, xi = xr + (pr * cr - pi_ * ci), xi + (pr * ci + pi_ * cr)
            sre[pl.ds(r0, SUBLANES), sl] = xr
            sim[pl.ds(r0, SUBLANES), sl] = xi
            if chained:
                car_re[:, sl] = xr[SUBLANES - 1:SUBLANES, :]
                car_im[:, sl] = xi[SUBLANES - 1:SUBLANES, :]
            else:
                xlre_ref[pl.ds(i, 1), sl] = xr[SUBLANES - 1:SUBLANES, :]
                xlim_ref[pl.ds(i, 1), sl] = xi[SUBLANES - 1:SUBLANES, :]
        return carry

    lax.fori_loop(0, tm // SUBLANES, blk, 0)

    if chained:
        xlre_ref[0] = car_re[...]
        xlim_ref[0] = car_im[...]

    ys = []
    for q in range(S5_Q):
        sl = slice(S5_QW * q, S5_QW * (q + 1))
        ys.append(_mm(sre[:, sl].astype(BF16), cre_ref[q]) - _mm(sim[:, sl].astype(BF16), cim_ref[q]))
    y = jnp.concatenate(ys, axis=1) + d_ref[...] * u
    y = _gelu_tanh(y)
    y = y * _sigmoid(_mm(y.astype(BF16), wglu_ref[...]))
    y_ref[...] = y.astype(BF16)


def _const_spec(shape):
    zeros = (0,) * len(shape)
    return pl.BlockSpec(shape, lambda *a: zeros)


def _s5_param_specs():
    return [_const_spec((S5_Q, LANES, S5_QW)), _const_spec((S5_Q, LANES, S5_QW)),
            _const_spec((S5_Q, S5_QW, LANES)), _const_spec((S5_Q, S5_QW, LANES)),
            _const_spec((3, SUBLANES, S5_NSTATE)), _const_spec((3, SUBLANES, S5_NSTATE)),
            _const_spec((SUBLANES, S5_NSTATE)), _const_spec((SUBLANES, S5_NSTATE)),
            _const_spec((1, S5_WIDTH)), _const_spec((S5_WIDTH, S5_WIDTH))]


def _s5_prompt(proj, sp, nb, seq, tm):
    nt = seq // tm
    cb = P_S5 // S5_WIDTH
    y, xre, xim = pl.pallas_call(
        functools.partial(_s5_kernel, chained=True, tm=tm),
        grid=(nb, nt),
        in_specs=[pl.BlockSpec((tm, S5_WIDTH), lambda b, t: (b * nt + t, cb))] + _s5_param_specs(),
        out_specs=[pl.BlockSpec((tm, S5_WIDTH), lambda b, t: (b * nt + t, 0)),
                   pl.BlockSpec((1, 1, S5_NSTATE), lambda b, t: (b, 0, 0)),
                   pl.BlockSpec((1, 1, S5_NSTATE), lambda b, t: (b, 0, 0))],
        out_shape=[jax.ShapeDtypeStruct((nb * seq, S5_WIDTH), BF16),
                   jax.ShapeDtypeStruct((nb, 1, S5_NSTATE), F32),
                   jax.ShapeDtypeStruct((nb, 1, S5_NSTATE), F32)],
        scratch_shapes=[pltpu.VMEM((tm, S5_NSTATE), F32), pltpu.VMEM((tm, S5_NSTATE), F32),
                        pltpu.VMEM((1, S5_NSTATE), F32), pltpu.VMEM((1, S5_NSTATE), F32)],
        compiler_params=_cparams(("parallel", "arbitrary")),
        name="s5_prompt",
    )(proj, *sp)
    return y, xre.reshape(nb, S5_GROUPS, S5_STATE), xim.reshape(nb, S5_GROUPS, S5_STATE)


def _s5_sample(proj, x0re, x0im, sp, row0, nseq, tm):
    ns = tm // SUBLANES
    cb = P_S5 // S5_WIDTH
    r0 = row0 // tm
    st = pl.BlockSpec((ns, S5_NSTATE), lambda i: (i, 0))
    y, xre, xim = pl.pallas_call(
        functools.partial(_s5_kernel, chained=False, tm=tm),
        grid=(nseq // ns,),
        in_specs=[pl.BlockSpec((tm, S5_WIDTH), lambda i: (r0 + i, cb)), st, st] + _s5_param_specs(),
        out_specs=[pl.BlockSpec((tm, S5_WIDTH), lambda i: (i, 0)), st, st],
        out_shape=[jax.ShapeDtypeStruct((nseq * SUBLANES, S5_WIDTH), BF16),
                   jax.ShapeDtypeStruct((nseq, S5_NSTATE), F32),
                   jax.ShapeDtypeStruct((nseq, S5_NSTATE), F32)],
        scratch_shapes=[pltpu.VMEM((tm, S5_NSTATE), F32), pltpu.VMEM((tm, S5_NSTATE), F32)],
        compiler_params=_cparams(("parallel",)),
        name="s5_sample",
    )(proj, x0re.reshape(nseq, S5_NSTATE), x0im.reshape(nseq, S5_NSTATE), *sp)
    return y, xre.reshape(nseq, S5_GROUPS, S5_STATE), xim.reshape(nseq, S5_GROUPS, S5_STATE)


def _s5_params(lam_re, lam_im, b_re, b_im, c_re, c_im, d, log_step, w_glu):
    lam = lax.complex(lam_re, lam_im)
    step = jnp.exp(log_step)[:, None]
    lam_bar = jnp.exp(lam * step)
    b_bar = ((lam_bar - 1.0) / lam)[..., None] * lax.complex(b_re, b_im)

    def bblk(bm):
        bm = bm.reshape(S5_Q, 8, S5_STATE, S5_GROUP)
        eye = jnp.eye(8, dtype=F32)
        out = jnp.einsum('qgpc,gh->qgchp', bm, eye)
        return out.reshape(S5_Q, LANES, S5_QW).astype(BF16)

    def cblk(cm):
        cm = cm.reshape(S5_Q, 8, S5_GROUP, S5_STATE)
        eye = jnp.eye(8, dtype=F32)
        out = jnp.einsum('qgcp,gh->qgphc', cm, eye)
        return out.reshape(S5_Q, S5_QW, LANES).astype(BF16)

    lam_flat = lam_bar.reshape(1, S5_NSTATE)
    pows = [lam_flat]
    for _ in range(SUBLANES - 1):
        pows.append(pows[-1] * lam_flat)
    row = jnp.arange(SUBLANES)[:, None]
    m = jnp.stack([jnp.where(row >= dd, pows[dd - 1], 0.0) for dd in (1, 2, 4)])
    p = jnp.concatenate(pows, axis=0)
    return (bblk(b_bar.real), bblk(b_bar.imag), cblk(c_re), cblk(c_im),
            m.real, m.imag, p.real, p.imag, d.reshape(1, S5_WIDTH), w_glu.astype(BF16))


def _seg64_sum(x):
    outs = []
    for t in range(x.shape[1] // LANES):
        xt = x[:, LANES * t:LANES * (t + 1)]
        lo = lax.broadcasted_iota(jnp.int32, xt.shape, 1) < RW_HD
        s_lo = jnp.sum(jnp.where(lo, xt, 0.0), axis=-1, keepdims=True)
        s_hi = jnp.sum(jnp.where(lo, 0.0, xt), axis=-1, keepdims=True)
        outs.append(jnp.where(lo, s_lo, s_hi))
    return jnp.concatenate(outs, axis=1)


def _rwkv_kernel(*refs, per_seq, tm, aliased=False):
    if aliased:
        refs = refs[1:]
    if per_seq:
        (h_ref, sh_ref, si_ref, mu_ref, w0_ref, w2_ref, a0_ref, a2_ref, g2_ref, kk_ref, ka_ref, rk_ref,
         lnw_ref, lnb_ref, y_ref, so_ref, r_s, k_s, v_s, a_s, b_s, lw_s, y_s) = refs
    else:
        (h_ref, mu_ref, w0_ref, w2_ref, a0_ref, a2_ref, g2_ref, kk_ref, ka_ref, rk_ref, lnw_ref, lnb_ref,
         y_ref, so_ref, r_s, k_s, v_s, a_s, b_s, lw_s, y_s, prev_s, st_s) = refs
    c, w, hd_w = CHUNK, RW_WIDTH, RW_HD

    h = h_ref[...]
    if per_seq:
        prev = _shift_seq8(h, 1, sh_ref[...])
    else:
        @pl.when(pl.program_id(1) == 0)
        def _():
            prev_s[...] = jnp.zeros_like(prev_s)
            st_s[...] = jnp.zeros_like(st_s)
        prev = _shift_chain(h, 1, prev_s[...])
        prev_s[...] = h[tm - SUBLANES:, :]
    hs = h + (prev - h) * mu_ref[...]
    r = hs[:, 0:w]
    k = hs[:, w:2 * w]
    v = hs[:, 2 * w:3 * w]
    wd = hs[:, 3 * w:3 * w + LANES]
    ad = hs[:, 3 * w + LANES:3 * w + 2 * LANES]
    gd = hs[:, 3 * w + 2 * LANES:]
    w_log = -_softplus(-(w0_ref[...] + _mm(jnp.tanh(wd).astype(BF16), w2_ref[...]))) - 0.5
    lw_s[...] = -jnp.exp(w_log)
    a_ic = _sigmoid(a0_ref[...] + _mm(ad.astype(BF16), a2_ref[...]))
    gate = _mm(_sigmoid(gd).astype(BF16), g2_ref[...])
    kx = k * kk_ref[...]
    kkn = kx * lax.rsqrt(_seg64_sum(kx * kx) + 1e-6)
    k2 = k * (1.0 + (a_ic - 1.0) * ka_ref[...])
    r_s[...] = r
    k_s[...] = k2
    v_s[...] = v
    a_s[...] = -kkn
    b_s[...] = kkn * a_ic

    strict, causal = _chunk_masks(per_seq)
    if per_seq:
        big_mask = _seq_block_mask(2 * c, hd_w)

    heads = range(RW_HEADS)
    sls = [slice(hd_w * hd, hd_w * (hd + 1)) for hd in heads]
    group = _chunks_per_iteration(tm // c)

    def chunk_group(gidx, carry):
        pre = []
        for t in range(group):
            idx = gidx * group + t
            rows = pl.ds(pl.multiple_of(idx * c, c), c)
            lw = lw_s[rows, :]
            cum = _cumsum_rows(lw, groupwise=per_seq)
            cl = _group_last(cum) if per_seq else cum[c - 1:c, :]
            e_pos, e_neg, e_end = jnp.exp(cum), jnp.exp(-cum), jnp.exp(cl - cum)
            rr, kc, vc, ac, bc = r_s[rows, :], k_s[rows, :], v_s[rows, :], a_s[rows, :], b_s[rows, :]
            pre.append(dict(rows=rows, seqs=pl.ds(idx * SEQ_PER_CHUNK, SEQ_PER_CHUNK), vc=vc,
                            rt=rr * e_pos, kt=kc * e_neg, bt=bc * e_neg, at=ac * jnp.exp(cum - lw),
                            kh=kc * e_end, bh=bc * e_end, wc=jnp.exp(cl)))
        pairs = [(t, hd) for t in range(group) for hd in heads]
        ath = [pre[t]['at'][:, sls[hd]] for t, hd in pairs]
        vh = [pre[t]['vc'][:, sls[hd]] for t, hd in pairs]
        rth = [pre[t]['rt'][:, sls[hd]] for t, hd in pairs]
        ab = [_mm_nt(jnp.concatenate([ath[p], rth[p]], axis=0).astype(BF16),
                     jnp.concatenate([pre[t]['kt'][:, sls[hd]], pre[t]['bt'][:, sls[hd]]], axis=0).astype(BF16))
              for p, (t, hd) in enumerate(pairs)]
        a_ak = [jnp.where(strict, m[:c, :c], 0.0) for m in ab]
        n_ab = [jnp.where(strict, -m[:c, c:], 0.0) for m in ab]
        a_r = [jnp.concatenate([jnp.where(causal, m[c:, :c], 0.0), jnp.where(causal, m[c:, c:], 0.0)],
                               axis=1).astype(BF16) for m in ab]
        akv = [_mm(a_ak[p].astype(BF16), vh[p].astype(BF16)) for p in range(len(pairs))]
        sol = _solve_unit_lower_each(
            n_ab, [jnp.concatenate([ath[p], akv[p]], axis=1) for p in range(len(pairs))], per_seq)
        kb = [jnp.concatenate([pre[t]['kh'][:, sls[hd]], pre[t]['bh'][:, sls[hd]]], axis=0).astype(BF16)
              for t, hd in pairs]
        for t in range(group):
            rows, seqs, wc = pre[t]['rows'], pre[t]['seqs'], pre[t]['wc']
            ps = [t * RW_HEADS + hd for hd in heads]
            if per_seq:
                st = [si_ref[seqs, hd].reshape(SEQ_PER_CHUNK * hd_w, hd_w) for hd in heads]
            else:
                st = [st_s[hd] for hd in heads]
            pr = [_mm_nt(jnp.concatenate([sol[p][:, :hd_w], rth[p]], axis=0).astype(BF16), st[hd].astype(BF16))
                  for hd, p in zip(heads, ps)]
            if per_seq:
                u = [_pick_own_seq(pr[hd][:c], hd_w) + sol[p][:, hd_w:] for hd, p in zip(heads, ps)]
                yst = [_pick_own_seq(pr[hd][c:], hd_w) for hd in heads]
            else:
                u = [pr[hd][:c] + sol[p][:, hd_w:] for hd, p in zip(heads, ps)]
                yst = [pr[hd][c:] for hd in heads]
            vu = [jnp.concatenate([vh[p], u[hd]], axis=0) for hd, p in zip(heads, ps)]
            ya = [_mm(a_r[p], vu[hd].astype(BF16)) for hd, p in zip(heads, ps)]
            if per_seq:
                upd = [_mm_tn(jnp.where(big_mask, jnp.concatenate([vu[hd]] * SEQ_PER_CHUNK, axis=1), 0.0
                                        ).astype(BF16), kb[p]) for hd, p in zip(heads, ps)]
            else:
                upd = [_mm_tn(vu[hd].astype(BF16), kb[p]) for hd, p in zip(heads, ps)]
            for hd in heads:
                if per_seq:
                    wcb = jnp.concatenate(
                        [jnp.broadcast_to(wc[SUBLANES * s:SUBLANES * s + 1, sls[hd]], (hd_w, hd_w))
                         for s in range(SEQ_PER_CHUNK)], axis=0)
                    so_ref[seqs, hd] = (st[hd] * wcb + upd[hd]).reshape(SEQ_PER_CHUNK, hd_w, hd_w)
                else:
                    st_s[hd] = st[hd] * wc[:, sls[hd]] + upd[hd]
                y_s[rows, sls[hd]] = yst[hd] + ya[hd]
        return carry

    lax.fori_loop(0, tm // (c * group), chunk_group, 0)

    if not per_seq:
        so_ref[0] = st_s[...]

    y = y_s[...]
    mean = _seg64_sum(y) * (1.0 / hd_w)
    yc = y - mean
    var = _seg64_sum(yc * yc) * (1.0 / hd_w)
    yn = yc * lax.rsqrt(var + RW_GN_EPS) * lnw_ref[...] + lnb_ref[...]
    bonus = _seg64_sum(r_s[...] * k_s[...] * rk_ref[...]) * v_s[...]
    y_ref[...] = ((yn + bonus) * gate).astype(BF16)


def _rwkv_param_specs():
    row = _const_spec((1, RW_WIDTH))
    return [_const_spec((1, RW_PCOLS)), row, _const_spec((LANES, RW_WIDTH)), row, _const_spec((LANES, RW_WIDTH)),
            _const_spec((RW_GATE, RW_WIDTH)), row, row, row, row, row]


def _rwkv_scratch(tm):
    return [pltpu.VMEM((tm, RW_WIDTH), F32) for _ in range(7)]


def _rwkv_prompt(proj, rp, nb, seq, tm):
    nt = seq // tm
    cb = P_RW // RW_PCOLS
    y, st = pl.pallas_call(
        functools.partial(_rwkv_kernel, per_seq=False, tm=tm),
        grid=(nb, nt),
        in_specs=[pl.BlockSpec((tm, RW_PCOLS), lambda b, t: (b * nt + t, cb))] + _rwkv_param_specs(),
        out_specs=[pl.BlockSpec((tm, RW_WIDTH), lambda b, t: (b * nt + t, 0)),
                   pl.BlockSpec((1, RW_HEADS, RW_HD, RW_HD), lambda b, t: (b, 0, 0, 0))],
        out_shape=[jax.ShapeDtypeStruct((nb * seq, RW_WIDTH), BF16),
                   jax.ShapeDtypeStruct((nb, RW_HEADS, RW_HD, RW_HD), F32)],
        scratch_shapes=_rwkv_scratch(tm) + [pltpu.VMEM((SUBLANES, RW_PCOLS), F32),
                                            pltpu.VMEM((RW_HEADS, RW_HD, RW_HD), F32)],
        compiler_params=_cparams(("parallel", "arbitrary")),
        name="rwkv_prompt",
    )(proj, *rp)
    return y, st


def _stacked_out(prev):
    if prev is None:
        return (), [], {}
    return (prev,), [pl.BlockSpec(memory_space=pl.ANY)], {0: 1}


def _rwkv_sample(proj, shift0, st0, layer, prev_st, rp, row0, nseq, tm):
    ns = tm // SUBLANES
    cb = P_RW // RW_PCOLS
    r0 = row0 // tm
    st_io = _layer_spec((ns, RW_HEADS, RW_HD, RW_HD), lambda i: (i, 0, 0, 0), layer)
    extra, extra_specs, aliases = _stacked_out(prev_st)
    y, st = pl.pallas_call(
        functools.partial(_rwkv_kernel, per_seq=True, tm=tm, aliased=prev_st is not None),
        grid=(nseq // ns,),
        in_specs=extra_specs + [pl.BlockSpec((tm, RW_PCOLS), lambda i: (r0 + i, cb)),
                                pl.BlockSpec((ns, 1, RW_PCOLS), lambda i: (i, 0, 0)), st_io] + _rwkv_param_specs(),
        out_specs=[pl.BlockSpec((tm, RW_WIDTH), lambda i: (i, 0)), st_io],
        out_shape=[jax.ShapeDtypeStruct((nseq * SUBLANES, RW_WIDTH), BF16),
                   jax.ShapeDtypeStruct(st0.shape, F32)],
        input_output_aliases=aliases,
        scratch_shapes=_rwkv_scratch(tm),
        compiler_params=_cparams(("parallel",)),
        name="rwkv_sample",
    )(*extra, proj, shift0, st0, *rp)
    return y, st


def _rw_pad_cols(x):
    z = jnp.zeros(x.shape[:-1] + (LANES - RW_LORA,), x.dtype)
    a, b = 3 * RW_WIDTH, 3 * RW_WIDTH + RW_LORA
    return jnp.concatenate([x[..., :a], x[..., a:b], z, x[..., b:b + RW_LORA], z, x[..., b + RW_LORA:]], axis=-1)


def _rw_unpad_cols(x):
    a = 3 * RW_WIDTH
    return jnp.concatenate([x[..., :a + RW_LORA], x[..., a + LANES:a + LANES + RW_LORA], x[..., a + 2 * LANES:]],
                           axis=-1)


def _rwkv_params(mu, w0, w2, a0, a2, g2, k_k, k_a, r_k, ln_w, ln_b):
    row = lambda t: t.reshape(1, RW_WIDTH)
    padr = lambda t: jnp.concatenate([t, jnp.zeros((LANES - RW_LORA, RW_WIDTH), t.dtype)], axis=0).astype(BF16)
    return (_rw_pad_cols(mu).reshape(1, RW_PCOLS), row(w0), padr(w2), row(a0), padr(a2), g2.astype(BF16),
            row(k_k), row(k_a), row(r_k), row(ln_w), row(ln_b))


def _gdn_kernel(*refs, per_seq, tm, aliased=False):
    if aliased:
        refs = refs[1:]
    if per_seq:
        (q_ref, k_ref, v_ref, z_ref, ba_ref, fq_ref, fk_ref, fv_ref, si_ref, cwq_ref, cwk_ref, cwv_ref,
         al_ref, dt_ref, ng_ref, y_ref, so_ref, q_s, k_s, v_s, ba_s, o_s) = refs
        conv_state = (fq_ref, fk_ref, fv_ref)
    else:
        (q_ref, k_ref, v_ref, z_ref, ba_ref, cwq_ref, cwk_ref, cwv_ref, al_ref, dt_ref, ng_ref,
         y_ref, so_ref, q_s, k_s, v_s, ba_s, o_s, pq_s, pk_s, pv_s, st_s) = refs
        prevs = (pq_s, pk_s, pv_s)

        @pl.when(pl.program_id(1) == 0)
        def _():
            for p in prevs:
                p[...] = jnp.zeros_like(p)
            st_s[...] = jnp.zeros_like(st_s)
    c, hw = CHUNK, GD_HD

    def conv_act(idx, u_ref, cw_ref):
        u = u_ref[...]
        if per_seq:
            f = conv_state[idx][...]
            sh = [_shift_seq8(u, d, f) for d in (1, 2, 3)]
        else:
            p = prevs[idx][...]
            sh = [_shift_chain(u, d, p) for d in (1, 2, 3)]
            prevs[idx][...] = u[tm - SUBLANES:, :]
        x = cw_ref[3:4, :] * u + cw_ref[2:3, :] * sh[0] + cw_ref[1:2, :] * sh[1] + cw_ref[0:1, :] * sh[2]
        return x * _sigmoid(x)

    def l2n(x):
        outs = []
        for hd in range(GD_HEADS):
            xh = x[:, hw * hd:hw * (hd + 1)]
            outs.append(xh * lax.rsqrt(jnp.sum(xh * xh, axis=-1, keepdims=True) + 1e-6))
        return jnp.concatenate(outs, axis=1)

    q_s[...] = l2n(conv_act(0, q_ref, cwq_ref)) * (hw ** -0.5)
    k_s[...] = l2n(conv_act(1, k_ref, cwk_ref))
    v_s[...] = conv_act(2, v_ref, cwv_ref)
    ba = ba_ref[...]
    lane = lax.broadcasted_iota(jnp.int32, ba.shape, 1)
    g_all = -jnp.exp(al_ref[...]) * _softplus(ba + dt_ref[...])
    ba_s[...] = jnp.where(lane < GD_HEADS, _sigmoid(ba), g_all)

    strict, causal = _chunk_masks(per_seq)
    if per_seq:
        big_mask = _seq_block_mask(c, hw)

    heads = range(GD_HEADS)
    sls = [slice(hw * hd, hw * (hd + 1)) for hd in heads]
    group = _chunks_per_iteration(tm // c)

    def chunk_group(gidx, carry):
        pre = []
        for t in range(group):
            idx = gidx * group + t
            rows = pl.ds(pl.multiple_of(idx * c, c), c)
            bg = ba_s[rows, :]
            gc_all = _cumsum_rows(bg, groupwise=per_seq)
            pre.append(dict(rows=rows, seqs=pl.ds(idx * SEQ_PER_CHUNK, SEQ_PER_CHUNK), bg=bg, gc_all=gc_all,
                            gc_t=gc_all.T, gl_all=_group_last(gc_all) if per_seq else gc_all[c - 1:c, :],
                            qa=q_s[rows, :], ka=k_s[rows, :], va=v_s[rows, :]))
        pairs = [(t, hd) for t in range(group) for hd in heads]
        npair = len(pairs)
        beta = [pre[t]['bg'][:, hd:hd + 1] for t, hd in pairs]
        gc = [pre[t]['gc_all'][:, GD_HEADS + hd:GD_HEADS + hd + 1] for t, hd in pairs]
        gl = [pre[t]['gl_all'][:, GD_HEADS + hd:GD_HEADS + hd + 1] for t, hd in pairs]
        dec = [jnp.exp(jnp.where(causal, gc[p] - pre[t]['gc_t'][GD_HEADS + hd:GD_HEADS + hd + 1, :], -1e30))
               for p, (t, hd) in enumerate(pairs)]
        qh = [pre[t]['qa'][:, sls[hd]] for t, hd in pairs]
        kh = [pre[t]['ka'][:, sls[hd]] for t, hd in pairs]
        kb = [kh[p] * beta[p] for p in range(npair)]
        eg = [jnp.exp(g) for g in gc]
        kq = [_mm_nt(jnp.concatenate([kb[p], qh[p]], axis=0).astype(BF16), kh[p].astype(BF16))
              for p in range(npair)]
        low = [jnp.where(strict, kq[p][:c] * dec[p], 0.0) for p in range(npair)]
        intra = [(kq[p][c:] * dec[p]).astype(BF16) for p in range(npair)]
        sol = _solve_unit_lower_each(
            low, [jnp.concatenate([pre[t]['va'][:, sls[hd]] * beta[p], kb[p] * eg[p]], axis=1)
                  for p, (t, hd) in enumerate(pairs)], per_seq)
        wq = [jnp.concatenate([sol[p][:, hw:], qh[p] * eg[p]], axis=0).astype(BF16) for p in range(npair)]
        kdec = [kh[p] * jnp.exp(gl[p] - gc[p]) for p in range(npair)]
        for t in range(group):
            rows, seqs = pre[t]['rows'], pre[t]['seqs']
            ps = [t * GD_HEADS + hd for hd in heads]
            if per_seq:
                st3 = [si_ref[seqs, hd] for hd in heads]
                st_rhs = [jnp.concatenate([s3[s] for s in range(SEQ_PER_CHUNK)], axis=1).astype(BF16)
                          for s3 in st3]
            else:
                st = [st_s[hd] for hd in heads]
                st_rhs = [s.astype(BF16) for s in st]
            ws = [_mm(wq[p], st_rhs[hd]) for hd, p in zip(heads, ps)]
            if per_seq:
                v_new = [sol[p][:, :hw] - _pick_own_seq(ws[hd][:c], hw) for hd, p in zip(heads, ps)]
                o_st = [_pick_own_seq(ws[hd][c:], hw) for hd in heads]
            else:
                v_new = [sol[p][:, :hw] - ws[hd][:c] for hd, p in zip(heads, ps)]
                o_st = [ws[hd][c:] for hd in heads]
            vnb = [v.astype(BF16) for v in v_new]
            o_in = [_mm(intra[p], vnb[hd]) for hd, p in zip(heads, ps)]
            if per_seq:
                upd = [_mm_tn(jnp.where(big_mask, jnp.concatenate([kdec[p]] * SEQ_PER_CHUNK, axis=1), 0.0
                                        ).astype(BF16), vnb[hd]) for hd, p in zip(heads, ps)]
            else:
                upd = [_mm_tn(kdec[p].astype(BF16), vnb[hd]) for hd, p in zip(heads, ps)]
            for hd, p in zip(heads, ps):
                if per_seq:
                    egl = jnp.exp(gl[p])
                    mult = jnp.concatenate([jnp.broadcast_to(egl[SUBLANES * s:SUBLANES * s + 1, :], (hw, hw))
                                            for s in range(SEQ_PER_CHUNK)], axis=0)
                    st_new = st3[hd].reshape(SEQ_PER_CHUNK * hw, hw) * mult + upd[hd]
                    so_ref[seqs, hd] = st_new.reshape(SEQ_PER_CHUNK, hw, hw)
                else:
                    st_s[hd] = st[hd] * jnp.exp(gl[p]) + upd[hd]
                o_s[rows, sls[hd]] = o_st[hd] + o_in[hd]
        return carry

    lax.fori_loop(0, tm // (c * group), chunk_group, 0)

    if not per_seq:
        so_ref[0] = st_s[...]

    outs = []
    for hd in range(GD_HEADS):
        sl = slice(hw * hd, hw * (hd + 1))
        o = o_s[:, sl]
        o = o * lax.rsqrt(jnp.mean(o * o, axis=-1, keepdims=True) + NORM_EPS) * ng_ref[...]
        z = z_ref[:, sl]
        outs.append(o * (z * _sigmoid(z)))
    y_ref[...] = jnp.concatenate(outs, axis=1).astype(BF16)


def _gdn_param_specs():
    c2 = lambda shape, j=0: pl.BlockSpec(shape, (lambda *a: (0, j)))
    return [c2((GD_CONV, GD_WIDTH), 0), c2((GD_CONV, GD_WIDTH), 1), c2((GD_CONV, GD_WIDTH), 2),
            c2((1, LANES)), c2((1, LANES)), c2((1, GD_HD))]


def _gdn_scratch(tm):
    return ([pltpu.VMEM((tm, GD_WIDTH), F32) for _ in range(3)] + [pltpu.VMEM((tm, LANES), F32),
                                                                  pltpu.VMEM((tm, GD_WIDTH), F32)])


def _gdn_prompt(proj, proj_sb, gp, nb, seq, tm):
    nt = seq // tm
    qb = P_Q // GD_WIDTH
    blk = lambda j: pl.BlockSpec((tm, GD_WIDTH), lambda b, t: (b * nt + t, j))
    y, st = pl.pallas_call(
        functools.partial(_gdn_kernel, per_seq=False, tm=tm),
        grid=(nb, nt),
        in_specs=[blk(qb), blk(qb + 1), blk(qb + 2), blk(P_Z // GD_WIDTH),
                  pl.BlockSpec((tm, LANES), lambda b, t: (b * nt + t, P_BA // LANES))] + _gdn_param_specs(),
        out_specs=[pl.BlockSpec((tm, GD_WIDTH), lambda b, t: (b * nt + t, 0)),
                   pl.BlockSpec((1, GD_HEADS, GD_HD, GD_HD), lambda b, t: (b, 0, 0, 0))],
        out_shape=[jax.ShapeDtypeStruct((nb * seq, GD_WIDTH), BF16),
                   jax.ShapeDtypeStruct((nb, GD_HEADS, GD_HD, GD_HD), F32)],
        scratch_shapes=_gdn_scratch(tm) + [pltpu.VMEM((SUBLANES, GD_WIDTH), F32) for _ in range(3)]
        + [pltpu.VMEM((GD_HEADS, GD_HD, GD_HD), F32)],
        compiler_params=_cparams(("parallel", "arbitrary")),
        name="gdn_prompt",
    )(proj, proj, proj, proj, proj_sb, gp[0], gp[0], gp[0], *gp[1:])
    return y, st


def _gdn_sample(proj, proj_sb, conv0, st0, layer, prev_st, gp, row0, nseq, tm):
    ns = tm // SUBLANES
    qb = P_Q // GD_WIDTH
    r0 = row0 // tm
    blk = lambda j: pl.BlockSpec((tm, GD_WIDTH), lambda i: (r0 + i, j))
    frm = lambda j: _layer_spec((ns, GD_CONV - 1, GD_WIDTH), lambda i: (i, 0, j), layer)
    st_io = _layer_spec((ns, GD_HEADS, GD_HD, GD_HD), lambda i: (i, 0, 0, 0), layer)
    extra, extra_specs, aliases = _stacked_out(prev_st)
    y, st = pl.pallas_call(
        functools.partial(_gdn_kernel, per_seq=True, tm=tm, aliased=prev_st is not None),
        grid=(nseq // ns,),
        in_specs=extra_specs + [blk(qb), blk(qb + 1), blk(qb + 2), blk(P_Z // GD_WIDTH),
                                pl.BlockSpec((tm, LANES), lambda i: (r0 + i, P_BA // LANES)),
                                frm(0), frm(1), frm(2), st_io] + _gdn_param_specs(),
        out_specs=[pl.BlockSpec((tm, GD_WIDTH), lambda i: (i, 0)), st_io],
        out_shape=[jax.ShapeDtypeStruct((nseq * SUBLANES, GD_WIDTH), BF16),
                   jax.ShapeDtypeStruct(st0.shape, F32)],
        input_output_aliases=aliases,
        scratch_shapes=_gdn_scratch(tm),
        compiler_params=_cparams(("parallel",)),
        name="gdn_sample",
    )(*extra, proj, proj, proj, proj, proj_sb, conv0, conv0, conv0, st0, gp[0], gp[0], gp[0], *gp[1:])
    return y, st


def _gdn_params(conv_w, a_log, dt_bias, norm_g):
    pad = lambda t: jnp.concatenate([jnp.zeros((GD_HEADS,), F32), t, jnp.zeros((LANES - 2 * GD_HEADS,), F32)]
                                    ).reshape(1, LANES)
    return (conv_w, pad(a_log), pad(dt_bias), norm_g.reshape(1, GD_HD))


W_IN_RW = S5_WIDTH
W_IN_QKVZ = W_IN_RW + RW_COLS
W_IN_BA = W_IN_QKVZ + 4 * GD_WIDTH
W_IN_GATES = W_IN_BA + 2 * GD_HEADS


def _split_w_in(w):
    w_t = jnp.swapaxes(w, 1, 2)
    zeros = lambda rows: jnp.zeros((w_t.shape[0], rows, w_t.shape[2]), w_t.dtype)
    a = W_IN_RW + 3 * RW_WIDTH + RW_LORA
    w_rw_t = jnp.concatenate([w_t[:, W_IN_RW:a], zeros(LANES - RW_LORA), w_t[:, a:a + RW_LORA],
                              zeros(LANES - RW_LORA), w_t[:, a + RW_LORA:W_IN_QKVZ]], axis=1)
    w_sb_t = jnp.concatenate([w_t[:, :S5_WIDTH], w_t[:, W_IN_BA:W_IN_GATES],
                              zeros(P_SB_COLS - P_BA - 2 * GD_HEADS)], axis=1)
    return w_t, w_rw_t, w_sb_t


def _tail_rows(a, cols, nrows, mp, seq):
    a3 = a.reshape(a.shape[0] // SUBLANES, SUBLANES, a.shape[1])
    g = seq // SUBLANES
    lo = SUBLANES - nrows
    return (a3[g - 1:mp // SUBLANES:g, lo:, cols[0]:cols[1]], a3[mp // SUBLANES:, lo:, cols[0]:cols[1]])


def kernel(x_prompt, x_sample, state_s5_re, state_s5_im, state_rwkv_shift, state_rwkv_wkv, state_gdn_conv, state_gdn, state_ffn_conv, norm1_g, norm2_g, final_norm_g, w_in, s5_lambda_re, s5_lambda_im, s5_b_re, s5_b_im, s5_c_re, s5_c_im, s5_d, s5_log_step, s5_w_glu, rwkv_mu, rwkv_w0, rwkv_w2, rwkv_a0, rwkv_a2, rwkv_g2, rwkv_k_k, rwkv_k_a, rwkv_r_k, rwkv_ln_w, rwkv_ln_b, gdn_conv_w, gdn_a_log, gdn_dt_bias, gdn_norm_g, w_br_s5, w_br_rwkv, w_br_gdn, w_out, ffn_w_up, ffn_conv_w, ffn_conv_b, ffn_w_down):
    nb, seq, d = x_prompt.shape
    ns, sl, _ = x_sample.shape
    assert sl == SUBLANES and d == D_MODEL
    mp, ms = nb * seq, ns * sl
    m = mp + ms
    tm_norm, tm_proj, tm_out = 512, 3072, 512
    x_groups = (x_prompt.reshape(mp, d), x_sample.reshape(ms, d), 0)

    w_t, w_rw_t, w_sb_t = _split_w_in(w_in)
    w_br = (w_br_s5.astype(BF16), w_br_rwkv.astype(BF16), w_br_gdn.astype(BF16))
    w_out_b, w_down_b = w_out.astype(BF16), ffn_w_down.astype(BF16)
    conv_b = ffn_conv_b.reshape(DEPTH, 1, 2 * D_FF)
    shift0 = _rw_pad_cols(state_rwkv_shift)[:, :, None, :]

    new_p = [[] for _ in range(7)]
    new_s = [[] for _ in range(7)]
    s_wkv = s_gdn = None
    for l in range(DEPTH):
        xp, xs, s_off = x_groups
        xn1 = _norm_cast(xp, xs, s_off * (mp // tm_norm), m, mp // tm_norm, norm1_g[l], tm=tm_norm)
        proj_g = _matmul_nt(xn1, w_t, l, W_IN_GATES, 3 * D_MODEL, tm=tm_proj, tn=512)
        proj_rw = _matmul_nt(xn1, w_rw_t, l, 0, RW_PCOLS, tm=tm_proj, tn=512)
        proj_qkvz = _matmul_nt(xn1, w_t, l, W_IN_QKVZ, 4 * GD_WIDTH, tm=tm_proj, tn=512)
        proj_sb = _matmul_nt(xn1, w_sb_t, l, 0, P_SB_COLS, tm=tm_proj, tn=512)

        sp = _s5_params(s5_lambda_re[l], s5_lambda_im[l], s5_b_re[l], s5_b_im[l], s5_c_re[l], s5_c_im[l],
                        s5_d[l], s5_log_step[l], s5_w_glu[l])
        ys5_p, p_re, p_im = _s5_prompt(proj_sb, sp, nb, seq, tm=256)
        ys5_s, s_re, s_im = _s5_sample(proj_sb, state_s5_re[l], state_s5_im[l], sp, mp, ns, tm=128)

        rp = _rwkv_params(rwkv_mu[l], rwkv_w0[l], rwkv_w2[l], rwkv_a0[l], rwkv_a2[l], rwkv_g2[l], rwkv_k_k[l],
                          rwkv_k_a[l], rwkv_r_k[l].reshape(RW_WIDTH), rwkv_ln_w[l], rwkv_ln_b[l])
        yrw_p, p_wkv = _rwkv_prompt(proj_rw, rp, nb, seq, tm=256)
        yrw_s, s_wkv = _rwkv_sample(proj_rw, shift0[l], state_rwkv_wkv, l, s_wkv, rp, mp, ns, tm=128)
        p_shift, s_shift = [_rw_unpad_cols(t[:, 0])
                            for t in _tail_rows(proj_rw, (P_RW, P_RW + RW_PCOLS), 1, mp, seq)]

        gp = _gdn_params(gdn_conv_w[l], gdn_a_log[l], gdn_dt_bias[l], gdn_norm_g[l])
        ygd_p, p_gdn = _gdn_prompt(proj_qkvz, proj_sb, gp, nb, seq, tm=256)
        ygd_s, s_gdn = _gdn_sample(proj_qkvz, proj_sb, state_gdn_conv, state_gdn, l, s_gdn, gp, mp, ns, tm=64)
        p_gconv, s_gconv = _tail_rows(proj_qkvz, (P_Q, P_Q + 3 * GD_WIDTH), GD_CONV - 1, mp, seq)

        merged = _merge((ys5_p, yrw_p, ygd_p), (ys5_s, yrw_s, ygd_s), w_br, l, proj_g, tm=512, tn=1024)
        x, xn = _out_norm(merged, w_out_b, l, xp, xs, s_off * (mp // tm_out), mp // tm_out, norm2_g[l], tm=tm_out)

        h, p_ffn, s_ffn = _ffn_up(xn, ffn_w_up, ffn_conv_w, conv_b, state_ffn_conv, l, nb, seq, ns,
                                  tm=1024, tn=512)
        x = _matmul_res(h, w_down_b, l, x, tm=1536, tn=512)
        x_groups = (x, x, 1)

        for lst, vals in ((new_p, (p_re, p_im, p_shift, p_wkv, p_gconv, p_gdn, p_ffn)),
                          (new_s, (s_re, s_im, s_shift, None, s_gconv, None, s_ffn))):
            for acc, val in zip(lst, vals):
                acc.append(val)

    y_p, y_s = _final_norm(x, final_norm_g, mp, tm=512)
    stack = lambda lst: [jnp.stack(v) for v in lst]
    out_p, out_s = stack(new_p), stack([v for v in new_s if v[0] is not None])
    out_s = out_s[:3] + [s_wkv, out_s[3], s_gdn, out_s[4]]
    return (y_p.reshape(nb, seq, d), y_s.reshape(ns, sl, d)) + tuple(out_p) + tuple(out_s)
```

```python
import functools
import math

import jax
import jax.numpy as jnp
from jax import lax
from jax.experimental import pallas as pl
from jax.experimental.pallas import tpu as pltpu

F32 = jnp.float32
BF16 = jnp.bfloat16

SUBLANES = 8
LANES = 128
VMEM_LIMIT = 52 * 1024 * 1024
CHUNK = 64
SEQ_PER_CHUNK = CHUNK // SUBLANES

D_MODEL = 2048
DEPTH = 2
S5_WIDTH = 512
S5_GROUP = 16
S5_GROUPS = 32
S5_STATE = 64
S5_NSTATE = S5_GROUPS * S5_STATE
RW_WIDTH = 512
RW_HEADS = 8
RW_HD = 64
RW_LORA = 96
RW_GATE = 256
RW_COLS = 3 * RW_WIDTH + 2 * RW_LORA + RW_GATE
RW_PCOLS = 2048
RW_GN_EPS = 64e-5
GD_WIDTH = 1024
GD_HEADS = 8
GD_HD = 128
GD_CONV = 4
D_FF = 5632
FFN_CONV = 3
NORM_EPS = 1e-6
P_GATES, P_RW, P_Q, P_Z, P_S5, P_BA, P_SB_COLS = 0, 0, 0, 3072, 0, 512, 1024


def _cparams(sem):
    return pltpu.CompilerParams(dimension_semantics=sem, vmem_limit_bytes=VMEM_LIMIT)


def _mm(a, b):
    return jnp.dot(a, b, preferred_element_type=F32)


def _mm_nt(a, b):
    return lax.dot_general(a, b, (((1,), (1,)), ((), ())), preferred_element_type=F32)


def _mm_tn(a, b):
    return lax.dot_general(a, b, (((0,), (0,)), ((), ())), preferred_element_type=F32)


def _sigmoid(x):
    return 1.0 / (1.0 + jnp.exp(-x))


def _softplus(x):
    return jnp.maximum(x, 0.0) + jnp.log1p(jnp.exp(-jnp.abs(x)))


def _gelu_tanh(x):
    return 0.5 * x * (1.0 + jnp.tanh(math.sqrt(2.0 / math.pi) * (x + 0.044715 * (x * x * x))))


def _row_iota(shape):
    return lax.broadcasted_iota(jnp.int32, shape, 0)


def _shift_chain(x, d, prev8):
    n, ch = x.shape
    x3 = x.reshape(n // SUBLANES, SUBLANES, ch)
    rot = pltpu.roll(x3, d, 1)
    before = jnp.concatenate([pltpu.roll(prev8, d, 0)[None], rot[:-1]], axis=0)
    row = lax.broadcasted_iota(jnp.int32, x3.shape, 1)
    return jnp.where(row < d, before, rot).reshape(n, ch)


def _shift_seq8(x, d, state):
    n, ch = x.shape
    ns, w1, _ = state.shape
    out = pltpu.roll(x.reshape(ns, SUBLANES, ch), d, 1)
    row = lax.broadcasted_iota(jnp.int32, out.shape, 1)
    for r in range(d):
        out = jnp.where(row == r, state[:, w1 - d + r:w1 - d + r + 1, :], out)
    return out.reshape(n, ch)


def _group_last(x):
    c, n = x.shape
    x3 = x.reshape(c // SUBLANES, SUBLANES, n)
    return jnp.broadcast_to(x3[:, SUBLANES - 1:SUBLANES, :], x3.shape).reshape(c, n)


def _cumsum_rows(x, groupwise=False):
    c = x.shape[0]
    row = _row_iota(x.shape) % SUBLANES
    for d in (1, 2, 4):
        x = x + jnp.where(row >= d, pltpu.roll(x, d, 0), 0.0)
    if c > SUBLANES and not groupwise:
        blocks = [x[SUBLANES * i:SUBLANES * (i + 1)] for i in range(c // SUBLANES)]
        for i in range(1, len(blocks)):
            blocks[i] = blocks[i] + blocks[i - 1][SUBLANES - 1:SUBLANES, :]
        x = jnp.concatenate(blocks, axis=0)
    return x


def _split2(x):
    hi = x.astype(BF16)
    return hi, (x - hi.astype(F32)).astype(BF16)


def _hi_lo_f32(x):
    hi = x.astype(BF16).astype(F32)
    return hi, x - hi


def _mm_hilo_each(a_list, b_list):
    lhs, rhs = [], []
    for a, b in zip(a_list, b_list):
        ah, al = _hi_lo_f32(a)
        bh, bl = _hi_lo_f32(b)
        lhs.append(jnp.concatenate([ah, ah, al], axis=1).astype(BF16))
        rhs.append(jnp.concatenate([bh, bl, bh], axis=0).astype(BF16))
    return [_mm(l, r) for l, r in zip(lhs, rhs)]


def _solve_unit_lower_each(lows, rhss, diag_only):
    c = lows[0].shape[0]
    nb = c // SUBLANES
    nh = len(lows)
    n = rhss[0].shape[1]
    blk = lambda a, i: a[SUBLANES * i:SUBLANES * (i + 1)]
    rb = [[blk(r, i) for i in range(nb)] for r in rhss]
    with_off = nb > 1 and not diag_only
    same = (lax.broadcasted_iota(jnp.int32, (c, c), 0) // SUBLANES
            == lax.broadcasted_iota(jnp.int32, (c, c), 1) // SUBLANES)
    if with_off:
        ob = [[blk(jnp.where(same, 0.0, low), i) for i in range(nb)] for low in lows]
    sel = (lax.broadcasted_iota(jnp.int32, (c, (SUBLANES - 1) * LANES), 0) % SUBLANES
           == lax.broadcasted_iota(jnp.int32, (c, (SUBLANES - 1) * LANES), 1) // LANES)
    sel = jnp.where(sel, 1.0, 0.0).astype(BF16)
    sel2 = jnp.concatenate([sel, sel], axis=0)
    diag = [_hi_lo_f32(jnp.where(same, low, 0.0)) if not diag_only else _hi_lo_f32(low) for low in lows]
    cols = [_mm(jnp.concatenate(d, axis=1).astype(BF16), sel2) for d in diag]
    reps = n // LANES
    for j in range(SUBLANES - 1):
        for h in range(nh):
            cj = cols[h][:, LANES * j:LANES * (j + 1)]
            cj_n = jnp.concatenate([cj] * reps, axis=1) if reps > 1 else cj
            for i in range(nb):
                rb[h][i] = rb[h][i] - blk(cj_n, i) * rb[h][i][j:j + 1, :]
                if with_off and i > 0:
                    ob[h][i] = ob[h][i] - blk(cj, i)[:, :c] * ob[h][i][j:j + 1, :]
    r1 = [jnp.concatenate(b, axis=0) if nb > 1 else b[0] for b in rb]
    if not with_off:
        return r1
    n1 = [jnp.concatenate(b, axis=0) for b in ob]
    n2 = _mm_hilo_each(n1, n1)
    t = _mm_hilo_each(n1 + n2, r1 + n2)
    y = [r - nr for r, nr in zip(r1, t[:nh])]
    n4 = t[nh:]
    y = [a + b for a, b in zip(y, _mm_hilo_each(n2, y))]
    return [a + b for a, b in zip(y, _mm_hilo_each(n4, y))]


def _chunks_per_iteration(n_chunks):
    return 2 if n_chunks % 2 == 0 else 1


def _chunk_masks(per_seq):
    ri = lax.broadcasted_iota(jnp.int32, (CHUNK, CHUNK), 0)
    ci = lax.broadcasted_iota(jnp.int32, (CHUNK, CHUNK), 1)
    strict, causal = ri > ci, ri >= ci
    if per_seq:
        same = (ri // SUBLANES) == (ci // SUBLANES)
        strict, causal = jnp.logical_and(strict, same), jnp.logical_and(causal, same)
    return strict, causal


def _seq_block_mask(rows, width):
    r = lax.broadcasted_iota(jnp.int32, (rows, SEQ_PER_CHUNK * width), 0)
    l = lax.broadcasted_iota(jnp.int32, (rows, SEQ_PER_CHUNK * width), 1)
    return ((r % CHUNK) // SUBLANES) == (l // width)


def _pick_own_seq(a, width):
    return jnp.concatenate([a[SUBLANES * s:SUBLANES * (s + 1), width * s:width * (s + 1)]
                            for s in range(SEQ_PER_CHUNK)], axis=0)


def _rmsnorm_rows(x_ref, g_ref, o_ref, rows, chunk=128):
    def body(i, carry):
        r0 = pl.multiple_of(i * chunk, chunk)
        x = x_ref[pl.ds(r0, chunk), :]
        ms = jnp.mean(x * x, axis=-1, keepdims=True)
        o_ref[pl.ds(r0, chunk), :] = (x * lax.rsqrt(ms + NORM_EPS) * g_ref[...]).astype(o_ref.dtype)
        return carry
    lax.fori_loop(0, rows // chunk, body, 0)


def _two_group_specs(tm, width, prompt_tiles, sample_off, pipeline_mode=None):
    p = pl.BlockSpec((tm, width), lambda i, *_: (jnp.minimum(i, prompt_tiles - 1), 0), pipeline_mode=pipeline_mode)
    s = pl.BlockSpec((tm, width), lambda i, *_: (jnp.maximum(i - prompt_tiles, 0) + sample_off, 0),
                     pipeline_mode=pipeline_mode)
    return p, s


def _layer_spec(shape, index_map, layer):
    return pl.BlockSpec((None,) + shape, lambda *a: (layer,) + tuple(index_map(*a)))


def _norm_cast_kernel(xp_ref, xs_ref, g_ref, o_ref, *, rows, prompt_tiles):
    i = pl.program_id(0)

    @pl.when(i < prompt_tiles)
    def _():
        _rmsnorm_rows(xp_ref, g_ref, o_ref, rows)

    @pl.when(i >= prompt_tiles)
    def _():
        _rmsnorm_rows(xs_ref, g_ref, o_ref, rows)


def _norm_cast(xp, xs, sample_off, m, prompt_tiles, g, tm):
    k = xp.shape[1]
    psp, ssp = _two_group_specs(tm, k, prompt_tiles, sample_off)
    return pl.pallas_call(
        functools.partial(_norm_cast_kernel, rows=tm, prompt_tiles=prompt_tiles),
        grid=(m // tm,),
        in_specs=[psp, ssp, pl.BlockSpec((1, k), lambda i: (0, 0))],
        out_specs=pl.BlockSpec((tm, k), lambda i: (i, 0)),
        out_shape=jax.ShapeDtypeStruct((m, k), BF16),
        compiler_params=_cparams(("arbitrary",)),
        name="norm_cast",
    )(xp, xs, g.reshape(1, k))


def _matmul_nt_kernel(a_ref, w_ref, o_ref):
    o_ref[...] = _mm_nt(a_ref[...], w_ref[0].astype(BF16))


def _matmul_nt(a, w_t, layer, row0, n, tm, tn):
    m, k = a.shape
    assert row0 % SUBLANES == 0 and n % tn == 0
    return pl.pallas_call(
        _matmul_nt_kernel,
        grid=(m // tm, n // tn),
        in_specs=[pl.BlockSpec((tm, k), lambda i, j: (i, 0), pipeline_mode=pl.Buffered(1)),
                  pl.BlockSpec((pl.Element(1), pl.Element(tn), pl.Element(k)),
                               lambda i, j: (layer, pl.multiple_of(row0 + j * tn, SUBLANES), 0))],
        out_specs=pl.BlockSpec((tm, tn), lambda i, j: (i, j)),
        out_shape=jax.ShapeDtypeStruct((m, n), F32),
        compiler_params=_cparams(("parallel", "arbitrary")),
        name="proj_matmul",
    )(a, w_t)


def _matmul_res_kernel(a_ref, w_ref, r_ref, o_ref):
    o_ref[...] = r_ref[...] + _mm(a_ref[...], w_ref[...])


def _matmul_res(a, w, layer, res, tm, tn):
    m, k = a.shape
    n = w.shape[-1]
    return pl.pallas_call(
        _matmul_res_kernel,
        grid=(m // tm, n // tn),
        in_specs=[pl.BlockSpec((tm, k), lambda i, j: (i, 0), pipeline_mode=pl.Buffered(1)),
                  _layer_spec((k, tn), lambda i, j: (0, j), layer),
                  pl.BlockSpec((tm, tn), lambda i, j: (i, j))],
        out_specs=pl.BlockSpec((tm, tn), lambda i, j: (i, j)),
        out_shape=jax.ShapeDtypeStruct((m, n), F32),
        compiler_params=_cparams(("parallel", "arbitrary")),
        name="matmul_res",
    )(a, w, res)


def _out_norm_kernel(a_ref, w_ref, rp_ref, rs_ref, g_ref, x_ref, xn_ref, *, rows, prompt_tiles):
    res = jnp.where(pl.program_id(0) < prompt_tiles, rp_ref[...], rs_ref[...])
    x_ref[...] = res + _mm(a_ref[...], w_ref[...])
    _rmsnorm_rows(x_ref, g_ref, xn_ref, rows)


def _out_norm(a, w, layer, rp, rs, sample_off, prompt_tiles, g, tm):
    m, k = a.shape
    n = w.shape[-1]
    psp, ssp = _two_group_specs(tm, n, prompt_tiles, sample_off)
    return pl.pallas_call(
        functools.partial(_out_norm_kernel, rows=tm, prompt_tiles=prompt_tiles),
        grid=(m // tm,),
        in_specs=[pl.BlockSpec((tm, k), lambda i: (i, 0)),
                  _layer_spec((k, n), lambda i: (0, 0), layer),
                  psp, ssp, pl.BlockSpec((1, n), lambda i: (0, 0))],
        out_specs=[pl.BlockSpec((tm, n), lambda i: (i, 0)), pl.BlockSpec((tm, n), lambda i: (i, 0))],
        out_shape=[jax.ShapeDtypeStruct((m, n), F32), jax.ShapeDtypeStruct((m, n), BF16)],
        compiler_params=_cparams(("parallel",)),
        name="out_norm",
    )(a, w, rp, rs, g.reshape(1, n))


def _merge_kernel(y1p, y2p, y3p, y1s, y2s, y3s, w1_ref, w2_ref, w3_ref, g1_ref, g2_ref, g3_ref, o_ref, *,
                  prompt_tiles, tm, rc):
    is_p = pl.program_id(0) < prompt_tiles
    for r in range(tm // rc):
        rows = slice(r * rc, (r + 1) * rc)
        pick = lambda p, s: jnp.where(is_p, p[rows, :], s[rows, :])
        acc = _sigmoid(g1_ref[rows, :]) * _mm(pick(y1p, y1s), w1_ref[...])
        acc = acc + _sigmoid(g2_ref[rows, :]) * _mm(pick(y2p, y2s), w2_ref[...])
        acc = acc + _sigmoid(g3_ref[rows, :]) * _mm(pick(y3p, y3s), w3_ref[...])
        o_ref[rows, :] = acc.astype(BF16)


def _merge(ys_p, ys_s, ws, layer, proj, tm, tn):
    mp, ms = ys_p[0].shape[0], ys_s[0].shape[0]
    n = ws[0].shape[-1]
    nj = n // tn
    pt = mp // tm
    gate_spec = lambda b: pl.BlockSpec((tm, tn), lambda i, j: (i, P_GATES // tn + b * nj + j))
    y_specs = [_two_group_specs(tm, y.shape[1], pt, 0) for y in ys_p]
    w_spec = lambda w: _layer_spec((w.shape[1], tn), lambda i, j: (0, j), layer)
    return pl.pallas_call(
        functools.partial(_merge_kernel, prompt_tiles=pt, tm=tm, rc=128),
        grid=((mp + ms) // tm, nj),
        in_specs=[s[0] for s in y_specs] + [s[1] for s in y_specs] + [w_spec(w) for w in ws]
        + [gate_spec(0), gate_spec(1), gate_spec(2)],
        out_specs=pl.BlockSpec((tm, tn), lambda i, j: (i, j)),
        out_shape=jax.ShapeDtypeStruct((mp + ms, n), BF16),
        compiler_params=_cparams(("parallel", "arbitrary")),
        name="merge",
    )(*ys_p, *ys_s, *ws, proj, proj, proj)


def _final_norm_kernel(x_ref, g_ref, yp_ref, ys_ref, *, prompt_tiles):
    x = x_ref[...]
    ms = jnp.mean(x * x, axis=-1, keepdims=True)
    y = x * lax.rsqrt(ms + NORM_EPS) * g_ref[...]
    i = pl.program_id(0)

    @pl.when(i < prompt_tiles)
    def _():
        yp_ref[...] = y

    @pl.when(i >= prompt_tiles)
    def _():
        ys_ref[...] = y


def _final_norm(x, g, mp, tm):
    m, k = x.shape
    pt = mp // tm
    psp, ssp = _two_group_specs(tm, k, pt, 0)
    return pl.pallas_call(
        functools.partial(_final_norm_kernel, prompt_tiles=pt),
        grid=(m // tm,),
        in_specs=[pl.BlockSpec((tm, k), lambda i: (i, 0)), pl.BlockSpec((1, k), lambda i: (0, 0))],
        out_specs=[psp, ssp],
        out_shape=[jax.ShapeDtypeStruct((mp, k), F32), jax.ShapeDtypeStruct((m - mp, k), F32)],
        compiler_params=_cparams(("arbitrary",)),
        name="final_norm",
    )(x, g.reshape(1, k))


def _ffn_up_kernel(xn_ref, wg_ref, wv_ref, cwg_ref, cwv_ref, bg_ref, bv_ref, sg_ref, sv_ref,
                   h_ref, pg_ref, pv_ref, og_ref, ov_ref, wgb, wvb, cg, cv, *, tm, rc, prompt_tiles, tiles_per_seq):
    i = pl.program_id(1)
    w1 = FFN_CONV - 1

    @pl.when(i == 0)
    def _():
        wgb[...] = wg_ref[...].astype(BF16)
        wvb[...] = wv_ref[...].astype(BF16)

    def conv(u, s1, s2, cw_ref, b_ref):
        return cw_ref[2:3, :] * u + cw_ref[1:2, :] * s1 + cw_ref[0:1, :] * s2 + b_ref[...]

    def up(rows):
        x = xn_ref[rows, :]
        return _mm(x, wgb[...]), _mm(x, wvb[...])

    @pl.when(i < prompt_tiles)
    def _():
        first = (i % tiles_per_seq) == 0
        pg = jnp.where(first, 0.0, cg[...])
        pv = jnp.where(first, 0.0, cv[...])
        for r in range(tm // rc):
            rows = slice(r * rc, (r + 1) * rc)
            ug, uv = up(rows)
            gate = conv(ug, _shift_chain(ug, 1, pg), _shift_chain(ug, 2, pg), cwg_ref, bg_ref)
            val = conv(uv, _shift_chain(uv, 1, pv), _shift_chain(uv, 2, pv), cwv_ref, bv_ref)
            h_ref[rows, :] = (gate * _sigmoid(gate) * val).astype(BF16)
            pg, pv = ug[rc - SUBLANES:, :], uv[rc - SUBLANES:, :]
        cg[...] = pg
        cv[...] = pv
        pg_ref[0] = pg[SUBLANES - w1:, :]
        pv_ref[0] = pv[SUBLANES - w1:, :]

    @pl.when(i >= prompt_tiles)
    def _():
        nsr = rc // SUBLANES
        for r in range(tm // rc):
            rows = slice(r * rc, (r + 1) * rc)
            seqs = slice(r * nsr, (r + 1) * nsr)
            ug, uv = up(rows)
            sg, sv = sg_ref[seqs], sv_ref[seqs]
            gate = conv(ug, _shift_seq8(ug, 1, sg), _shift_seq8(ug, 2, sg), cwg_ref, bg_ref)
            val = conv(uv, _shift_seq8(uv, 1, sv), _shift_seq8(uv, 2, sv), cwv_ref, bv_ref)
            h_ref[rows, :] = (gate * _sigmoid(gate) * val).astype(BF16)
            og_ref[seqs] = ug.reshape(nsr, SUBLANES, ug.shape[1])[:, SUBLANES - w1:, :]
            ov_ref[seqs] = uv.reshape(nsr, SUBLANES, uv.shape[1])[:, SUBLANES - w1:, :]


def _ffn_up(xn, w_up, conv_w, conv_b, state, layer, nb, seq, ns, tm, tn):
    m, k = xn.shape
    mp = nb * seq
    assert ns * SUBLANES == tm and (m - mp) == tm and seq % tm == 0
    nj, pt, tps = D_FF // tn, mp // tm, seq // tm
    w1 = FFN_CONV - 1
    wsp = lambda off: _layer_spec((k, tn), lambda j, i: (0, off + j), layer)
    par = lambda rows, off: _layer_spec((rows, tn), lambda j, i: (0, off + j), layer)
    stsp = lambda off: _layer_spec((ns, w1, tn), lambda j, i: (0, 0, off + j), layer)
    psp = pl.BlockSpec((1, w1, tn), lambda j, i: (jnp.minimum(i, pt - 1) // tps, 0, j))
    osp = pl.BlockSpec((ns, w1, tn), lambda j, i: (0, 0, j))
    h, pg, pv, og, ov = pl.pallas_call(
        functools.partial(_ffn_up_kernel, tm=tm, rc=256, prompt_tiles=pt, tiles_per_seq=tps),
        grid=(nj, m // tm),
        in_specs=[pl.BlockSpec((tm, k), lambda j, i: (i, 0)), wsp(0), wsp(nj),
                  par(FFN_CONV, 0), par(FFN_CONV, nj), par(1, 0), par(1, nj), stsp(0), stsp(nj)],
        out_specs=[pl.BlockSpec((tm, tn), lambda j, i: (i, j)), psp, psp, osp, osp],
        out_shape=[jax.ShapeDtypeStruct((m, D_FF), BF16),
                   jax.ShapeDtypeStruct((nb, w1, D_FF), F32), jax.ShapeDtypeStruct((nb, w1, D_FF), F32),
                   jax.ShapeDtypeStruct((ns, w1, D_FF), F32), jax.ShapeDtypeStruct((ns, w1, D_FF), F32)],
        scratch_shapes=[pltpu.VMEM((k, tn), BF16), pltpu.VMEM((k, tn), BF16),
                        pltpu.VMEM((SUBLANES, tn), F32), pltpu.VMEM((SUBLANES, tn), F32)],
        compiler_params=_cparams(("arbitrary", "arbitrary")),
        name="ffn_up",
    )(xn, w_up, w_up, conv_w, conv_w, conv_b, conv_b, state, state)
    return h, jnp.concatenate([pg, pv], axis=-1), jnp.concatenate([og, ov], axis=-1)


S5_Q = 4
S5_QW = S5_NSTATE // S5_Q


def _s5_kernel(*refs, chained, tm):
    if chained:
        (u_ref, bre_ref, bim_ref, cre_ref, cim_ref, mre_ref, mim_ref, pre_ref, pim_ref, d_ref, wglu_ref,
         y_ref, xlre_ref, xlim_ref, sre, sim, car_re, car_im) = refs
    else:
        (u_ref, x0re_ref, x0im_ref, bre_ref, bim_ref, cre_ref, cim_ref, mre_ref, mim_ref, pre_ref, pim_ref,
         d_ref, wglu_ref, y_ref, xlre_ref, xlim_ref, sre, sim) = refs

    u = u_ref[...]
    ub = u.astype(BF16)
    for q in range(S5_Q):
        uq = ub[:, LANES * q:LANES * (q + 1)]
        sre[:, S5_QW * q:S5_QW * (q + 1)] = _mm(uq, bre_ref[q])
        sim[:, S5_QW * q:S5_QW * (q + 1)] = _mm(uq, bim_ref[q])

    if chained:
        @pl.when(pl.program_id(1) == 0)
        def _():
            car_re[...] = jnp.zeros_like(car_re)
            car_im[...] = jnp.zeros_like(car_im)

    def blk(i, carry):
        r0 = pl.multiple_of(i * SUBLANES, SUBLANES)
        for q in range(S5_Q):
            sl = slice(S5_QW * q, S5_QW * (q + 1))
            xr = sre[pl.ds(r0, SUBLANES), sl]
            xi = sim[pl.ds(r0, SUBLANES), sl]
            for li, d in enumerate((1, 2, 4)):
                mr, mi = mre_ref[li, :, sl], mim_ref[li, :, sl]
                sr, si = pltpu.roll(xr, d, 0), pltpu.roll(xi, d, 0)
                xr, xi = xr + (mr * sr - mi * si), xi + (mr * si + mi * sr)
            if chained:
                cr, ci = car_re[:, sl], car_im[:, sl]
            else:
                cr, ci = x0re_ref[pl.ds(i, 1), sl], x0im_ref[pl.ds(i, 1), sl]
            pr, pi_ = pre_ref[:, sl], pim_ref[:, sl]
            xr, xi = xr + (pr * cr - pi_ * ci), xi + (pr * ci + pi_ * cr)
            sre[pl.ds(r0, SUBLANES), sl] = xr
            sim[pl.ds(r0, SUBLANES), sl] = xi
            if chained:
                car_re[:, sl] = xr[SUBLANES - 1:SUBLANES, :]
                car_im[:, sl] = xi[SUBLANES - 1:SUBLANES, :]
            else:
                xlre_ref[pl.ds(i, 1), sl] = xr[SUBLANES - 1:SUBLANES, :]
                xlim_ref[pl.ds(i, 1), sl] = xi[SUBLANES - 1:SUBLANES, :]
        return carry

    lax.fori_loop(0, tm // SUBLANES, blk, 0)

    if chained:
        xlre_ref[0] = car_re[...]
        xlim_ref[0] = car_im[...]

    ys = []
    for q in range(S5_Q):
        sl = slice(S5_QW * q, S5_QW * (q + 1))
        ys.append(_mm(sre[:, sl].astype(BF16), cre_ref[q]) - _mm(sim[:, sl].astype(BF16), cim_ref[q]))
    y = jnp.concatenate(ys, axis=1) + d_ref[...] * u
    y = _gelu_tanh(y)
    y = y * _sigmoid(_mm(y.astype(BF16), wglu_ref[...]))
    y_ref[...] = y.astype(BF16)


def _const_spec(shape):
    zeros = (0,) * len(shape)
    return pl.BlockSpec(shape, lambda *a: zeros)


def _s5_param_specs():
    return [_const_spec((S5_Q, LANES, S5_QW)), _const_spec((S5_Q, LANES, S5_QW)),
            _const_spec((S5_Q, S5_QW, LANES)), _const_spec((S5_Q, S5_QW, LANES)),
            _const_spec((3, SUBLANES, S5_NSTATE)), _const_spec((3, SUBLANES, S5_NSTATE)),
            _const_spec((SUBLANES, S5_NSTATE)), _const_spec((SUBLANES, S5_NSTATE)),
            _const_spec((1, S5_WIDTH)), _const_spec((S5_WIDTH, S5_WIDTH))]


def _s5_prompt(proj, sp, nb, seq, tm):
    nt = seq // tm
    cb = P_S5 // S5_WIDTH
    y, xre, xim = pl.pallas_call(
        functools.partial(_s5_kernel, chained=True, tm=tm),
        grid=(nb, nt),
        in_specs=[pl.BlockSpec((tm, S5_WIDTH), lambda b, t: (b * nt + t, cb))] + _s5_param_specs(),
        out_specs=[pl.BlockSpec((tm, S5_WIDTH), lambda b, t: (b * nt + t, 0)),
                   pl.BlockSpec((1, 1, S5_NSTATE), lambda b, t: (b, 0, 0)),
                   pl.BlockSpec((1, 1, S5_NSTATE), lambda b, t: (b, 0, 0))],
        out_shape=[jax.ShapeDtypeStruct((nb * seq, S5_WIDTH), BF16),
                   jax.ShapeDtypeStruct((nb, 1, S5_NSTATE), F32),
                   jax.ShapeDtypeStruct((nb, 1, S5_NSTATE), F32)],
        scratch_shapes=[pltpu.VMEM((tm, S5_NSTATE), F32), pltpu.VMEM((tm, S5_NSTATE), F32),
                        pltpu.VMEM((1, S5_NSTATE), F32), pltpu.VMEM((1, S5_NSTATE), F32)],
        compiler_params=_cparams(("parallel", "arbitrary")),
        name="s5_prompt",
    )(proj, *sp)
    return y, xre.reshape(nb, S5_GROUPS, S5_STATE), xim.reshape(nb, S5_GROUPS, S5_STATE)


def _s5_sample(proj, x0re, x0im, sp, row0, nseq, tm):
    ns = tm // SUBLANES
    cb = P_S5 // S5_WIDTH
    r0 = row0 // tm
    st = pl.BlockSpec((ns, S5_NSTATE), lambda i: (i, 0))
    y, xre, xim = pl.pallas_call(
        functools.partial(_s5_kernel, chained=False, tm=tm),
        grid=(nseq // ns,),
        in_specs=[pl.BlockSpec((tm, S5_WIDTH), lambda i: (r0 + i, cb)), st, st] + _s5_param_specs(),
        out_specs=[pl.BlockSpec((tm, S5_WIDTH), lambda i: (i, 0)), st, st],
        out_shape=[jax.ShapeDtypeStruct((nseq * SUBLANES, S5_WIDTH), BF16),
                   jax.ShapeDtypeStruct((nseq, S5_NSTATE), F32),
                   jax.ShapeDtypeStruct((nseq, S5_NSTATE), F32)],
        scratch_shapes=[pltpu.VMEM((tm, S5_NSTATE), F32), pltpu.VMEM((tm, S5_NSTATE), F32)],
        compiler_params=_cparams(("parallel",)),
        name="s5_sample",
    )(proj, x0re.reshape(nseq, S5_NSTATE), x0im.reshape(nseq, S5_NSTATE), *sp)
    return y, xre.reshape(nseq, S5_GROUPS, S5_STATE), xim.reshape(nseq, S5_GROUPS, S5_STATE)


def _s5_params(lam_re, lam_im, b_re, b_im, c_re, c_im, d, log_step, w_glu):
    lam = lax.complex(lam_re, lam_im)
    step = jnp.exp(log_step)[:, None]
    lam_bar = jnp.exp(lam * step)
    b_bar = ((lam_bar - 1.0) / lam)[..., None] * lax.complex(b_re, b_im)

    def bblk(bm):
        bm = bm.reshape(S5_Q, 8, S5_STATE, S5_GROUP)
        eye = jnp.eye(8, dtype=F32)
        out = jnp.einsum('qgpc,gh->qgchp', bm, eye)
        return out.reshape(S5_Q, LANES, S5_QW).astype(BF16)

    def cblk(cm):
        cm = cm.reshape(S5_Q, 8, S5_GROUP, S5_STATE)
        eye = jnp.eye(8, dtype=F32)
        out = jnp.einsum('qgcp,gh->qgphc', cm, eye)
        return out.reshape(S5_Q, S5_QW, LANES).astype(BF16)

    lam_flat = lam_bar.reshape(1, S5_NSTATE)
    pows = [lam_flat]
    for _ in range(SUBLANES - 1):
        pows.append(pows[-1] * lam_flat)
    row = jnp.arange(SUBLANES)[:, None]
    m = jnp.stack([jnp.where(row >= dd, pows[dd - 1], 0.0) for dd in (1, 2, 4)])
    p = jnp.concatenate(pows, axis=0)
    return (bblk(b_bar.real), bblk(b_bar.imag), cblk(c_re), cblk(c_im),
            m.real, m.imag, p.real, p.imag, d.reshape(1, S5_WIDTH), w_glu.astype(BF16))


def _seg64_sum(x):
    outs = []
    for t in range(x.shape[1] // LANES):
        xt = x[:, LANES * t:LANES * (t + 1)]
        lo = lax.broadcasted_iota(jnp.int32, xt.shape, 1) < RW_HD
        s_lo = jnp.sum(jnp.where(lo, xt, 0.0), axis=-1, keepdims=True)
        s_hi = jnp.sum(jnp.where(lo, 0.0, xt), axis=-1, keepdims=True)
        outs.append(jnp.where(lo, s_lo, s_hi))
    return jnp.concatenate(outs, axis=1)


def _rwkv_kernel(*refs, per_seq, tm):
    if per_seq:
        (h_ref, sh_ref, si_ref, mu_ref, w0_ref, w2_ref, a0_ref, a2_ref, g2_ref, kk_ref, ka_ref, rk_ref,
         lnw_ref, lnb_ref, y_ref, so_ref, r_s, k_s, v_s, a_s, b_s, lw_s, y_s) = refs
    else:
        (h_ref, mu_ref, w0_ref, w2_ref, a0_ref, a2_ref, g2_ref, kk_ref, ka_ref, rk_ref, lnw_ref, lnb_ref,
         y_ref, so_ref, r_s, k_s, v_s, a_s, b_s, lw_s, y_s, prev_s, st_s) = refs
    c, w, hd_w = CHUNK, RW_WIDTH, RW_HD

    h = h_ref[...]
    if per_seq:
        prev = _shift_seq8(h, 1, sh_ref[...])
    else:
        @pl.when(pl.program_id(1) == 0)
        def _():
            prev_s[...] = jnp.zeros_like(prev_s)
            st_s[...] = jnp.zeros_like(st_s)
        prev = _shift_chain(h, 1, prev_s[...])
        prev_s[...] = h[tm - SUBLANES:, :]
    hs = h + (prev - h) * mu_ref[...]
    r = hs[:, 0:w]
    k = hs[:, w:2 * w]
    v = hs[:, 2 * w:3 * w]
    wd = hs[:, 3 * w:3 * w + LANES]
    ad = hs[:, 3 * w + LANES:3 * w + 2 * LANES]
    gd = hs[:, 3 * w + 2 * LANES:]
    w_log = -_softplus(-(w0_ref[...] + _mm(jnp.tanh(wd).astype(BF16), w2_ref[...]))) - 0.5
    lw_s[...] = -jnp.exp(w_log)
    a_ic = _sigmoid(a0_ref[...] + _mm(ad.astype(BF16), a2_ref[...]))
    gate = _mm(_sigmoid(gd).astype(BF16), g2_ref[...])
    kx = k * kk_ref[...]
    kkn = kx * lax.rsqrt(_seg64_sum(kx * kx) + 1e-6)
    k2 = k * (1.0 + (a_ic - 1.0) * ka_ref[...])
    r_s[...] = r
    k_s[...] = k2
    v_s[...] = v
    a_s[...] = -kkn
    b_s[...] = kkn * a_ic

    strict, causal = _chunk_masks(per_seq)
    if per_seq:
        big_mask = _seq_block_mask(2 * c, hd_w)

    heads = range(RW_HEADS)
    sls = [slice(hd_w * hd, hd_w * (hd + 1)) for hd in heads]
    group = _chunks_per_iteration(tm // c)

    def chunk_group(gidx, carry):
        pre = []
        for t in range(group):
            idx = gidx * group + t
            rows = pl.ds(pl.multiple_of(idx * c, c), c)
            lw = lw_s[rows, :]
            cum = _cumsum_rows(lw, groupwise=per_seq)
            cl = _group_last(cum) if per_seq else cum[c - 1:c, :]
            e_pos, e_neg, e_end = jnp.exp(cum), jnp.exp(-cum), jnp.exp(cl - cum)
            rr, kc, vc, ac, bc = r_s[rows, :], k_s[rows, :], v_s[rows, :], a_s[rows, :], b_s[rows, :]
            pre.append(dict(rows=rows, seqs=pl.ds(idx * SEQ_PER_CHUNK, SEQ_PER_CHUNK), vc=vc,
                            rt=rr * e_pos, kt=kc * e_neg, bt=bc * e_neg, at=ac * jnp.exp(cum - lw),
                            kh=kc * e_end, bh=bc * e_end, wc=jnp.exp(cl)))
        pairs = [(t, hd) for t in range(group) for hd in heads]
        ath = [pre[t]['at'][:, sls[hd]] for t, hd in pairs]
        vh = [pre[t]['vc'][:, sls[hd]] for t, hd in pairs]
        rth = [pre[t]['rt'][:, sls[hd]] for t, hd in pairs]
        ab = [_mm_nt(jnp.concatenate([ath[p], rth[p]], axis=0).astype(BF16),
                     jnp.concatenate([pre[t]['kt'][:, sls[hd]], pre[t]['bt'][:, sls[hd]]], axis=0).astype(BF16))
              for p, (t, hd) in enumerate(pairs)]
        a_ak = [jnp.where(strict, m[:c, :c], 0.0) for m in ab]
        n_ab = [jnp.where(strict, -m[:c, c:], 0.0) for m in ab]
        a_r = [jnp.concatenate([jnp.where(causal, m[c:, :c], 0.0), jnp.where(causal, m[c:, c:], 0.0)],
                               axis=1).astype(BF16) for m in ab]
        akv = [_mm(a_ak[p].astype(BF16), vh[p].astype(BF16)) for p in range(len(pairs))]
        sol = _solve_unit_lower_each(
            n_ab, [jnp.concatenate([ath[p], akv[p]], axis=1) for p in range(len(pairs))], per_seq)
        kb = [jnp.concatenate([pre[t]['kh'][:, sls[hd]], pre[t]['bh'][:, sls[hd]]], axis=0).astype(BF16)
              for t, hd in pairs]
        for t in range(group):
            rows, seqs, wc = pre[t]['rows'], pre[t]['seqs'], pre[t]['wc']
            ps = [t * RW_HEADS + hd for hd in heads]
            if per_seq:
                st = [si_ref[seqs, hd].reshape(SEQ_PER_CHUNK * hd_w, hd_w) for hd in heads]
            else:
                st = [st_s[hd] for hd in heads]
            pr = [_mm_nt(jnp.concatenate([sol[p][:, :hd_w], rth[p]], axis=0).astype(BF16), st[hd].astype(BF16))
                  for hd, p in zip(heads, ps)]
            if per_seq:
                u = [_pick_own_seq(pr[hd][:c], hd_w) + sol[p][:, hd_w:] for hd, p in zip(heads, ps)]
                yst = [_pick_own_seq(pr[hd][c:], hd_w) for hd in heads]
            else:
                u = [pr[hd][:c] + sol[p][:, hd_w:] for hd, p in zip(heads, ps)]
                yst = [pr[hd][c:] for hd in heads]
            vu = [jnp.concatenate([vh[p], u[hd]], axis=0) for hd, p in zip(heads, ps)]
            ya = [_mm(a_r[p], vu[hd].astype(BF16)) for hd, p in zip(heads, ps)]
            if per_seq:
                upd = [_mm_tn(jnp.where(big_mask, jnp.concatenate([vu[hd]] * SEQ_PER_CHUNK, axis=1), 0.0
                                        ).astype(BF16), kb[p]) for hd, p in zip(heads, ps)]
            else:
                upd = [_mm_tn(vu[hd].astype(BF16), kb[p]) for hd, p in zip(heads, ps)]
            for hd in heads:
                if per_seq:
                    wcb = jnp.concatenate(
                        [jnp.broadcast_to(wc[SUBLANES * s:SUBLANES * s + 1, sls[hd]], (hd_w, hd_w))
                         for s in range(SEQ_PER_CHUNK)], axis=0)
                    so_ref[seqs, hd] = (st[hd] * wcb + upd[hd]).reshape(SEQ_PER_CHUNK, hd_w, hd_w)
                else:
                    st_s[hd] = st[hd] * wc[:, sls[hd]] + upd[hd]
                y_s[rows, sls[hd]] = yst[hd] + ya[hd]
        return carry

    lax.fori_loop(0, tm // (c * group), chunk_group, 0)

    if not per_seq:
        so_ref[0] = st_s[...]

    y = y_s[...]
    mean = _seg64_sum(y) * (1.0 / hd_w)
    yc = y - mean
    var = _seg64_sum(yc * yc) * (1.0 / hd_w)
    yn = yc * lax.rsqrt(var + RW_GN_EPS) * lnw_ref[...] + lnb_ref[...]
    bonus = _seg64_sum(r_s[...] * k_s[...] * rk_ref[...]) * v_s[...]
    y_ref[...] = ((yn + bonus) * gate).astype(BF16)


def _rwkv_param_specs():
    row = _const_spec((1, RW_WIDTH))
    return [_const_spec((1, RW_PCOLS)), row, _const_spec((LANES, RW_WIDTH)), row, _const_spec((LANES, RW_WIDTH)),
            _const_spec((RW_GATE, RW_WIDTH)), row, row, row, row, row]


def _rwkv_scratch(tm):
    return [pltpu.VMEM((tm, RW_WIDTH), F32) for _ in range(7)]


def _rwkv_prompt(proj, rp, nb, seq, tm):
    nt = seq // tm
    cb = P_RW // RW_PCOLS
    y, st = pl.pallas_call(
        functools.partial(_rwkv_kernel, per_seq=False, tm=tm),
        grid=(nb, nt),
        in_specs=[pl.BlockSpec((tm, RW_PCOLS), lambda b, t: (b * nt + t, cb))] + _rwkv_param_specs(),
        out_specs=[pl.BlockSpec((tm, RW_WIDTH), lambda b, t: (b * nt + t, 0)),
                   pl.BlockSpec((1, RW_HEADS, RW_HD, RW_HD), lambda b, t: (b, 0, 0, 0))],
        out_shape=[jax.ShapeDtypeStruct((nb * seq, RW_WIDTH), BF16),
                   jax.ShapeDtypeStruct((nb, RW_HEADS, RW_HD, RW_HD), F32)],
        scratch_shapes=_rwkv_scratch(tm) + [pltpu.VMEM((SUBLANES, RW_PCOLS), F32),
                                            pltpu.VMEM((RW_HEADS, RW_HD, RW_HD), F32)],
        compiler_params=_cparams(("parallel", "arbitrary")),
        name="rwkv_prompt",
    )(proj, *rp)
    return y, st


def _rwkv_sample(proj, shift0, st0, layer, rp, row0, nseq, tm):
    ns = tm // SUBLANES
    cb = P_RW // RW_PCOLS
    r0 = row0 // tm
    st_spec = pl.BlockSpec((ns, RW_HEADS, RW_HD, RW_HD), lambda i: (i, 0, 0, 0))
    st_in = _layer_spec((ns, RW_HEADS, RW_HD, RW_HD), lambda i: (i, 0, 0, 0), layer)
    y, st = pl.pallas_call(
        functools.partial(_rwkv_kernel, per_seq=True, tm=tm),
        grid=(nseq // ns,),
        in_specs=[pl.BlockSpec((tm, RW_PCOLS), lambda i: (r0 + i, cb)),
                  pl.BlockSpec((ns, 1, RW_PCOLS), lambda i: (i, 0, 0)), st_in] + _rwkv_param_specs(),
        out_specs=[pl.BlockSpec((tm, RW_WIDTH), lambda i: (i, 0)), st_spec],
        out_shape=[jax.ShapeDtypeStruct((nseq * SUBLANES, RW_WIDTH), BF16),
                   jax.ShapeDtypeStruct((nseq, RW_HEADS, RW_HD, RW_HD), F32)],
        scratch_shapes=_rwkv_scratch(tm),
        compiler_params=_cparams(("parallel",)),
        name="rwkv_sample",
    )(proj, shift0, st0, *rp)
    return y, st


def _rw_pad_cols(x):
    z = jnp.zeros(x.shape[:-1] + (LANES - RW_LORA,), x.dtype)
    a, b = 3 * RW_WIDTH, 3 * RW_WIDTH + RW_LORA
    return jnp.concatenate([x[..., :a], x[..., a:b], z, x[..., b:b + RW_LORA], z, x[..., b + RW_LORA:]], axis=-1)


def _rw_unpad_cols(x):
    a = 3 * RW_WIDTH
    return jnp.concatenate([x[..., :a + RW_LORA], x[..., a + LANES:a + LANES + RW_LORA], x[..., a + 2 * LANES:]],
                           axis=-1)


def _rwkv_params(mu, w0, w2, a0, a2, g2, k_k, k_a, r_k, ln_w, ln_b):
    row = lambda t: t.reshape(1, RW_WIDTH)
    padr = lambda t: jnp.concatenate([t, jnp.zeros((LANES - RW_LORA, RW_WIDTH), t.dtype)], axis=0).astype(BF16)
    return (_rw_pad_cols(mu).reshape(1, RW_PCOLS), row(w0), padr(w2), row(a0), padr(a2), g2.astype(BF16),
            row(k_k), row(k_a), row(r_k), row(ln_w), row(ln_b))


def _gdn_kernel(*refs, per_seq, tm):
    if per_seq:
        (q_ref, k_ref, v_ref, z_ref, ba_ref, fq_ref, fk_ref, fv_ref, si_ref, cwq_ref, cwk_ref, cwv_ref,
         al_ref, dt_ref, ng_ref, y_ref, so_ref, q_s, k_s, v_s, ba_s, o_s) = refs
        conv_state = (fq_ref, fk_ref, fv_ref)
    else:
        (q_ref, k_ref, v_ref, z_ref, ba_ref, cwq_ref, cwk_ref, cwv_ref, al_ref, dt_ref, ng_ref,
         y_ref, so_ref, q_s, k_s, v_s, ba_s, o_s, pq_s, pk_s, pv_s, st_s) = refs
        prevs = (pq_s, pk_s, pv_s)

        @pl.when(pl.program_id(1) == 0)
        def _():
            for p in prevs:
                p[...] = jnp.zeros_like(p)
            st_s[...] = jnp.zeros_like(st_s)
    c, hw = CHUNK, GD_HD

    def conv_act(idx, u_ref, cw_ref):
        u = u_ref[...]
        if per_seq:
            f = conv_state[idx][...]
            sh = [_shift_seq8(u, d, f) for d in (1, 2, 3)]
        else:
            p = prevs[idx][...]
            sh = [_shift_chain(u, d, p) for d in (1, 2, 3)]
            prevs[idx][...] = u[tm - SUBLANES:, :]
        x = cw_ref[3:4, :] * u + cw_ref[2:3, :] * sh[0] + cw_ref[1:2, :] * sh[1] + cw_ref[0:1, :] * sh[2]
        return x * _sigmoid(x)

    def l2n(x):
        outs = []
        for hd in range(GD_HEADS):
            xh = x[:, hw * hd:hw * (hd + 1)]
            outs.append(xh * lax.rsqrt(jnp.sum(xh * xh, axis=-1, keepdims=True) + 1e-6))
        return jnp.concatenate(outs, axis=1)

    q_s[...] = l2n(conv_act(0, q_ref, cwq_ref)) * (hw ** -0.5)
    k_s[...] = l2n(conv_act(1, k_ref, cwk_ref))
    v_s[...] = conv_act(2, v_ref, cwv_ref)
    ba = ba_ref[...]
    lane = lax.broadcasted_iota(jnp.int32, ba.shape, 1)
    g_all = -jnp.exp(al_ref[...]) * _softplus(ba + dt_ref[...])
    ba_s[...] = jnp.where(lane < GD_HEADS, _sigmoid(ba), g_all)

    strict, causal = _chunk_masks(per_seq)
    if per_seq:
        big_mask = _seq_block_mask(c, hw)

    heads = range(GD_HEADS)
    sls = [slice(hw * hd, hw * (hd + 1)) for hd in heads]
    group = _chunks_per_iteration(tm // c)

    def chunk_group(gidx, carry):
        pre = []
        for t in range(group):
            idx = gidx * group + t
            rows = pl.ds(pl.multiple_of(idx * c, c), c)
            bg = ba_s[rows, :]
            gc_all = _cumsum_rows(bg, groupwise=per_seq)
            pre.append(dict(rows=rows, seqs=pl.ds(idx * SEQ_PER_CHUNK, SEQ_PER_CHUNK), bg=bg, gc_all=gc_all,
                            gc_t=gc_all.T, gl_all=_group_last(gc_all) if per_seq else gc_all[c - 1:c, :],
                            qa=q_s[rows, :], ka=k_s[rows, :], va=v_s[rows, :]))
        pairs = [(t, hd) for t in range(group) for hd in heads]
        npair = len(pairs)
        beta = [pre[t]['bg'][:, hd:hd + 1] for t, hd in pairs]
        gc = [pre[t]['gc_all'][:, GD_HEADS + hd:GD_HEADS + hd + 1] for t, hd in pairs]
        gl = [pre[t]['gl_all'][:, GD_HEADS + hd:GD_HEADS + hd + 1] for t, hd in pairs]
        dec = [jnp.exp(jnp.where(causal, gc[p] - pre[t]['gc_t'][GD_HEADS + hd:GD_HEADS + hd + 1, :], -1e30))
               for p, (t, hd) in enumerate(pairs)]
        qh = [pre[t]['qa'][:, sls[hd]] for t, hd in pairs]
        kh = [pre[t]['ka'][:, sls[hd]] for t, hd in pairs]
        kb = [kh[p] * beta[p] for p in range(npair)]
        eg = [jnp.exp(g) for g in gc]
        kq = [_mm_nt(jnp.concatenate([kb[p], qh[p]], axis=0).astype(BF16), kh[p].astype(BF16))
              for p in range(npair)]
        low = [jnp.where(strict, kq[p][:c] * dec[p], 0.0) for p in range(npair)]
        intra = [(kq[p][c:] * dec[p]).astype(BF16) for p in range(npair)]
        sol = _solve_unit_lower_each(
            low, [jnp.concatenate([pre[t]['va'][:, sls[hd]] * beta[p], kb[p] * eg[p]], axis=1)
                  for p, (t, hd) in enumerate(pairs)], per_seq)
        wq = [jnp.concatenate([sol[p][:, hw:], qh[p] * eg[p]], axis=0).astype(BF16) for p in range(npair)]
        kdec = [kh[p] * jnp.exp(gl[p] - gc[p]) for p in range(npair)]
        for t in range(group):
            rows, seqs = pre[t]['rows'], pre[t]['seqs']
            ps = [t * GD_HEADS + hd for hd in heads]
            if per_seq:
                st3 = [si_ref[seqs, hd] for hd in heads]
                st_rhs = [jnp.concatenate([s3[s] for s in range(SEQ_PER_CHUNK)], axis=1).astype(BF16)
                          for s3 in st3]
            else:
                st = [st_s[hd] for hd in heads]
                st_rhs = [s.astype(BF16) for s in st]
            ws = [_mm(wq[p], st_rhs[hd]) for hd, p in zip(heads, ps)]
            if per_seq:
                v_new = [sol[p][:, :hw] - _pick_own_seq(ws[hd][:c], hw) for hd, p in zip(heads, ps)]
                o_st = [_pick_own_seq(ws[hd][c:], hw) for hd in heads]
            else:
                v_new = [sol[p][:, :hw] - ws[hd][:c] for hd, p in zip(heads, ps)]
                o_st = [ws[hd][c:] for hd in heads]
            vnb = [v.astype(BF16) for v in v_new]
            o_in = [_mm(intra[p], vnb[hd]) for hd, p in zip(heads, ps)]
            if per_seq:
                upd = [_mm_tn(jnp.where(big_mask, jnp.concatenate([kdec[p]] * SEQ_PER_CHUNK, axis=1), 0.0
                                        ).astype(BF16), vnb[hd]) for hd, p in zip(heads, ps)]
            else:
                upd = [_mm_tn(kdec[p].astype(BF16), vnb[hd]) for hd, p in zip(heads, ps)]
            for hd, p in zip(heads, ps):
                if per_seq:
                    egl = jnp.exp(gl[p])
                    mult = jnp.concatenate([jnp.broadcast_to(egl[SUBLANES * s:SUBLANES * s + 1, :], (hw, hw))
                                            for s in range(SEQ_PER_CHUNK)], axis=0)
                    st_new = st3[hd].reshape(SEQ_PER_CHUNK * hw, hw) * mult + upd[hd]
                    so_ref[seqs, hd] = st_new.reshape(SEQ_PER_CHUNK, hw, hw)
                else:
                    st_s[hd] = st[hd] * jnp.exp(gl[p]) + upd[hd]
                o_s[rows, sls[hd]] = o_st[hd] + o_in[hd]
        return carry

    lax.fori_loop(0, tm // (c * group), chunk_group, 0)

    if not per_seq:
        so_ref[0] = st_s[...]

    outs = []
    for hd in range(GD_HEADS):
        sl = slice(hw * hd, hw * (hd + 1))
        o = o_s[:, sl]
        o = o * lax.rsqrt(jnp.mean(o * o, axis=-1, keepdims=True) + NORM_EPS) * ng_ref[...]
        z = z_ref[:, sl]
        outs.append(o * (z * _sigmoid(z)))
    y_ref[...] = jnp.concatenate(outs, axis=1).astype(BF16)


def _gdn_param_specs():
    c2 = lambda shape, j=0: pl.BlockSpec(shape, (lambda *a: (0, j)))
    return [c2((GD_CONV, GD_WIDTH), 0), c2((GD_CONV, GD_WIDTH), 1), c2((GD_CONV, GD_WIDTH), 2),
            c2((1, LANES)), c2((1, LANES)), c2((1, GD_HD))]


def _gdn_scratch(tm):
    return ([pltpu.VMEM((tm, GD_WIDTH), F32) for _ in range(3)] + [pltpu.VMEM((tm, LANES), F32),
                                                                  pltpu.VMEM((tm, GD_WIDTH), F32)])


def _gdn_prompt(proj, proj_sb, gp, nb, seq, tm):
    nt = seq // tm
    qb = P_Q // GD_WIDTH
    blk = lambda j: pl.BlockSpec((tm, GD_WIDTH), lambda b, t: (b * nt + t, j))
    y, st = pl.pallas_call(
        functools.partial(_gdn_kernel, per_seq=False, tm=tm),
        grid=(nb, nt),
        in_specs=[blk(qb), blk(qb + 1), blk(qb + 2), blk(P_Z // GD_WIDTH),
                  pl.BlockSpec((tm, LANES), lambda b, t: (b * nt + t, P_BA // LANES))] + _gdn_param_specs(),
        out_specs=[pl.BlockSpec((tm, GD_WIDTH), lambda b, t: (b * nt + t, 0)),
                   pl.BlockSpec((1, GD_HEADS, GD_HD, GD_HD), lambda b, t: (b, 0, 0, 0))],
        out_shape=[jax.ShapeDtypeStruct((nb * seq, GD_WIDTH), BF16),
                   jax.ShapeDtypeStruct((nb, GD_HEADS, GD_HD, GD_HD), F32)],
        scratch_shapes=_gdn_scratch(tm) + [pltpu.VMEM((SUBLANES, GD_WIDTH), F32) for _ in range(3)]
        + [pltpu.VMEM((GD_HEADS, GD_HD, GD_HD), F32)],
        compiler_params=_cparams(("parallel", "arbitrary")),
        name="gdn_prompt",
    )(proj, proj, proj, proj, proj_sb, gp[0], gp[0], gp[0], *gp[1:])
    return y, st


def _gdn_sample(proj, proj_sb, conv0, st0, layer, gp, row0, nseq, tm):
    ns = tm // SUBLANES
    qb = P_Q // GD_WIDTH
    r0 = row0 // tm
    blk = lambda j: pl.BlockSpec((tm, GD_WIDTH), lambda i: (r0 + i, j))
    frm = lambda j: _layer_spec((ns, GD_CONV - 1, GD_WIDTH), lambda i: (i, 0, j), layer)
    st_spec = pl.BlockSpec((ns, GD_HEADS, GD_HD, GD_HD), lambda i: (i, 0, 0, 0))
    st_in = _layer_spec((ns, GD_HEADS, GD_HD, GD_HD), lambda i: (i, 0, 0, 0), layer)
    y, st = pl.pallas_call(
        functools.partial(_gdn_kernel, per_seq=True, tm=tm),
        grid=(nseq // ns,),
        in_specs=[blk(qb), blk(qb + 1), blk(qb + 2), blk(P_Z // GD_WIDTH),
                  pl.BlockSpec((tm, LANES), lambda i: (r0 + i, P_BA // LANES)),
                  frm(0), frm(1), frm(2), st_in] + _gdn_param_specs(),
        out_specs=[pl.BlockSpec((tm, GD_WIDTH), lambda i: (i, 0)), st_spec],
        out_shape=[jax.ShapeDtypeStruct((nseq * SUBLANES, GD_WIDTH), BF16),
                   jax.ShapeDtypeStruct((nseq, GD_HEADS, GD_HD, GD_HD), F32)],
        scratch_shapes=_gdn_scratch(tm),
        compiler_params=_cparams(("parallel",)),
        name="gdn_sample",
    )(proj, proj, proj, proj, proj_sb, conv0, conv0, conv0, st0, gp[0], gp[0], gp[0], *gp[1:])
    return y, st


def _gdn_params(conv_w, a_log, dt_bias, norm_g):
    pad = lambda t: jnp.concatenate([jnp.zeros((GD_HEADS,), F32), t, jnp.zeros((LANES - 2 * GD_HEADS,), F32)]
                                    ).reshape(1, LANES)
    return (conv_w, pad(a_log), pad(dt_bias), norm_g.reshape(1, GD_HD))


W_IN_RW = S5_WIDTH
W_IN_QKVZ = W_IN_RW + RW_COLS
W_IN_BA = W_IN_QKVZ + 4 * GD_WIDTH
W_IN_GATES = W_IN_BA + 2 * GD_HEADS


def _split_w_in(w):
    w_t = jnp.swapaxes(w, 1, 2)
    zeros = lambda rows: jnp.zeros((w_t.shape[0], rows, w_t.shape[2]), w_t.dtype)
    a = W_IN_RW + 3 * RW_WIDTH + RW_LORA
    w_rw_t = jnp.concatenate([w_t[:, W_IN_RW:a], zeros(LANES - RW_LORA), w_t[:, a:a + RW_LORA],
                              zeros(LANES - RW_LORA), w_t[:, a + RW_LORA:W_IN_QKVZ]], axis=1)
    w_sb_t = jnp.concatenate([w_t[:, :S5_WIDTH], w_t[:, W_IN_BA:W_IN_GATES],
                              zeros(P_SB_COLS - P_BA - 2 * GD_HEADS)], axis=1)
    return w_t, w_rw_t, w_sb_t


def _tail_rows(a, cols, nrows, mp, seq):
    a3 = a.reshape(a.shape[0] // SUBLANES, SUBLANES, a.shape[1])
    g = seq // SUBLANES
    lo = SUBLANES - nrows
    return (a3[g - 1:mp // SUBLANES:g, lo:, cols[0]:cols[1]], a3[mp // SUBLANES:, lo:, cols[0]:cols[1]])


def kernel(x_prompt, x_sample, state_s5_re, state_s5_im, state_rwkv_shift, state_rwkv_wkv, state_gdn_conv, state_gdn, state_ffn_conv, norm1_g, norm2_g, final_norm_g, w_in, s5_lambda_re, s5_lambda_im, s5_b_re, s5_b_im, s5_c_re, s5_c_im, s5_d, s5_log_step, s5_w_glu, rwkv_mu, rwkv_w0, rwkv_w2, rwkv_a0, rwkv_a2, rwkv_g2, rwkv_k_k, rwkv_k_a, rwkv_r_k, rwkv_ln_w, rwkv_ln_b, gdn_conv_w, gdn_a_log, gdn_dt_bias, gdn_norm_g, w_br_s5, w_br_rwkv, w_br_gdn, w_out, ffn_w_up, ffn_conv_w, ffn_conv_b, ffn_w_down):
    nb, seq, d = x_prompt.shape
    ns, sl, _ = x_sample.shape
    assert sl == SUBLANES and d == D_MODEL
    mp, ms = nb * seq, ns * sl
    m = mp + ms
    tm_norm, tm_proj, tm_out = 512, 3072, 512
    x_groups = (x_prompt.reshape(mp, d), x_sample.reshape(ms, d), 0)

    w_t, w_rw_t, w_sb_t = _split_w_in(w_in)
    w_br = (w_br_s5.astype(BF16), w_br_rwkv.astype(BF16), w_br_gdn.astype(BF16))
    w_out_b, w_down_b = w_out.astype(BF16), ffn_w_down.astype(BF16)
    conv_b = ffn_conv_b.reshape(DEPTH, 1, 2 * D_FF)
    shift0 = _rw_pad_cols(state_rwkv_shift)[:, :, None, :]

    new_p = [[] for _ in range(7)]
    new_s = [[] for _ in range(7)]
    for l in range(DEPTH):
        xp, xs, s_off = x_groups
        xn1 = _norm_cast(xp, xs, s_off * (mp // tm_norm), m, mp // tm_norm, norm1_g[l], tm=tm_norm)
        proj_g = _matmul_nt(xn1, w_t, l, W_IN_GATES, 3 * D_MODEL, tm=tm_proj, tn=512)
        proj_rw = _matmul_nt(xn1, w_rw_t, l, 0, RW_PCOLS, tm=tm_proj, tn=512)
        proj_qkvz = _matmul_nt(xn1, w_t, l, W_IN_QKVZ, 4 * GD_WIDTH, tm=tm_proj, tn=512)
        proj_sb = _matmul_nt(xn1, w_sb_t, l, 0, P_SB_COLS, tm=tm_proj, tn=512)

        sp = _s5_params(s5_lambda_re[l], s5_lambda_im[l], s5_b_re[l], s5_b_im[l], s5_c_re[l], s5_c_im[l],
                        s5_d[l], s5_log_step[l], s5_w_glu[l])
        ys5_p, p_re, p_im = _s5_prompt(proj_sb, sp, nb, seq, tm=256)
        ys5_s, s_re, s_im = _s5_sample(proj_sb, state_s5_re[l], state_s5_im[l], sp, mp, ns, tm=128)

        rp = _rwkv_params(rwkv_mu[l], rwkv_w0[l], rwkv_w2[l], rwkv_a0[l], rwkv_a2[l], rwkv_g2[l], rwkv_k_k[l],
                          rwkv_k_a[l], rwkv_r_k[l].reshape(RW_WIDTH), rwkv_ln_w[l], rwkv_ln_b[l])
        yrw_p, p_wkv = _rwkv_prompt(proj_rw, rp, nb, seq, tm=256)
        yrw_s, s_wkv = _rwkv_sample(proj_rw, shift0[l], state_rwkv_wkv, l, rp, mp, ns, tm=128)
        p_shift, s_shift = [_rw_unpad_cols(t[:, 0])
                            for t in _tail_rows(proj_rw, (P_RW, P_RW + RW_PCOLS), 1, mp, seq)]

        gp = _gdn_params(gdn_conv_w[l], gdn_a_log[l], gdn_dt_bias[l], gdn_norm_g[l])
        ygd_p, p_gdn = _gdn_prompt(proj_qkvz, proj_sb, gp, nb, seq, tm=256)
        ygd_s, s_gdn = _gdn_sample(proj_qkvz, proj_sb, state_gdn_conv, state_gdn, l, gp, mp, ns, tm=64)
        p_gconv, s_gconv = _tail_rows(proj_qkvz, (P_Q, P_Q + 3 * GD_WIDTH), GD_CONV - 1, mp, seq)

        merged = _merge((ys5_p, yrw_p, ygd_p), (ys5_s, yrw_s, ygd_s), w_br, l, proj_g, tm=512, tn=1024)
        x, xn = _out_norm(merged, w_out_b, l, xp, xs, s_off * (mp // tm_out), mp // tm_out, norm2_g[l], tm=tm_out)

        h, p_ffn, s_ffn = _ffn_up(xn, ffn_w_up, ffn_conv_w, conv_b, state_ffn_conv, l, nb, seq, ns,
                                  tm=1024, tn=512)
        x = _matmul_res(h, w_down_b, l, x, tm=1536, tn=512)
        x_groups = (x, x, 1)

        for lst, vals in ((new_p, (p_re, p_im, p_shift, p_wkv, p_gconv, p_gdn, p_ffn)),
                          (new_s, (s_re, s_im, s_shift, s_wkv, s_gconv, s_gdn, s_ffn))):
            for acc, val in zip(lst, vals):
                acc.append(val)

    y_p, y_s = _final_norm(x, final_norm_g, mp, tm=512)
    stack = lambda lst: tuple(jnp.stack(v) for v in lst)
    return (y_p.reshape(nb, seq, d), y_s.reshape(ns, sl, d)) + stack(new_p) + stack(new_s)
```

```python
import functools
import math

import jax
import jax.numpy as jnp
from jax import lax
from jax.experimental import pallas as pl
from jax.experimental.pallas import tpu as pltpu

F32 = jnp.float32
BF16 = jnp.bfloat16

SUBLANES = 8
LANES = 128
VMEM_LIMIT = 52 * 1024 * 1024
CHUNK = 64
SEQ_PER_CHUNK = CHUNK // SUBLANES

D_MODEL = 2048
DEPTH = 2
S5_WIDTH = 512
S5_GROUP = 16
S5_GROUPS = 32
S5_STATE = 64
S5_NSTATE = S5_GROUPS * S5_STATE
RW_WIDTH = 512
RW_HEADS = 8
RW_HD = 64
RW_LORA = 96
RW_GATE = 256
RW_COLS = 3 * RW_WIDTH + 2 * RW_LORA + RW_GATE
RW_PCOLS = 2048
RW_GN_EPS = 64e-5
GD_WIDTH = 1024
GD_HEADS = 8
GD_HD = 128
GD_CONV = 4
D_FF = 5632
FFN_CONV = 3
NORM_EPS = 1e-6
P_GATES, P_RW, P_Q, P_Z, P_S5, P_BA, P_SB_COLS = 0, 0, 0, 3072, 0, 512, 1024


def _cparams(sem):
    return pltpu.CompilerParams(dimension_semantics=sem, vmem_limit_bytes=VMEM_LIMIT)


def _mm(a, b):
    return jnp.dot(a, b, preferred_element_type=F32)


def _mm_nt(a, b):
    return lax.dot_general(a, b, (((1,), (1,)), ((), ())), preferred_element_type=F32)


def _mm_tn(a, b):
    return lax.dot_general(a, b, (((0,), (0,)), ((), ())), preferred_element_type=F32)


def _sigmoid(x):
    return 1.0 / (1.0 + jnp.exp(-x))


def _softplus(x):
    return jnp.maximum(x, 0.0) + jnp.log1p(jnp.exp(-jnp.abs(x)))


def _gelu_tanh(x):
    return 0.5 * x * (1.0 + jnp.tanh(math.sqrt(2.0 / math.pi) * (x + 0.044715 * (x * x * x))))


def _row_iota(shape):
    return lax.broadcasted_iota(jnp.int32, shape, 0)


def _shift_chain(x, d, prev8):
    n, ch = x.shape
    x3 = x.reshape(n // SUBLANES, SUBLANES, ch)
    rot = pltpu.roll(x3, d, 1)
    before = jnp.concatenate([pltpu.roll(prev8, d, 0)[None], rot[:-1]], axis=0)
    row = lax.broadcasted_iota(jnp.int32, x3.shape, 1)
    return jnp.where(row < d, before, rot).reshape(n, ch)


def _shift_seq8(x, d, state):
    n, ch = x.shape
    ns, w1, _ = state.shape
    out = pltpu.roll(x.reshape(ns, SUBLANES, ch), d, 1)
    row = lax.broadcasted_iota(jnp.int32, out.shape, 1)
    for r in range(d):
        out = jnp.where(row == r, state[:, w1 - d + r:w1 - d + r + 1, :], out)
    return out.reshape(n, ch)


def _group_last(x):
    c, n = x.shape
    x3 = x.reshape(c // SUBLANES, SUBLANES, n)
    return jnp.broadcast_to(x3[:, SUBLANES - 1:SUBLANES, :], x3.shape).reshape(c, n)


def _cumsum_rows(x, groupwise=False):
    c = x.shape[0]
    row = _row_iota(x.shape) % SUBLANES
    for d in (1, 2, 4):
        x = x + jnp.where(row >= d, pltpu.roll(x, d, 0), 0.0)
    if c > SUBLANES and not groupwise:
        blocks = [x[SUBLANES * i:SUBLANES * (i + 1)] for i in range(c // SUBLANES)]
        for i in range(1, len(blocks)):
            blocks[i] = blocks[i] + blocks[i - 1][SUBLANES - 1:SUBLANES, :]
        x = jnp.concatenate(blocks, axis=0)
    return x


def _split2(x):
    hi = x.astype(BF16)
    return hi, (x - hi.astype(F32)).astype(BF16)


def _hi_lo_f32(x):
    hi = x.astype(BF16).astype(F32)
    return hi, x - hi


def _mm_hilo_each(a_list, b_list):
    lhs, rhs = [], []
    for a, b in zip(a_list, b_list):
        ah, al = _hi_lo_f32(a)
        bh, bl = _hi_lo_f32(b)
        lhs.append(jnp.concatenate([ah, ah, al], axis=1).astype(BF16))
        rhs.append(jnp.concatenate([bh, bl, bh], axis=0).astype(BF16))
    return [_mm(l, r) for l, r in zip(lhs, rhs)]


def _solve_unit_lower_each(lows, rhss, diag_only):
    c = lows[0].shape[0]
    nb = c // SUBLANES
    nh = len(lows)
    n = rhss[0].shape[1]
    blk = lambda a, i: a[SUBLANES * i:SUBLANES * (i + 1)]
    rb = [[blk(r, i) for i in range(nb)] for r in rhss]
    with_off = nb > 1 and not diag_only
    same = (lax.broadcasted_iota(jnp.int32, (c, c), 0) // SUBLANES
            == lax.broadcasted_iota(jnp.int32, (c, c), 1) // SUBLANES)
    if with_off:
        ob = [[blk(jnp.where(same, 0.0, low), i) for i in range(nb)] for low in lows]
    sel = (lax.broadcasted_iota(jnp.int32, (c, (SUBLANES - 1) * LANES), 0) % SUBLANES
           == lax.broadcasted_iota(jnp.int32, (c, (SUBLANES - 1) * LANES), 1) // LANES)
    sel = jnp.where(sel, 1.0, 0.0).astype(BF16)
    sel2 = jnp.concatenate([sel, sel], axis=0)
    diag = [_hi_lo_f32(jnp.where(same, low, 0.0)) if not diag_only else _hi_lo_f32(low) for low in lows]
    cols = [_mm(jnp.concatenate(d, axis=1).astype(BF16), sel2) for d in diag]
    reps = n // LANES
    for j in range(SUBLANES - 1):
        for h in range(nh):
            cj = cols[h][:, LANES * j:LANES * (j + 1)]
            cj_n = jnp.concatenate([cj] * reps, axis=1) if reps > 1 else cj
            for i in range(nb):
                rb[h][i] = rb[h][i] - blk(cj_n, i) * rb[h][i][j:j + 1, :]
                if with_off and i > 0:
                    ob[h][i] = ob[h][i] - blk(cj, i)[:, :c] * ob[h][i][j:j + 1, :]
    r1 = [jnp.concatenate(b, axis=0) if nb > 1 else b[0] for b in rb]
    if not with_off:
        return r1
    n1 = [jnp.concatenate(b, axis=0) for b in ob]
    n2 = _mm_hilo_each(n1, n1)
    t = _mm_hilo_each(n1 + n2, r1 + n2)
    y = [r - nr for r, nr in zip(r1, t[:nh])]
    n4 = t[nh:]
    y = [a + b for a, b in zip(y, _mm_hilo_each(n2, y))]
    return [a + b for a, b in zip(y, _mm_hilo_each(n4, y))]


def _chunks_per_iteration(n_chunks):
    del n_chunks
    return 1


def _chunk_masks(per_seq):
    ri = lax.broadcasted_iota(jnp.int32, (CHUNK, CHUNK), 0)
    ci = lax.broadcasted_iota(jnp.int32, (CHUNK, CHUNK), 1)
    strict, causal = ri > ci, ri >= ci
    if per_seq:
        same = (ri // SUBLANES) == (ci // SUBLANES)
        strict, causal = jnp.logical_and(strict, same), jnp.logical_and(causal, same)
    return strict, causal


def _seq_block_mask(rows, width):
    r = lax.broadcasted_iota(jnp.int32, (rows, SEQ_PER_CHUNK * width), 0)
    l = lax.broadcasted_iota(jnp.int32, (rows, SEQ_PER_CHUNK * width), 1)
    return ((r % CHUNK) // SUBLANES) == (l // width)


def _pick_own_seq(a, width):
    return jnp.concatenate([a[SUBLANES * s:SUBLANES * (s + 1), width * s:width * (s + 1)]
                            for s in range(SEQ_PER_CHUNK)], axis=0)


def _rmsnorm_rows(x_ref, g_ref, o_ref, rows, chunk=128):
    def body(i, carry):
        r0 = pl.multiple_of(i * chunk, chunk)
        x = x_ref[pl.ds(r0, chunk), :]
        ms = jnp.mean(x * x, axis=-1, keepdims=True)
        o_ref[pl.ds(r0, chunk), :] = (x * lax.rsqrt(ms + NORM_EPS) * g_ref[...]).astype(o_ref.dtype)
        return carry
    lax.fori_loop(0, rows // chunk, body, 0)


def _two_group_specs(tm, width, prompt_tiles, sample_off, pipeline_mode=None):
    p = pl.BlockSpec((tm, width), lambda i, *_: (jnp.minimum(i, prompt_tiles - 1), 0), pipeline_mode=pipeline_mode)
    s = pl.BlockSpec((tm, width), lambda i, *_: (jnp.maximum(i - prompt_tiles, 0) + sample_off, 0),
                     pipeline_mode=pipeline_mode)
    return p, s


def _layer_spec(shape, index_map, layer):
    return pl.BlockSpec((None,) + shape, lambda *a: (layer,) + tuple(index_map(*a)))


def _norm_cast_kernel(xp_ref, xs_ref, g_ref, o_ref, *, rows, prompt_tiles):
    i = pl.program_id(0)

    @pl.when(i < prompt_tiles)
    def _():
        _rmsnorm_rows(xp_ref, g_ref, o_ref, rows)

    @pl.when(i >= prompt_tiles)
    def _():
        _rmsnorm_rows(xs_ref, g_ref, o_ref, rows)


def _norm_cast(xp, xs, sample_off, m, prompt_tiles, g, tm):
    k = xp.shape[1]
    psp, ssp = _two_group_specs(tm, k, prompt_tiles, sample_off)
    return pl.pallas_call(
        functools.partial(_norm_cast_kernel, rows=tm, prompt_tiles=prompt_tiles),
        grid=(m // tm,),
        in_specs=[psp, ssp, pl.BlockSpec((1, k), lambda i: (0, 0))],
        out_specs=pl.BlockSpec((tm, k), lambda i: (i, 0)),
        out_shape=jax.ShapeDtypeStruct((m, k), BF16),
        compiler_params=_cparams(("arbitrary",)),
        name="norm_cast",
    )(xp, xs, g.reshape(1, k))


def _matmul_nt_kernel(a_ref, w_ref, o_ref):
    o_ref[...] = _mm_nt(a_ref[...], w_ref[0].astype(BF16))


def _matmul_nt(a, w_t, layer, row0, n, tm, tn):
    m, k = a.shape
    assert row0 % SUBLANES == 0 and n % tn == 0
    return pl.pallas_call(
        _matmul_nt_kernel,
        grid=(m // tm, n // tn),
        in_specs=[pl.BlockSpec((tm, k), lambda i, j: (i, 0), pipeline_mode=pl.Buffered(1)),
                  pl.BlockSpec((pl.Element(1), pl.Element(tn), pl.Element(k)),
                               lambda i, j: (layer, pl.multiple_of(row0 + j * tn, SUBLANES), 0))],
        out_specs=pl.BlockSpec((tm, tn), lambda i, j: (i, j)),
        out_shape=jax.ShapeDtypeStruct((m, n), F32),
        compiler_params=_cparams(("parallel", "arbitrary")),
        name="proj_matmul",
    )(a, w_t)


def _matmul_res_kernel(a_ref, w_ref, r_ref, o_ref):
    o_ref[...] = r_ref[...] + _mm(a_ref[...], w_ref[...])


def _matmul_res(a, w, layer, res, tm, tn):
    m, k = a.shape
    n = w.shape[-1]
    return pl.pallas_call(
        _matmul_res_kernel,
        grid=(m // tm, n // tn),
        in_specs=[pl.BlockSpec((tm, k), lambda i, j: (i, 0), pipeline_mode=pl.Buffered(1)),
                  _layer_spec((k, tn), lambda i, j: (0, j), layer),
                  pl.BlockSpec((tm, tn), lambda i, j: (i, j))],
        out_specs=pl.BlockSpec((tm, tn), lambda i, j: (i, j)),
        out_shape=jax.ShapeDtypeStruct((m, n), F32),
        compiler_params=_cparams(("parallel", "arbitrary")),
        name="matmul_res",
    )(a, w, res)


def _out_norm_kernel(a_ref, w_ref, rp_ref, rs_ref, g_ref, x_ref, xn_ref, *, rows, prompt_tiles):
    res = jnp.where(pl.program_id(0) < prompt_tiles, rp_ref[...], rs_ref[...])
    x_ref[...] = res + _mm(a_ref[...], w_ref[...])
    _rmsnorm_rows(x_ref, g_ref, xn_ref, rows)


def _out_norm(a, w, layer, rp, rs, sample_off, prompt_tiles, g, tm):
    m, k = a.shape
    n = w.shape[-1]
    psp, ssp = _two_group_specs(tm, n, prompt_tiles, sample_off)
    return pl.pallas_call(
        functools.partial(_out_norm_kernel, rows=tm, prompt_tiles=prompt_tiles),
        grid=(m // tm,),
        in_specs=[pl.BlockSpec((tm, k), lambda i: (i, 0)),
                  _layer_spec((k, n), lambda i: (0, 0), layer),
                  psp, ssp, pl.BlockSpec((1, n), lambda i: (0, 0))],
        out_specs=[pl.BlockSpec((tm, n), lambda i: (i, 0)), pl.BlockSpec((tm, n), lambda i: (i, 0))],
        out_shape=[jax.ShapeDtypeStruct((m, n), F32), jax.ShapeDtypeStruct((m, n), BF16)],
        compiler_params=_cparams(("parallel",)),
        name="out_norm",
    )(a, w, rp, rs, g.reshape(1, n))


def _merge_kernel(y1p, y2p, y3p, y1s, y2s, y3s, w1_ref, w2_ref, w3_ref, g1_ref, g2_ref, g3_ref, o_ref, *,
                  prompt_tiles, tm, rc):
    is_p = pl.program_id(0) < prompt_tiles
    for r in range(tm // rc):
        rows = slice(r * rc, (r + 1) * rc)
        pick = lambda p, s: jnp.where(is_p, p[rows, :], s[rows, :])
        acc = _sigmoid(g1_ref[rows, :]) * _mm(pick(y1p, y1s), w1_ref[...])
        acc = acc + _sigmoid(g2_ref[rows, :]) * _mm(pick(y2p, y2s), w2_ref[...])
        acc = acc + _sigmoid(g3_ref[rows, :]) * _mm(pick(y3p, y3s), w3_ref[...])
        o_ref[rows, :] = acc.astype(BF16)


def _merge(ys_p, ys_s, ws, layer, proj, tm, tn):
    mp, ms = ys_p[0].shape[0], ys_s[0].shape[0]
    n = ws[0].shape[-1]
    nj = n // tn
    pt = mp // tm
    gate_spec = lambda b: pl.BlockSpec((tm, tn), lambda i, j: (i, P_GATES // tn + b * nj + j))
    y_specs = [_two_group_specs(tm, y.shape[1], pt, 0) for y in ys_p]
    w_spec = lambda w: _layer_spec((w.shape[1], tn), lambda i, j: (0, j), layer)
    return pl.pallas_call(
        functools.partial(_merge_kernel, prompt_tiles=pt, tm=tm, rc=128),
        grid=((mp + ms) // tm, nj),
        in_specs=[s[0] for s in y_specs] + [s[1] for s in y_specs] + [w_spec(w) for w in ws]
        + [gate_spec(0), gate_spec(1), gate_spec(2)],
        out_specs=pl.BlockSpec((tm, tn), lambda i, j: (i, j)),
        out_shape=jax.ShapeDtypeStruct((mp + ms, n), BF16),
        compiler_params=_cparams(("parallel", "arbitrary")),
        name="merge",
    )(*ys_p, *ys_s, *ws, proj, proj, proj)


def _final_norm_kernel(x_ref, g_ref, yp_ref, ys_ref, *, prompt_tiles):
    x = x_ref[...]
    ms = jnp.mean(x * x, axis=-1, keepdims=True)
    y = x * lax.rsqrt(ms + NORM_EPS) * g_ref[...]
    i = pl.program_id(0)

    @pl.when(i < prompt_tiles)
    def _():
        yp_ref[...] = y

    @pl.when(i >= prompt_tiles)
    def _():
        ys_ref[...] = y


def _final_norm(x, g, mp, tm):
    m, k = x.shape
    pt = mp // tm
    psp, ssp = _two_group_specs(tm, k, pt, 0)
    return pl.pallas_call(
        functools.partial(_final_norm_kernel, prompt_tiles=pt),
        grid=(m // tm,),
        in_specs=[pl.BlockSpec((tm, k), lambda i: (i, 0)), pl.BlockSpec((1, k), lambda i: (0, 0))],
        out_specs=[psp, ssp],
        out_shape=[jax.ShapeDtypeStruct((mp, k), F32), jax.ShapeDtypeStruct((m - mp, k), F32)],
        compiler_params=_cparams(("arbitrary",)),
        name="final_norm",
    )(x, g.reshape(1, k))


def _ffn_up_kernel(xn_ref, wg_ref, wv_ref, cwg_ref, cwv_ref, bg_ref, bv_ref, sg_ref, sv_ref,
                   h_ref, pg_ref, pv_ref, og_ref, ov_ref, wgb, wvb, cg, cv, *, tm, rc, prompt_tiles, tiles_per_seq):
    i = pl.program_id(1)
    w1 = FFN_CONV - 1

    @pl.when(i == 0)
    def _():
        wgb[...] = wg_ref[...].astype(BF16)
        wvb[...] = wv_ref[...].astype(BF16)

    def conv(u, s1, s2, cw_ref, b_ref):
        return cw_ref[2:3, :] * u + cw_ref[1:2, :] * s1 + cw_ref[0:1, :] * s2 + b_ref[...]

    def up(rows):
        x = xn_ref[rows, :]
        return _mm(x, wgb[...]), _mm(x, wvb[...])

    @pl.when(i < prompt_tiles)
    def _():
        first = (i % tiles_per_seq) == 0
        pg = jnp.where(first, 0.0, cg[...])
        pv = jnp.where(first, 0.0, cv[...])
        for r in range(tm // rc):
            rows = slice(r * rc, (r + 1) * rc)
            ug, uv = up(rows)
            gate = conv(ug, _shift_chain(ug, 1, pg), _shift_chain(ug, 2, pg), cwg_ref, bg_ref)
            val = conv(uv, _shift_chain(uv, 1, pv), _shift_chain(uv, 2, pv), cwv_ref, bv_ref)
            h_ref[rows, :] = (gate * _sigmoid(gate) * val).astype(BF16)
            pg, pv = ug[rc - SUBLANES:, :], uv[rc - SUBLANES:, :]
        cg[...] = pg
        cv[...] = pv
        pg_ref[0] = pg[SUBLANES - w1:, :]
        pv_ref[0] = pv[SUBLANES - w1:, :]

    @pl.when(i >= prompt_tiles)
    def _():
        nsr = rc // SUBLANES
        for r in range(tm // rc):
            rows = slice(r * rc, (r + 1) * rc)
            seqs = slice(r * nsr, (r + 1) * nsr)
            ug, uv = up(rows)
            sg, sv = sg_ref[seqs], sv_ref[seqs]
            gate = conv(ug, _shift_seq8(ug, 1, sg), _shift_seq8(ug, 2, sg), cwg_ref, bg_ref)
            val = conv(uv, _shift_seq8(uv, 1, sv), _shift_seq8(uv, 2, sv), cwv_ref, bv_ref)
            h_ref[rows, :] = (gate * _sigmoid(gate) * val).astype(BF16)
            og_ref[seqs] = ug.reshape(nsr, SUBLANES, ug.shape[1])[:, SUBLANES - w1:, :]
            ov_ref[seqs] = uv.reshape(nsr, SUBLANES, uv.shape[1])[:, SUBLANES - w1:, :]


def _ffn_up(xn, w_up, conv_w, conv_b, state, layer, nb, seq, ns, tm, tn):
    m, k = xn.shape
    mp = nb * seq
    assert ns * SUBLANES == tm and (m - mp) == tm and seq % tm == 0
    nj, pt, tps = D_FF // tn, mp // tm, seq // tm
    w1 = FFN_CONV - 1
    wsp = lambda off: _layer_spec((k, tn), lambda j, i: (0, off + j), layer)
    par = lambda rows, off: _layer_spec((rows, tn), lambda j, i: (0, off + j), layer)
    stsp = lambda off: _layer_spec((ns, w1, tn), lambda j, i: (0, 0, off + j), layer)
    psp = pl.BlockSpec((1, w1, tn), lambda j, i: (jnp.minimum(i, pt - 1) // tps, 0, j))
    osp = pl.BlockSpec((ns, w1, tn), lambda j, i: (0, 0, j))
    h, pg, pv, og, ov = pl.pallas_call(
        functools.partial(_ffn_up_kernel, tm=tm, rc=256, prompt_tiles=pt, tiles_per_seq=tps),
        grid=(nj, m // tm),
        in_specs=[pl.BlockSpec((tm, k), lambda j, i: (i, 0)), wsp(0), wsp(nj),
                  par(FFN_CONV, 0), par(FFN_CONV, nj), par(1, 0), par(1, nj), stsp(0), stsp(nj)],
        out_specs=[pl.BlockSpec((tm, tn), lambda j, i: (i, j)), psp, psp, osp, osp],
        out_shape=[jax.ShapeDtypeStruct((m, D_FF), BF16),
                   jax.ShapeDtypeStruct((nb, w1, D_FF), F32), jax.ShapeDtypeStruct((nb, w1, D_FF), F32),
                   jax.ShapeDtypeStruct((ns, w1, D_FF), F32), jax.ShapeDtypeStruct((ns, w1, D_FF), F32)],
        scratch_shapes=[pltpu.VMEM((k, tn), BF16), pltpu.VMEM((k, tn), BF16),
                        pltpu.VMEM((SUBLANES, tn), F32), pltpu.VMEM((SUBLANES, tn), F32)],
        compiler_params=_cparams(("arbitrary", "arbitrary")),
        name="ffn_up",
    )(xn, w_up, w_up, conv_w, conv_w, conv_b, conv_b, state, state)
    return h, jnp.concatenate([pg, pv], axis=-1), jnp.concatenate([og, ov], axis=-1)


S5_Q = 4
S5_QW = S5_NSTATE // S5_Q


def _s5_kernel(*refs, chained, tm):
    if chained:
        (u_ref, bre_ref, bim_ref, cre_ref, cim_ref, mre_ref, mim_ref, pre_ref, pim_ref, d_ref, wglu_ref,
         y_ref, xlre_ref, xlim_ref, sre, sim, car_re, car_im) = refs
    else:
        (u_ref, x0re_ref, x0im_ref, bre_ref, bim_ref, cre_ref, cim_ref, mre_ref, mim_ref, pre_ref, pim_ref,
         d_ref, wglu_ref, y_ref, xlre_ref, xlim_ref, sre, sim) = refs

    u = u_ref[...]
    ub = u.astype(BF16)
    for q in range(S5_Q):
        uq = ub[:, LANES * q:LANES * (q + 1)]
        sre[:, S5_QW * q:S5_QW * (q + 1)] = _mm(uq, bre_ref[q])
        sim[:, S5_QW * q:S5_QW * (q + 1)] = _mm(uq, bim_ref[q])

    if chained:
        @pl.when(pl.program_id(1) == 0)
        def _():
            car_re[...] = jnp.zeros_like(car_re)
            car_im[...] = jnp.zeros_like(car_im)

    def blk(i, carry):
        r0 = pl.multiple_of(i * SUBLANES, SUBLANES)
        for q in range(S5_Q):
            sl = slice(S5_QW * q, S5_QW * (q + 1))
            xr = sre[pl.ds(r0, SUBLANES), sl]
            xi = sim[pl.ds(r0, SUBLANES), sl]
            for li, d in enumerate((1, 2, 4)):
                mr, mi = mre_ref[li, :, sl], mim_ref[li, :, sl]
                sr, si = pltpu.roll(xr, d, 0), pltpu.roll(xi, d, 0)
                xr, xi = xr + (mr * sr - mi * si), xi + (mr * si + mi * sr)
            if chained:
                cr, ci = car_re[:, sl], car_im[:, sl]
            else:
                cr, ci = x0re_ref[pl.ds(i, 1), sl], x0im_ref[pl.ds(i, 1), sl]
            pr, pi_ = pre_ref[:, sl], pim_ref[:, sl]
            xr, xi = xr + (pr * cr - pi_ * ci), xi + (pr * ci + pi_ * cr)
            sre[pl.ds(r0, SUBLANES), sl] = xr
            sim[pl.ds(r0, SUBLANES), sl] = xi
            if chained:
                car_re[:, sl] = xr[SUBLANES - 1:SUBLANES, :]
                car_im[:, sl] = xi[SUBLANES - 1:SUBLANES, :]
            else:
                xlre_ref[pl.ds(i, 1), sl] = xr[SUBLANES - 1:SUBLANES, :]
                xlim_ref[pl.ds(i, 1), sl] = xi[SUBLANES - 1:SUBLANES, :]
        return carry

    lax.fori_loop(0, tm // SUBLANES, blk, 0)

    if chained:
        xlre_ref[0] = car_re[...]
        xlim_ref[0] = car_im[...]

    ys = []
    for q in range(S5_Q):
        sl = slice(S5_QW * q, S5_QW * (q + 1))
        ys.append(_mm(sre[:, sl].astype(BF16), cre_ref[q]) - _mm(sim[:, sl].astype(BF16), cim_ref[q]))
    y = jnp.concatenate(ys, axis=1) + d_ref[...] * u
    y = _gelu_tanh(y)
    y = y * _sigmoid(_mm(y.astype(BF16), wglu_ref[...]))
    y_ref[...] = y.astype(BF16)


def _const_spec(shape):
    zeros = (0,) * len(shape)
    return pl.BlockSpec(shape, lambda *a: zeros)


def _s5_param_specs():
    return [_const_spec((S5_Q, LANES, S5_QW)), _const_spec((S5_Q, LANES, S5_QW)),
            _const_spec((S5_Q, S5_QW, LANES)), _const_spec((S5_Q, S5_QW, LANES)),
            _const_spec((3, SUBLANES, S5_NSTATE)), _const_spec((3, SUBLANES, S5_NSTATE)),
            _const_spec((SUBLANES, S5_NSTATE)), _const_spec((SUBLANES, S5_NSTATE)),
            _const_spec((1, S5_WIDTH)), _const_spec((S5_WIDTH, S5_WIDTH))]


def _s5_prompt(proj, sp, nb, seq, tm):
    nt = seq // tm
    cb = P_S5 // S5_WIDTH
    y, xre, xim = pl.pallas_call(
        functools.partial(_s5_kernel, chained=True, tm=tm),
        grid=(nb, nt),
        in_specs=[pl.BlockSpec((tm, S5_WIDTH), lambda b, t: (b * nt + t, cb))] + _s5_param_specs(),
        out_specs=[pl.BlockSpec((tm, S5_WIDTH), lambda b, t: (b * nt + t, 0)),
                   pl.BlockSpec((1, 1, S5_NSTATE), lambda b, t: (b, 0, 0)),
                   pl.BlockSpec((1, 1, S5_NSTATE), lambda b, t: (b, 0, 0))],
        out_shape=[jax.ShapeDtypeStruct((nb * seq, S5_WIDTH), BF16),
                   jax.ShapeDtypeStruct((nb, 1, S5_NSTATE), F32),
                   jax.ShapeDtypeStruct((nb, 1, S5_NSTATE), F32)],
        scratch_shapes=[pltpu.VMEM((tm, S5_NSTATE), F32), pltpu.VMEM((tm, S5_NSTATE), F32),
                        pltpu.VMEM((1, S5_NSTATE), F32), pltpu.VMEM((1, S5_NSTATE), F32)],
        compiler_params=_cparams(("parallel", "arbitrary")),
        name="s5_prompt",
    )(proj, *sp)
    return y, xre.reshape(nb, S5_GROUPS, S5_STATE), xim.reshape(nb, S5_GROUPS, S5_STATE)


def _s5_sample(proj, x0re, x0im, sp, row0, nseq, tm):
    ns = tm // SUBLANES
    cb = P_S5 // S5_WIDTH
    r0 = row0 // tm
    st = pl.BlockSpec((ns, S5_NSTATE), lambda i: (i, 0))
    y, xre, xim = pl.pallas_call(
        functools.partial(_s5_kernel, chained=False, tm=tm),
        grid=(nseq // ns,),
        in_specs=[pl.BlockSpec((tm, S5_WIDTH), lambda i: (r0 + i, cb)), st, st] + _s5_param_specs(),
        out_specs=[pl.BlockSpec((tm, S5_WIDTH), lambda i: (i, 0)), st, st],
        out_shape=[jax.ShapeDtypeStruct((nseq * SUBLANES, S5_WIDTH), BF16),
                   jax.ShapeDtypeStruct((nseq, S5_NSTATE), F32),
                   jax.ShapeDtypeStruct((nseq, S5_NSTATE), F32)],
        scratch_shapes=[pltpu.VMEM((tm, S5_NSTATE), F32), pltpu.VMEM((tm, S5_NSTATE), F32)],
        compiler_params=_cparams(("parallel",)),
        name="s5_sample",
    )(proj, x0re.reshape(nseq, S5_NSTATE), x0im.reshape(nseq, S5_NSTATE), *sp)
    return y, xre.reshape(nseq, S5_GROUPS, S5_STATE), xim.reshape(nseq, S5_GROUPS, S5_STATE)


def _s5_params(lam_re, lam_im, b_re, b_im, c_re, c_im, d, log_step, w_glu):
    lam = lax.complex(lam_re, lam_im)
    step = jnp.exp(log_step)[:, None]
    lam_bar = jnp.exp(lam * step)
    b_bar = ((lam_bar - 1.0) / lam)[..., None] * lax.complex(b_re, b_im)

    def bblk(bm):
        bm = bm.reshape(S5_Q, 8, S5_STATE, S5_GROUP)
        eye = jnp.eye(8, dtype=F32)
        out = jnp.einsum('qgpc,gh->qgchp', bm, eye)
        return out.reshape(S5_Q, LANES, S5_QW).astype(BF16)

    def cblk(cm):
        cm = cm.reshape(S5_Q, 8, S5_GROUP, S5_STATE)
        eye = jnp.eye(8, dtype=F32)
        out = jnp.einsum('qgcp,gh->qgphc', cm, eye)
        return out.reshape(S5_Q, S5_QW, LANES).astype(BF16)

    lam_flat = lam_bar.reshape(1, S5_NSTATE)
    pows = [lam_flat]
    for _ in range(SUBLANES - 1):
        pows.append(pows[-1] * lam_flat)
    row = jnp.arange(SUBLANES)[:, None]
    m = jnp.stack([jnp.where(row >= dd, pows[dd - 1], 0.0) for dd in (1, 2, 4)])
    p = jnp.concatenate(pows, axis=0)
    return (bblk(b_bar.real), bblk(b_bar.imag), cblk(c_re), cblk(c_im),
            m.real, m.imag, p.real, p.imag, d.reshape(1, S5_WIDTH), w_glu.astype(BF16))


def _seg64_sum(x):
    outs = []
    for t in range(x.shape[1] // LANES):
        xt = x[:, LANES * t:LANES * (t + 1)]
        lo = lax.broadcasted_iota(jnp.int32, xt.shape, 1) < RW_HD
        s_lo = jnp.sum(jnp.where(lo, xt, 0.0), axis=-1, keepdims=True)
        s_hi = jnp.sum(jnp.where(lo, 0.0, xt), axis=-1, keepdims=True)
        outs.append(jnp.where(lo, s_lo, s_hi))
    return jnp.concatenate(outs, axis=1)


def _rwkv_kernel(*refs, per_seq, tm):
    if per_seq:
        (h_ref, sh_ref, si_ref, mu_ref, w0_ref, w2_ref, a0_ref, a2_ref, g2_ref, kk_ref, ka_ref, rk_ref,
         lnw_ref, lnb_ref, y_ref, so_ref, r_s, k_s, v_s, a_s, b_s, lw_s, y_s) = refs
    else:
        (h_ref, mu_ref, w0_ref, w2_ref, a0_ref, a2_ref, g2_ref, kk_ref, ka_ref, rk_ref, lnw_ref, lnb_ref,
         y_ref, so_ref, r_s, k_s, v_s, a_s, b_s, lw_s, y_s, prev_s, st_s) = refs
    c, w, hd_w = CHUNK, RW_WIDTH, RW_HD

    h = h_ref[...]
    if per_seq:
        prev = _shift_seq8(h, 1, sh_ref[...])
    else:
        @pl.when(pl.program_id(1) == 0)
        def _():
            prev_s[...] = jnp.zeros_like(prev_s)
            st_s[...] = jnp.zeros_like(st_s)
        prev = _shift_chain(h, 1, prev_s[...])
        prev_s[...] = h[tm - SUBLANES:, :]
    hs = h + (prev - h) * mu_ref[...]
    r = hs[:, 0:w]
    k = hs[:, w:2 * w]
    v = hs[:, 2 * w:3 * w]
    wd = hs[:, 3 * w:3 * w + LANES]
    ad = hs[:, 3 * w + LANES:3 * w + 2 * LANES]
    gd = hs[:, 3 * w + 2 * LANES:]
    w_log = -_softplus(-(w0_ref[...] + _mm(jnp.tanh(wd).astype(BF16), w2_ref[...]))) - 0.5
    lw_s[...] = -jnp.exp(w_log)
    a_ic = _sigmoid(a0_ref[...] + _mm(ad.astype(BF16), a2_ref[...]))
    gate = _mm(_sigmoid(gd).astype(BF16), g2_ref[...])
    kx = k * kk_ref[...]
    kkn = kx * lax.rsqrt(_seg64_sum(kx * kx) + 1e-6)
    k2 = k * (1.0 + (a_ic - 1.0) * ka_ref[...])
    r_s[...] = r
    k_s[...] = k2
    v_s[...] = v
    a_s[...] = -kkn
    b_s[...] = kkn * a_ic

    strict, causal = _chunk_masks(per_seq)
    if per_seq:
        big_mask = _seq_block_mask(2 * c, hd_w)

    heads = range(RW_HEADS)
    sls = [slice(hd_w * hd, hd_w * (hd + 1)) for hd in heads]
    group = _chunks_per_iteration(tm // c)

    def chunk_group(gidx, carry):
        pre = []
        for t in range(group):
            idx = gidx * group + t
            rows = pl.ds(pl.multiple_of(idx * c, c), c)
            lw = lw_s[rows, :]
            cum = _cumsum_rows(lw, groupwise=per_seq)
            cl = _group_last(cum) if per_seq else cum[c - 1:c, :]
            e_pos, e_neg, e_end = jnp.exp(cum), jnp.exp(-cum), jnp.exp(cl - cum)
            rr, kc, vc, ac, bc = r_s[rows, :], k_s[rows, :], v_s[rows, :], a_s[rows, :], b_s[rows, :]
            pre.append(dict(rows=rows, seqs=pl.ds(idx * SEQ_PER_CHUNK, SEQ_PER_CHUNK), vc=vc,
                            rt=rr * e_pos, kt=kc * e_neg, bt=bc * e_neg, at=ac * jnp.exp(cum - lw),
                            kh=kc * e_end, bh=bc * e_end, wc=jnp.exp(cl)))
        pairs = [(t, hd) for t in range(group) for hd in heads]
        ath = [pre[t]['at'][:, sls[hd]] for t, hd in pairs]
        vh = [pre[t]['vc'][:, sls[hd]] for t, hd in pairs]
        rth = [pre[t]['rt'][:, sls[hd]] for t, hd in pairs]
        ab = [_mm_nt(jnp.concatenate([ath[p], rth[p]], axis=0).astype(BF16),
                     jnp.concatenate([pre[t]['kt'][:, sls[hd]], pre[t]['bt'][:, sls[hd]]], axis=0).astype(BF16))
              for p, (t, hd) in enumerate(pairs)]
        a_ak = [jnp.where(strict, m[:c, :c], 0.0) for m in ab]
        n_ab = [jnp.where(strict, -m[:c, c:], 0.0) for m in ab]
        a_r = [jnp.concatenate([jnp.where(causal, m[c:, :c], 0.0), jnp.where(causal, m[c:, c:], 0.0)],
                               axis=1).astype(BF16) for m in ab]
        akv = [_mm(a_ak[p].astype(BF16), vh[p].astype(BF16)) for p in range(len(pairs))]
        sol = _solve_unit_lower_each(
            n_ab, [jnp.concatenate([ath[p], akv[p]], axis=1) for p in range(len(pairs))], per_seq)
        kb = [jnp.concatenate([pre[t]['kh'][:, sls[hd]], pre[t]['bh'][:, sls[hd]]], axis=0).astype(BF16)
              for t, hd in pairs]
        for t in range(group):
            rows, seqs, wc = pre[t]['rows'], pre[t]['seqs'], pre[t]['wc']
            ps = [t * RW_HEADS + hd for hd in heads]
            if per_seq:
                st = [si_ref[seqs, hd].reshape(SEQ_PER_CHUNK * hd_w, hd_w) for hd in heads]
            else:
                st = [st_s[hd] for hd in heads]
            pr = [_mm_nt(jnp.concatenate([sol[p][:, :hd_w], rth[p]], axis=0).astype(BF16), st[hd].astype(BF16))
                  for hd, p in zip(heads, ps)]
            if per_seq:
                u = [_pick_own_seq(pr[hd][:c], hd_w) + sol[p][:, hd_w:] for hd, p in zip(heads, ps)]
                yst = [_pick_own_seq(pr[hd][c:], hd_w) for hd in heads]
            else:
                u = [pr[hd][:c] + sol[p][:, hd_w:] for hd, p in zip(heads, ps)]
                yst = [pr[hd][c:] for hd in heads]
            vu = [jnp.concatenate([vh[p], u[hd]], axis=0) for hd, p in zip(heads, ps)]
            ya = [_mm(a_r[p], vu[hd].astype(BF16)) for hd, p in zip(heads, ps)]
            if per_seq:
                upd = [_mm_tn(jnp.where(big_mask, jnp.concatenate([vu[hd]] * SEQ_PER_CHUNK, axis=1), 0.0
                                        ).astype(BF16), kb[p]) for hd, p in zip(heads, ps)]
            else:
                upd = [_mm_tn(vu[hd].astype(BF16), kb[p]) for hd, p in zip(heads, ps)]
            for hd in heads:
                if per_seq:
                    wcb = jnp.concatenate(
                        [jnp.broadcast_to(wc[SUBLANES * s:SUBLANES * s + 1, sls[hd]], (hd_w, hd_w))
                         for s in range(SEQ_PER_CHUNK)], axis=0)
                    so_ref[seqs, hd] = (st[hd] * wcb + upd[hd]).reshape(SEQ_PER_CHUNK, hd_w, hd_w)
                else:
                    st_s[hd] = st[hd] * wc[:, sls[hd]] + upd[hd]
                y_s[rows, sls[hd]] = yst[hd] + ya[hd]
        return carry

    lax.fori_loop(0, tm // (c * group), chunk_group, 0)

    if not per_seq:
        so_ref[0] = st_s[...]

    y = y_s[...]
    mean = _seg64_sum(y) * (1.0 / hd_w)
    yc = y - mean
    var = _seg64_sum(yc * yc) * (1.0 / hd_w)
    yn = yc * lax.rsqrt(var + RW_GN_EPS) * lnw_ref[...] + lnb_ref[...]
    bonus = _seg64_sum(r_s[...] * k_s[...] * rk_ref[...]) * v_s[...]
    y_ref[...] = ((yn + bonus) * gate).astype(BF16)


def _rwkv_param_specs():
    row = _const_spec((1, RW_WIDTH))
    return [_const_spec((1, RW_PCOLS)), row, _const_spec((LANES, RW_WIDTH)), row, _const_spec((LANES, RW_WIDTH)),
            _const_spec((RW_GATE, RW_WIDTH)), row, row, row, row, row]


def _rwkv_scratch(tm):
    return [pltpu.VMEM((tm, RW_WIDTH), F32) for _ in range(7)]


def _rwkv_prompt(proj, rp, nb, seq, tm):
    nt = seq // tm
    cb = P_RW // RW_PCOLS
    y, st = pl.pallas_call(
        functools.partial(_rwkv_kernel, per_seq=False, tm=tm),
        grid=(nb, nt),
        in_specs=[pl.BlockSpec((tm, RW_PCOLS), lambda b, t: (b * nt + t, cb))] + _rwkv_param_specs(),
        out_specs=[pl.BlockSpec((tm, RW_WIDTH), lambda b, t: (b * nt + t, 0)),
                   pl.BlockSpec((1, RW_HEADS, RW_HD, RW_HD), lambda b, t: (b, 0, 0, 0))],
        out_shape=[jax.ShapeDtypeStruct((nb * seq, RW_WIDTH), BF16),
                   jax.ShapeDtypeStruct((nb, RW_HEADS, RW_HD, RW_HD), F32)],
        scratch_shapes=_rwkv_scratch(tm) + [pltpu.VMEM((SUBLANES, RW_PCOLS), F32),
                                            pltpu.VMEM((RW_HEADS, RW_HD, RW_HD), F32)],
        compiler_params=_cparams(("parallel", "arbitrary")),
        name="rwkv_prompt",
    )(proj, *rp)
    return y, st


def _rwkv_sample(proj, shift0, st0, layer, rp, row0, nseq, tm):
    ns = tm // SUBLANES
    cb = P_RW // RW_PCOLS
    r0 = row0 // tm
    st_spec = pl.BlockSpec((ns, RW_HEADS, RW_HD, RW_HD), lambda i: (i, 0, 0, 0))
    st_in = _layer_spec((ns, RW_HEADS, RW_HD, RW_HD), lambda i: (i, 0, 0, 0), layer)
    y, st = pl.pallas_call(
        functools.partial(_rwkv_kernel, per_seq=True, tm=tm),
        grid=(nseq // ns,),
        in_specs=[pl.BlockSpec((tm, RW_PCOLS), lambda i: (r0 + i, cb)),
                  pl.BlockSpec((ns, 1, RW_PCOLS), lambda i: (i, 0, 0)), st_in] + _rwkv_param_specs(),
        out_specs=[pl.BlockSpec((tm, RW_WIDTH), lambda i: (i, 0)), st_spec],
        out_shape=[jax.ShapeDtypeStruct((nseq * SUBLANES, RW_WIDTH), BF16),
                   jax.ShapeDtypeStruct((nseq, RW_HEADS, RW_HD, RW_HD), F32)],
        scratch_shapes=_rwkv_scratch(tm),
        compiler_params=_cparams(("parallel",)),
        name="rwkv_sample",
    )(proj, shift0, st0, *rp)
    return y, st


def _rw_pad_cols(x):
    z = jnp.zeros(x.shape[:-1] + (LANES - RW_LORA,), x.dtype)
    a, b = 3 * RW_WIDTH, 3 * RW_WIDTH + RW_LORA
    return jnp.concatenate([x[..., :a], x[..., a:b], z, x[..., b:b + RW_LORA], z, x[..., b + RW_LORA:]], axis=-1)


def _rw_unpad_cols(x):
    a = 3 * RW_WIDTH
    return jnp.concatenate([x[..., :a + RW_LORA], x[..., a + LANES:a + LANES + RW_LORA], x[..., a + 2 * LANES:]],
                           axis=-1)


def _rwkv_params(mu, w0, w2, a0, a2, g2, k_k, k_a, r_k, ln_w, ln_b):
    row = lambda t: t.reshape(1, RW_WIDTH)
    padr = lambda t: jnp.concatenate([t, jnp.zeros((LANES - RW_LORA, RW_WIDTH), t.dtype)], axis=0).astype(BF16)
    return (_rw_pad_cols(mu).reshape(1, RW_PCOLS), row(w0), padr(w2), row(a0), padr(a2), g2.astype(BF16),
            row(k_k), row(k_a), row(r_k), row(ln_w), row(ln_b))


def _gdn_kernel(*refs, per_seq, tm):
    if per_seq:
        (q_ref, k_ref, v_ref, z_ref, ba_ref, fq_ref, fk_ref, fv_ref, si_ref, cwq_ref, cwk_ref, cwv_ref,
         al_ref, dt_ref, ng_ref, y_ref, so_ref, q_s, k_s, v_s, ba_s, o_s) = refs
        conv_state = (fq_ref, fk_ref, fv_ref)
    else:
        (q_ref, k_ref, v_ref, z_ref, ba_ref, cwq_ref, cwk_ref, cwv_ref, al_ref, dt_ref, ng_ref,
         y_ref, so_ref, q_s, k_s, v_s, ba_s, o_s, pq_s, pk_s, pv_s, st_s) = refs
        prevs = (pq_s, pk_s, pv_s)

        @pl.when(pl.program_id(1) == 0)
        def _():
            for p in prevs:
                p[...] = jnp.zeros_like(p)
            st_s[...] = jnp.zeros_like(st_s)
    c, hw = CHUNK, GD_HD

    def conv_act(idx, u_ref, cw_ref):
        u = u_ref[...]
        if per_seq:
            f = conv_state[idx][...]
            sh = [_shift_seq8(u, d, f) for d in (1, 2, 3)]
        else:
            p = prevs[idx][...]
            sh = [_shift_chain(u, d, p) for d in (1, 2, 3)]
            prevs[idx][...] = u[tm - SUBLANES:, :]
        x = cw_ref[3:4, :] * u + cw_ref[2:3, :] * sh[0] + cw_ref[1:2, :] * sh[1] + cw_ref[0:1, :] * sh[2]
        return x * _sigmoid(x)

    def l2n(x):
        outs = []
        for hd in range(GD_HEADS):
            xh = x[:, hw * hd:hw * (hd + 1)]
            outs.append(xh * lax.rsqrt(jnp.sum(xh * xh, axis=-1, keepdims=True) + 1e-6))
        return jnp.concatenate(outs, axis=1)

    q_s[...] = l2n(conv_act(0, q_ref, cwq_ref)) * (hw ** -0.5)
    k_s[...] = l2n(conv_act(1, k_ref, cwk_ref))
    v_s[...] = conv_act(2, v_ref, cwv_ref)
    ba = ba_ref[...]
    lane = lax.broadcasted_iota(jnp.int32, ba.shape, 1)
    g_all = -jnp.exp(al_ref[...]) * _softplus(ba + dt_ref[...])
    ba_s[...] = jnp.where(lane < GD_HEADS, _sigmoid(ba), g_all)

    strict, causal = _chunk_masks(per_seq)
    if per_seq:
        big_mask = _seq_block_mask(c, hw)

    heads = range(GD_HEADS)
    sls = [slice(hw * hd, hw * (hd + 1)) for hd in heads]
    group = _chunks_per_iteration(tm // c)

    def chunk_group(gidx, carry):
        pre = []
        for t in range(group):
            idx = gidx * group + t
            rows = pl.ds(pl.multiple_of(idx * c, c), c)
            bg = ba_s[rows, :]
            gc_all = _cumsum_rows(bg, groupwise=per_seq)
            pre.append(dict(rows=rows, seqs=pl.ds(idx * SEQ_PER_CHUNK, SEQ_PER_CHUNK), bg=bg, gc_all=gc_all,
                            gc_t=gc_all.T, gl_all=_group_last(gc_all) if per_seq else gc_all[c - 1:c, :],
                            qa=q_s[rows, :], ka=k_s[rows, :], va=v_s[rows, :]))
        pairs = [(t, hd) for t in range(group) for hd in heads]
        npair = len(pairs)
        beta = [pre[t]['bg'][:, hd:hd + 1] for t, hd in pairs]
        gc = [pre[t]['gc_all'][:, GD_HEADS + hd:GD_HEADS + hd + 1] for t, hd in pairs]
        gl = [pre[t]['gl_all'][:, GD_HEADS + hd:GD_HEADS + hd + 1] for t, hd in pairs]
        dec = [jnp.exp(jnp.where(causal, gc[p] - pre[t]['gc_t'][GD_HEADS + hd:GD_HEADS + hd + 1, :], -1e30))
               for p, (t, hd) in enumerate(pairs)]
        qh = [pre[t]['qa'][:, sls[hd]] for t, hd in pairs]
        kh = [pre[t]['ka'][:, sls[hd]] for t, hd in pairs]
        kb = [kh[p] * beta[p] for p in range(npair)]
        eg = [jnp.exp(g) for g in gc]
        kq = [_mm_nt(jnp.concatenate([kb[p], qh[p]], axis=0).astype(BF16), kh[p].astype(BF16))
              for p in range(npair)]
        low = [jnp.where(strict, kq[p][:c] * dec[p], 0.0) for p in range(npair)]
        intra = [(kq[p][c:] * dec[p]).astype(BF16) for p in range(npair)]
        sol = _solve_unit_lower_each(
            low, [jnp.concatenate([pre[t]['va'][:, sls[hd]] * beta[p], kb[p] * eg[p]], axis=1)
                  for p, (t, hd) in enumerate(pairs)], per_seq)
        wq = [jnp.concatenate([sol[p][:, hw:], qh[p] * eg[p]], axis=0).astype(BF16) for p in range(npair)]
        kdec = [kh[p] * jnp.exp(gl[p] - gc[p]) for p in range(npair)]
        for t in range(group):
            rows, seqs = pre[t]['rows'], pre[t]['seqs']
            ps = [t * GD_HEADS + hd for hd in heads]
            if per_seq:
                st3 = [si_ref[seqs, hd] for hd in heads]
                st_rhs = [jnp.concatenate([s3[s] for s in range(SEQ_PER_CHUNK)], axis=1).astype(BF16)
                          for s3 in st3]
            else:
                st = [st_s[hd] for hd in heads]
                st_rhs = [s.astype(BF16) for s in st]
            ws = [_mm(wq[p], st_rhs[hd]) for hd, p in zip(heads, ps)]
            if per_seq:
                v_new = [sol[p][:, :hw] - _pick_own_seq(ws[hd][:c], hw) for hd, p in zip(heads, ps)]
                o_st = [_pick_own_seq(ws[hd][c:], hw) for hd in heads]
            else:
                v_new = [sol[p][:, :hw] - ws[hd][:c] for hd, p in zip(heads, ps)]
                o_st = [ws[hd][c:] for hd in heads]
            vnb = [v.astype(BF16) for v in v_new]
            o_in = [_mm(intra[p], vnb[hd]) for hd, p in zip(heads, ps)]
            if per_seq:
                upd = [_mm_tn(jnp.where(big_mask, jnp.concatenate([kdec[p]] * SEQ_PER_CHUNK, axis=1), 0.0
                                        ).astype(BF16), vnb[hd]) for hd, p in zip(heads, ps)]
            else:
                upd = [_mm_tn(kdec[p].astype(BF16), vnb[hd]) for hd, p in zip(heads, ps)]
            for hd, p in zip(heads, ps):
                if per_seq:
                    egl = jnp.exp(gl[p])
                    mult = jnp.concatenate([jnp.broadcast_to(egl[SUBLANES * s:SUBLANES * s + 1, :], (hw, hw))
                                            for s in range(SEQ_PER_CHUNK)], axis=0)
                    st_new = st3[hd].reshape(SEQ_PER_CHUNK * hw, hw) * mult + upd[hd]
                    so_ref[seqs, hd] = st_new.reshape(SEQ_PER_CHUNK, hw, hw)
                else:
                    st_s[hd] = st[hd] * jnp.exp(gl[p]) + upd[hd]
                o_s[rows, sls[hd]] = o_st[hd] + o_in[hd]
        return carry

    lax.fori_loop(0, tm // (c * group), chunk_group, 0)

    if not per_seq:
        so_ref[0] = st_s[...]

    outs = []
    for hd in range(GD_HEADS):
        sl = slice(hw * hd, hw * (hd + 1))
        o = o_s[:, sl]
        o = o * lax.rsqrt(jnp.mean(o * o, axis=-1, keepdims=True) + NORM_EPS) * ng_ref[...]
        z = z_ref[:, sl]
        outs.append(o * (z * _sigmoid(z)))
    y_ref[...] = jnp.concatenate(outs, axis=1).astype(BF16)


def _gdn_param_specs():
    c2 = lambda shape, j=0: pl.BlockSpec(shape, (lambda *a: (0, j)))
    return [c2((GD_CONV, GD_WIDTH), 0), c2((GD_CONV, GD_WIDTH), 1), c2((GD_CONV, GD_WIDTH), 2),
            c2((1, LANES)), c2((1, LANES)), c2((1, GD_HD))]


def _gdn_scratch(tm):
    return ([pltpu.VMEM((tm, GD_WIDTH), F32) for _ in range(3)] + [pltpu.VMEM((tm, LANES), F32),
                                                                  pltpu.VMEM((tm, GD_WIDTH), F32)])


def _gdn_prompt(proj, proj_sb, gp, nb, seq, tm):
    nt = seq // tm
    qb = P_Q // GD_WIDTH
    blk = lambda j: pl.BlockSpec((tm, GD_WIDTH), lambda b, t: (b * nt + t, j))
    y, st = pl.pallas_call(
        functools.partial(_gdn_kernel, per_seq=False, tm=tm),
        grid=(nb, nt),
        in_specs=[blk(qb), blk(qb + 1), blk(qb + 2), blk(P_Z // GD_WIDTH),
                  pl.BlockSpec((tm, LANES), lambda b, t: (b * nt + t, P_BA // LANES))] + _gdn_param_specs(),
        out_specs=[pl.BlockSpec((tm, GD_WIDTH), lambda b, t: (b * nt + t, 0)),
                   pl.BlockSpec((1, GD_HEADS, GD_HD, GD_HD), lambda b, t: (b, 0, 0, 0))],
        out_shape=[jax.ShapeDtypeStruct((nb * seq, GD_WIDTH), BF16),
                   jax.ShapeDtypeStruct((nb, GD_HEADS, GD_HD, GD_HD), F32)],
        scratch_shapes=_gdn_scratch(tm) + [pltpu.VMEM((SUBLANES, GD_WIDTH), F32) for _ in range(3)]
        + [pltpu.VMEM((GD_HEADS, GD_HD, GD_HD), F32)],
        compiler_params=_cparams(("parallel", "arbitrary")),
        name="gdn_prompt",
    )(proj, proj, proj, proj, proj_sb, gp[0], gp[0], gp[0], *gp[1:])
    return y, st


def _gdn_sample(proj, proj_sb, conv0, st0, layer, gp, row0, nseq, tm):
    ns = tm // SUBLANES
    qb = P_Q // GD_WIDTH
    r0 = row0 // tm
    blk = lambda j: pl.BlockSpec((tm, GD_WIDTH), lambda i: (r0 + i, j))
    frm = lambda j: _layer_spec((ns, GD_CONV - 1, GD_WIDTH), lambda i: (i, 0, j), layer)
    st_spec = pl.BlockSpec((ns, GD_HEADS, GD_HD, GD_HD), lambda i: (i, 0, 0, 0))
    st_in = _layer_spec((ns, GD_HEADS, GD_HD, GD_HD), lambda i: (i, 0, 0, 0), layer)
    y, st = pl.pallas_call(
        functools.partial(_gdn_kernel, per_seq=True, tm=tm),
        grid=(nseq // ns,),
        in_specs=[blk(qb), blk(qb + 1), blk(qb + 2), blk(P_Z // GD_WIDTH),
                  pl.BlockSpec((tm, LANES), lambda i: (r0 + i, P_BA // LANES)),
                  frm(0), frm(1), frm(2), st_in] + _gdn_param_specs(),
        out_specs=[pl.BlockSpec((tm, GD_WIDTH), lambda i: (i, 0)), st_spec],
        out_shape=[jax.ShapeDtypeStruct((nseq * SUBLANES, GD_WIDTH), BF16),
                   jax.ShapeDtypeStruct((nseq, GD_HEADS, GD_HD, GD_HD), F32)],
        scratch_shapes=_gdn_scratch(tm),
        compiler_params=_cparams(("parallel",)),
        name="gdn_sample",
    )(proj, proj, proj, proj, proj_sb, conv0, conv0, conv0, st0, gp[0], gp[0], gp[0], *gp[1:])
    return y, st


def _gdn_params(conv_w, a_log, dt_bias, norm_g):
    pad = lambda t: jnp.concatenate([jnp.zeros((GD_HEADS,), F32), t, jnp.zeros((LANES - 2 * GD_HEADS,), F32)]
                                    ).reshape(1, LANES)
    return (conv_w, pad(a_log), pad(dt_bias), norm_g.reshape(1, GD_HD))


W_IN_RW = S5_WIDTH
W_IN_QKVZ = W_IN_RW + RW_COLS
W_IN_BA = W_IN_QKVZ + 4 * GD_WIDTH
W_IN_GATES = W_IN_BA + 2 * GD_HEADS


def _split_w_in(w):
    w_t = jnp.swapaxes(w, 1, 2)
    zeros = lambda rows: jnp.zeros((w_t.shape[0], rows, w_t.shape[2]), w_t.dtype)
    a = W_IN_RW + 3 * RW_WIDTH + RW_LORA
    w_rw_t = jnp.concatenate([w_t[:, W_IN_RW:a], zeros(LANES - RW_LORA), w_t[:, a:a + RW_LORA],
                              zeros(LANES - RW_LORA), w_t[:, a + RW_LORA:W_IN_QKVZ]], axis=1)
    w_sb_t = jnp.concatenate([w_t[:, :S5_WIDTH], w_t[:, W_IN_BA:W_IN_GATES],
                              zeros(P_SB_COLS - P_BA - 2 * GD_HEADS)], axis=1)
    return w_t, w_rw_t, w_sb_t


def _tail_rows(a, cols, nrows, mp, seq):
    a3 = a.reshape(a.shape[0] // SUBLANES, SUBLANES, a.shape[1])
    g = seq // SUBLANES
    lo = SUBLANES - nrows
    return (a3[g - 1:mp // SUBLANES:g, lo:, cols[0]:cols[1]], a3[mp // SUBLANES:, lo:, cols[0]:cols[1]])


def kernel(x_prompt, x_sample, state_s5_re, state_s5_im, state_rwkv_shift, state_rwkv_wkv, state_gdn_conv, state_gdn, state_ffn_conv, norm1_g, norm2_g, final_norm_g, w_in, s5_lambda_re, s5_lambda_im, s5_b_re, s5_b_im, s5_c_re, s5_c_im, s5_d, s5_log_step, s5_w_glu, rwkv_mu, rwkv_w0, rwkv_w2, rwkv_a0, rwkv_a2, rwkv_g2, rwkv_k_k, rwkv_k_a, rwkv_r_k, rwkv_ln_w, rwkv_ln_b, gdn_conv_w, gdn_a_log, gdn_dt_bias, gdn_norm_g, w_br_s5, w_br_rwkv, w_br_gdn, w_out, ffn_w_up, ffn_conv_w, ffn_conv_b, ffn_w_down):
    nb, seq, d = x_prompt.shape
    ns, sl, _ = x_sample.shape
    assert sl == SUBLANES and d == D_MODEL
    mp, ms = nb * seq, ns * sl
    m = mp + ms
    tm_norm, tm_proj, tm_out = 512, 3072, 512
    x_groups = (x_prompt.reshape(mp, d), x_sample.reshape(ms, d), 0)

    w_t, w_rw_t, w_sb_t = _split_w_in(w_in)
    w_br = (w_br_s5.astype(BF16), w_br_rwkv.astype(BF16), w_br_gdn.astype(BF16))
    w_out_b, w_down_b = w_out.astype(BF16), ffn_w_down.astype(BF16)
    conv_b = ffn_conv_b.reshape(DEPTH, 1, 2 * D_FF)
    shift0 = _rw_pad_cols(state_rwkv_shift)[:, :, None, :]

    new_p = [[] for _ in range(7)]
    new_s = [[] for _ in range(7)]
    for l in range(DEPTH):
        xp, xs, s_off = x_groups
        xn1 = _norm_cast(xp, xs, s_off * (mp // tm_norm), m, mp // tm_norm, norm1_g[l], tm=tm_norm)
        proj_g = _matmul_nt(xn1, w_t, l, W_IN_GATES, 3 * D_MODEL, tm=tm_proj, tn=512)
        proj_rw = _matmul_nt(xn1, w_rw_t, l, 0, RW_PCOLS, tm=tm_proj, tn=512)
        proj_qkvz = _matmul_nt(xn1, w_t, l, W_IN_QKVZ, 4 * GD_WIDTH, tm=tm_proj, tn=512)
        proj_sb = _matmul_nt(xn1, w_sb_t, l, 0, P_SB_COLS, tm=tm_proj, tn=512)

        sp = _s5_params(s5_lambda_re[l], s5_lambda_im[l], s5_b_re[l], s5_b_im[l], s5_c_re[l], s5_c_im[l],
                        s5_d[l], s5_log_step[l], s5_w_glu[l])
        ys5_p, p_re, p_im = _s5_prompt(proj_sb, sp, nb, seq, tm=256)
        ys5_s, s_re, s_im = _s5_sample(proj_sb, state_s5_re[l], state_s5_im[l], sp, mp, ns, tm=128)

        rp = _rwkv_params(rwkv_mu[l], rwkv_w0[l], rwkv_w2[l], rwkv_a0[l], rwkv_a2[l], rwkv_g2[l], rwkv_k_k[l],
                          rwkv_k_a[l], rwkv_r_k[l].reshape(RW_WIDTH), rwkv_ln_w[l], rwkv_ln_b[l])
        yrw_p, p_wkv = _rwkv_prompt(proj_rw, rp, nb, seq, tm=256)
        yrw_s, s_wkv = _rwkv_sample(proj_rw, shift0[l], state_rwkv_wkv, l, rp, mp, ns, tm=128)
        p_shift, s_shift = [_rw_unpad_cols(t[:, 0])
                            for t in _tail_rows(proj_rw, (P_RW, P_RW + RW_PCOLS), 1, mp, seq)]

        gp = _gdn_params(gdn_conv_w[l], gdn_a_log[l], gdn_dt_bias[l], gdn_norm_g[l])
        ygd_p, p_gdn = _gdn_prompt(proj_qkvz, proj_sb, gp, nb, seq, tm=256)
        ygd_s, s_gdn = _gdn_sample(proj_qkvz, proj_sb, state_gdn_conv, state_gdn, l, gp, mp, ns, tm=64)
        p_gconv, s_gconv = _tail_rows(proj_qkvz, (P_Q, P_Q + 3 * GD_WIDTH), GD_CONV - 1, mp, seq)

        merged = _merge((ys5_p, yrw_p, ygd_p), (ys5_s, yrw_s, ygd_s), w_br, l, proj_g, tm=512, tn=1024)
        x, xn = _out_norm(merged, w_out_b, l, xp, xs, s_off * (mp // tm_out), mp // tm_out, norm2_g[l], tm=tm_out)

        h, p_ffn, s_ffn = _ffn_up(xn, ffn_w_up, ffn_conv_w, conv_b, state_ffn_conv, l, nb, seq, ns,
                                  tm=1024, tn=512)
        x = _matmul_res(h, w_down_b, l, x, tm=1536, tn=512)
        x_groups = (x, x, 1)

        for lst, vals in ((new_p, (p_re, p_im, p_shift, p_wkv, p_gconv, p_gdn, p_ffn)),
                          (new_s, (s_re, s_im, s_shift, s_wkv, s_gconv, s_gdn, s_ffn))):
            for acc, val in zip(lst, vals):
                acc.append(val)

    y_p, y_s = _final_norm(x, final_norm_g, mp, tm=512)
    stack = lambda lst: tuple(jnp.stack(v) for v in lst)
    return (y_p.reshape(nb, seq, d), y_s.reshape(ns, sl, d)) + stack(new_p) + stack(new_s)
```
